```python
import math
import jax
import jax.numpy as jnp
from jax import lax
import numpy as np


D_MODEL = 1024
BATCH = 2
SEQ = 8192
DEPTH = 1

N_HEADS = 8
HEAD_DIM = 64
ATTN_WIDTH = N_HEADS * HEAD_DIM
MOBA_BLOCK = 256
MOBA_TOPK = 3
Q_CHUNK = 128
CONV_WIDTH = D_MODEL // 2
CONV_K = 3
N_BRANCHES = 2
D_IN = 3 * ATTN_WIDTH + 3 * CONV_WIDTH + N_BRANCHES * D_MODEL
N_BUCKETS = 32
MAX_EXACT = N_BUCKETS // 2
MAX_DISTANCE = 128
N_GROUPS = 4
EXPERTS_PER_GROUP = 8
N_EXPERTS = N_GROUPS * EXPERTS_PER_GROUP
TOP_K_EXPERTS = 2
D_EXPERT = D_MODEL // 2
EXPERT_BLOCK = 128
N_MOD = 6
NORM_EPS = 1e-6
NEG_INF = -1e30

kernel_name = 'hybrid_moba_shortconv_hmoe_block'


def rms_norm(x, g):
    xf = x.astype(jnp.float32)
    y = xf * lax.rsqrt(jnp.mean(xf * xf, axis=-1, keepdims=True) + NORM_EPS)
    return y.astype(x.dtype) * g


def rel_bucket(dist):
    n = jnp.maximum(dist, 0)
    nf = jnp.maximum(n, 1).astype(jnp.float32)
    large = MAX_EXACT + (jnp.log(nf / MAX_EXACT) / math.log(MAX_DISTANCE / MAX_EXACT)
                         * (N_BUCKETS - MAX_EXACT)).astype(jnp.int32)
    large = jnp.minimum(large, N_BUCKETS - 1)
    return jnp.where(n < MAX_EXACT, n, large)


def moba_attention(q, k, v, rel_bias):
    B, S, H, Dh = q.shape
    Sp = -(-S // MOBA_BLOCK) * MOBA_BLOCK
    pad = ((0, 0), (0, Sp - S), (0, 0), (0, 0))
    q = jnp.pad(q, pad)
    k = jnp.pad(k, pad)
    v = jnp.pad(v, pad)
    nb = Sp // MOBA_BLOCK
    tk = max(1, min(MOBA_TOPK, nb - 1))
    scale = HEAD_DIM ** -0.5

    kb = k.reshape(B, nb, MOBA_BLOCK, H, Dh)
    vb = v.reshape(B, nb, MOBA_BLOCK, H, Dh)
    k_mean = jnp.mean(kb.astype(jnp.float32), axis=2)
    q_block = jnp.arange(Sp) // MOBA_BLOCK
    gate = jnp.einsum('bshd,bnhd->bshn', q.astype(jnp.float32), k_mean)
    past = jnp.arange(nb)[None, :] < q_block[:, None]
    gate = jnp.where(past[None, :, None, :], gate, NEG_INF)
    _, sel = lax.top_k(gate, tk)
    sel_valid = sel < q_block[None, :, None, None]

    k_blocks = kb.transpose(0, 3, 1, 2, 4)
    v_blocks = vb.transpose(0, 3, 1, 2, 4)
    b_ix = jnp.arange(B)[:, None, None, None]
    h_ix = jnp.arange(H)[None, None, :, None]
    key_off = jnp.arange(MOBA_BLOCK)
    n_chunks = Sp // Q_CHUNK

    def to_chunks(t):
        return jnp.moveaxis(t.reshape(B, n_chunks, Q_CHUNK, *t.shape[2:]), 1, 0)

    def chunk_attn(args):
        ci, qc, sc, vc = args
        q_pos = ci * Q_CHUNK + jnp.arange(Q_CHUNK)
        k_sel = k_blocks[b_ix, h_ix, sc]
        v_sel = v_blocks[b_ix, h_ix, sc]
        kpos_sel = sc[..., None] * MOBA_BLOCK + key_off
        bias_sel = rel_bias[rel_bucket(q_pos[None, :, None, None, None] - kpos_sel), h_ix[..., None]]
        s_sel = jnp.einsum('bqhd,bqhtkd->bqhtk', qc, k_sel).astype(jnp.float32) * scale + bias_sel
        s_sel = jnp.where(vc[..., None], s_sel, NEG_INF).reshape(B, Q_CHUNK, H, tk * MOBA_BLOCK)
        start = (ci * Q_CHUNK // MOBA_BLOCK) * MOBA_BLOCK
        k_own = lax.dynamic_slice_in_dim(k, start, MOBA_BLOCK, axis=1)
        v_own = lax.dynamic_slice_in_dim(v, start, MOBA_BLOCK, axis=1)
        dist = q_pos[:, None] - (start + key_off)[None, :]
        bias_own = jnp.transpose(rel_bias[rel_bucket(dist)], (0, 2, 1))
        s_own = jnp.einsum('bqhd,bkhd->bqhk', qc, k_own).astype(jnp.float32) * scale + bias_own[None]
        s_own = jnp.where((dist >= 0)[None, :, None, :], s_own, NEG_INF)
        p = jax.nn.softmax(jnp.concatenate([s_sel, s_own], axis=-1), axis=-1).astype(v.dtype)
        p_sel = p[..., :tk * MOBA_BLOCK].reshape(B, Q_CHUNK, H, tk, MOBA_BLOCK)
        p_own = p[..., tk * MOBA_BLOCK:]
        return (jnp.einsum('bqhtk,bqhtkd->bqhd', p_sel, v_sel)
                + jnp.einsum('bqhk,bkhd->bqhd', p_own, v_own))

    out = lax.map(chunk_attn, (jnp.arange(n_chunks), to_chunks(q), to_chunks(sel), to_chunks(sel_valid)))
    return jnp.moveaxis(out, 0, 1).reshape(B, Sp, H, Dh)[:, :S]


def short_conv(cg, bg, xv, w_conv):
    u = cg * xv
    S = u.shape[1]
    up = jnp.pad(u, ((0, 0), (CONV_K - 1, 0), (0, 0)))
    y = sum(w_conv[j] * up[:, j:j + S] for j in range(CONV_K))
    return bg * y


def hierarchical_moe(h2d, w_rg, b_rg, w_re, b_re, w_gate, w_up, w_down):
    T, D = h2d.shape
    hf = h2d.astype(jnp.float32)
    g_logits = hf @ w_rg.astype(jnp.float32) + b_rg.astype(jnp.float32)
    g_prob = jax.nn.softmax(g_logits, axis=-1)
    g_sel = jnp.argmax(g_logits, axis=-1).astype(jnp.int32)
    g_w = jnp.take_along_axis(g_prob, g_sel[:, None], axis=-1)
    e_logits = (hf @ w_re.astype(jnp.float32) + b_re.astype(jnp.float32)).reshape(T, N_GROUPS, EXPERTS_PER_GROUP)
    e_logits = jnp.take_along_axis(e_logits, g_sel[:, None, None], axis=1)[:, 0]
    e_prob = jax.nn.softmax(e_logits, axis=-1)
    top_p, top_i = lax.top_k(e_prob, TOP_K_EXPERTS)
    weights = top_p / jnp.sum(top_p, axis=-1, keepdims=True) * g_w
    expert = g_sel[:, None] * EXPERTS_PER_GROUP + top_i.astype(jnp.int32)

    A = T * TOP_K_EXPERTS
    a_e = expert.reshape(A)
    a_w = weights.reshape(A)
    a_tok = jnp.repeat(jnp.arange(T, dtype=jnp.int32), TOP_K_EXPERTS)
    order = jnp.argsort(a_e)
    e_s = a_e[order]
    tok_s = a_tok[order]
    w_s = a_w[order]
    counts = jnp.bincount(a_e, length=N_EXPERTS).astype(jnp.int32)
    starts = jnp.cumsum(counts) - counts
    padded = (counts + EXPERT_BLOCK - 1) // EXPERT_BLOCK * EXPERT_BLOCK
    pends = jnp.cumsum(padded)
    pstarts = pends - padded
    dest = pstarts[e_s] + jnp.arange(A, dtype=jnp.int32) - starts[e_s]
    n_blk = -(-A // EXPERT_BLOCK) + N_EXPERTS
    buf = jnp.zeros((n_blk * EXPERT_BLOCK, D), h2d.dtype).at[dest].set(h2d[tok_s])
    blk_e = jnp.minimum(jnp.searchsorted(pends, jnp.arange(n_blk, dtype=jnp.int32) * EXPERT_BLOCK,
                                         side='right'), N_EXPERTS - 1)

    def expert_ffn(args):
        xb, e = args
        return (jax.nn.silu(xb @ w_gate[e]) * (xb @ w_up[e])) @ w_down[e]

    out = lax.map(expert_ffn, (buf.reshape(n_blk, EXPERT_BLOCK, D), blk_e)).reshape(-1, D)
    contrib = (out[dest] * w_s[:, None]).astype(h2d.dtype)
    return jnp.zeros_like(h2d).at[tok_s].add(contrib)


def setup_inputs(seed: int = 0) -> dict:
    key = jax.random.key(seed)
    ks = jax.random.split(key, 21)
    f32 = jnp.float32

    def nrm(k, shape, scale):
        return jax.random.normal(k, shape, f32) * scale

    return {
        'x': nrm(ks[0], (BATCH, SEQ, D_MODEL), 1.0),
        'c': nrm(ks[1], (BATCH, D_MODEL), 1.0),
        'rel_bias': nrm(ks[2], (N_BUCKETS, N_HEADS), 0.5),
        'norm1_g': 1.0 + nrm(ks[3], (DEPTH, D_MODEL), 0.02),
        'norm2_g': 1.0 + nrm(ks[4], (DEPTH, D_MODEL), 0.02),
        'w_ada': nrm(ks[5], (DEPTH, D_MODEL, N_MOD * D_MODEL), 0.5 * D_MODEL ** -0.5),
        'b_ada': nrm(ks[6], (DEPTH, N_MOD * D_MODEL), 0.02),
        'w_in': nrm(ks[7], (DEPTH, D_MODEL, D_IN), D_MODEL ** -0.5),
        'b_gate': nrm(ks[8], (DEPTH, N_BRANCHES * D_MODEL), 0.02),
        'conv_w': nrm(ks[9], (DEPTH, CONV_K, CONV_WIDTH), CONV_K ** -0.5),
        'w_attn_out': nrm(ks[10], (DEPTH, ATTN_WIDTH, D_MODEL), ATTN_WIDTH ** -0.5),
        'w_conv_out': nrm(ks[11], (DEPTH, CONV_WIDTH, D_MODEL), CONV_WIDTH ** -0.5),
        'w_o': nrm(ks[12], (DEPTH, D_MODEL, D_MODEL), D_MODEL ** -0.5),
        'w_router_group': nrm(ks[13], (DEPTH, D_MODEL, N_GROUPS), D_MODEL ** -0.5),
        'b_router_group': nrm(ks[14], (DEPTH, N_GROUPS), 0.01),
        'w_router_expert': nrm(ks[15], (DEPTH, D_MODEL, N_EXPERTS), D_MODEL ** -0.5),
        'b_router_expert': nrm(ks[16], (DEPTH, N_EXPERTS), 0.01),
        'w_exp_gate': nrm(ks[17], (DEPTH, N_EXPERTS, D_MODEL, D_EXPERT), D_MODEL ** -0.5),
        'w_exp_up': nrm(ks[18], (DEPTH, N_EXPERTS, D_MODEL, D_EXPERT), D_MODEL ** -0.5),
        'w_exp_down': nrm(ks[19], (DEPTH, N_EXPERTS, D_EXPERT, D_MODEL), D_EXPERT ** -0.5),
        'final_norm_g': 1.0 + nrm(ks[20], (D_MODEL,), 0.02),
    }


def reference(x, c, rel_bias, norm1_g, norm2_g, w_ada, b_ada, w_in, b_gate, conv_w,
              w_attn_out, w_conv_out, w_o, w_router_group, b_router_group,
              w_router_expert, b_router_expert, w_exp_gate, w_exp_up, w_exp_down,
              final_norm_g):
    B, S, D = x.shape
    widths = [ATTN_WIDTH, ATTN_WIDTH, ATTN_WIDTH, CONV_WIDTH, CONV_WIDTH, CONV_WIDTH, D_MODEL, D_MODEL]
    split_at = [int(i) for i in np.cumsum(widths)[:-1]]
    c_act = jax.nn.silu(c)
    for l in range(DEPTH):
        mod = c_act @ w_ada[l] + b_ada[l]
        sh1, sc1, g1, sh2, sc2, g2 = jnp.split(mod, N_MOD, axis=-1)

        h = rms_norm(x, norm1_g[l]) * (1.0 + sc1[:, None]) + sh1[:, None]
        proj = h @ w_in[l]
        q, k, v, cg, bg, xv, gl_a, gl_c = jnp.split(proj, split_at, axis=-1)
        q = q.reshape(B, S, N_HEADS, HEAD_DIM)
        k = k.reshape(B, S, N_HEADS, HEAD_DIM)
        v = v.reshape(B, S, N_HEADS, HEAD_DIM)
        y_attn = moba_attention(q, k, v, rel_bias).reshape(B, S, ATTN_WIDTH) @ w_attn_out[l]
        y_conv = short_conv(cg, bg, xv, conv_w[l]) @ w_conv_out[l]
        b_a, b_c = jnp.split(b_gate[l], N_BRANCHES, axis=-1)
        merged = jax.nn.sigmoid(gl_a + b_a) * y_attn + jax.nn.sigmoid(gl_c + b_c) * y_conv
        x = x + g1[:, None] * (merged @ w_o[l])

        h = rms_norm(x, norm2_g[l]) * (1.0 + sc2[:, None]) + sh2[:, None]
        y = hierarchical_moe(h.reshape(B * S, D), w_router_group[l], b_router_group[l],
                             w_router_expert[l], b_router_expert[l],
                             w_exp_gate[l], w_exp_up[l], w_exp_down[l]).reshape(B, S, D)
        x = x + g2[:, None] * y
    return rms_norm(x, final_norm_g)
```

```python
import functools
import math

import numpy as np
import jax
import jax.numpy as jnp
from jax import lax
from jax.experimental import pallas as pl
from jax.experimental.pallas import tpu as pltpu

f32 = jnp.float32
bf16 = jnp.bfloat16

N_HEADS = 8
HEAD_DIM = 64
ATTN_WIDTH = N_HEADS * HEAD_DIM
MOBA_BLOCK = 256
MOBA_TOPK = 3
CONV_K = 3
N_BUCKETS = 32
MAX_EXACT = N_BUCKETS // 2
MAX_DISTANCE = 128
N_GROUPS = 4
EXPERTS_PER_GROUP = 8
N_EXPERTS = N_GROUPS * EXPERTS_PER_GROUP
TOP_K_EXPERTS = 2
NORM_EPS = 1e-6
NEG_INF = -1e30

LANES = 128
ROW_TILE = 512
EXPERT_ROWS = 128
MOE_CHUNKS = 4
VMEM_LIMIT = 56 * 1024 * 1024
HIGHEST = lax.Precision.HIGHEST


def _params(sem, vmem=VMEM_LIMIT):
    return pltpu.CompilerParams(dimension_semantics=sem, vmem_limit_bytes=vmem)


def _mod_kernel(c_ref, w_ref, b_ref, o_ref):
    c = c_ref[...]
    o_ref[...] = jnp.dot(jax.nn.silu(c), w_ref[...], precision=HIGHEST,
                         preferred_element_type=f32) + b_ref[...]


def _mod(c, w_ada, b_ada):
    B, D = c.shape
    n_out = w_ada.shape[1]
    rows = 8
    cp = jnp.zeros((rows, D), f32).at[:B].set(c)
    tn = 1024
    out = pl.pallas_call(
        _mod_kernel,
        grid=(n_out // tn,),
        in_specs=[pl.BlockSpec((rows, D), lambda j: (0, 0)),
                  pl.BlockSpec((D, tn), lambda j: (0, j)),
                  pl.BlockSpec((1, tn), lambda j: (0, j))],
        out_specs=pl.BlockSpec((rows, tn), lambda j: (0, j)),
        out_shape=jax.ShapeDtypeStruct((rows, n_out), f32),
        compiler_params=_params(("arbitrary",)),
        name="mod",
    )(cp, w_ada, b_ada.reshape(1, n_out))
    return out[:B]


def _bucket_thresholds():
    n = np.arange(2 * MOBA_BLOCK)

    def buckets(ft):
        nf = np.maximum(n, 1).astype(ft)
        large = MAX_EXACT + (np.log(nf / ft(MAX_EXACT)) / ft(math.log(MAX_DISTANCE / MAX_EXACT))
                             * ft(N_BUCKETS - MAX_EXACT)).astype(np.int32)
        return np.where(n < MAX_EXACT, n, np.minimum(large, N_BUCKETS - 1))

    b = buckets(np.float32)
    assert np.array_equal(b, buckets(np.float64)) and np.all(np.diff(b) >= 0)
    assert b[-1] == N_BUCKETS - 1 and b[MOBA_BLOCK + 1] == N_BUCKETS - 1
    return [int(np.argmax(b >= j)) for j in range(N_BUCKETS)]


def _bias_kernel(rb_ref, own_ref, adj_ref, *, thresholds):
    h = pl.program_id(0)
    kk = lax.broadcasted_iota(jnp.int32, (MOBA_BLOCK, MOBA_BLOCK), 0)
    qq = lax.broadcasted_iota(jnp.int32, (MOBA_BLOCK, MOBA_BLOCK), 1)
    far = rb_ref[N_BUCKETS - 1, h]

    def table(dist):
        acc = jnp.full(dist.shape, rb_ref[0, h], f32)
        for j in range(1, N_BUCKETS):
            acc = jnp.where(dist >= thresholds[j], rb_ref[j, h], acc)
        return acc - far

    own_ref[0] = table(qq - kk)
    adj_ref[0] = table(qq - kk + MOBA_BLOCK)


def _bias_tiles(rel_bias):
    shp = jax.ShapeDtypeStruct((N_HEADS, MOBA_BLOCK, MOBA_BLOCK), f32)
    spec = pl.BlockSpec((1, MOBA_BLOCK, MOBA_BLOCK), lambda h: (h, 0, 0))
    return pl.pallas_call(
        functools.partial(_bias_kernel, thresholds=_bucket_thresholds()),
        grid=(N_HEADS,),
        in_specs=[pl.BlockSpec(memory_space=pltpu.SMEM)],
        out_specs=(spec, spec),
        out_shape=(shp, shp),
        compiler_params=_params(("arbitrary",)),
        name="bias",
    )(rel_bias)


def _ada_norm(x, g, scale, shift):
    y = x * lax.rsqrt(jnp.mean(x * x, axis=-1, keepdims=True) + NORM_EPS)
    return (y * g) * (1.0 + scale) + shift


def _inproj_kernel(x_ref, g_ref, sc_ref, sh_ref, wqT_ref, wk_ref, wvT_ref, wc_ref, wgl_ref,
                   wco_ref, cw_ref, bg_ref,
                   qT_ref, k_ref, vT_ref, km_ref, mc_ref, ga_ref, ubuf):
    tm = x_ref.shape[1]
    cwid = wco_ref.shape[0]
    d = x_ref.shape[2]
    s = pl.program_id(1)
    hb = _ada_norm(x_ref[0], g_ref[...], sc_ref[0], sh_ref[0]).astype(bf16)

    nt = (((1,), (1,)), ((), ()))
    qT_ref[0] = lax.dot_general(wqT_ref[...], hb, nt, preferred_element_type=f32).astype(bf16)
    k = jnp.dot(hb, wk_ref[...], preferred_element_type=f32)
    vT = lax.dot_general(wvT_ref[...], hb, nt, preferred_element_type=f32).astype(bf16)
    for j in range(tm // MOBA_BLOCK):
        rows = slice(j * MOBA_BLOCK, (j + 1) * MOBA_BLOCK)
        k_ref[0, j] = k[rows].astype(bf16)
        vT_ref[0, j] = vT[:, rows]
        km_ref[0, 0, j:j + 1, :] = jnp.mean(k[rows], axis=0, keepdims=True)

    cbx = jnp.dot(hb, wc_ref[...], preferred_element_type=f32)
    u = cbx[:, :cwid] * cbx[:, 2 * cwid:]

    @pl.when(s == 0)
    def _():
        ubuf[0:8, :] = jnp.zeros((8, cwid), f32)

    ubuf[8:8 + tm, :] = u
    cw = cw_ref[...]
    conv = cw[0:1] * ubuf[6:6 + tm, :] + cw[1:2] * ubuf[7:7 + tm, :] + cw[2:3] * u
    ubuf[0:8, :] = ubuf[tm:tm + 8, :]
    cv = (cbx[:, cwid:2 * cwid] * conv).astype(bf16)
    y_conv = jnp.dot(cv, wco_ref[...], preferred_element_type=f32)

    gl = jnp.dot(hb, wgl_ref[...], preferred_element_type=f32) + bg_ref[...]
    ga_ref[0] = jax.nn.sigmoid(gl[:, :d]).astype(bf16)
    mc_ref[0] = (jax.nn.sigmoid(gl[:, d:]) * y_conv).astype(bf16)


def _inproj(x, g, sc, sh, w_in, b_gate, conv_w, w_conv_out):
    B, S, D = x.shape
    tm = ROW_TILE
    nb = S // MOBA_BLOCK
    bpt = tm // MOBA_BLOCK
    aw = ATTN_WIDTH
    cwid = conv_w.shape[1]
    scale = HEAD_DIM ** -0.5
    wqT = (w_in[:, :aw] * scale).T.astype(bf16)
    wk = w_in[:, aw:2 * aw].astype(bf16)
    wvT = w_in[:, 2 * aw:3 * aw].T.astype(bf16)
    wc = w_in[:, 3 * aw:3 * aw + 3 * cwid].astype(bf16)
    wgl = w_in[:, 3 * aw + 3 * cwid:].astype(bf16)
    wco = w_conv_out.astype(bf16)

    const = lambda shape: pl.BlockSpec(shape, lambda b, s: (0,) * len(shape))
    vec = pl.BlockSpec((1, 1, D), lambda b, s: (b, 0, 0))
    row = lambda width: pl.BlockSpec((1, tm, width), lambda b, s: (b, s, 0))
    outs = pl.pallas_call(
        _inproj_kernel,
        grid=(B, S // tm),
        in_specs=[row(D), const((1, D)), vec, vec,
                  const((aw, D)), const((D, aw)), const((aw, D)), const((D, 3 * cwid)),
                  const((D, 2 * D)), const((cwid, D)), const((CONV_K, cwid)), const((1, 2 * D))],
        out_specs=(pl.BlockSpec((1, aw, tm), lambda b, s: (b, 0, s)),
                   pl.BlockSpec((1, bpt, MOBA_BLOCK, aw), lambda b, s: (b, s, 0, 0)),
                   pl.BlockSpec((1, bpt, aw, MOBA_BLOCK), lambda b, s: (b, s, 0, 0)),
                   pl.BlockSpec((1, 1, bpt, aw), lambda b, s: (b, s, 0, 0)),
                   row(D), row(D)),
        out_shape=(jax.ShapeDtypeStruct((B, aw, S), bf16),
                   jax.ShapeDtypeStruct((B, nb, MOBA_BLOCK, aw), bf16),
                   jax.ShapeDtypeStruct((B, nb, aw, MOBA_BLOCK), bf16),
                   jax.ShapeDtypeStruct((B, S // tm, bpt, aw), f32),
                   jax.ShapeDtypeStruct((B, S, D), bf16),
                   jax.ShapeDtypeStruct((B, S, D), bf16)),
        scratch_shapes=[pltpu.VMEM((tm + 8, cwid), f32)],
        compiler_params=_params(("parallel", "arbitrary")),
        name="inproj",
    )(x, g.reshape(1, D), sc.reshape(B, 1, D), sh.reshape(B, 1, D),
      wqT, wk, wvT, wc, wgl, wco, conv_w, b_gate.reshape(1, 2 * D))
    qT, k4, vT4, km, mc, ga = outs
    return qT, k4, vT4, km.reshape(B, nb, aw), mc, ga


def _attn_kernel(qT_ref, k_ref, vT_ref, km_ref, own_ref, adj_ref, o_ref):
    i = pl.program_id(2)
    bs = MOBA_BLOCK
    nb = km_ref.shape[1]
    qT = qT_ref[0]
    row = lax.broadcasted_iota(jnp.int32, qT.shape, 0)
    blk = lax.broadcasted_iota(jnp.int32, (nb, bs), 0)
    km = km_ref[0] * float(HEAD_DIM ** 0.5)

    qcat = []
    for a in range(2):
        qa = jnp.where((row >= a * HEAD_DIM) & (row < (a + 1) * HEAD_DIM), qT, jnp.zeros_like(qT))
        g = jnp.dot(km, qa.astype(f32), precision=HIGHEST, preferred_element_type=f32)
        g = jnp.where(blk < i, g, NEG_INF)
        fq = jnp.full((nb, bs), NEG_INF, f32)
        for _ in range(MOBA_TOPK):
            top = jnp.max(g, axis=0, keepdims=True)
            first = jnp.min(jnp.where(g == top, blk, nb), axis=0, keepdims=True)
            pick = blk == first
            fq = jnp.where(pick, 0.0, fq)
            g = jnp.where(pick, -jnp.inf, g)
        fq = jnp.where(blk < i, fq, 0.0).astype(bf16)
        pad = jnp.zeros((LANES - nb, bs), bf16)
        qcat.append(jnp.concatenate([qa, fq, pad], axis=0))

    lane = lax.broadcasted_iota(jnp.int32, (bs, LANES), 1)
    kk = lax.broadcasted_iota(jnp.int32, (bs, bs), 0)
    qq = lax.broadcasted_iota(jnp.int32, (bs, bs), 1)

    def scores(n, a):
        onehot = jnp.where(lane == n, 1.0, 0.0).astype(bf16)
        kcat = jnp.concatenate([k_ref[0, n], onehot], axis=1)
        return jnp.dot(kcat, qcat[a], preferred_element_type=f32)

    def values(n, a):
        return vT_ref[0, n, a * HEAD_DIM:(a + 1) * HEAD_DIM, :]

    def first_step(a):
        sT = jnp.where(kk <= qq, scores(i, a) + own_ref[a], NEG_INF)
        m = jnp.max(sT, axis=0, keepdims=True)
        p = jnp.exp(sT - m)
        l = jnp.sum(p, axis=0, keepdims=True)
        acc = jnp.dot(values(i, a), p.astype(bf16), preferred_element_type=f32)
        return m, l, acc

    def next_step(n, a, sT, state):
        m, l, acc = state
        m_new = jnp.maximum(m, jnp.max(sT, axis=0, keepdims=True))
        alpha = jnp.exp(m - m_new)
        p = jnp.exp(sT - m_new)
        l = alpha * l + jnp.sum(p, axis=0, keepdims=True)
        acc = alpha * acc + jnp.dot(values(n, a), p.astype(bf16), preferred_element_type=f32)
        return m_new, l, acc

    state = (first_step(0), first_step(1))

    def adjacent(st):
        n = i - 1
        return tuple(next_step(n, a, scores(n, a) + adj_ref[a], st[a]) for a in range(2))

    state = lax.cond(i >= 1, adjacent, lambda st: st, state)

    def far(n, st):
        return tuple(next_step(n, a, scores(n, a), st[a]) for a in range(2))

    state = lax.fori_loop(0, jnp.maximum(i - 1, 0), far, state)
    outT = jnp.concatenate([state[a][2] / state[a][1] for a in range(2)], axis=0)
    o_ref[0] = outT.T.astype(bf16)


def _attention(qT, k4, vT4, km, own, adj):
    B, aw, S = qT.shape
    nb = S // MOBA_BLOCK
    assert MOBA_TOPK < nb <= LANES
    pairs = aw // LANES
    return pl.pallas_call(
        _attn_kernel,
        grid=(B, pairs, nb),
        in_specs=[pl.BlockSpec((1, LANES, MOBA_BLOCK), lambda b, j, i: (b, j, i)),
                  pl.BlockSpec((1, nb, MOBA_BLOCK, LANES), lambda b, j, i: (b, 0, 0, j)),
                  pl.BlockSpec((1, nb, LANES, MOBA_BLOCK), lambda b, j, i: (b, 0, j, 0)),
                  pl.BlockSpec((1, nb, LANES), lambda b, j, i: (b, 0, j)),
                  pl.BlockSpec((2, MOBA_BLOCK, MOBA_BLOCK), lambda b, j, i: (j, 0, 0)),
                  pl.BlockSpec((2, MOBA_BLOCK, MOBA_BLOCK), lambda b, j, i: (j, 0, 0))],
        out_specs=pl.BlockSpec((1, MOBA_BLOCK, LANES), lambda b, j, i: (b, i, j)),
        out_shape=jax.ShapeDtypeStruct((B, S, aw), bf16),
        compiler_params=_params(("parallel", "parallel", "arbitrary")),
        name="attn",
    )(qT, k4, vT4, km, own, adj)


def _post_kernel(a_ref, mc_ref, ga_ref, x_ref, g1_ref, sc_ref, sh_ref, ng_ref,
                 wao_ref, wo_ref, wr_ref, br_ref, x1_ref, h2p_ref, route_ref):
    tm = x_ref.shape[1]
    d = x_ref.shape[2]
    ya = jnp.dot(a_ref[0], wao_ref[...], preferred_element_type=f32)
    merged = ga_ref[0].astype(f32) * ya + mc_ref[0].astype(f32)
    z = jnp.dot(merged.astype(bf16), wo_ref[...], preferred_element_type=f32)
    x1 = x_ref[0] + g1_ref[0] * z
    x1_ref[0] = x1
    h2 = _ada_norm(x1, ng_ref[...], sc_ref[0], sh_ref[0])

    bits = pltpu.bitcast(h2.astype(bf16).astype(f32), jnp.uint32)
    h2p_ref[...] = (bits[:, :d // 2] >> 16) | (bits[:, d // 2:] & jnp.uint32(0xFFFF0000))

    logits = jnp.dot(h2, wr_ref[...], precision=HIGHEST, preferred_element_type=f32) + br_ref[...]
    lane = lax.broadcasted_iota(jnp.int32, (tm, LANES), 1)
    ninf = -jnp.inf
    gl = jnp.where(lane < N_GROUPS, logits, ninf)
    gmax = jnp.max(gl, axis=1, keepdims=True)
    g_sel = jnp.min(jnp.where(gl == gmax, lane, LANES), axis=1, keepdims=True)
    g_w = 1.0 / jnp.sum(jnp.exp(gl - gmax), axis=1, keepdims=True)
    off = lane - (N_GROUPS + EXPERTS_PER_GROUP * g_sel)
    in_group = (off >= 0) & (off < EXPERTS_PER_GROUP)
    el = jnp.where(in_group, logits, ninf)
    ex = jnp.exp(el - jnp.max(el, axis=1, keepdims=True))
    ep = jnp.where(in_group, ex / jnp.sum(ex, axis=1, keepdims=True), -1.0)
    p1 = jnp.max(ep, axis=1, keepdims=True)
    i1 = jnp.min(jnp.where(ep == p1, lane, LANES), axis=1, keepdims=True)
    ep2 = jnp.where(lane == i1, -1.0, ep)
    p2 = jnp.max(ep2, axis=1, keepdims=True)
    i2 = jnp.min(jnp.where(ep2 == p2, lane, LANES), axis=1, keepdims=True)
    tot = p1 + p2
    w1 = p1 / tot * g_w
    w2 = p2 / tot * g_w
    e1 = (i1 - N_GROUPS).astype(f32)
    e2 = (i2 - N_GROUPS).astype(f32)
    route_ref[...] = jnp.where(lane == 0, e1, jnp.where(lane == 1, e2,
                               jnp.where(lane == 2, w1, jnp.where(lane == 3, w2, 0.0))))


def _post(a, mc, ga, x, g1, sc2, sh2, ng, w_attn_out, w_o, w_rg, b_rg, w_re, b_re):
    B, S, D = x.shape
    tm = ROW_TILE
    aw = a.shape[2]
    nt = S // tm
    wr = jnp.zeros((D, LANES), f32).at[:, :N_GROUPS].set(w_rg).at[:, N_GROUPS:N_GROUPS + N_EXPERTS].set(w_re)
    br = jnp.zeros((1, LANES), f32).at[0, :N_GROUPS].set(b_rg).at[0, N_GROUPS:N_GROUPS + N_EXPERTS].set(b_re)
    const = lambda shape: pl.BlockSpec(shape, lambda b, s: (0,) * len(shape))
    vec = pl.BlockSpec((1, 1, D), lambda b, s: (b, 0, 0))
    row = lambda width: pl.BlockSpec((1, tm, width), lambda b, s: (b, s, 0))
    flat = lambda width: pl.BlockSpec((tm, width), lambda b, s: (b * nt + s, 0))
    return pl.pallas_call(
        _post_kernel,
        grid=(B, nt),
        in_specs=[row(aw), row(D), row(D), row(D), vec, vec, vec, const((1, D)),
                  const((aw, D)), const((D, D)), const((D, LANES)), const((1, LANES))],
        out_specs=(row(D), flat(D // 2), flat(LANES)),
        out_shape=(jax.ShapeDtypeStruct((B, S, D), f32),
                   jax.ShapeDtypeStruct((B * S, D // 2), jnp.uint32),
                   jax.ShapeDtypeStruct((B * S, LANES), f32)),
        compiler_params=_params(("parallel", "parallel")),
        name="post",
    )(a, mc, ga, x, g1.reshape(B, 1, D), sc2.reshape(B, 1, D), sh2.reshape(B, 1, D), ng.reshape(1, D),
      w_attn_out.astype(bf16), w_o.astype(bf16), wr, br)


def _dispatch_tables(route, n_chunks):
    T = route.shape[0]
    tc = T // n_chunks
    na = tc * TOP_K_EXPERTS
    e = route[:, :TOP_K_EXPERTS].astype(jnp.int32).reshape(n_chunks, na)
    w = route[:, TOP_K_EXPERTS:2 * TOP_K_EXPERTS].reshape(n_chunks, na)
    order = jnp.argsort(e, axis=1, stable=True).astype(jnp.int32)
    stok = order // TOP_K_EXPERTS
    sw = jnp.take_along_axis(w, order, axis=1)
    counts = jnp.sum((e[:, :, None] == jnp.arange(N_EXPERTS)[None, None, :]).astype(jnp.int32), axis=1)
    starts = jnp.cumsum(counts, axis=1) - counts
    nblk = (counts + EXPERT_ROWS - 1) // EXPERT_ROWS
    bend = jnp.cumsum(nblk, axis=1)
    bstart = bend - nblk
    total = bend[:, -1:]
    n_blocks = na // EXPERT_ROWS + N_EXPERTS
    bidx = jnp.arange(n_blocks, dtype=jnp.int32)[None, :]
    be = jnp.sum((bidx[:, :, None] >= bend[:, None, :]).astype(jnp.int32), axis=2)
    be = jnp.minimum(be, N_EXPERTS - 1)
    valid = bidx < total
    be = jnp.where(valid, be, jnp.take_along_axis(be, jnp.maximum(total - 1, 0), axis=1))
    local = bidx - jnp.take_along_axis(bstart, be, axis=1)
    bs = jnp.take_along_axis(starts, be, axis=1) + EXPERT_ROWS * local
    bn = jnp.clip(jnp.take_along_axis(counts, be, axis=1) - EXPERT_ROWS * local, 0, EXPERT_ROWS)
    bn = jnp.where(valid, bn, 0)
    bs = jnp.where(valid, bs, 0)
    flat = lambda t: t.reshape(-1).astype(jnp.int32)
    return flat(be), flat(bs), flat(bn), stok.reshape(n_chunks, 1, na), sw.reshape(n_chunks, 1, na), n_blocks


def _moe_kernel(be_ref, bs_ref, bn_ref, stok_ref, sw_ref, h2p_ref, wg_ref, wu_ref, wd_ref,
                y_hbm, yacc, xs, outs, sem):
    c = pl.program_id(0)
    b = pl.program_id(1)
    nbk = pl.num_programs(1)
    tc = yacc.shape[0]
    half = xs.shape[1]
    n = bn_ref[c * nbk + b]
    st = bs_ref[c * nbk + b]

    @pl.when(b == 0)
    def _():
        yacc[...] = jnp.zeros(yacc.shape, f32)

    @pl.when((b == 0) & (c == 0))
    def _():
        xs[...] = jnp.zeros(xs.shape, jnp.uint32)

    @pl.when(n > 0)
    def _():
        def gather(r, carry):
            tok = stok_ref[0, 0, st + r]
            xs[pl.ds(r, 1), :] = h2p_ref[pl.ds(tok, 1), :]
            return carry

        lax.fori_loop(0, n, gather, 0)
        packed = xs[...]
        x_lo = pltpu.bitcast(packed << 16, f32).astype(bf16)
        x_hi = pltpu.bitcast(packed & jnp.uint32(0xFFFF0000), f32).astype(bf16)

        def up(w_ref):
            w = w_ref[0].astype(bf16)
            return (jnp.dot(x_lo, w[:half], preferred_element_type=f32)
                    + jnp.dot(x_hi, w[half:], preferred_element_type=f32))

        hidden = (jax.nn.silu(up(wg_ref)) * up(wu_ref)).astype(bf16)
        outs[...] = jnp.dot(hidden, wd_ref[0].astype(bf16), preferred_element_type=f32)

        def scatter(r, carry):
            tok = stok_ref[0, 0, st + r]
            wgt = sw_ref[0, 0, st + r]
            yacc[pl.ds(tok, 1), :] = yacc[pl.ds(tok, 1), :] + wgt * outs[pl.ds(r, 1), :]
            return carry

        lax.fori_loop(0, n, scatter, 0)

    @pl.when(b == nbk - 1)
    def _():
        copy = pltpu.make_async_copy(yacc, y_hbm.at[pl.ds(c * tc, tc), :], sem)
        copy.start()
        copy.wait()


def _moe(route, h2p, w_gate, w_up, w_down):
    T, half = h2p.shape
    D = 2 * half
    de = w_gate.shape[2]
    n_chunks = MOE_CHUNKS
    tc = T // n_chunks
    be, bs, bn, stok, sw, n_blocks = _dispatch_tables(route, n_chunks)
    na = stok.shape[2]
    wmap = lambda c, b, be_r, bs_r, bn_r: (be_r[c * n_blocks + b], 0, 0)
    smem = lambda: pl.BlockSpec((1, 1, na), lambda c, b, *_: (c, 0, 0), memory_space=pltpu.SMEM)
    grid_spec = pltpu.PrefetchScalarGridSpec(
        num_scalar_prefetch=3,
        grid=(n_chunks, n_blocks),
        in_specs=[smem(), smem(),
                  pl.BlockSpec((tc, half), lambda c, b, *_: (c, 0)),
                  pl.BlockSpec((1, D, de), wmap),
                  pl.BlockSpec((1, D, de), wmap),
                  pl.BlockSpec((1, de, D), wmap)],
        out_specs=pl.BlockSpec(memory_space=pl.ANY),
        scratch_shapes=[pltpu.VMEM((tc, D), f32),
                        pltpu.VMEM((EXPERT_ROWS, half), jnp.uint32),
                        pltpu.VMEM((EXPERT_ROWS, D), f32),
                        pltpu.SemaphoreType.DMA(())],
    )
    return pl.pallas_call(
        _moe_kernel,
        grid_spec=grid_spec,
        out_shape=jax.ShapeDtypeStruct((T, D), f32),
        compiler_params=_params(("arbitrary", "arbitrary")),
        name="moe",
    )(be, bs, bn, stok, sw, h2p, w_gate, w_up, w_down)


def _final_kernel(x1_ref, y_ref, g2_ref, fg_ref, o_ref, *, normalize):
    x2 = x1_ref[0] + g2_ref[0] * y_ref[0]
    if normalize:
        x2 = (x2 * lax.rsqrt(jnp.mean(x2 * x2, axis=-1, keepdims=True) + NORM_EPS)) * fg_ref[...]
    o_ref[0] = x2


def _residual(x1, y, g2, fg, normalize):
    B, S, D = x1.shape
    tm = ROW_TILE
    row = pl.BlockSpec((1, tm, D), lambda b, s: (b, s, 0))
    return pl.pallas_call(
        functools.partial(_final_kernel, normalize=normalize),
        grid=(B, S // tm),
        in_specs=[row, row, pl.BlockSpec((1, 1, D), lambda b, s: (b, 0, 0)),
                  pl.BlockSpec((1, D), lambda b, s: (0, 0))],
        out_specs=row,
        out_shape=jax.ShapeDtypeStruct((B, S, D), f32),
        compiler_params=_params(("parallel", "parallel")),
        name="final",
    )(x1, y.reshape(B, S, D), g2.reshape(B, 1, D), fg.reshape(1, D))


def kernel(x, c, rel_bias, norm1_g, norm2_g, w_ada, b_ada, w_in, b_gate, conv_w, w_attn_out,
           w_conv_out, w_o, w_router_group, b_router_group, w_router_expert, b_router_expert,
           w_exp_gate, w_exp_up, w_exp_down, final_norm_g):
    B, S, D = x.shape
    depth = w_ada.shape[0]
    assert S % ROW_TILE == 0 and ROW_TILE % MOBA_BLOCK == 0 and (B * S) % (MOE_CHUNKS * EXPERT_ROWS) == 0
    own, adj = _bias_tiles(rel_bias)
    for l in range(depth):
        mod = _mod(c, w_ada[l], b_ada[l])
        sh1, sc1, g1, sh2, sc2, g2 = jnp.split(mod, 6, axis=-1)
        qT, k4, vT4, km, mc, ga = _inproj(x, norm1_g[l], sc1, sh1, w_in[l], b_gate[l], conv_w[l],
                                          w_conv_out[l])
        a = _attention(qT, k4, vT4, km, own, adj)
        x1, h2p, route = _post(a, mc, ga, x, g1, sc2, sh2, norm2_g[l], w_attn_out[l], w_o[l],
                               w_router_group[l], b_router_group[l], w_router_expert[l],
                               b_router_expert[l])
        y = _moe(route, h2p, w_exp_gate[l], w_exp_up[l], w_exp_down[l])
        x = _residual(x1, y, g2, final_norm_g, normalize=(l + 1 == depth))
    return x
```

```python
import functools
import math

import numpy as np
import jax
import jax.numpy as jnp
from jax import lax
from jax.experimental import pallas as pl
from jax.experimental.pallas import tpu as pltpu

f32 = jnp.float32
bf16 = jnp.bfloat16

N_HEADS = 8
HEAD_DIM = 64
ATTN_WIDTH = N_HEADS * HEAD_DIM
MOBA_BLOCK = 256
MOBA_TOPK = 3
CONV_K = 3
N_BUCKETS = 32
MAX_EXACT = N_BUCKETS // 2
MAX_DISTANCE = 128
N_GROUPS = 4
EXPERTS_PER_GROUP = 8
N_EXPERTS = N_GROUPS * EXPERTS_PER_GROUP
TOP_K_EXPERTS = 2
NORM_EPS = 1e-6
NEG_INF = -1e30

LANES = 128
ROW_TILE = 512
EXPERT_ROWS = 128
MOE_CHUNKS = 4
FAR_BLOCKS_PER_ITER = 2
VMEM_LIMIT = 56 * 1024 * 1024
HIGHEST = lax.Precision.HIGHEST


def _params(sem, vmem=VMEM_LIMIT):
    return pltpu.CompilerParams(dimension_semantics=sem, vmem_limit_bytes=vmem)


def _mod_kernel(c_ref, w_ref, b_ref, o_ref):
    c = c_ref[...]
    o_ref[...] = jnp.dot(jax.nn.silu(c), w_ref[...], precision=HIGHEST,
                         preferred_element_type=f32) + b_ref[...]


def _mod(c, w_ada, b_ada):
    B, D = c.shape
    n_out = w_ada.shape[1]
    rows = 8
    cp = jnp.zeros((rows, D), f32).at[:B].set(c)
    tn = 1024
    out = pl.pallas_call(
        _mod_kernel,
        grid=(n_out // tn,),
        in_specs=[pl.BlockSpec((rows, D), lambda j: (0, 0)),
                  pl.BlockSpec((D, tn), lambda j: (0, j)),
                  pl.BlockSpec((1, tn), lambda j: (0, j))],
        out_specs=pl.BlockSpec((rows, tn), lambda j: (0, j)),
        out_shape=jax.ShapeDtypeStruct((rows, n_out), f32),
        compiler_params=_params(("arbitrary",)),
        name="mod",
    )(cp, w_ada, b_ada.reshape(1, n_out))
    return out[:B]


def _bucket_thresholds():
    n = np.arange(2 * MOBA_BLOCK)

    def buckets(ft):
        nf = np.maximum(n, 1).astype(ft)
        large = MAX_EXACT + (np.log(nf / ft(MAX_EXACT)) / ft(math.log(MAX_DISTANCE / MAX_EXACT))
                             * ft(N_BUCKETS - MAX_EXACT)).astype(np.int32)
        return np.where(n < MAX_EXACT, n, np.minimum(large, N_BUCKETS - 1))

    b = buckets(np.float32)
    assert np.array_equal(b, buckets(np.float64)) and np.all(np.diff(b) >= 0)
    assert b[-1] == N_BUCKETS - 1 and b[MOBA_BLOCK + 1] == N_BUCKETS - 1
    return [int(np.argmax(b >= j)) for j in range(N_BUCKETS)]


def _bias_kernel(rb_ref, own_ref, adj_ref, *, thresholds):
    h = pl.program_id(0)
    kk = lax.broadcasted_iota(jnp.int32, (MOBA_BLOCK, MOBA_BLOCK), 0)
    qq = lax.broadcasted_iota(jnp.int32, (MOBA_BLOCK, MOBA_BLOCK), 1)
    far = rb_ref[N_BUCKETS - 1, h]

    def table(dist):
        acc = jnp.full(dist.shape, rb_ref[0, h], f32)
        for j in range(1, N_BUCKETS):
            acc = jnp.where(dist >= thresholds[j], rb_ref[j, h], acc)
        return acc - far

    own_ref[0] = table(qq - kk)
    adj_ref[0] = table(qq - kk + MOBA_BLOCK)


def _bias_tiles(rel_bias):
    shp = jax.ShapeDtypeStruct((N_HEADS, MOBA_BLOCK, MOBA_BLOCK), f32)
    spec = pl.BlockSpec((1, MOBA_BLOCK, MOBA_BLOCK), lambda h: (h, 0, 0))
    return pl.pallas_call(
        functools.partial(_bias_kernel, thresholds=_bucket_thresholds()),
        grid=(N_HEADS,),
        in_specs=[pl.BlockSpec(memory_space=pltpu.SMEM)],
        out_specs=(spec, spec),
        out_shape=(shp, shp),
        compiler_params=_params(("arbitrary",)),
        name="bias",
    )(rel_bias)


def _ada_norm(x, g, scale, shift):
    y = x * lax.rsqrt(jnp.mean(x * x, axis=-1, keepdims=True) + NORM_EPS)
    return (y * g) * (1.0 + scale) + shift


def _inproj_kernel(x_ref, g_ref, sc_ref, sh_ref, wqT_ref, wk_ref, wvT_ref, wc_ref, wgl_ref,
                   wco_ref, cw_ref, bg_ref,
                   qT_ref, k_ref, vT_ref, km_ref, mc_ref, ga_ref, ubuf):
    tm = x_ref.shape[1]
    cwid = wco_ref.shape[0]
    d = x_ref.shape[2]
    s = pl.program_id(1)
    hb = _ada_norm(x_ref[0], g_ref[...], sc_ref[0], sh_ref[0]).astype(bf16)

    nt = (((1,), (1,)), ((), ()))
    qT_ref[0] = lax.dot_general(wqT_ref[...], hb, nt, preferred_element_type=f32).astype(bf16)
    k = jnp.dot(hb, wk_ref[...], preferred_element_type=f32)
    vT = lax.dot_general(wvT_ref[...], hb, nt, preferred_element_type=f32).astype(bf16)
    for j in range(tm // MOBA_BLOCK):
        rows = slice(j * MOBA_BLOCK, (j + 1) * MOBA_BLOCK)
        k_ref[0, j] = k[rows].astype(bf16)
        vT_ref[0, j] = vT[:, rows]
        km_ref[0, 0, j:j + 1, :] = jnp.mean(k[rows], axis=0, keepdims=True)

    cbx = jnp.dot(hb, wc_ref[...], preferred_element_type=f32)
    u = cbx[:, :cwid] * cbx[:, 2 * cwid:]

    @pl.when(s == 0)
    def _():
        ubuf[0:8, :] = jnp.zeros((8, cwid), f32)

    ubuf[8:8 + tm, :] = u
    cw = cw_ref[...]
    conv = cw[0:1] * ubuf[6:6 + tm, :] + cw[1:2] * ubuf[7:7 + tm, :] + cw[2:3] * u
    ubuf[0:8, :] = ubuf[tm:tm + 8, :]
    cv = (cbx[:, cwid:2 * cwid] * conv).astype(bf16)
    y_conv = jnp.dot(cv, wco_ref[...], preferred_element_type=f32)

    gl = jnp.dot(hb, wgl_ref[...], preferred_element_type=f32) + bg_ref[...]
    ga_ref[0] = jax.nn.sigmoid(gl[:, :d]).astype(bf16)
    mc_ref[0] = (jax.nn.sigmoid(gl[:, d:]) * y_conv).astype(bf16)


def _inproj(x, g, sc, sh, w_in, b_gate, conv_w, w_conv_out):
    B, S, D = x.shape
    tm = ROW_TILE
    nb = S // MOBA_BLOCK
    bpt = tm // MOBA_BLOCK
    aw = ATTN_WIDTH
    cwid = conv_w.shape[1]
    scale = HEAD_DIM ** -0.5
    wqT = (w_in[:, :aw] * scale).T.astype(bf16)
    wk = w_in[:, aw:2 * aw].astype(bf16)
    wvT = w_in[:, 2 * aw:3 * aw].T.astype(bf16)
    wc = w_in[:, 3 * aw:3 * aw + 3 * cwid].astype(bf16)
    wgl = w_in[:, 3 * aw + 3 * cwid:].astype(bf16)
    wco = w_conv_out.astype(bf16)

    const = lambda shape: pl.BlockSpec(shape, lambda b, s: (0,) * len(shape))
    vec = pl.BlockSpec((1, 1, D), lambda b, s: (b, 0, 0))
    row = lambda width: pl.BlockSpec((1, tm, width), lambda b, s: (b, s, 0))
    outs = pl.pallas_call(
        _inproj_kernel,
        grid=(B, S // tm),
        in_specs=[row(D), const((1, D)), vec, vec,
                  const((aw, D)), const((D, aw)), const((aw, D)), const((D, 3 * cwid)),
                  const((D, 2 * D)), const((cwid, D)), const((CONV_K, cwid)), const((1, 2 * D))],
        out_specs=(pl.BlockSpec((1, aw, tm), lambda b, s: (b, 0, s)),
                   pl.BlockSpec((1, bpt, MOBA_BLOCK, aw), lambda b, s: (b, s, 0, 0)),
                   pl.BlockSpec((1, bpt, aw, MOBA_BLOCK), lambda b, s: (b, s, 0, 0)),
                   pl.BlockSpec((1, 1, bpt, aw), lambda b, s: (b, s, 0, 0)),
                   row(D), row(D)),
        out_shape=(jax.ShapeDtypeStruct((B, aw, S), bf16),
                   jax.ShapeDtypeStruct((B, nb, MOBA_BLOCK, aw), bf16),
                   jax.ShapeDtypeStruct((B, nb, aw, MOBA_BLOCK), bf16),
                   jax.ShapeDtypeStruct((B, S // tm, bpt, aw), f32),
                   jax.ShapeDtypeStruct((B, S, D), bf16),
                   jax.ShapeDtypeStruct((B, S, D), bf16)),
        scratch_shapes=[pltpu.VMEM((tm + 8, cwid), f32)],
        compiler_params=_params(("parallel", "arbitrary")),
        name="inproj",
    )(x, g.reshape(1, D), sc.reshape(B, 1, D), sh.reshape(B, 1, D),
      wqT, wk, wvT, wc, wgl, wco, conv_w, b_gate.reshape(1, 2 * D))
    qT, k4, vT4, km, mc, ga = outs
    return qT, k4, vT4, km.reshape(B, nb, aw), mc, ga


def _attn_kernel(qT_ref, k_ref, vT_ref, km_ref, own_ref, adj_ref, o_ref):
    i = pl.program_id(1)
    bs = MOBA_BLOCK
    nb = km_ref.shape[1]
    n_heads = qT_ref.shape[1] // HEAD_DIM
    heads = range(n_heads)
    blk = lax.broadcasted_iota(jnp.int32, (nb, bs), 0)

    qcat = []
    for h in heads:
        pair, a = divmod(h, LANES // HEAD_DIM)
        qT = qT_ref[0, pair * LANES:(pair + 1) * LANES, :]
        row = lax.broadcasted_iota(jnp.int32, qT.shape, 0)
        km = km_ref[0, :, pair * LANES:(pair + 1) * LANES] * float(HEAD_DIM ** 0.5)
        qa = jnp.where((row >= a * HEAD_DIM) & (row < (a + 1) * HEAD_DIM), qT, jnp.zeros_like(qT))
        g = jnp.dot(km, qa.astype(f32), precision=HIGHEST, preferred_element_type=f32)
        g = jnp.where(blk < i, g, NEG_INF)
        fq = jnp.full((nb, bs), NEG_INF, f32)
        for _ in range(MOBA_TOPK):
            top = jnp.max(g, axis=0, keepdims=True)
            first = jnp.min(jnp.where(g == top, blk, nb), axis=0, keepdims=True)
            pick = blk == first
            fq = jnp.where(pick, 0.0, fq)
            g = jnp.where(pick, -jnp.inf, g)
        fq = jnp.where(blk < i, fq, 0.0).astype(bf16)
        pad = jnp.zeros((LANES - nb, bs), bf16)
        qcat.append(jnp.concatenate([qa, fq, pad], axis=0))

    lane = lax.broadcasted_iota(jnp.int32, (bs, LANES), 1)
    kk = lax.broadcasted_iota(jnp.int32, (bs, bs), 0)
    qq = lax.broadcasted_iota(jnp.int32, (bs, bs), 1)

    def all_scores(ns):
        onehots = [jnp.where(lane == n, 1.0, 0.0).astype(bf16) for n in ns]
        out = []
        for h in heads:
            pair = h // (LANES // HEAD_DIM)
            cols = slice(pair * LANES, (pair + 1) * LANES)
            out.append([jnp.dot(jnp.concatenate([k_ref[0, n, :, cols], onehot], axis=1), qcat[h],
                                preferred_element_type=f32) for n, onehot in zip(ns, onehots)])
        return out

    def values(n, h):
        return vT_ref[0, n, h * HEAD_DIM:(h + 1) * HEAD_DIM, :]

    def first_step(h, sT):
        sT = jnp.where(kk <= qq, sT + own_ref[h], NEG_INF)
        m = jnp.max(sT, axis=0, keepdims=True)
        p = jnp.exp(sT - m)
        l = jnp.sum(p, axis=0, keepdims=True)
        acc = jnp.dot(values(i, h), p.astype(bf16), preferred_element_type=f32)
        return m, l, acc

    def next_step(ns, h, sTs, state):
        m, l, acc = state
        m_new = m
        for sT in sTs:
            m_new = jnp.maximum(m_new, jnp.max(sT, axis=0, keepdims=True))
        alpha = jnp.exp(m - m_new)
        l = alpha * l
        acc = alpha * acc
        for n, sT in zip(ns, sTs):
            p = jnp.exp(sT - m_new)
            l = l + jnp.sum(p, axis=0, keepdims=True)
            acc = acc + jnp.dot(values(n, h), p.astype(bf16), preferred_element_type=f32)
        return m_new, l, acc

    def steps(ns, st, bias_ref=None):
        per_head = all_scores(ns)
        out = []
        for h in heads:
            sTs = per_head[h]
            if bias_ref is not None:
                sTs = [sT + bias_ref[h] for sT in sTs]
            out.append(next_step(ns, h, sTs, st[h]))
        return tuple(out)

    own_scores = all_scores([i])
    state = tuple(first_step(h, own_scores[h][0]) for h in heads)
    state = lax.cond(i >= 1, lambda st: steps([i - 1], st, adj_ref), lambda st: st, state)
    n_far = jnp.maximum(i - 1, 0)
    state = lax.fori_loop(0, n_far // FAR_BLOCKS_PER_ITER,
                          lambda it, st: steps([it * FAR_BLOCKS_PER_ITER + j for j in range(FAR_BLOCKS_PER_ITER)], st),
                          state)
    for r in range(FAR_BLOCKS_PER_ITER - 1):
        last = n_far - n_far % FAR_BLOCKS_PER_ITER + r
        state = lax.cond(n_far % FAR_BLOCKS_PER_ITER > r, lambda st, n=last: steps([n], st), lambda st: st, state)
    outT = jnp.concatenate([acc / l for (_, l, acc) in state], axis=0)
    o_ref[0] = outT.T.astype(bf16)


def _attention(qT, k4, vT4, km, own, adj):
    B, aw, S = qT.shape
    nb = S // MOBA_BLOCK
    assert MOBA_TOPK < nb <= LANES
    n_heads = aw // HEAD_DIM
    whole = lambda shape: pl.BlockSpec(shape, lambda b, i: (b,) + (0,) * (len(shape) - 1))
    table = pl.BlockSpec((n_heads, MOBA_BLOCK, MOBA_BLOCK), lambda b, i: (0, 0, 0))
    return pl.pallas_call(
        _attn_kernel,
        grid=(B, nb),
        in_specs=[pl.BlockSpec((1, aw, MOBA_BLOCK), lambda b, i: (b, 0, i)),
                  whole((1, nb, MOBA_BLOCK, aw)), whole((1, nb, aw, MOBA_BLOCK)), whole((1, nb, aw)),
                  table, table],
        out_specs=pl.BlockSpec((1, MOBA_BLOCK, aw), lambda b, i: (b, i, 0)),
        out_shape=jax.ShapeDtypeStruct((B, S, aw), bf16),
        compiler_params=_params(("parallel", "arbitrary")),
        name="attn",
    )(qT, k4, vT4, km, own, adj)


def _post_kernel(a_ref, mc_ref, ga_ref, x_ref, g1_ref, sc_ref, sh_ref, ng_ref,
                 wao_ref, wo_ref, wr_ref, br_ref, x1_ref, h2p_ref, route_ref):
    tm = x_ref.shape[1]
    d = x_ref.shape[2]
    ya = jnp.dot(a_ref[0], wao_ref[...], preferred_element_type=f32)
    merged = ga_ref[0].astype(f32) * ya + mc_ref[0].astype(f32)
    z = jnp.dot(merged.astype(bf16), wo_ref[...], preferred_element_type=f32)
    x1 = x_ref[0] + g1_ref[0] * z
    x1_ref[0] = x1
    h2 = _ada_norm(x1, ng_ref[...], sc_ref[0], sh_ref[0])

    bits = pltpu.bitcast(h2.astype(bf16).astype(f32), jnp.uint32)
    h2p_ref[...] = (bits[:, :d // 2] >> 16) | (bits[:, d // 2:] & jnp.uint32(0xFFFF0000))

    logits = jnp.dot(h2, wr_ref[...], precision=HIGHEST, preferred_element_type=f32) + br_ref[...]
    lane = lax.broadcasted_iota(jnp.int32, (tm, LANES), 1)
    ninf = -jnp.inf
    gl = jnp.where(lane < N_GROUPS, logits, ninf)
    gmax = jnp.max(gl, axis=1, keepdims=True)
    g_sel = jnp.min(jnp.where(gl == gmax, lane, LANES), axis=1, keepdims=True)
    g_w = 1.0 / jnp.sum(jnp.exp(gl - gmax), axis=1, keepdims=True)
    off = lane - (N_GROUPS + EXPERTS_PER_GROUP * g_sel)
    in_group = (off >= 0) & (off < EXPERTS_PER_GROUP)
    el = jnp.where(in_group, logits, ninf)
    ex = jnp.exp(el - jnp.max(el, axis=1, keepdims=True))
    ep = jnp.where(in_group, ex / jnp.sum(ex, axis=1, keepdims=True), -1.0)
    p1 = jnp.max(ep, axis=1, keepdims=True)
    i1 = jnp.min(jnp.where(ep == p1, lane, LANES), axis=1, keepdims=True)
    ep2 = jnp.where(lane == i1, -1.0, ep)
    p2 = jnp.max(ep2, axis=1, keepdims=True)
    i2 = jnp.min(jnp.where(ep2 == p2, lane, LANES), axis=1, keepdims=True)
    tot = p1 + p2
    w1 = p1 / tot * g_w
    w2 = p2 / tot * g_w
    e1 = (i1 - N_GROUPS).astype(f32)
    e2 = (i2 - N_GROUPS).astype(f32)
    route_ref[...] = jnp.where(lane == 0, e1, jnp.where(lane == 1, e2,
                               jnp.where(lane == 2, w1, jnp.where(lane == 3, w2, 0.0))))


def _post(a, mc, ga, x, g1, sc2, sh2, ng, w_attn_out, w_o, w_rg, b_rg, w_re, b_re):
    B, S, D = x.shape
    tm = ROW_TILE
    aw = a.shape[2]
    nt = S // tm
    wr = jnp.zeros((D, LANES), f32).at[:, :N_GROUPS].set(w_rg).at[:, N_GROUPS:N_GROUPS + N_EXPERTS].set(w_re)
    br = jnp.zeros((1, LANES), f32).at[0, :N_GROUPS].set(b_rg).at[0, N_GROUPS:N_GROUPS + N_EXPERTS].set(b_re)
    const = lambda shape: pl.BlockSpec(shape, lambda b, s: (0,) * len(shape))
    vec = pl.BlockSpec((1, 1, D), lambda b, s: (b, 0, 0))
    row = lambda width: pl.BlockSpec((1, tm, width), lambda b, s: (b, s, 0))
    flat = lambda width: pl.BlockSpec((tm, width), lambda b, s: (b * nt + s, 0))
    return pl.pallas_call(
        _post_kernel,
        grid=(B, nt),
        in_specs=[row(aw), row(D), row(D), row(D), vec, vec, vec, const((1, D)),
                  const((aw, D)), const((D, D)), const((D, LANES)), const((1, LANES))],
        out_specs=(row(D), flat(D // 2), flat(LANES)),
        out_shape=(jax.ShapeDtypeStruct((B, S, D), f32),
                   jax.ShapeDtypeStruct((B * S, D // 2), jnp.uint32),
                   jax.ShapeDtypeStruct((B * S, LANES), f32)),
        compiler_params=_params(("parallel", "parallel")),
        name="post",
    )(a, mc, ga, x, g1.reshape(B, 1, D), sc2.reshape(B, 1, D), sh2.reshape(B, 1, D), ng.reshape(1, D),
      w_attn_out.astype(bf16), w_o.astype(bf16), wr, br)


def _dispatch_tables(route, n_chunks):
    T = route.shape[0]
    tc = T // n_chunks
    na = tc * TOP_K_EXPERTS
    e = route[:, :TOP_K_EXPERTS].astype(jnp.int32).reshape(n_chunks, na)
    w = route[:, TOP_K_EXPERTS:2 * TOP_K_EXPERTS].reshape(n_chunks, na)
    order = jnp.argsort(e, axis=1, stable=True).astype(jnp.int32)
    stok = order // TOP_K_EXPERTS
    sw = jnp.take_along_axis(w, order, axis=1)
    counts = jnp.sum((e[:, :, None] == jnp.arange(N_EXPERTS)[None, None, :]).astype(jnp.int32), axis=1)
    starts = jnp.cumsum(counts, axis=1) - counts
    nblk = (counts + EXPERT_ROWS - 1) // EXPERT_ROWS
    bend = jnp.cumsum(nblk, axis=1)
    bstart = bend - nblk
    total = bend[:, -1:]
    n_blocks = na // EXPERT_ROWS + N_EXPERTS
    bidx = jnp.arange(n_blocks, dtype=jnp.int32)[None, :]
    be = jnp.sum((bidx[:, :, None] >= bend[:, None, :]).astype(jnp.int32), axis=2)
    be = jnp.minimum(be, N_EXPERTS - 1)
    valid = bidx < total
    be = jnp.where(valid, be, jnp.take_along_axis(be, jnp.maximum(total - 1, 0), axis=1))
    local = bidx - jnp.take_along_axis(bstart, be, axis=1)
    bs = jnp.take_along_axis(starts, be, axis=1) + EXPERT_ROWS * local
    bn = jnp.clip(jnp.take_along_axis(counts, be, axis=1) - EXPERT_ROWS * local, 0, EXPERT_ROWS)
    bn = jnp.where(valid, bn, 0)
    bs = jnp.where(valid, bs, 0)
    flat = lambda t: t.reshape(-1).astype(jnp.int32)
    return flat(be), flat(bs), flat(bn), stok.reshape(n_chunks, 1, na), sw.reshape(n_chunks, 1, na), n_blocks


def _moe_kernel(be_ref, bs_ref, bn_ref, stok_ref, sw_ref, h2p_ref, wg_ref, wu_ref, wd_ref,
                y_hbm, yacc, xs, outs, sem):
    c = pl.program_id(0)
    b = pl.program_id(1)
    nbk = pl.num_programs(1)
    tc = yacc.shape[0]
    half = xs.shape[1]
    n = bn_ref[c * nbk + b]
    st = bs_ref[c * nbk + b]

    @pl.when(b == 0)
    def _():
        yacc[...] = jnp.zeros(yacc.shape, f32)

    @pl.when((b == 0) & (c == 0))
    def _():
        xs[...] = jnp.zeros(xs.shape, jnp.uint32)

    @pl.when(n > 0)
    def _():
        def gather(r, carry):
            tok = stok_ref[0, 0, st + r]
            xs[pl.ds(r, 1), :] = h2p_ref[pl.ds(tok, 1), :]
            return carry

        lax.fori_loop(0, n, gather, 0)
        packed = xs[...]
        x_lo = pltpu.bitcast(packed << 16, f32).astype(bf16)
        x_hi = pltpu.bitcast(packed & jnp.uint32(0xFFFF0000), f32).astype(bf16)

        def up(w_ref):
            w = w_ref[0].astype(bf16)
            return (jnp.dot(x_lo, w[:half], preferred_element_type=f32)
                    + jnp.dot(x_hi, w[half:], preferred_element_type=f32))

        hidden = (jax.nn.silu(up(wg_ref)) * up(wu_ref)).astype(bf16)
        outs[...] = jnp.dot(hidden, wd_ref[0].astype(bf16), preferred_element_type=f32)

        def scatter(r, carry):
            tok = stok_ref[0, 0, st + r]
            wgt = sw_ref[0, 0, st + r]
            yacc[pl.ds(tok, 1), :] = yacc[pl.ds(tok, 1), :] + wgt * outs[pl.ds(r, 1), :]
            return carry

        lax.fori_loop(0, n, scatter, 0)

    @pl.when(b == nbk - 1)
    def _():
        copy = pltpu.make_async_copy(yacc, y_hbm.at[pl.ds(c * tc, tc), :], sem)
        copy.start()
        copy.wait()


def _moe(route, h2p, w_gate, w_up, w_down):
    T, half = h2p.shape
    D = 2 * half
    de = w_gate.shape[2]
    n_chunks = MOE_CHUNKS
    tc = T // n_chunks
    be, bs, bn, stok, sw, n_blocks = _dispatch_tables(route, n_chunks)
    na = stok.shape[2]
    wmap = lambda c, b, be_r, bs_r, bn_r: (be_r[c * n_blocks + b], 0, 0)
    smem = lambda: pl.BlockSpec((1, 1, na), lambda c, b, *_: (c, 0, 0), memory_space=pltpu.SMEM)
    grid_spec = pltpu.PrefetchScalarGridSpec(
        num_scalar_prefetch=3,
        grid=(n_chunks, n_blocks),
        in_specs=[smem(), smem(),
                  pl.BlockSpec((tc, half), lambda c, b, *_: (c, 0)),
                  pl.BlockSpec((1, D, de), wmap),
                  pl.BlockSpec((1, D, de), wmap),
                  pl.BlockSpec((1, de, D), wmap)],
        out_specs=pl.BlockSpec(memory_space=pl.ANY),
        scratch_shapes=[pltpu.VMEM((tc, D), f32),
                        pltpu.VMEM((EXPERT_ROWS, half), jnp.uint32),
                        pltpu.VMEM((EXPERT_ROWS, D), f32),
                        pltpu.SemaphoreType.DMA(())],
    )
    return pl.pallas_call(
        _moe_kernel,
        grid_spec=grid_spec,
        out_shape=jax.ShapeDtypeStruct((T, D), f32),
        compiler_params=_params(("arbitrary", "arbitrary")),
        name="moe",
    )(be, bs, bn, stok, sw, h2p, w_gate, w_up, w_down)


def _final_kernel(x1_ref, y_ref, g2_ref, fg_ref, o_ref, *, normalize):
    x2 = x1_ref[0] + g2_ref[0] * y_ref[0]
    if normalize:
        x2 = (x2 * lax.rsqrt(jnp.mean(x2 * x2, axis=-1, keepdims=True) + NORM_EPS)) * fg_ref[...]
    o_ref[0] = x2


def _residual(x1, y, g2, fg, normalize):
    B, S, D = x1.shape
    tm = ROW_TILE
    row = pl.BlockSpec((1, tm, D), lambda b, s: (b, s, 0))
    return pl.pallas_call(
        functools.partial(_final_kernel, normalize=normalize),
        grid=(B, S // tm),
        in_specs=[row, row, pl.BlockSpec((1, 1, D), lambda b, s: (b, 0, 0)),
                  pl.BlockSpec((1, D), lambda b, s: (0, 0))],
        out_specs=row,
        out_shape=jax.ShapeDtypeStruct((B, S, D), f32),
        compiler_params=_params(("parallel", "parallel")),
        name="final",
    )(x1, y.reshape(B, S, D), g2.reshape(B, 1, D), fg.reshape(1, D))


def kernel(x, c, rel_bias, norm1_g, norm2_g, w_ada, b_ada, w_in, b_gate, conv_w, w_attn_out,
           w_conv_out, w_o, w_router_group, b_router_group, w_router_expert, b_router_expert,
           w_exp_gate, w_exp_up, w_exp_down, final_norm_g):
    B, S, D = x.shape
    depth = w_ada.shape[0]
    assert S % ROW_TILE == 0 and ROW_TILE % MOBA_BLOCK == 0 and (B * S) % (MOE_CHUNKS * EXPERT_ROWS) == 0
    own, adj = _bias_tiles(rel_bias)
    for l in range(depth):
        mod = _mod(c, w_ada[l], b_ada[l])
        sh1, sc1, g1, sh2, sc2, g2 = jnp.split(mod, 6, axis=-1)
        qT, k4, vT4, km, mc, ga = _inproj(x, norm1_g[l], sc1, sh1, w_in[l], b_gate[l], conv_w[l],
                                          w_conv_out[l])
        a = _attention(qT, k4, vT4, km, own, adj)
        x1, h2p, route = _post(a, mc, ga, x, g1, sc2, sh2, norm2_g[l], w_attn_out[l], w_o[l],
                               w_router_group[l], b_router_group[l], w_router_expert[l],
                               b_router_expert[l])
        y = _moe(route, h2p, w_exp_gate[l], w_exp_up[l], w_exp_down[l])
        x = _residual(x1, y, g2, final_norm_g, normalize=(l + 1 == depth))
    return x
```

```python
import functools
import math

import numpy as np
import jax
import jax.numpy as jnp
from jax import lax
from jax.experimental import pallas as pl
from jax.experimental.pallas import tpu as pltpu

f32 = jnp.float32
bf16 = jnp.bfloat16

N_HEADS = 8
HEAD_DIM = 64
ATTN_WIDTH = N_HEADS * HEAD_DIM
MOBA_BLOCK = 256
MOBA_TOPK = 3
CONV_K = 3
N_BUCKETS = 32
MAX_EXACT = N_BUCKETS // 2
MAX_DISTANCE = 128
N_GROUPS = 4
EXPERTS_PER_GROUP = 8
N_EXPERTS = N_GROUPS * EXPERTS_PER_GROUP
TOP_K_EXPERTS = 2
NORM_EPS = 1e-6
NEG_INF = -1e30

LANES = 128
SUBLANES = 8
ROW_TILE = 512
EXPERT_ROWS = 256
MOE_CHUNKS = 4
FAR_BLOCKS_PER_ITER = 2
VMEM_LIMIT = 56 * 1024 * 1024
HIGHEST = lax.Precision.HIGHEST


def _params(sem, vmem=VMEM_LIMIT):
    return pltpu.CompilerParams(dimension_semantics=sem, vmem_limit_bytes=vmem)


def _mod_kernel(c_ref, w_ref, b_ref, o_ref):
    c = c_ref[...]
    o_ref[...] = jnp.dot(jax.nn.silu(c), w_ref[...], precision=HIGHEST,
                         preferred_element_type=f32) + b_ref[...]


def _mod(c, w_ada, b_ada):
    B, D = c.shape
    n_out = w_ada.shape[1]
    rows = 8
    cp = jnp.zeros((rows, D), f32).at[:B].set(c)
    tn = 1024
    out = pl.pallas_call(
        _mod_kernel,
        grid=(n_out // tn,),
        in_specs=[pl.BlockSpec((rows, D), lambda j: (0, 0)),
                  pl.BlockSpec((D, tn), lambda j: (0, j)),
                  pl.BlockSpec((1, tn), lambda j: (0, j))],
        out_specs=pl.BlockSpec((rows, tn), lambda j: (0, j)),
        out_shape=jax.ShapeDtypeStruct((rows, n_out), f32),
        compiler_params=_params(("arbitrary",)),
        name="mod",
    )(cp, w_ada, b_ada.reshape(1, n_out))
    return out[:B]


def _bucket_thresholds():
    n = np.arange(2 * MOBA_BLOCK)

    def buckets(ft):
        nf = np.maximum(n, 1).astype(ft)
        large = MAX_EXACT + (np.log(nf / ft(MAX_EXACT)) / ft(math.log(MAX_DISTANCE / MAX_EXACT))
                             * ft(N_BUCKETS - MAX_EXACT)).astype(np.int32)
        return np.where(n < MAX_EXACT, n, np.minimum(large, N_BUCKETS - 1))

    b = buckets(np.float32)
    assert np.array_equal(b, buckets(np.float64)) and np.all(np.diff(b) >= 0)
    assert b[-1] == N_BUCKETS - 1 and b[MOBA_BLOCK + 1] == N_BUCKETS - 1
    return [int(np.argmax(b >= j)) for j in range(N_BUCKETS)]


def _bias_kernel(rb_ref, own_ref, adj_ref, *, thresholds):
    h = pl.program_id(0)
    kk = lax.broadcasted_iota(jnp.int32, (MOBA_BLOCK, MOBA_BLOCK), 0)
    qq = lax.broadcasted_iota(jnp.int32, (MOBA_BLOCK, MOBA_BLOCK), 1)
    far = rb_ref[N_BUCKETS - 1, h]

    def table(dist):
        acc = jnp.full(dist.shape, rb_ref[0, h], f32)
        for j in range(1, N_BUCKETS):
            acc = jnp.where(dist >= thresholds[j], rb_ref[j, h], acc)
        return acc - far

    own_ref[0] = table(qq - kk)
    adj_ref[0] = table(qq - kk + MOBA_BLOCK)


def _bias_tiles(rel_bias):
    shp = jax.ShapeDtypeStruct((N_HEADS, MOBA_BLOCK, MOBA_BLOCK), f32)
    spec = pl.BlockSpec((1, MOBA_BLOCK, MOBA_BLOCK), lambda h: (h, 0, 0))
    return pl.pallas_call(
        functools.partial(_bias_kernel, thresholds=_bucket_thresholds()),
        grid=(N_HEADS,),
        in_specs=[pl.BlockSpec(memory_space=pltpu.SMEM)],
        out_specs=(spec, spec),
        out_shape=(shp, shp),
        compiler_params=_params(("arbitrary",)),
        name="bias",
    )(rel_bias)


def _ada_norm(x, g, scale, shift):
    y = x * lax.rsqrt(jnp.mean(x * x, axis=-1, keepdims=True) + NORM_EPS)
    return (y * g) * (1.0 + scale) + shift


def _inproj_kernel(x_ref, g_ref, sc_ref, sh_ref, wqT_ref, wk_ref, wvT_ref, wc_ref, wgl_ref,
                   wco_ref, cw_ref, bg_ref,
                   qT_ref, k_ref, vT_ref, km_ref, mc_ref, ga_ref, ubuf):
    tm = x_ref.shape[1]
    cwid = wco_ref.shape[0]
    d = x_ref.shape[2]
    s = pl.program_id(1)
    hb = _ada_norm(x_ref[0], g_ref[...], sc_ref[0], sh_ref[0]).astype(bf16)

    nt = (((1,), (1,)), ((), ()))
    qT_ref[0] = lax.dot_general(wqT_ref[...], hb, nt, preferred_element_type=f32).astype(bf16)
    k = jnp.dot(hb, wk_ref[...], preferred_element_type=f32)
    vT = lax.dot_general(wvT_ref[...], hb, nt, preferred_element_type=f32).astype(bf16)
    for j in range(tm // MOBA_BLOCK):
        rows = slice(j * MOBA_BLOCK, (j + 1) * MOBA_BLOCK)
        k_ref[0, j] = k[rows].astype(bf16)
        vT_ref[0, j] = vT[:, rows]
        km_ref[0, 0, j:j + 1, :] = jnp.mean(k[rows], axis=0, keepdims=True)

    cbx = jnp.dot(hb, wc_ref[...], preferred_element_type=f32)
    u = cbx[:, :cwid] * cbx[:, 2 * cwid:]

    @pl.when(s == 0)
    def _():
        ubuf[0:8, :] = jnp.zeros((8, cwid), f32)

    ubuf[8:8 + tm, :] = u
    cw = cw_ref[...]
    conv = cw[0:1] * ubuf[6:6 + tm, :] + cw[1:2] * ubuf[7:7 + tm, :] + cw[2:3] * u
    ubuf[0:8, :] = ubuf[tm:tm + 8, :]
    cv = (cbx[:, cwid:2 * cwid] * conv).astype(bf16)
    y_conv = jnp.dot(cv, wco_ref[...], preferred_element_type=f32)

    gl = jnp.dot(hb, wgl_ref[...], preferred_element_type=f32) + bg_ref[...]
    ga_ref[0] = jax.nn.sigmoid(gl[:, :d]).astype(bf16)
    mc_ref[0] = (jax.nn.sigmoid(gl[:, d:]) * y_conv).astype(bf16)


def _inproj(x, g, sc, sh, w_in, b_gate, conv_w, w_conv_out):
    B, S, D = x.shape
    tm = ROW_TILE
    nb = S // MOBA_BLOCK
    bpt = tm // MOBA_BLOCK
    aw = ATTN_WIDTH
    cwid = conv_w.shape[1]
    scale = HEAD_DIM ** -0.5
    wqT = (w_in[:, :aw] * scale).T.astype(bf16)
    wk = w_in[:, aw:2 * aw].astype(bf16)
    wvT = w_in[:, 2 * aw:3 * aw].T.astype(bf16)
    wc = w_in[:, 3 * aw:3 * aw + 3 * cwid].astype(bf16)
    wgl = w_in[:, 3 * aw + 3 * cwid:].astype(bf16)
    wco = w_conv_out.astype(bf16)

    const = lambda shape: pl.BlockSpec(shape, lambda b, s: (0,) * len(shape))
    vec = pl.BlockSpec((1, 1, D), lambda b, s: (b, 0, 0))
    row = lambda width: pl.BlockSpec((1, tm, width), lambda b, s: (b, s, 0))
    outs = pl.pallas_call(
        _inproj_kernel,
        grid=(B, S // tm),
        in_specs=[row(D), const((1, D)), vec, vec,
                  const((aw, D)), const((D, aw)), const((aw, D)), const((D, 3 * cwid)),
                  const((D, 2 * D)), const((cwid, D)), const((CONV_K, cwid)), const((1, 2 * D))],
        out_specs=(pl.BlockSpec((1, aw, tm), lambda b, s: (b, 0, s)),
                   pl.BlockSpec((1, bpt, MOBA_BLOCK, aw), lambda b, s: (b, s, 0, 0)),
                   pl.BlockSpec((1, bpt, aw, MOBA_BLOCK), lambda b, s: (b, s, 0, 0)),
                   pl.BlockSpec((1, 1, bpt, aw), lambda b, s: (b, s, 0, 0)),
                   row(D), row(D)),
        out_shape=(jax.ShapeDtypeStruct((B, aw, S), bf16),
                   jax.ShapeDtypeStruct((B, nb, MOBA_BLOCK, aw), bf16),
                   jax.ShapeDtypeStruct((B, nb, aw, MOBA_BLOCK), bf16),
                   jax.ShapeDtypeStruct((B, S // tm, bpt, aw), f32),
                   jax.ShapeDtypeStruct((B, S, D), bf16),
                   jax.ShapeDtypeStruct((B, S, D), bf16)),
        scratch_shapes=[pltpu.VMEM((tm + 8, cwid), f32)],
        compiler_params=_params(("parallel", "arbitrary")),
        name="inproj",
    )(x, g.reshape(1, D), sc.reshape(B, 1, D), sh.reshape(B, 1, D),
      wqT, wk, wvT, wc, wgl, wco, conv_w, b_gate.reshape(1, 2 * D))
    qT, k4, vT4, km, mc, ga = outs
    return qT, k4, vT4, km.reshape(B, nb, aw), mc, ga


def _attn_kernel(qT_ref, k_ref, vT_ref, km_ref, own_ref, adj_ref, o_ref):
    i = pl.program_id(1)
    bs = MOBA_BLOCK
    nb = km_ref.shape[1]
    n_heads = qT_ref.shape[1] // HEAD_DIM
    heads = range(n_heads)
    blk = lax.broadcasted_iota(jnp.int32, (nb, bs), 0)

    qcat = []
    for h in heads:
        pair, a = divmod(h, LANES // HEAD_DIM)
        qT = qT_ref[0, pair * LANES:(pair + 1) * LANES, :]
        row = lax.broadcasted_iota(jnp.int32, qT.shape, 0)
        km = km_ref[0, :, pair * LANES:(pair + 1) * LANES] * float(HEAD_DIM ** 0.5)
        qa = jnp.where((row >= a * HEAD_DIM) & (row < (a + 1) * HEAD_DIM), qT, jnp.zeros_like(qT))
        g = jnp.dot(km, qa.astype(f32), precision=HIGHEST, preferred_element_type=f32)
        g = jnp.where(blk < i, g, NEG_INF)
        fq = jnp.full((nb, bs), NEG_INF, f32)
        for _ in range(MOBA_TOPK):
            top = jnp.max(g, axis=0, keepdims=True)
            first = jnp.min(jnp.where(g == top, blk, nb), axis=0, keepdims=True)
            pick = blk == first
            fq = jnp.where(pick, 0.0, fq)
            g = jnp.where(pick, -jnp.inf, g)
        fq = jnp.where(blk < i, fq, 0.0).astype(bf16)
        pad = jnp.zeros((LANES - nb, bs), bf16)
        qcat.append(jnp.concatenate([qa, fq, pad], axis=0))

    lane = lax.broadcasted_iota(jnp.int32, (bs, LANES), 1)
    kk = lax.broadcasted_iota(jnp.int32, (bs, bs), 0)
    qq = lax.broadcasted_iota(jnp.int32, (bs, bs), 1)

    def all_scores(ns):
        onehots = [jnp.where(lane == n, 1.0, 0.0).astype(bf16) for n in ns]
        out = []
        for h in heads:
            pair = h // (LANES // HEAD_DIM)
            cols = slice(pair * LANES, (pair + 1) * LANES)
            out.append([jnp.dot(jnp.concatenate([k_ref[0, n, :, cols], onehot], axis=1), qcat[h],
                                preferred_element_type=f32) for n, onehot in zip(ns, onehots)])
        return out

    def values(n, h):
        return vT_ref[0, n, h * HEAD_DIM:(h + 1) * HEAD_DIM, :]

    def first_step(h, sT):
        sT = jnp.where(kk <= qq, sT + own_ref[h], NEG_INF)
        m = jnp.max(sT, axis=0, keepdims=True)
        p = jnp.exp(sT - m)
        l = jnp.sum(p, axis=0, keepdims=True)
        acc = jnp.dot(values(i, h), p.astype(bf16), preferred_element_type=f32)
        return m, l, acc

    def next_step(ns, h, sTs, state):
        m, l, acc = state
        m_new = m
        for sT in sTs:
            m_new = jnp.maximum(m_new, jnp.max(sT, axis=0, keepdims=True))
        alpha = jnp.exp(m - m_new)
        l = alpha * l
        acc = alpha * acc
        for n, sT in zip(ns, sTs):
            p = jnp.exp(sT - m_new)
            l = l + jnp.sum(p, axis=0, keepdims=True)
            acc = acc + jnp.dot(values(n, h), p.astype(bf16), preferred_element_type=f32)
        return m_new, l, acc

    def steps(ns, st, bias_ref=None):
        per_head = all_scores(ns)
        out = []
        for h in heads:
            sTs = per_head[h]
            if bias_ref is not None:
                sTs = [sT + bias_ref[h] for sT in sTs]
            out.append(next_step(ns, h, sTs, st[h]))
        return tuple(out)

    own_scores = all_scores([i])
    state = tuple(first_step(h, own_scores[h][0]) for h in heads)
    state = lax.cond(i >= 1, lambda st: steps([i - 1], st, adj_ref), lambda st: st, state)
    n_far = jnp.maximum(i - 1, 0)
    state = lax.fori_loop(0, n_far // FAR_BLOCKS_PER_ITER,
                          lambda it, st: steps([it * FAR_BLOCKS_PER_ITER + j for j in range(FAR_BLOCKS_PER_ITER)], st),
                          state)
    for r in range(FAR_BLOCKS_PER_ITER - 1):
        last = n_far - n_far % FAR_BLOCKS_PER_ITER + r
        state = lax.cond(n_far % FAR_BLOCKS_PER_ITER > r, lambda st, n=last: steps([n], st), lambda st: st, state)
    outT = jnp.concatenate([acc / l for (_, l, acc) in state], axis=0)
    o_ref[0] = outT.T.astype(bf16)


def _attention(qT, k4, vT4, km, own, adj):
    B, aw, S = qT.shape
    nb = S // MOBA_BLOCK
    assert MOBA_TOPK < nb <= LANES
    n_heads = aw // HEAD_DIM
    whole = lambda shape: pl.BlockSpec(shape, lambda b, i: (b,) + (0,) * (len(shape) - 1))
    table = pl.BlockSpec((n_heads, MOBA_BLOCK, MOBA_BLOCK), lambda b, i: (0, 0, 0))
    return pl.pallas_call(
        _attn_kernel,
        grid=(B, nb),
        in_specs=[pl.BlockSpec((1, aw, MOBA_BLOCK), lambda b, i: (b, 0, i)),
                  whole((1, nb, MOBA_BLOCK, aw)), whole((1, nb, aw, MOBA_BLOCK)), whole((1, nb, aw)),
                  table, table],
        out_specs=pl.BlockSpec((1, MOBA_BLOCK, aw), lambda b, i: (b, i, 0)),
        out_shape=jax.ShapeDtypeStruct((B, S, aw), bf16),
        compiler_params=_params(("parallel", "arbitrary")),
        name="attn",
    )(qT, k4, vT4, km, own, adj)


def _post_kernel(a_ref, mc_ref, ga_ref, x_ref, g1_ref, sc_ref, sh_ref, ng_ref,
                 wao_ref, wo_ref, wr_ref, br_ref, x1_ref, h2p_ref, route_ref):
    tm = x_ref.shape[1]
    d = x_ref.shape[2]
    ya = jnp.dot(a_ref[0], wao_ref[...], preferred_element_type=f32)
    merged = ga_ref[0].astype(f32) * ya + mc_ref[0].astype(f32)
    z = jnp.dot(merged.astype(bf16), wo_ref[...], preferred_element_type=f32)
    x1 = x_ref[0] + g1_ref[0] * z
    x1_ref[0] = x1
    h2 = _ada_norm(x1, ng_ref[...], sc_ref[0], sh_ref[0])

    bits = pltpu.bitcast(h2.astype(bf16).astype(f32), jnp.uint32)
    h2p_ref[...] = (bits[:, :d // 2] >> 16) | (bits[:, d // 2:] & jnp.uint32(0xFFFF0000))

    logits = jnp.dot(h2, wr_ref[...], precision=HIGHEST, preferred_element_type=f32) + br_ref[...]
    lane = lax.broadcasted_iota(jnp.int32, (tm, LANES), 1)
    ninf = -jnp.inf
    gl = jnp.where(lane < N_GROUPS, logits, ninf)
    gmax = jnp.max(gl, axis=1, keepdims=True)
    g_sel = jnp.min(jnp.where(gl == gmax, lane, LANES), axis=1, keepdims=True)
    g_w = 1.0 / jnp.sum(jnp.exp(gl - gmax), axis=1, keepdims=True)
    off = lane - (N_GROUPS + EXPERTS_PER_GROUP * g_sel)
    in_group = (off >= 0) & (off < EXPERTS_PER_GROUP)
    el = jnp.where(in_group, logits, ninf)
    ex = jnp.exp(el - jnp.max(el, axis=1, keepdims=True))
    ep = jnp.where(in_group, ex / jnp.sum(ex, axis=1, keepdims=True), -1.0)
    p1 = jnp.max(ep, axis=1, keepdims=True)
    i1 = jnp.min(jnp.where(ep == p1, lane, LANES), axis=1, keepdims=True)
    ep2 = jnp.where(lane == i1, -1.0, ep)
    p2 = jnp.max(ep2, axis=1, keepdims=True)
    i2 = jnp.min(jnp.where(ep2 == p2, lane, LANES), axis=1, keepdims=True)
    tot = p1 + p2
    w1 = p1 / tot * g_w
    w2 = p2 / tot * g_w
    e1 = (i1 - N_GROUPS).astype(f32)
    e2 = (i2 - N_GROUPS).astype(f32)
    route_ref[...] = jnp.where(lane == 0, e1, jnp.where(lane == 1, e2,
                               jnp.where(lane == 2, w1, jnp.where(lane == 3, w2, 0.0))))


def _post(a, mc, ga, x, g1, sc2, sh2, ng, w_attn_out, w_o, w_rg, b_rg, w_re, b_re):
    B, S, D = x.shape
    tm = ROW_TILE
    aw = a.shape[2]
    nt = S // tm
    wr = jnp.zeros((D, LANES), f32).at[:, :N_GROUPS].set(w_rg).at[:, N_GROUPS:N_GROUPS + N_EXPERTS].set(w_re)
    br = jnp.zeros((1, LANES), f32).at[0, :N_GROUPS].set(b_rg).at[0, N_GROUPS:N_GROUPS + N_EXPERTS].set(b_re)
    const = lambda shape: pl.BlockSpec(shape, lambda b, s: (0,) * len(shape))
    vec = pl.BlockSpec((1, 1, D), lambda b, s: (b, 0, 0))
    row = lambda width: pl.BlockSpec((1, tm, width), lambda b, s: (b, s, 0))
    flat = lambda width: pl.BlockSpec((tm, width), lambda b, s: (b * nt + s, 0))
    return pl.pallas_call(
        _post_kernel,
        grid=(B, nt),
        in_specs=[row(aw), row(D), row(D), row(D), vec, vec, vec, const((1, D)),
                  const((aw, D)), const((D, D)), const((D, LANES)), const((1, LANES))],
        out_specs=(row(D), flat(D // 2), flat(LANES)),
        out_shape=(jax.ShapeDtypeStruct((B, S, D), f32),
                   jax.ShapeDtypeStruct((B * S, D // 2), jnp.uint32),
                   jax.ShapeDtypeStruct((B * S, LANES), f32)),
        compiler_params=_params(("parallel", "parallel")),
        name="post",
    )(a, mc, ga, x, g1.reshape(B, 1, D), sc2.reshape(B, 1, D), sh2.reshape(B, 1, D), ng.reshape(1, D),
      w_attn_out.astype(bf16), w_o.astype(bf16), wr, br)


def _dispatch_tables(route, n_chunks):
    T = route.shape[0]
    tc = T // n_chunks
    na = tc * TOP_K_EXPERTS
    e = route[:, :TOP_K_EXPERTS].astype(jnp.int32).reshape(n_chunks, na)
    w = route[:, TOP_K_EXPERTS:2 * TOP_K_EXPERTS].reshape(n_chunks, na)
    order = jnp.argsort(e, axis=1, stable=True).astype(jnp.int32)
    stok = order // TOP_K_EXPERTS
    sw = jnp.take_along_axis(w, order, axis=1)
    counts = jnp.sum((e[:, :, None] == jnp.arange(N_EXPERTS)[None, None, :]).astype(jnp.int32), axis=1)
    starts = jnp.cumsum(counts, axis=1) - counts
    nblk = (counts + EXPERT_ROWS - 1) // EXPERT_ROWS
    bend = jnp.cumsum(nblk, axis=1)
    bstart = bend - nblk
    total = bend[:, -1:]
    n_blocks = na // EXPERT_ROWS + N_EXPERTS
    bidx = jnp.arange(n_blocks, dtype=jnp.int32)[None, :]
    be = jnp.sum((bidx[:, :, None] >= bend[:, None, :]).astype(jnp.int32), axis=2)
    be = jnp.minimum(be, N_EXPERTS - 1)
    valid = bidx < total
    be = jnp.where(valid, be, jnp.take_along_axis(be, jnp.maximum(total - 1, 0), axis=1))
    local = bidx - jnp.take_along_axis(bstart, be, axis=1)
    bs = jnp.take_along_axis(starts, be, axis=1) + EXPERT_ROWS * local
    bn = jnp.clip(jnp.take_along_axis(counts, be, axis=1) - EXPERT_ROWS * local, 0, EXPERT_ROWS)
    bn = jnp.where(valid, bn, 0)
    bs = jnp.where(valid, bs, 0)
    flat = lambda t: t.reshape(-1).astype(jnp.int32)
    return flat(be), flat(bs), flat(bn), stok.reshape(n_chunks, 1, na), sw.reshape(n_chunks, 1, na), n_blocks


def _moe_kernel(be_ref, bs_ref, bn_ref, stok_ref, sw_ref, h2p_ref, wg_ref, wu_ref, wd_ref,
                y_hbm, yacc, xs, outs, sem):
    c = pl.program_id(0)
    b = pl.program_id(1)
    nbk = pl.num_programs(1)
    tc = yacc.shape[0]
    half = xs.shape[2]
    n = bn_ref[c * nbk + b]
    st = bs_ref[c * nbk + b]

    @pl.when(b == 0)
    def _():
        yacc[...] = jnp.zeros(yacc.shape, f32)

    @pl.when((b == 0) & (c == 0))
    def _():
        xs[...] = jnp.zeros(xs.shape, jnp.uint32)

    def for_rows(rows_fn):
        full = n // SUBLANES

        def group(g, carry):
            rows_fn(g, list(range(SUBLANES)), st + g * SUBLANES)
            return carry

        def single(j, carry):
            rows_fn(full, [j], st + full * SUBLANES)
            return carry

        lax.fori_loop(0, full, group, 0)
        lax.fori_loop(0, n - full * SUBLANES, single, 0)

    @pl.when(n > 0)
    def _():
        def gather(g, js, i0):
            for j in js:
                xs[g, pl.ds(j, 1), :] = h2p_ref[pl.ds(stok_ref[0, 0, i0 + j], 1), :]

        for_rows(gather)
        packed = xs[...].reshape(xs.shape[0] * SUBLANES, half)
        x_lo = pltpu.bitcast(packed << 16, f32).astype(bf16)
        x_hi = pltpu.bitcast(packed & jnp.uint32(0xFFFF0000), f32).astype(bf16)

        def up(w_ref):
            return (jnp.dot(x_lo, w_ref[0, :half, :], preferred_element_type=f32)
                    + jnp.dot(x_hi, w_ref[0, half:, :], preferred_element_type=f32))

        hidden = (jax.nn.silu(up(wg_ref)) * up(wu_ref)).astype(bf16)
        outs[...] = jnp.dot(hidden, wd_ref[0], preferred_element_type=f32).reshape(outs.shape)

        def scatter(g, js, i0):
            toks = [stok_ref[0, 0, i0 + j] for j in js]
            acc = [yacc[pl.ds(tok, 1), :] for tok in toks]
            for j, tok, a in zip(js, toks, acc):
                yacc[pl.ds(tok, 1), :] = a + sw_ref[0, 0, i0 + j] * outs[g, pl.ds(j, 1), :]

        for_rows(scatter)

    @pl.when(b == nbk - 1)
    def _():
        copy = pltpu.make_async_copy(yacc, y_hbm.at[pl.ds(c * tc, tc), :], sem)
        copy.start()
        copy.wait()


def _moe(route, h2p, w_gate, w_up, w_down):
    T, half = h2p.shape
    D = 2 * half
    de = w_gate.shape[2]
    n_chunks = MOE_CHUNKS
    tc = T // n_chunks
    be, bs, bn, stok, sw, n_blocks = _dispatch_tables(route, n_chunks)
    na = stok.shape[2]
    wmap = lambda c, b, be_r, bs_r, bn_r: (be_r[c * n_blocks + b], 0, 0)
    smem = lambda: pl.BlockSpec((1, 1, na), lambda c, b, *_: (c, 0, 0), memory_space=pltpu.SMEM)
    grid_spec = pltpu.PrefetchScalarGridSpec(
        num_scalar_prefetch=3,
        grid=(n_chunks, n_blocks),
        in_specs=[smem(), smem(),
                  pl.BlockSpec((tc, half), lambda c, b, *_: (c, 0)),
                  pl.BlockSpec((1, D, de), wmap),
                  pl.BlockSpec((1, D, de), wmap),
                  pl.BlockSpec((1, de, D), wmap)],
        out_specs=pl.BlockSpec(memory_space=pl.ANY),
        scratch_shapes=[pltpu.VMEM((tc, D), f32),
                        pltpu.VMEM((EXPERT_ROWS // SUBLANES, SUBLANES, half), jnp.uint32),
                        pltpu.VMEM((EXPERT_ROWS // SUBLANES, SUBLANES, D), f32),
                        pltpu.SemaphoreType.DMA(())],
    )
    return pl.pallas_call(
        _moe_kernel,
        grid_spec=grid_spec,
        out_shape=jax.ShapeDtypeStruct((T, D), f32),
        compiler_params=_params(("arbitrary", "arbitrary")),
        name="moe",
    )(be, bs, bn, stok, sw, h2p, w_gate.astype(bf16), w_up.astype(bf16), w_down.astype(bf16))


def _final_kernel(x1_ref, y_ref, g2_ref, fg_ref, o_ref, *, normalize):
    x2 = x1_ref[0] + g2_ref[0] * y_ref[0]
    if normalize:
        x2 = (x2 * lax.rsqrt(jnp.mean(x2 * x2, axis=-1, keepdims=True) + NORM_EPS)) * fg_ref[...]
    o_ref[0] = x2


def _residual(x1, y, g2, fg, normalize):
    B, S, D = x1.shape
    tm = ROW_TILE
    row = pl.BlockSpec((1, tm, D), lambda b, s: (b, s, 0))
    return pl.pallas_call(
        functools.partial(_final_kernel, normalize=normalize),
        grid=(B, S // tm),
        in_specs=[row, row, pl.BlockSpec((1, 1, D), lambda b, s: (b, 0, 0)),
                  pl.BlockSpec((1, D), lambda b, s: (0, 0))],
        out_specs=row,
        out_shape=jax.ShapeDtypeStruct((B, S, D), f32),
        compiler_params=_params(("parallel", "parallel")),
        name="final",
    )(x1, y.reshape(B, S, D), g2.reshape(B, 1, D), fg.reshape(1, D))


def kernel(x, c, rel_bias, norm1_g, norm2_g, w_ada, b_ada, w_in, b_gate, conv_w, w_attn_out,
           w_conv_out, w_o, w_router_group, b_router_group, w_router_expert, b_router_expert,
           w_exp_gate, w_exp_up, w_exp_down, final_norm_g):
    B, S, D = x.shape
    depth = w_ada.shape[0]
    assert S % ROW_TILE == 0 and ROW_TILE % MOBA_BLOCK == 0 and (B * S) % (MOE_CHUNKS * EXPERT_ROWS) == 0
    own, adj = _bias_tiles(rel_bias)
    for l in range(depth):
        mod = _mod(c, w_ada[l], b_ada[l])
        sh1, sc1, g1, sh2, sc2, g2 = jnp.split(mod, 6, axis=-1)
        qT, k4, vT4, km, mc, ga = _inproj(x, norm1_g[l], sc1, sh1, w_in[l], b_gate[l], conv_w[l],
                                          w_conv_out[l])
        a = _attention(qT, k4, vT4, km, own, adj)
        x1, h2p, route = _post(a, mc, ga, x, g1, sc2, sh2, norm2_g[l], w_attn_out[l], w_o[l],
                               w_router_group[l], b_router_group[l], w_router_expert[l],
                               b_router_expert[l])
        y = _moe(route, h2p, w_exp_gate[l], w_exp_up[l], w_exp_down[l])
        x = _residual(x1, y, g2, final_norm_g, normalize=(l + 1 == depth))
    return x
```

```python
import functools
import math

import numpy as np
import jax
import jax.numpy as jnp
from jax import lax
from jax.experimental import pallas as pl
from jax.experimental.pallas import tpu as pltpu

f32 = jnp.float32
bf16 = jnp.bfloat16

N_HEADS = 8
HEAD_DIM = 64
ATTN_WIDTH = N_HEADS * HEAD_DIM
MOBA_BLOCK = 256
MOBA_TOPK = 3
CONV_K = 3
N_BUCKETS = 32
MAX_EXACT = N_BUCKETS // 2
MAX_DISTANCE = 128
N_GROUPS = 4
EXPERTS_PER_GROUP = 8
N_EXPERTS = N_GROUPS * EXPERTS_PER_GROUP
TOP_K_EXPERTS = 2
NORM_EPS = 1e-6
NEG_INF = -1e30

LANES = 128
SUBLANES = 8
ROW_TILE = 512
EXPERT_ROWS = 256
MOE_CHUNKS = 4
FAR_BLOCKS_PER_ITER = 2
SUM_ROWS = 16
LOG2E = 1.4426950408889634
VMEM_LIMIT = 56 * 1024 * 1024
HIGHEST = lax.Precision.HIGHEST


def _params(sem, vmem=VMEM_LIMIT):
    return pltpu.CompilerParams(dimension_semantics=sem, vmem_limit_bytes=vmem)


def _mod_kernel(c_ref, w_ref, b_ref, o_ref):
    c = c_ref[...]
    o_ref[...] = jnp.dot(jax.nn.silu(c), w_ref[...], precision=HIGHEST,
                         preferred_element_type=f32) + b_ref[...]


def _mod(c, w_ada, b_ada):
    B, D = c.shape
    n_out = w_ada.shape[1]
    rows = 8
    cp = jnp.zeros((rows, D), f32).at[:B].set(c)
    tn = 1024
    out = pl.pallas_call(
        _mod_kernel,
        grid=(n_out // tn,),
        in_specs=[pl.BlockSpec((rows, D), lambda j: (0, 0)),
                  pl.BlockSpec((D, tn), lambda j: (0, j)),
                  pl.BlockSpec((1, tn), lambda j: (0, j))],
        out_specs=pl.BlockSpec((rows, tn), lambda j: (0, j)),
        out_shape=jax.ShapeDtypeStruct((rows, n_out), f32),
        compiler_params=_params(("arbitrary",)),
        name="mod",
    )(cp, w_ada, b_ada.reshape(1, n_out))
    return out[:B]


def _bucket_thresholds():
    n = np.arange(2 * MOBA_BLOCK)

    def buckets(ft):
        nf = np.maximum(n, 1).astype(ft)
        large = MAX_EXACT + (np.log(nf / ft(MAX_EXACT)) / ft(math.log(MAX_DISTANCE / MAX_EXACT))
                             * ft(N_BUCKETS - MAX_EXACT)).astype(np.int32)
        return np.where(n < MAX_EXACT, n, np.minimum(large, N_BUCKETS - 1))

    b = buckets(np.float32)
    assert np.array_equal(b, buckets(np.float64)) and np.all(np.diff(b) >= 0)
    assert b[-1] == N_BUCKETS - 1 and b[MOBA_BLOCK + 1] == N_BUCKETS - 1
    return [int(np.argmax(b >= j)) for j in range(N_BUCKETS)]


def _bias_kernel(rb_ref, own_ref, adj_ref, *, thresholds):
    h = pl.program_id(0)
    kk = lax.broadcasted_iota(jnp.int32, (MOBA_BLOCK, MOBA_BLOCK), 0)
    qq = lax.broadcasted_iota(jnp.int32, (MOBA_BLOCK, MOBA_BLOCK), 1)
    far = rb_ref[N_BUCKETS - 1, h]

    def table(dist):
        acc = jnp.full(dist.shape, rb_ref[0, h], f32)
        for j in range(1, N_BUCKETS):
            acc = jnp.where(dist >= thresholds[j], rb_ref[j, h], acc)
        return (acc - far) * LOG2E

    own_ref[0] = table(qq - kk)
    adj_ref[0] = table(qq - kk + MOBA_BLOCK)


def _bias_tiles(rel_bias):
    shp = jax.ShapeDtypeStruct((N_HEADS, MOBA_BLOCK, MOBA_BLOCK), f32)
    spec = pl.BlockSpec((1, MOBA_BLOCK, MOBA_BLOCK), lambda h: (h, 0, 0))
    return pl.pallas_call(
        functools.partial(_bias_kernel, thresholds=_bucket_thresholds()),
        grid=(N_HEADS,),
        in_specs=[pl.BlockSpec(memory_space=pltpu.SMEM)],
        out_specs=(spec, spec),
        out_shape=(shp, shp),
        compiler_params=_params(("arbitrary",)),
        name="bias",
    )(rel_bias)


def _ada_norm(x, g, scale, shift):
    y = x * lax.rsqrt(jnp.mean(x * x, axis=-1, keepdims=True) + NORM_EPS)
    return (y * g) * (1.0 + scale) + shift


def _inproj_kernel(x_ref, g_ref, sc_ref, sh_ref, wqT_ref, wk_ref, wvT_ref, wc_ref, wgl_ref,
                   wco_ref, cw_ref, bg_ref,
                   qT_ref, k_ref, vT_ref, km_ref, mc_ref, ga_ref, ubuf):
    tm = x_ref.shape[1]
    cwid = wco_ref.shape[0]
    d = x_ref.shape[2]
    s = pl.program_id(1)
    hb = _ada_norm(x_ref[0], g_ref[...], sc_ref[0], sh_ref[0]).astype(bf16)

    nt = (((1,), (1,)), ((), ()))
    qT_ref[0] = lax.dot_general(wqT_ref[...], hb, nt, preferred_element_type=f32).astype(bf16)
    k = jnp.dot(hb, wk_ref[...], preferred_element_type=f32)
    vT = lax.dot_general(wvT_ref[...], hb, nt, preferred_element_type=f32).astype(bf16)
    for j in range(tm // MOBA_BLOCK):
        rows = slice(j * MOBA_BLOCK, (j + 1) * MOBA_BLOCK)
        k_ref[0, j] = k[rows].astype(bf16)
        vT_ref[0, j] = vT[:, rows]
        km_ref[0, 0, j:j + 1, :] = jnp.mean(k[rows], axis=0, keepdims=True)

    cbx = jnp.dot(hb, wc_ref[...], preferred_element_type=f32)
    u = cbx[:, :cwid] * cbx[:, 2 * cwid:]

    @pl.when(s == 0)
    def _():
        ubuf[0:8, :] = jnp.zeros((8, cwid), f32)

    ubuf[8:8 + tm, :] = u
    cw = cw_ref[...]
    conv = cw[0:1] * ubuf[6:6 + tm, :] + cw[1:2] * ubuf[7:7 + tm, :] + cw[2:3] * u
    ubuf[0:8, :] = ubuf[tm:tm + 8, :]
    cv = (cbx[:, cwid:2 * cwid] * conv).astype(bf16)
    y_conv = jnp.dot(cv, wco_ref[...], preferred_element_type=f32)

    gl = jnp.dot(hb, wgl_ref[...], preferred_element_type=f32) + bg_ref[...]
    ga_ref[0] = jax.nn.sigmoid(gl[:, :d]).astype(bf16)
    mc_ref[0] = (jax.nn.sigmoid(gl[:, d:]) * y_conv).astype(bf16)


def _inproj(x, g, sc, sh, w_in, b_gate, conv_w, w_conv_out):
    B, S, D = x.shape
    tm = ROW_TILE
    nb = S // MOBA_BLOCK
    bpt = tm // MOBA_BLOCK
    aw = ATTN_WIDTH
    cwid = conv_w.shape[1]
    scale = HEAD_DIM ** -0.5 * LOG2E
    wqT = (w_in[:, :aw] * scale).T.astype(bf16)
    wk = w_in[:, aw:2 * aw].astype(bf16)
    wvT = w_in[:, 2 * aw:3 * aw].T.astype(bf16)
    wc = w_in[:, 3 * aw:3 * aw + 3 * cwid].astype(bf16)
    wgl = w_in[:, 3 * aw + 3 * cwid:].astype(bf16)
    wco = w_conv_out.astype(bf16)

    const = lambda shape: pl.BlockSpec(shape, lambda b, s: (0,) * len(shape))
    vec = pl.BlockSpec((1, 1, D), lambda b, s: (b, 0, 0))
    row = lambda width: pl.BlockSpec((1, tm, width), lambda b, s: (b, s, 0))
    outs = pl.pallas_call(
        _inproj_kernel,
        grid=(B, S // tm),
        in_specs=[row(D), const((1, D)), vec, vec,
                  const((aw, D)), const((D, aw)), const((aw, D)), const((D, 3 * cwid)),
                  const((D, 2 * D)), const((cwid, D)), const((CONV_K, cwid)), const((1, 2 * D))],
        out_specs=(pl.BlockSpec((1, aw, tm), lambda b, s: (b, 0, s)),
                   pl.BlockSpec((1, bpt, MOBA_BLOCK, aw), lambda b, s: (b, s, 0, 0)),
                   pl.BlockSpec((1, bpt, aw, MOBA_BLOCK), lambda b, s: (b, s, 0, 0)),
                   pl.BlockSpec((1, 1, bpt, aw), lambda b, s: (b, s, 0, 0)),
                   row(D), row(D)),
        out_shape=(jax.ShapeDtypeStruct((B, aw, S), bf16),
                   jax.ShapeDtypeStruct((B, nb, MOBA_BLOCK, aw), bf16),
                   jax.ShapeDtypeStruct((B, nb, aw, MOBA_BLOCK), bf16),
                   jax.ShapeDtypeStruct((B, S // tm, bpt, aw), f32),
                   jax.ShapeDtypeStruct((B, S, D), bf16),
                   jax.ShapeDtypeStruct((B, S, D), bf16)),
        scratch_shapes=[pltpu.VMEM((tm + 8, cwid), f32)],
        compiler_params=_params(("parallel", "arbitrary")),
        name="inproj",
    )(x, g.reshape(1, D), sc.reshape(B, 1, D), sh.reshape(B, 1, D),
      wqT, wk, wvT, wc, wgl, wco, conv_w, b_gate.reshape(1, 2 * D))
    qT, k4, vT4, km, mc, ga = outs
    return qT, k4, vT4, km.reshape(B, nb, aw), mc, ga


def _attn_kernel(qT_ref, k_ref, vT_ref, km_ref, own_ref, adj_ref, o_ref):
    i = pl.program_id(1)
    bs = MOBA_BLOCK
    nb = km_ref.shape[1]
    n_heads = qT_ref.shape[1] // HEAD_DIM
    heads = range(n_heads)
    blk = lax.broadcasted_iota(jnp.int32, (nb, bs), 0)

    qcat = []
    for h in heads:
        pair, a = divmod(h, LANES // HEAD_DIM)
        qT = qT_ref[0, pair * LANES:(pair + 1) * LANES, :]
        row = lax.broadcasted_iota(jnp.int32, qT.shape, 0)
        km = km_ref[0, :, pair * LANES:(pair + 1) * LANES]
        qa = jnp.where((row >= a * HEAD_DIM) & (row < (a + 1) * HEAD_DIM), qT, jnp.zeros_like(qT))
        g = jnp.dot(km, qa.astype(f32), precision=HIGHEST, preferred_element_type=f32)
        g = jnp.where(blk < i, g, NEG_INF)
        fq = jnp.full((nb, bs), NEG_INF, f32)
        for _ in range(MOBA_TOPK):
            top = jnp.max(g, axis=0, keepdims=True)
            first = jnp.min(jnp.where(g == top, blk, nb), axis=0, keepdims=True)
            pick = blk == first
            fq = jnp.where(pick, 0.0, fq)
            g = jnp.where(pick, -jnp.inf, g)
        fq = jnp.where(blk < i, fq, 0.0).astype(bf16)
        pad = jnp.zeros((LANES - nb, bs), bf16)
        qcat.append(jnp.concatenate([qa, fq, pad], axis=0))

    lane = lax.broadcasted_iota(jnp.int32, (bs, LANES), 1)
    causal = (lax.broadcasted_iota(jnp.int32, (bs, bs), 0) <= lax.broadcasted_iota(jnp.int32, (bs, bs), 1))
    ones_rows = jnp.ones((SUM_ROWS, bs), bf16)

    def step(blocks, state):
        onehots = [jnp.where(lane == n, 1.0, 0.0).astype(bf16) for n, _ in blocks]
        scores = []
        for h in heads:
            pair = h // (LANES // HEAD_DIM)
            cols = slice(pair * LANES, (pair + 1) * LANES)
            scores.append([jnp.dot(jnp.concatenate([k_ref[0, n, :, cols], onehot], axis=1), qcat[h],
                                   preferred_element_type=f32) for (n, _), onehot in zip(blocks, onehots)])
        out = []
        for h in heads:
            sTs = []
            for (n, kind), sT in zip(blocks, scores[h]):
                if kind == "own":
                    sT = jnp.where(causal, own_ref[h] + sT, NEG_INF)
                elif kind == "adj":
                    sT = adj_ref[h] + sT
                sTs.append(sT)
            m_new = None if state is None else state[h][0]
            for sT in sTs:
                top = jnp.max(sT, axis=0, keepdims=True)
                m_new = top if m_new is None else jnp.maximum(m_new, top)
            acc = None if state is None else jnp.exp2(state[h][0] - m_new) * state[h][1]
            for (n, _), sT in zip(blocks, sTs):
                p = jnp.exp2(sT - m_new).astype(bf16)
                v_ext = jnp.concatenate([vT_ref[0, n, h * HEAD_DIM:(h + 1) * HEAD_DIM, :], ones_rows], axis=0)
                pv = jnp.dot(v_ext, p, preferred_element_type=f32)
                acc = pv if acc is None else acc + pv
            out.append((m_new, acc))
        return tuple(out)

    state = lax.cond(i >= 1,
                     lambda: step([(i, "own"), (i - 1, "adj")], None),
                     lambda: step([(i, "own")], None))
    n_far = jnp.maximum(i - 1, 0)
    per = FAR_BLOCKS_PER_ITER
    state = lax.fori_loop(0, n_far // per,
                          lambda it, st: step([(it * per + j, "far") for j in range(per)], st), state)
    done = n_far - n_far % per
    size = per // 2
    while size >= 1:
        take = (n_far % per) & size
        state = lax.cond(take > 0,
                         lambda st, d=done, sz=size: step([(d + j, "far") for j in range(sz)], st),
                         lambda st: st, state)
        done = done + take
        size //= 2
    outT = jnp.concatenate([acc[:HEAD_DIM] / acc[HEAD_DIM:HEAD_DIM + 1] for (_, acc) in state], axis=0)
    o_ref[0] = outT.T.astype(bf16)


def _attention(qT, k4, vT4, km, own, adj):
    B, aw, S = qT.shape
    nb = S // MOBA_BLOCK
    assert MOBA_TOPK < nb <= LANES
    n_heads = aw // HEAD_DIM
    whole = lambda shape: pl.BlockSpec(shape, lambda b, i: (b,) + (0,) * (len(shape) - 1))
    table = pl.BlockSpec((n_heads, MOBA_BLOCK, MOBA_BLOCK), lambda b, i: (0, 0, 0))
    return pl.pallas_call(
        _attn_kernel,
        grid=(B, nb),
        in_specs=[pl.BlockSpec((1, aw, MOBA_BLOCK), lambda b, i: (b, 0, i)),
                  whole((1, nb, MOBA_BLOCK, aw)), whole((1, nb, aw, MOBA_BLOCK)), whole((1, nb, aw)),
                  table, table],
        out_specs=pl.BlockSpec((1, MOBA_BLOCK, aw), lambda b, i: (b, i, 0)),
        out_shape=jax.ShapeDtypeStruct((B, S, aw), bf16),
        compiler_params=_params(("parallel", "arbitrary")),
        name="attn",
    )(qT, k4, vT4, km, own, adj)


def _post_kernel(a_ref, mc_ref, ga_ref, x_ref, g1_ref, sc_ref, sh_ref, ng_ref,
                 wao_ref, wo_ref, wr_ref, br_ref, x1_ref, h2p_ref, route_ref):
    tm = x_ref.shape[1]
    d = x_ref.shape[2]
    ya = jnp.dot(a_ref[0], wao_ref[...], preferred_element_type=f32)
    merged = ga_ref[0].astype(f32) * ya + mc_ref[0].astype(f32)
    z = jnp.dot(merged.astype(bf16), wo_ref[...], preferred_element_type=f32)
    x1 = x_ref[0] + g1_ref[0] * z
    x1_ref[0] = x1
    h2 = _ada_norm(x1, ng_ref[...], sc_ref[0], sh_ref[0])

    bits = pltpu.bitcast(h2.astype(bf16).astype(f32), jnp.uint32)
    h2p_ref[...] = (bits[:, :d // 2] >> 16) | (bits[:, d // 2:] & jnp.uint32(0xFFFF0000))

    logits = jnp.dot(h2, wr_ref[...], precision=HIGHEST, preferred_element_type=f32) + br_ref[...]
    lane = lax.broadcasted_iota(jnp.int32, (tm, LANES), 1)
    ninf = -jnp.inf
    gl = jnp.where(lane < N_GROUPS, logits, ninf)
    gmax = jnp.max(gl, axis=1, keepdims=True)
    g_sel = jnp.min(jnp.where(gl == gmax, lane, LANES), axis=1, keepdims=True)
    g_w = 1.0 / jnp.sum(jnp.exp(gl - gmax), axis=1, keepdims=True)
    off = lane - (N_GROUPS + EXPERTS_PER_GROUP * g_sel)
    in_group = (off >= 0) & (off < EXPERTS_PER_GROUP)
    el = jnp.where(in_group, logits, ninf)
    ex = jnp.exp(el - jnp.max(el, axis=1, keepdims=True))
    ep = jnp.where(in_group, ex / jnp.sum(ex, axis=1, keepdims=True), -1.0)
    p1 = jnp.max(ep, axis=1, keepdims=True)
    i1 = jnp.min(jnp.where(ep == p1, lane, LANES), axis=1, keepdims=True)
    ep2 = jnp.where(lane == i1, -1.0, ep)
    p2 = jnp.max(ep2, axis=1, keepdims=True)
    i2 = jnp.min(jnp.where(ep2 == p2, lane, LANES), axis=1, keepdims=True)
    tot = p1 + p2
    w1 = p1 / tot * g_w
    w2 = p2 / tot * g_w
    e1 = (i1 - N_GROUPS).astype(f32)
    e2 = (i2 - N_GROUPS).astype(f32)
    route_ref[...] = jnp.where(lane == 0, e1, jnp.where(lane == 1, e2,
                               jnp.where(lane == 2, w1, jnp.where(lane == 3, w2, 0.0))))


def _post(a, mc, ga, x, g1, sc2, sh2, ng, w_attn_out, w_o, w_rg, b_rg, w_re, b_re):
    B, S, D = x.shape
    tm = ROW_TILE
    aw = a.shape[2]
    nt = S // tm
    wr = jnp.zeros((D, LANES), f32).at[:, :N_GROUPS].set(w_rg).at[:, N_GROUPS:N_GROUPS + N_EXPERTS].set(w_re)
    br = jnp.zeros((1, LANES), f32).at[0, :N_GROUPS].set(b_rg).at[0, N_GROUPS:N_GROUPS + N_EXPERTS].set(b_re)
    const = lambda shape: pl.BlockSpec(shape, lambda b, s: (0,) * len(shape))
    vec = pl.BlockSpec((1, 1, D), lambda b, s: (b, 0, 0))
    row = lambda width: pl.BlockSpec((1, tm, width), lambda b, s: (b, s, 0))
    flat = lambda width: pl.BlockSpec((tm, width), lambda b, s: (b * nt + s, 0))
    return pl.pallas_call(
        _post_kernel,
        grid=(B, nt),
        in_specs=[row(aw), row(D), row(D), row(D), vec, vec, vec, const((1, D)),
                  const((aw, D)), const((D, D)), const((D, LANES)), const((1, LANES))],
        out_specs=(row(D), flat(D // 2), flat(LANES)),
        out_shape=(jax.ShapeDtypeStruct((B, S, D), f32),
                   jax.ShapeDtypeStruct((B * S, D // 2), jnp.uint32),
                   jax.ShapeDtypeStruct((B * S, LANES), f32)),
        compiler_params=_params(("parallel", "parallel")),
        name="post",
    )(a, mc, ga, x, g1.reshape(B, 1, D), sc2.reshape(B, 1, D), sh2.reshape(B, 1, D), ng.reshape(1, D),
      w_attn_out.astype(bf16), w_o.astype(bf16), wr, br)


def _dispatch_tables(route, n_chunks):
    T = route.shape[0]
    tc = T // n_chunks
    na = tc * TOP_K_EXPERTS
    e = route[:, :TOP_K_EXPERTS].astype(jnp.int32).reshape(n_chunks, na)
    w = route[:, TOP_K_EXPERTS:2 * TOP_K_EXPERTS].reshape(n_chunks, na)
    order = jnp.argsort(e, axis=1, stable=True).astype(jnp.int32)
    stok = order // TOP_K_EXPERTS
    sw = jnp.take_along_axis(w, order, axis=1)
    counts = jnp.sum((e[:, :, None] == jnp.arange(N_EXPERTS)[None, None, :]).astype(jnp.int32), axis=1)
    starts = jnp.cumsum(counts, axis=1) - counts
    nblk = (counts + EXPERT_ROWS - 1) // EXPERT_ROWS
    bend = jnp.cumsum(nblk, axis=1)
    bstart = bend - nblk
    total = bend[:, -1:]
    n_blocks = na // EXPERT_ROWS + N_EXPERTS
    bidx = jnp.arange(n_blocks, dtype=jnp.int32)[None, :]
    be = jnp.sum((bidx[:, :, None] >= bend[:, None, :]).astype(jnp.int32), axis=2)
    be = jnp.minimum(be, N_EXPERTS - 1)
    valid = bidx < total
    be = jnp.where(valid, be, jnp.take_along_axis(be, jnp.maximum(total - 1, 0), axis=1))
    local = bidx - jnp.take_along_axis(bstart, be, axis=1)
    bs = jnp.take_along_axis(starts, be, axis=1) + EXPERT_ROWS * local
    bn = jnp.clip(jnp.take_along_axis(counts, be, axis=1) - EXPERT_ROWS * local, 0, EXPERT_ROWS)
    bn = jnp.where(valid, bn, 0)
    bs = jnp.where(valid, bs, 0)
    flat = lambda t: t.reshape(-1).astype(jnp.int32)
    return flat(be), flat(bs), flat(bn), stok.reshape(n_chunks, 1, na), sw.reshape(n_chunks, 1, na), n_blocks


def _moe_kernel(be_ref, bs_ref, bn_ref, stok_ref, sw_ref, h2p_ref, wg_ref, wu_ref, wd_ref,
                y_hbm, yacc, xs, outs, sem):
    c = pl.program_id(0)
    b = pl.program_id(1)
    nbk = pl.num_programs(1)
    tc = yacc.shape[0]
    half = xs.shape[2]
    n = bn_ref[c * nbk + b]
    st = bs_ref[c * nbk + b]

    @pl.when(b == 0)
    def _():
        yacc[...] = jnp.zeros(yacc.shape, f32)

    @pl.when((b == 0) & (c == 0))
    def _():
        xs[...] = jnp.zeros(xs.shape, jnp.uint32)

    def for_rows(rows_fn):
        full = n // SUBLANES

        def group(g, carry):
            rows_fn(g, list(range(SUBLANES)), st + g * SUBLANES)
            return carry

        def single(j, carry):
            rows_fn(full, [j], st + full * SUBLANES)
            return carry

        lax.fori_loop(0, full, group, 0)
        lax.fori_loop(0, n - full * SUBLANES, single, 0)

    @pl.when(n > 0)
    def _():
        def gather(g, js, i0):
            for j in js:
                xs[g, pl.ds(j, 1), :] = h2p_ref[pl.ds(stok_ref[0, 0, i0 + j], 1), :]

        for_rows(gather)
        packed = xs[...].reshape(xs.shape[0] * SUBLANES, half)
        x_lo = pltpu.bitcast(packed << 16, f32).astype(bf16)
        x_hi = pltpu.bitcast(packed & jnp.uint32(0xFFFF0000), f32).astype(bf16)

        def up(w_ref):
            return (jnp.dot(x_lo, w_ref[0, :half, :], preferred_element_type=f32)
                    + jnp.dot(x_hi, w_ref[0, half:, :], preferred_element_type=f32))

        hidden = (jax.nn.silu(up(wg_ref)) * up(wu_ref)).astype(bf16)
        outs[...] = jnp.dot(hidden, wd_ref[0], preferred_element_type=f32).reshape(outs.shape)

        def scatter(g, js, i0):
            toks = [stok_ref[0, 0, i0 + j] for j in js]
            acc = [yacc[pl.ds(tok, 1), :] for tok in toks]
            for j, tok, a in zip(js, toks, acc):
                yacc[pl.ds(tok, 1), :] = a + sw_ref[0, 0, i0 + j] * outs[g, pl.ds(j, 1), :]

        for_rows(scatter)

    @pl.when(b == nbk - 1)
    def _():
        copy = pltpu.make_async_copy(yacc, y_hbm.at[pl.ds(c * tc, tc), :], sem)
        copy.start()
        copy.wait()


def _moe(route, h2p, w_gate, w_up, w_down):
    T, half = h2p.shape
    D = 2 * half
    de = w_gate.shape[2]
    n_chunks = MOE_CHUNKS
    tc = T // n_chunks
    be, bs, bn, stok, sw, n_blocks = _dispatch_tables(route, n_chunks)
    na = stok.shape[2]
    wmap = lambda c, b, be_r, bs_r, bn_r: (be_r[c * n_blocks + b], 0, 0)
    smem = lambda: pl.BlockSpec((1, 1, na), lambda c, b, *_: (c, 0, 0), memory_space=pltpu.SMEM)
    grid_spec = pltpu.PrefetchScalarGridSpec(
        num_scalar_prefetch=3,
        grid=(n_chunks, n_blocks),
        in_specs=[smem(), smem(),
                  pl.BlockSpec((tc, half), lambda c, b, *_: (c, 0)),
                  pl.BlockSpec((1, D, de), wmap),
                  pl.BlockSpec((1, D, de), wmap),
                  pl.BlockSpec((1, de, D), wmap)],
        out_specs=pl.BlockSpec(memory_space=pl.ANY),
        scratch_shapes=[pltpu.VMEM((tc, D), f32),
                        pltpu.VMEM((EXPERT_ROWS // SUBLANES, SUBLANES, half), jnp.uint32),
                        pltpu.VMEM((EXPERT_ROWS // SUBLANES, SUBLANES, D), f32),
                        pltpu.SemaphoreType.DMA(())],
    )
    return pl.pallas_call(
        _moe_kernel,
        grid_spec=grid_spec,
        out_shape=jax.ShapeDtypeStruct((T, D), f32),
        compiler_params=_params(("arbitrary", "arbitrary")),
        name="moe",
    )(be, bs, bn, stok, sw, h2p, w_gate.astype(bf16), w_up.astype(bf16), w_down.astype(bf16))


def _final_kernel(x1_ref, y_ref, g2_ref, fg_ref, o_ref, *, normalize):
    x2 = x1_ref[0] + g2_ref[0] * y_ref[0]
    if normalize:
        x2 = (x2 * lax.rsqrt(jnp.mean(x2 * x2, axis=-1, keepdims=True) + NORM_EPS)) * fg_ref[...]
    o_ref[0] = x2


def _residual(x1, y, g2, fg, normalize):
    B, S, D = x1.shape
    tm = ROW_TILE
    row = pl.BlockSpec((1, tm, D), lambda b, s: (b, s, 0))
    return pl.pallas_call(
        functools.partial(_final_kernel, normalize=normalize),
        grid=(B, S // tm),
        in_specs=[row, row, pl.BlockSpec((1, 1, D), lambda b, s: (b, 0, 0)),
                  pl.BlockSpec((1, D), lambda b, s: (0, 0))],
        out_specs=row,
        out_shape=jax.ShapeDtypeStruct((B, S, D), f32),
        compiler_params=_params(("parallel", "parallel")),
        name="final",
    )(x1, y.reshape(B, S, D), g2.reshape(B, 1, D), fg.reshape(1, D))


def kernel(x, c, rel_bias, norm1_g, norm2_g, w_ada, b_ada, w_in, b_gate, conv_w, w_attn_out,
           w_conv_out, w_o, w_router_group, b_router_group, w_router_expert, b_router_expert,
           w_exp_gate, w_exp_up, w_exp_down, final_norm_g):
    B, S, D = x.shape
    depth = w_ada.shape[0]
    assert S % ROW_TILE == 0 and ROW_TILE % MOBA_BLOCK == 0 and (B * S) % (MOE_CHUNKS * EXPERT_ROWS) == 0
    own, adj = _bias_tiles(rel_bias)
    for l in range(depth):
        mod = _mod(c, w_ada[l], b_ada[l])
        sh1, sc1, g1, sh2, sc2, g2 = jnp.split(mod, 6, axis=-1)
        qT, k4, vT4, km, mc, ga = _inproj(x, norm1_g[l], sc1, sh1, w_in[l], b_gate[l], conv_w[l],
                                          w_conv_out[l])
        a = _attention(qT, k4, vT4, km, own, adj)
        x1, h2p, route = _post(a, mc, ga, x, g1, sc2, sh2, norm2_g[l], w_attn_out[l], w_o[l],
                               w_router_group[l], b_router_group[l], w_router_expert[l],
                               b_router_expert[l])
        y = _moe(route, h2p, w_exp_gate[l], w_exp_up[l], w_exp_down[l])
        x = _residual(x1, y, g2, final_norm_g, normalize=(l + 1 == depth))
    return x
```

```python
import functools
import math

import numpy as np
import jax
import jax.numpy as jnp
from jax import lax
from jax.experimental import pallas as pl
from jax.experimental.pallas import tpu as pltpu

f32 = jnp.float32
bf16 = jnp.bfloat16

N_HEADS = 8
HEAD_DIM = 64
ATTN_WIDTH = N_HEADS * HEAD_DIM
MOBA_BLOCK = 256
MOBA_TOPK = 3
CONV_K = 3
N_BUCKETS = 32
MAX_EXACT = N_BUCKETS // 2
MAX_DISTANCE = 128
N_GROUPS = 4
EXPERTS_PER_GROUP = 8
N_EXPERTS = N_GROUPS * EXPERTS_PER_GROUP
TOP_K_EXPERTS = 2
NORM_EPS = 1e-6
NEG_INF = -1e30

LANES = 128
SUBLANES = 8
ROW_TILE = 1024
POST_SPLIT = 2
EXPERT_ROWS = 256
MOE_CHUNKS = 4
FAR_BLOCKS_PER_ITER = 2
SUM_ROWS = 16
LOG2E = 1.4426950408889634
VMEM_LIMIT = 56 * 1024 * 1024
HIGHEST = lax.Precision.HIGHEST


def _params(sem, vmem=VMEM_LIMIT):
    return pltpu.CompilerParams(dimension_semantics=sem, vmem_limit_bytes=vmem)


def _mod_kernel(c_ref, w_ref, b_ref, o_ref):
    c = c_ref[...]
    o_ref[...] = jnp.dot(jax.nn.silu(c), w_ref[...], precision=HIGHEST,
                         preferred_element_type=f32) + b_ref[...]


def _mod(c, w_ada, b_ada):
    B, D = c.shape
    n_out = w_ada.shape[1]
    rows = 8
    cp = jnp.zeros((rows, D), f32).at[:B].set(c)
    tn = 1024
    out = pl.pallas_call(
        _mod_kernel,
        grid=(n_out // tn,),
        in_specs=[pl.BlockSpec((rows, D), lambda j: (0, 0)),
                  pl.BlockSpec((D, tn), lambda j: (0, j)),
                  pl.BlockSpec((1, tn), lambda j: (0, j))],
        out_specs=pl.BlockSpec((rows, tn), lambda j: (0, j)),
        out_shape=jax.ShapeDtypeStruct((rows, n_out), f32),
        compiler_params=_params(("arbitrary",)),
        name="mod",
    )(cp, w_ada, b_ada.reshape(1, n_out))
    return out[:B]


def _bucket_thresholds():
    n = np.arange(2 * MOBA_BLOCK)

    def buckets(ft):
        nf = np.maximum(n, 1).astype(ft)
        large = MAX_EXACT + (np.log(nf / ft(MAX_EXACT)) / ft(math.log(MAX_DISTANCE / MAX_EXACT))
                             * ft(N_BUCKETS - MAX_EXACT)).astype(np.int32)
        return np.where(n < MAX_EXACT, n, np.minimum(large, N_BUCKETS - 1))

    b = buckets(np.float32)
    assert np.array_equal(b, buckets(np.float64)) and np.all(np.diff(b) >= 0)
    assert b[-1] == N_BUCKETS - 1 and b[MOBA_BLOCK + 1] == N_BUCKETS - 1
    return [int(np.argmax(b >= j)) for j in range(N_BUCKETS)]


def _bias_kernel(rb_ref, own_ref, adj_ref, *, thresholds):
    h = pl.program_id(0)
    kk = lax.broadcasted_iota(jnp.int32, (MOBA_BLOCK, MOBA_BLOCK), 0)
    qq = lax.broadcasted_iota(jnp.int32, (MOBA_BLOCK, MOBA_BLOCK), 1)
    far = rb_ref[N_BUCKETS - 1, h]

    def table(dist):
        acc = jnp.full(dist.shape, rb_ref[0, h], f32)
        for j in range(1, N_BUCKETS):
            acc = jnp.where(dist >= thresholds[j], rb_ref[j, h], acc)
        return (acc - far) * LOG2E

    own_ref[0] = table(qq - kk)
    adj_ref[0] = table(qq - kk + MOBA_BLOCK)


def _bias_tiles(rel_bias):
    shp = jax.ShapeDtypeStruct((N_HEADS, MOBA_BLOCK, MOBA_BLOCK), f32)
    spec = pl.BlockSpec((1, MOBA_BLOCK, MOBA_BLOCK), lambda h: (h, 0, 0))
    return pl.pallas_call(
        functools.partial(_bias_kernel, thresholds=_bucket_thresholds()),
        grid=(N_HEADS,),
        in_specs=[pl.BlockSpec(memory_space=pltpu.SMEM)],
        out_specs=(spec, spec),
        out_shape=(shp, shp),
        compiler_params=_params(("arbitrary",)),
        name="bias",
    )(rel_bias)


def _ada_norm(x, g, scale, shift):
    y = x * lax.rsqrt(jnp.mean(x * x, axis=-1, keepdims=True) + NORM_EPS)
    return (y * g) * (1.0 + scale) + shift


def _inproj_kernel(x_ref, g_ref, sc_ref, sh_ref, wqT_ref, wk_ref, wvT_ref, wc_ref, wgl_ref,
                   wco_ref, cw_ref, bg_ref,
                   qT_ref, k_ref, vT_ref, km_ref, mc_ref, ga_ref, ubuf):
    tm = x_ref.shape[1]
    cwid = wco_ref.shape[0]
    d = x_ref.shape[2]
    s = pl.program_id(1)
    hb = _ada_norm(x_ref[0], g_ref[...], sc_ref[0], sh_ref[0]).astype(bf16)

    nt = (((1,), (1,)), ((), ()))
    qT_ref[0] = lax.dot_general(wqT_ref[...], hb, nt, preferred_element_type=f32).astype(bf16)
    k = jnp.dot(hb, wk_ref[...], preferred_element_type=f32)
    vT = lax.dot_general(wvT_ref[...], hb, nt, preferred_element_type=f32).astype(bf16)
    for j in range(tm // MOBA_BLOCK):
        rows = slice(j * MOBA_BLOCK, (j + 1) * MOBA_BLOCK)
        k_ref[0, j] = k[rows].astype(bf16)
        vT_ref[0, j] = vT[:, rows]
        km_ref[0, 0, j:j + 1, :] = jnp.mean(k[rows], axis=0, keepdims=True)

    cbx = jnp.dot(hb, wc_ref[...], preferred_element_type=f32)
    u = cbx[:, :cwid] * cbx[:, 2 * cwid:]

    @pl.when(s == 0)
    def _():
        ubuf[0:8, :] = jnp.zeros((8, cwid), f32)

    ubuf[8:8 + tm, :] = u
    cw = cw_ref[...]
    conv = cw[0:1] * ubuf[6:6 + tm, :] + cw[1:2] * ubuf[7:7 + tm, :] + cw[2:3] * u
    ubuf[0:8, :] = ubuf[tm:tm + 8, :]
    cv = (cbx[:, cwid:2 * cwid] * conv).astype(bf16)
    y_conv = jnp.dot(cv, wco_ref[...], preferred_element_type=f32)

    gl = jnp.dot(hb, wgl_ref[...], preferred_element_type=f32) + bg_ref[...]
    ga_ref[0] = jax.nn.sigmoid(gl[:, :d]).astype(bf16)
    mc_ref[0] = (jax.nn.sigmoid(gl[:, d:]) * y_conv).astype(bf16)


def _inproj(x, g, sc, sh, w_in, b_gate, conv_w, w_conv_out):
    B, S, D = x.shape
    tm = ROW_TILE
    nb = S // MOBA_BLOCK
    bpt = tm // MOBA_BLOCK
    aw = ATTN_WIDTH
    cwid = conv_w.shape[1]
    scale = HEAD_DIM ** -0.5 * LOG2E
    wqT = (w_in[:, :aw] * scale).T.astype(bf16)
    wk = w_in[:, aw:2 * aw].astype(bf16)
    wvT = w_in[:, 2 * aw:3 * aw].T.astype(bf16)
    wc = w_in[:, 3 * aw:3 * aw + 3 * cwid].astype(bf16)
    wgl = w_in[:, 3 * aw + 3 * cwid:].astype(bf16)
    wco = w_conv_out.astype(bf16)

    const = lambda shape: pl.BlockSpec(shape, lambda b, s: (0,) * len(shape))
    vec = pl.BlockSpec((1, 1, D), lambda b, s: (b, 0, 0))
    row = lambda width: pl.BlockSpec((1, tm, width), lambda b, s: (b, s, 0))
    outs = pl.pallas_call(
        _inproj_kernel,
        grid=(B, S // tm),
        in_specs=[row(D), const((1, D)), vec, vec,
                  const((aw, D)), const((D, aw)), const((aw, D)), const((D, 3 * cwid)),
                  const((D, 2 * D)), const((cwid, D)), const((CONV_K, cwid)), const((1, 2 * D))],
        out_specs=(pl.BlockSpec((1, aw, tm), lambda b, s: (b, 0, s)),
                   pl.BlockSpec((1, bpt, MOBA_BLOCK, aw), lambda b, s: (b, s, 0, 0)),
                   pl.BlockSpec((1, bpt, aw, MOBA_BLOCK), lambda b, s: (b, s, 0, 0)),
                   pl.BlockSpec((1, 1, bpt, aw), lambda b, s: (b, s, 0, 0)),
                   row(D), row(D)),
        out_shape=(jax.ShapeDtypeStruct((B, aw, S), bf16),
                   jax.ShapeDtypeStruct((B, nb, MOBA_BLOCK, aw), bf16),
                   jax.ShapeDtypeStruct((B, nb, aw, MOBA_BLOCK), bf16),
                   jax.ShapeDtypeStruct((B, S // tm, bpt, aw), f32),
                   jax.ShapeDtypeStruct((B, S, D), bf16),
                   jax.ShapeDtypeStruct((B, S, D), bf16)),
        scratch_shapes=[pltpu.VMEM((tm + 8, cwid), f32)],
        compiler_params=_params(("parallel", "arbitrary")),
        name="inproj",
    )(x, g.reshape(1, D), sc.reshape(B, 1, D), sh.reshape(B, 1, D),
      wqT, wk, wvT, wc, wgl, wco, conv_w, b_gate.reshape(1, 2 * D))
    qT, k4, vT4, km, mc, ga = outs
    return qT, k4, vT4, km.reshape(B, nb, aw), mc, ga


def _attn_kernel(qT_ref, k_ref, vT_ref, km_ref, own_ref, adj_ref, o_ref):
    i = pl.program_id(1)
    bs = MOBA_BLOCK
    nb = km_ref.shape[1]
    n_heads = qT_ref.shape[1] // HEAD_DIM
    heads = range(n_heads)
    blk = lax.broadcasted_iota(jnp.int32, (nb, bs), 0)

    qcat = []
    for h in heads:
        pair, a = divmod(h, LANES // HEAD_DIM)
        qT = qT_ref[0, pair * LANES:(pair + 1) * LANES, :]
        row = lax.broadcasted_iota(jnp.int32, qT.shape, 0)
        qa = jnp.where((row >= a * HEAD_DIM) & (row < (a + 1) * HEAD_DIM), qT, jnp.zeros_like(qT))
        km = km_ref[0, :, pair * LANES:(pair + 1) * LANES]
        km_hi = km.astype(bf16)
        km_mid = (km - km_hi.astype(f32)).astype(bf16)
        km_lo = (km - km_hi.astype(f32) - km_mid.astype(f32)).astype(bf16)
        g = (jnp.dot(km_hi, qa, preferred_element_type=f32)
             + (jnp.dot(km_mid, qa, preferred_element_type=f32) + jnp.dot(km_lo, qa, preferred_element_type=f32)))
        g = jnp.where(blk < i, g, NEG_INF)
        fq = jnp.full((nb, bs), NEG_INF, f32)
        for _ in range(MOBA_TOPK):
            top = jnp.max(g, axis=0, keepdims=True)
            first = jnp.min(jnp.where(g == top, blk, nb), axis=0, keepdims=True)
            pick = blk == first
            fq = jnp.where(pick, 0.0, fq)
            g = jnp.where(pick, -jnp.inf, g)
        fq = jnp.where(blk < i, fq, 0.0).astype(bf16)
        pad = jnp.zeros((LANES - nb, bs), bf16)
        qcat.append(jnp.concatenate([qa, fq, pad], axis=0))

    lane = lax.broadcasted_iota(jnp.int32, (bs, LANES), 1)
    causal = (lax.broadcasted_iota(jnp.int32, (bs, bs), 0) <= lax.broadcasted_iota(jnp.int32, (bs, bs), 1))
    ones_rows = jnp.ones((SUM_ROWS, bs), bf16)

    def step(blocks, state):
        onehots = [jnp.where(lane == n, 1.0, 0.0).astype(bf16) for n, _ in blocks]
        scores = []
        for h in heads:
            pair = h // (LANES // HEAD_DIM)
            cols = slice(pair * LANES, (pair + 1) * LANES)
            scores.append([jnp.dot(jnp.concatenate([k_ref[0, n, :, cols], onehot], axis=1), qcat[h],
                                   preferred_element_type=f32) for (n, _), onehot in zip(blocks, onehots)])
        out = []
        for h in heads:
            sTs = []
            for (n, kind), sT in zip(blocks, scores[h]):
                if kind == "own":
                    sT = jnp.where(causal, own_ref[h] + sT, NEG_INF)
                elif kind == "adj":
                    sT = adj_ref[h] + sT
                sTs.append(sT)
            m_new = None if state is None else state[h][0]
            for sT in sTs:
                top = jnp.max(sT, axis=0, keepdims=True)
                m_new = top if m_new is None else jnp.maximum(m_new, top)
            acc = None if state is None else jnp.exp2(state[h][0] - m_new) * state[h][1]
            for (n, _), sT in zip(blocks, sTs):
                p = jnp.exp2(sT - m_new).astype(bf16)
                v_ext = jnp.concatenate([vT_ref[0, n, h * HEAD_DIM:(h + 1) * HEAD_DIM, :], ones_rows], axis=0)
                pv = jnp.dot(v_ext, p, preferred_element_type=f32)
                acc = pv if acc is None else acc + pv
            out.append((m_new, acc))
        return tuple(out)

    state = lax.cond(i >= 1,
                     lambda: step([(i, "own"), (i - 1, "adj")], None),
                     lambda: step([(i, "own")], None))
    n_far = jnp.maximum(i - 1, 0)
    per = FAR_BLOCKS_PER_ITER
    state = lax.fori_loop(0, n_far // per,
                          lambda it, st: step([(it * per + j, "far") for j in range(per)], st), state)
    done = n_far - n_far % per
    size = per // 2
    while size >= 1:
        take = (n_far % per) & size
        state = lax.cond(take > 0,
                         lambda st, d=done, sz=size: step([(d + j, "far") for j in range(sz)], st),
                         lambda st: st, state)
        done = done + take
        size //= 2
    outT = jnp.concatenate([acc[:HEAD_DIM] / acc[HEAD_DIM:HEAD_DIM + 1] for (_, acc) in state], axis=0)
    o_ref[0] = outT.T.astype(bf16)


def _attention(qT, k4, vT4, km, own, adj):
    B, aw, S = qT.shape
    nb = S // MOBA_BLOCK
    assert MOBA_TOPK < nb <= LANES
    n_heads = aw // HEAD_DIM
    whole = lambda shape: pl.BlockSpec(shape, lambda b, i: (b,) + (0,) * (len(shape) - 1))
    table = pl.BlockSpec((n_heads, MOBA_BLOCK, MOBA_BLOCK), lambda b, i: (0, 0, 0))
    return pl.pallas_call(
        _attn_kernel,
        grid=(B, nb),
        in_specs=[pl.BlockSpec((1, aw, MOBA_BLOCK), lambda b, i: (b, 0, i)),
                  whole((1, nb, MOBA_BLOCK, aw)), whole((1, nb, aw, MOBA_BLOCK)), whole((1, nb, aw)),
                  table, table],
        out_specs=pl.BlockSpec((1, MOBA_BLOCK, aw), lambda b, i: (b, i, 0)),
        out_shape=jax.ShapeDtypeStruct((B, S, aw), bf16),
        compiler_params=_params(("parallel", "arbitrary")),
        name="attn",
    )(qT, k4, vT4, km, own, adj)


def _post_kernel(a_ref, mc_ref, ga_ref, x_ref, g1_ref, sc_ref, sh_ref, ng_ref,
                 wao_ref, wo_ref, wrh_ref, wrl_ref, br_ref, x1_ref, h2p_ref, route_ref):
    tm = x_ref.shape[1]
    d = x_ref.shape[2]
    sub = tm // POST_SPLIT
    lane = lax.broadcasted_iota(jnp.int32, (sub, LANES), 1)
    ninf = -jnp.inf
    for t in range(POST_SPLIT):
        rows = slice(t * sub, (t + 1) * sub)
        ya = jnp.dot(a_ref[0, rows, :], wao_ref[...], preferred_element_type=f32)
        merged = ga_ref[0, rows, :].astype(f32) * ya + mc_ref[0, rows, :].astype(f32)
        z = jnp.dot(merged.astype(bf16), wo_ref[...], preferred_element_type=f32)
        x1 = x_ref[0, rows, :] + g1_ref[0] * z
        x1_ref[0, rows, :] = x1
        h2 = _ada_norm(x1, ng_ref[...], sc_ref[0], sh_ref[0])
        h_hi = h2.astype(bf16)
        h_hi32 = h_hi.astype(f32)

        bits = pltpu.bitcast(h_hi32, jnp.uint32)
        h2p_ref[rows, :] = (bits[:, :d // 2] >> 16) | (bits[:, d // 2:] & jnp.uint32(0xFFFF0000))

        h_lo = (h2 - h_hi32).astype(bf16)
        logits = (jnp.dot(h_hi, wrh_ref[...], preferred_element_type=f32)
                  + (jnp.dot(h_lo, wrh_ref[...], preferred_element_type=f32)
                     + jnp.dot(h_hi, wrl_ref[...], preferred_element_type=f32))) + br_ref[...]
        gl = jnp.where(lane < N_GROUPS, logits, ninf)
        gmax = jnp.max(gl, axis=1, keepdims=True)
        g_sel = jnp.min(jnp.where(gl == gmax, lane, LANES), axis=1, keepdims=True)
        g_w = 1.0 / jnp.sum(jnp.exp(gl - gmax), axis=1, keepdims=True)
        off = lane - (N_GROUPS + EXPERTS_PER_GROUP * g_sel)
        in_group = (off >= 0) & (off < EXPERTS_PER_GROUP)
        el = jnp.where(in_group, logits, ninf)
        ex = jnp.exp(el - jnp.max(el, axis=1, keepdims=True))
        ep = jnp.where(in_group, ex / jnp.sum(ex, axis=1, keepdims=True), -1.0)
        p1 = jnp.max(ep, axis=1, keepdims=True)
        i1 = jnp.min(jnp.where(ep == p1, lane, LANES), axis=1, keepdims=True)
        ep2 = jnp.where(lane == i1, -1.0, ep)
        p2 = jnp.max(ep2, axis=1, keepdims=True)
        i2 = jnp.min(jnp.where(ep2 == p2, lane, LANES), axis=1, keepdims=True)
        tot = p1 + p2
        w1 = p1 / tot * g_w
        w2 = p2 / tot * g_w
        e1 = (i1 - N_GROUPS).astype(f32)
        e2 = (i2 - N_GROUPS).astype(f32)
        route_ref[rows, :] = jnp.where(lane == 0, e1, jnp.where(lane == 1, e2,
                                       jnp.where(lane == 2, w1, jnp.where(lane == 3, w2, 0.0))))


def _post(a, mc, ga, x, g1, sc2, sh2, ng, w_attn_out, w_o, w_rg, b_rg, w_re, b_re):
    B, S, D = x.shape
    tm = ROW_TILE
    aw = a.shape[2]
    nt = S // tm
    wr = jnp.zeros((D, LANES), f32).at[:, :N_GROUPS].set(w_rg).at[:, N_GROUPS:N_GROUPS + N_EXPERTS].set(w_re)
    br = jnp.zeros((1, LANES), f32).at[0, :N_GROUPS].set(b_rg).at[0, N_GROUPS:N_GROUPS + N_EXPERTS].set(b_re)
    wr_hi = wr.astype(bf16)
    wr_lo = (wr - wr_hi.astype(f32)).astype(bf16)
    const = lambda shape: pl.BlockSpec(shape, lambda b, s: (0,) * len(shape))
    vec = pl.BlockSpec((1, 1, D), lambda b, s: (b, 0, 0))
    row = lambda width: pl.BlockSpec((1, tm, width), lambda b, s: (b, s, 0))
    flat = lambda width: pl.BlockSpec((tm, width), lambda b, s: (b * nt + s, 0))
    return pl.pallas_call(
        _post_kernel,
        grid=(B, nt),
        in_specs=[row(aw), row(D), row(D), row(D), vec, vec, vec, const((1, D)),
                  const((aw, D)), const((D, D)), const((D, LANES)), const((D, LANES)), const((1, LANES))],
        out_specs=(row(D), flat(D // 2), flat(LANES)),
        out_shape=(jax.ShapeDtypeStruct((B, S, D), f32),
                   jax.ShapeDtypeStruct((B * S, D // 2), jnp.uint32),
                   jax.ShapeDtypeStruct((B * S, LANES), f32)),
        compiler_params=_params(("parallel", "parallel")),
        name="post",
    )(a, mc, ga, x, g1.reshape(B, 1, D), sc2.reshape(B, 1, D), sh2.reshape(B, 1, D), ng.reshape(1, D),
      w_attn_out.astype(bf16), w_o.astype(bf16), wr_hi, wr_lo, br)


def _dispatch_tables(route, n_chunks):
    T = route.shape[0]
    tc = T // n_chunks
    na = tc * TOP_K_EXPERTS
    e = route[:, :TOP_K_EXPERTS].astype(jnp.int32).reshape(n_chunks, na)
    w = route[:, TOP_K_EXPERTS:2 * TOP_K_EXPERTS].reshape(n_chunks, na)
    order = jnp.argsort(e, axis=1, stable=True).astype(jnp.int32)
    stok = order // TOP_K_EXPERTS
    sw = jnp.take_along_axis(w, order, axis=1)
    counts = jnp.sum((e[:, :, None] == jnp.arange(N_EXPERTS)[None, None, :]).astype(jnp.int32), axis=1)
    starts = jnp.cumsum(counts, axis=1) - counts
    nblk = (counts + EXPERT_ROWS - 1) // EXPERT_ROWS
    bend = jnp.cumsum(nblk, axis=1)
    bstart = bend - nblk
    total = bend[:, -1:]
    n_blocks = na // EXPERT_ROWS + N_EXPERTS
    bidx = jnp.arange(n_blocks, dtype=jnp.int32)[None, :]
    be = jnp.sum((bidx[:, :, None] >= bend[:, None, :]).astype(jnp.int32), axis=2)
    be = jnp.minimum(be, N_EXPERTS - 1)
    valid = bidx < total
    be = jnp.where(valid, be, jnp.take_along_axis(be, jnp.maximum(total - 1, 0), axis=1))
    local = bidx - jnp.take_along_axis(bstart, be, axis=1)
    bs = jnp.take_along_axis(starts, be, axis=1) + EXPERT_ROWS * local
    bn = jnp.clip(jnp.take_along_axis(counts, be, axis=1) - EXPERT_ROWS * local, 0, EXPERT_ROWS)
    bn = jnp.where(valid, bn, 0)
    bs = jnp.where(valid, bs, 0)
    flat = lambda t: t.reshape(-1).astype(jnp.int32)
    return flat(be), flat(bs), flat(bn), stok.reshape(n_chunks, 1, na), sw.reshape(n_chunks, 1, na), n_blocks


def _moe_kernel(be_ref, bs_ref, bn_ref, stok_ref, sw_ref, h2p_ref, wg_ref, wu_ref, wd_ref,
                y_hbm, yacc, xs, outs, sem):
    c = pl.program_id(0)
    b = pl.program_id(1)
    nbk = pl.num_programs(1)
    tc = yacc.shape[0]
    half = xs.shape[2]
    n = bn_ref[c * nbk + b]
    st = bs_ref[c * nbk + b]

    @pl.when(b == 0)
    def _():
        yacc[...] = jnp.zeros(yacc.shape, f32)

    @pl.when((b == 0) & (c == 0))
    def _():
        xs[...] = jnp.zeros(xs.shape, jnp.uint32)

    def for_rows(rows_fn):
        full = n // SUBLANES

        def group(g, carry):
            rows_fn(g, list(range(SUBLANES)), st + g * SUBLANES)
            return carry

        def single(j, carry):
            rows_fn(full, [j], st + full * SUBLANES)
            return carry

        lax.fori_loop(0, full, group, 0)
        lax.fori_loop(0, n - full * SUBLANES, single, 0)

    @pl.when(n > 0)
    def _():
        def gather(g, js, i0):
            for j in js:
                xs[g, pl.ds(j, 1), :] = h2p_ref[pl.ds(stok_ref[0, 0, i0 + j], 1), :]

        for_rows(gather)
        packed = xs[...].reshape(xs.shape[0] * SUBLANES, half)
        x_lo = pltpu.bitcast(packed << 16, f32).astype(bf16)
        x_hi = pltpu.bitcast(packed & jnp.uint32(0xFFFF0000), f32).astype(bf16)

        def up(w_ref):
            return (jnp.dot(x_lo, w_ref[0, :half, :], preferred_element_type=f32)
                    + jnp.dot(x_hi, w_ref[0, half:, :], preferred_element_type=f32))

        hidden = (jax.nn.silu(up(wg_ref)) * up(wu_ref)).astype(bf16)
        outs[...] = jnp.dot(hidden, wd_ref[0], preferred_element_type=f32).reshape(outs.shape)

        def scatter(g, js, i0):
            toks = [stok_ref[0, 0, i0 + j] for j in js]
            acc = [yacc[pl.ds(tok, 1), :] for tok in toks]
            for j, tok, a in zip(js, toks, acc):
                yacc[pl.ds(tok, 1), :] = a + sw_ref[0, 0, i0 + j] * outs[g, pl.ds(j, 1), :]

        for_rows(scatter)

    @pl.when(b == nbk - 1)
    def _():
        copy = pltpu.make_async_copy(yacc, y_hbm.at[pl.ds(c * tc, tc), :], sem)
        copy.start()
        copy.wait()


def _moe(route, h2p, w_gate, w_up, w_down):
    T, half = h2p.shape
    D = 2 * half
    de = w_gate.shape[2]
    n_chunks = MOE_CHUNKS
    tc = T // n_chunks
    be, bs, bn, stok, sw, n_blocks = _dispatch_tables(route, n_chunks)
    na = stok.shape[2]
    wmap = lambda c, b, be_r, bs_r, bn_r: (be_r[c * n_blocks + b], 0, 0)
    smem = lambda: pl.BlockSpec((1, 1, na), lambda c, b, *_: (c, 0, 0), memory_space=pltpu.SMEM)
    grid_spec = pltpu.PrefetchScalarGridSpec(
        num_scalar_prefetch=3,
        grid=(n_chunks, n_blocks),
        in_specs=[smem(), smem(),
                  pl.BlockSpec((tc, half), lambda c, b, *_: (c, 0)),
                  pl.BlockSpec((1, D, de), wmap),
                  pl.BlockSpec((1, D, de), wmap),
                  pl.BlockSpec((1, de, D), wmap)],
        out_specs=pl.BlockSpec(memory_space=pl.ANY),
        scratch_shapes=[pltpu.VMEM((tc, D), f32),
                        pltpu.VMEM((EXPERT_ROWS // SUBLANES, SUBLANES, half), jnp.uint32),
                        pltpu.VMEM((EXPERT_ROWS // SUBLANES, SUBLANES, D), f32),
                        pltpu.SemaphoreType.DMA(())],
    )
    return pl.pallas_call(
        _moe_kernel,
        grid_spec=grid_spec,
        out_shape=jax.ShapeDtypeStruct((T, D), f32),
        compiler_params=_params(("arbitrary", "arbitrary")),
        name="moe",
    )(be, bs, bn, stok, sw, h2p, w_gate.astype(bf16), w_up.astype(bf16), w_down.astype(bf16))


def _final_kernel(x1_ref, y_ref, g2_ref, fg_ref, o_ref, *, normalize):
    x2 = x1_ref[0] + g2_ref[0] * y_ref[0]
    if normalize:
        x2 = (x2 * lax.rsqrt(jnp.mean(x2 * x2, axis=-1, keepdims=True) + NORM_EPS)) * fg_ref[...]
    o_ref[0] = x2


def _residual(x1, y, g2, fg, normalize):
    B, S, D = x1.shape
    tm = ROW_TILE
    row = pl.BlockSpec((1, tm, D), lambda b, s: (b, s, 0))
    return pl.pallas_call(
        functools.partial(_final_kernel, normalize=normalize),
        grid=(B, S // tm),
        in_specs=[row, row, pl.BlockSpec((1, 1, D), lambda b, s: (b, 0, 0)),
                  pl.BlockSpec((1, D), lambda b, s: (0, 0))],
        out_specs=row,
        out_shape=jax.ShapeDtypeStruct((B, S, D), f32),
        compiler_params=_params(("parallel", "parallel")),
        name="final",
    )(x1, y.reshape(B, S, D), g2.reshape(B, 1, D), fg.reshape(1, D))


def kernel(x, c, rel_bias, norm1_g, norm2_g, w_ada, b_ada, w_in, b_gate, conv_w, w_attn_out,
           w_conv_out, w_o, w_router_group, b_router_group, w_router_expert, b_router_expert,
           w_exp_gate, w_exp_up, w_exp_down, final_norm_g):
    B, S, D = x.shape
    depth = w_ada.shape[0]
    assert S % ROW_TILE == 0 and ROW_TILE % MOBA_BLOCK == 0 and (B * S) % (MOE_CHUNKS * EXPERT_ROWS) == 0
    own, adj = _bias_tiles(rel_bias)
    for l in range(depth):
        mod = _mod(c, w_ada[l], b_ada[l])
        sh1, sc1, g1, sh2, sc2, g2 = jnp.split(mod, 6, axis=-1)
        qT, k4, vT4, km, mc, ga = _inproj(x, norm1_g[l], sc1, sh1, w_in[l], b_gate[l], conv_w[l],
                                          w_conv_out[l])
        a = _attention(qT, k4, vT4, km, own, adj)
        x1, h2p, route = _post(a, mc, ga, x, g1, sc2, sh2, norm2_g[l], w_attn_out[l], w_o[l],
                               w_router_group[l], b_router_group[l], w_router_expert[l],
                               b_router_expert[l])
        y = _moe(route, h2p, w_exp_gate[l], w_exp_up[l], w_exp_down[l])
        x = _residual(x1, y, g2, final_norm_g, normalize=(l + 1 == depth))
    return x
```

```python
import functools
import math

import numpy as np
import jax
import jax.numpy as jnp
from jax import lax
from jax.experimental import pallas as pl
from jax.experimental.pallas import tpu as pltpu

f32 = jnp.float32
bf16 = jnp.bfloat16

N_HEADS = 8
HEAD_DIM = 64
ATTN_WIDTH = N_HEADS * HEAD_DIM
MOBA_BLOCK = 256
MOBA_TOPK = 3
CONV_K = 3
N_BUCKETS = 32
MAX_EXACT = N_BUCKETS // 2
MAX_DISTANCE = 128
N_GROUPS = 4
EXPERTS_PER_GROUP = 8
N_EXPERTS = N_GROUPS * EXPERTS_PER_GROUP
TOP_K_EXPERTS = 2
NORM_EPS = 1e-6
NEG_INF = -1e30

LANES = 128
SUBLANES = 8
ROW_TILE = 1024
POST_SPLIT = 2
EXPERT_ROWS = 256
MOE_CHUNKS = 4
FAR_BLOCKS_PER_ITER = 2
SUM_ROWS = 16
LOG2E = 1.4426950408889634
VMEM_LIMIT = 56 * 1024 * 1024
HIGHEST = lax.Precision.HIGHEST


def _params(sem, vmem=VMEM_LIMIT):
    return pltpu.CompilerParams(dimension_semantics=sem, vmem_limit_bytes=vmem)


def _mod_kernel(c_ref, w_ref, b_ref, o_ref):
    c = c_ref[...]
    o_ref[...] = jnp.dot(jax.nn.silu(c), w_ref[...], precision=HIGHEST,
                         preferred_element_type=f32) + b_ref[...]


def _mod(c, w_ada, b_ada):
    B, D = c.shape
    n_out = w_ada.shape[1]
    rows = 8
    cp = jnp.zeros((rows, D), f32).at[:B].set(c)
    tn = 1024
    out = pl.pallas_call(
        _mod_kernel,
        grid=(n_out // tn,),
        in_specs=[pl.BlockSpec((rows, D), lambda j: (0, 0)),
                  pl.BlockSpec((D, tn), lambda j: (0, j)),
                  pl.BlockSpec((1, tn), lambda j: (0, j))],
        out_specs=pl.BlockSpec((rows, tn), lambda j: (0, j)),
        out_shape=jax.ShapeDtypeStruct((rows, n_out), f32),
        compiler_params=_params(("arbitrary",)),
        name="mod",
    )(cp, w_ada, b_ada.reshape(1, n_out))
    return out[:B]


def _bucket_thresholds():
    n = np.arange(2 * MOBA_BLOCK)

    def buckets(ft):
        nf = np.maximum(n, 1).astype(ft)
        large = MAX_EXACT + (np.log(nf / ft(MAX_EXACT)) / ft(math.log(MAX_DISTANCE / MAX_EXACT))
                             * ft(N_BUCKETS - MAX_EXACT)).astype(np.int32)
        return np.where(n < MAX_EXACT, n, np.minimum(large, N_BUCKETS - 1))

    b = buckets(np.float32)
    assert np.array_equal(b, buckets(np.float64)) and np.all(np.diff(b) >= 0)
    assert b[-1] == N_BUCKETS - 1 and b[MOBA_BLOCK + 1] == N_BUCKETS - 1
    return [int(np.argmax(b >= j)) for j in range(N_BUCKETS)]


def _bias_kernel(rb_ref, own_ref, adj_ref, *, thresholds):
    h = pl.program_id(0)
    kk = lax.broadcasted_iota(jnp.int32, (MOBA_BLOCK, MOBA_BLOCK), 0)
    qq = lax.broadcasted_iota(jnp.int32, (MOBA_BLOCK, MOBA_BLOCK), 1)
    far = rb_ref[N_BUCKETS - 1, h]

    def table(dist):
        acc = jnp.full(dist.shape, rb_ref[0, h], f32)
        for j in range(1, N_BUCKETS):
            acc = jnp.where(dist >= thresholds[j], rb_ref[j, h], acc)
        return (acc - far) * LOG2E

    own_ref[0] = table(qq - kk)
    adj_ref[0] = table(qq - kk + MOBA_BLOCK)


def _bias_tiles(rel_bias):
    shp = jax.ShapeDtypeStruct((N_HEADS, MOBA_BLOCK, MOBA_BLOCK), f32)
    spec = pl.BlockSpec((1, MOBA_BLOCK, MOBA_BLOCK), lambda h: (h, 0, 0))
    return pl.pallas_call(
        functools.partial(_bias_kernel, thresholds=_bucket_thresholds()),
        grid=(N_HEADS,),
        in_specs=[pl.BlockSpec(memory_space=pltpu.SMEM)],
        out_specs=(spec, spec),
        out_shape=(shp, shp),
        compiler_params=_params(("arbitrary",)),
        name="bias",
    )(rel_bias)


def _ada_norm(x, g, scale, shift):
    y = x * lax.rsqrt(jnp.mean(x * x, axis=-1, keepdims=True) + NORM_EPS)
    return (y * g) * (1.0 + scale) + shift


def _inproj_kernel(x_ref, g_ref, sc_ref, sh_ref, wqT_ref, wk_ref, wvT_ref, wc_ref, wgl_ref,
                   wco_ref, cw_ref, bg_ref,
                   qT_ref, k_ref, vT_ref, km_ref, mc_ref, ga_ref, ubuf):
    tm = x_ref.shape[1]
    cwid = wco_ref.shape[0]
    d = x_ref.shape[2]
    s = pl.program_id(1)
    hb = _ada_norm(x_ref[0], g_ref[...], sc_ref[0], sh_ref[0]).astype(bf16)

    nt = (((1,), (1,)), ((), ()))
    qT_ref[0] = lax.dot_general(wqT_ref[...], hb, nt, preferred_element_type=f32).astype(bf16)
    k = jnp.dot(hb, wk_ref[...], preferred_element_type=f32)
    vT = lax.dot_general(wvT_ref[...], hb, nt, preferred_element_type=f32).astype(bf16)
    for j in range(tm // MOBA_BLOCK):
        rows = slice(j * MOBA_BLOCK, (j + 1) * MOBA_BLOCK)
        k_ref[0, j] = k[rows].astype(bf16)
        vT_ref[0, j] = vT[:, rows]
        km_ref[0, 0, j:j + 1, :] = jnp.mean(k[rows], axis=0, keepdims=True)

    cbx = jnp.dot(hb, wc_ref[...], preferred_element_type=f32)
    u = cbx[:, :cwid] * cbx[:, 2 * cwid:]

    @pl.when(s == 0)
    def _():
        ubuf[0:8, :] = jnp.zeros((8, cwid), f32)

    ubuf[8:8 + tm, :] = u
    cw = cw_ref[...]
    conv = cw[0:1] * ubuf[6:6 + tm, :] + cw[1:2] * ubuf[7:7 + tm, :] + cw[2:3] * u
    ubuf[0:8, :] = ubuf[tm:tm + 8, :]
    cv = (cbx[:, cwid:2 * cwid] * conv).astype(bf16)
    y_conv = jnp.dot(cv, wco_ref[...], preferred_element_type=f32)

    gl = jnp.dot(hb, wgl_ref[...], preferred_element_type=f32) + bg_ref[...]
    ga_ref[0] = jax.nn.sigmoid(gl[:, :d]).astype(bf16)
    mc_ref[0] = (jax.nn.sigmoid(gl[:, d:]) * y_conv).astype(bf16)


def _inproj(x, g, sc, sh, w_in, b_gate, conv_w, w_conv_out):
    B, S, D = x.shape
    tm = ROW_TILE
    nb = S // MOBA_BLOCK
    bpt = tm // MOBA_BLOCK
    aw = ATTN_WIDTH
    cwid = conv_w.shape[1]
    scale = HEAD_DIM ** -0.5 * LOG2E
    wqT = (w_in[:, :aw] * scale).T.astype(bf16)
    wk = w_in[:, aw:2 * aw].astype(bf16)
    wvT = w_in[:, 2 * aw:3 * aw].T.astype(bf16)
    wc = w_in[:, 3 * aw:3 * aw + 3 * cwid].astype(bf16)
    wgl = w_in[:, 3 * aw + 3 * cwid:].astype(bf16)
    wco = w_conv_out.astype(bf16)

    const = lambda shape: pl.BlockSpec(shape, lambda b, s: (0,) * len(shape))
    vec = pl.BlockSpec((1, 1, D), lambda b, s: (b, 0, 0))
    row = lambda width: pl.BlockSpec((1, tm, width), lambda b, s: (b, s, 0))
    outs = pl.pallas_call(
        _inproj_kernel,
        grid=(B, S // tm),
        in_specs=[row(D), const((1, D)), vec, vec,
                  const((aw, D)), const((D, aw)), const((aw, D)), const((D, 3 * cwid)),
                  const((D, 2 * D)), const((cwid, D)), const((CONV_K, cwid)), const((1, 2 * D))],
        out_specs=(pl.BlockSpec((1, aw, tm), lambda b, s: (b, 0, s)),
                   pl.BlockSpec((1, bpt, MOBA_BLOCK, aw), lambda b, s: (b, s, 0, 0)),
                   pl.BlockSpec((1, bpt, aw, MOBA_BLOCK), lambda b, s: (b, s, 0, 0)),
                   pl.BlockSpec((1, 1, bpt, aw), lambda b, s: (b, s, 0, 0)),
                   row(D), row(D)),
        out_shape=(jax.ShapeDtypeStruct((B, aw, S), bf16),
                   jax.ShapeDtypeStruct((B, nb, MOBA_BLOCK, aw), bf16),
                   jax.ShapeDtypeStruct((B, nb, aw, MOBA_BLOCK), bf16),
                   jax.ShapeDtypeStruct((B, S // tm, bpt, aw), f32),
                   jax.ShapeDtypeStruct((B, S, D), bf16),
                   jax.ShapeDtypeStruct((B, S, D), bf16)),
        scratch_shapes=[pltpu.VMEM((tm + 8, cwid), f32)],
        compiler_params=_params(("parallel", "arbitrary")),
        name="inproj",
    )(x, g.reshape(1, D), sc.reshape(B, 1, D), sh.reshape(B, 1, D),
      wqT, wk, wvT, wc, wgl, wco, conv_w, b_gate.reshape(1, 2 * D))
    qT, k4, vT4, km, mc, ga = outs
    return qT, k4, vT4, km.reshape(B, nb, aw), mc, ga


def _attn_kernel(qT_ref, k_ref, vT_ref, km_ref, own_ref, adj_ref, *side_refs):
    n_side = len(side_refs) // 2
    o_ref = side_refs[n_side]
    for src, dst in zip(side_refs[:n_side], side_refs[n_side + 1:]):
        dst[...] = src[...].astype(bf16)

    i = pl.program_id(1)
    bs = MOBA_BLOCK
    nb = km_ref.shape[1]
    n_heads = qT_ref.shape[1] // HEAD_DIM
    heads = range(n_heads)
    blk = lax.broadcasted_iota(jnp.int32, (nb, bs), 0)

    qcat = []
    for h in heads:
        pair, a = divmod(h, LANES // HEAD_DIM)
        qT = qT_ref[0, pair * LANES:(pair + 1) * LANES, :]
        row = lax.broadcasted_iota(jnp.int32, qT.shape, 0)
        qa = jnp.where((row >= a * HEAD_DIM) & (row < (a + 1) * HEAD_DIM), qT, jnp.zeros_like(qT))
        km = km_ref[0, :, pair * LANES:(pair + 1) * LANES]
        km_hi = km.astype(bf16)
        km_mid = (km - km_hi.astype(f32)).astype(bf16)
        km_lo = (km - km_hi.astype(f32) - km_mid.astype(f32)).astype(bf16)
        g = (jnp.dot(km_hi, qa, preferred_element_type=f32)
             + (jnp.dot(km_mid, qa, preferred_element_type=f32) + jnp.dot(km_lo, qa, preferred_element_type=f32)))
        g = jnp.where(blk < i, g, NEG_INF)
        fq = jnp.full((nb, bs), NEG_INF, f32)
        for _ in range(MOBA_TOPK):
            top = jnp.max(g, axis=0, keepdims=True)
            first = jnp.min(jnp.where(g == top, blk, nb), axis=0, keepdims=True)
            pick = blk == first
            fq = jnp.where(pick, 0.0, fq)
            g = jnp.where(pick, -jnp.inf, g)
        fq = jnp.where(blk < i, fq, 0.0).astype(bf16)
        pad = jnp.zeros((LANES - nb, bs), bf16)
        qcat.append(jnp.concatenate([qa, fq, pad], axis=0))

    lane = lax.broadcasted_iota(jnp.int32, (bs, LANES), 1)
    causal = (lax.broadcasted_iota(jnp.int32, (bs, bs), 0) <= lax.broadcasted_iota(jnp.int32, (bs, bs), 1))
    ones_rows = jnp.ones((SUM_ROWS, bs), bf16)

    def step(blocks, state):
        onehots = [jnp.where(lane == n, 1.0, 0.0).astype(bf16) for n, _ in blocks]
        scores = []
        for h in heads:
            pair = h // (LANES // HEAD_DIM)
            cols = slice(pair * LANES, (pair + 1) * LANES)
            scores.append([jnp.dot(jnp.concatenate([k_ref[0, n, :, cols], onehot], axis=1), qcat[h],
                                   preferred_element_type=f32) for (n, _), onehot in zip(blocks, onehots)])
        out = []
        for h in heads:
            sTs = []
            for (n, kind), sT in zip(blocks, scores[h]):
                if kind == "own":
                    sT = jnp.where(causal, own_ref[h] + sT, NEG_INF)
                elif kind == "adj":
                    sT = adj_ref[h] + sT
                sTs.append(sT)
            m_new = None if state is None else state[h][0]
            for sT in sTs:
                top = jnp.max(sT, axis=0, keepdims=True)
                m_new = top if m_new is None else jnp.maximum(m_new, top)
            acc = None if state is None else jnp.exp2(state[h][0] - m_new) * state[h][1]
            for (n, _), sT in zip(blocks, sTs):
                p = jnp.exp2(sT - m_new).astype(bf16)
                v_ext = jnp.concatenate([vT_ref[0, n, h * HEAD_DIM:(h + 1) * HEAD_DIM, :], ones_rows], axis=0)
                pv = jnp.dot(v_ext, p, preferred_element_type=f32)
                acc = pv if acc is None else acc + pv
            out.append((m_new, acc))
        return tuple(out)

    state = lax.cond(i >= 1,
                     lambda: step([(i, "own"), (i - 1, "adj")], None),
                     lambda: step([(i, "own")], None))
    n_far = jnp.maximum(i - 1, 0)
    per = FAR_BLOCKS_PER_ITER
    state = lax.fori_loop(0, n_far // per,
                          lambda it, st: step([(it * per + j, "far") for j in range(per)], st), state)
    done = n_far - n_far % per
    size = per // 2
    while size >= 1:
        take = (n_far % per) & size
        state = lax.cond(take > 0,
                         lambda st, d=done, sz=size: step([(d + j, "far") for j in range(sz)], st),
                         lambda st: st, state)
        done = done + take
        size //= 2
    outT = jnp.concatenate([acc[:HEAD_DIM] / acc[HEAD_DIM:HEAD_DIM + 1] for (_, acc) in state], axis=0)
    o_ref[0] = outT.T.astype(bf16)


def _attention(qT, k4, vT4, km, own, adj, side_weights):
    B, aw, S = qT.shape
    nb = S // MOBA_BLOCK
    assert MOBA_TOPK < nb <= LANES
    n_heads = aw // HEAD_DIM
    steps = B * nb
    whole = lambda shape: pl.BlockSpec(shape, lambda b, i: (b,) + (0,) * (len(shape) - 1))
    table = pl.BlockSpec((n_heads, MOBA_BLOCK, MOBA_BLOCK), lambda b, i: (0, 0, 0))
    slabs = [w.reshape(steps, -1, w.shape[-1]) for w in side_weights]
    slab_specs = [pl.BlockSpec((1,) + w.shape[1:], lambda b, i: (b * nb + i, 0, 0)) for w in slabs]
    outs = pl.pallas_call(
        _attn_kernel,
        grid=(B, nb),
        in_specs=[pl.BlockSpec((1, aw, MOBA_BLOCK), lambda b, i: (b, 0, i)),
                  whole((1, nb, MOBA_BLOCK, aw)), whole((1, nb, aw, MOBA_BLOCK)), whole((1, nb, aw)),
                  table, table] + slab_specs,
        out_specs=[pl.BlockSpec((1, MOBA_BLOCK, aw), lambda b, i: (b, i, 0))] + slab_specs,
        out_shape=[jax.ShapeDtypeStruct((B, S, aw), bf16)]
                  + [jax.ShapeDtypeStruct(w.shape, bf16) for w in slabs],
        compiler_params=_params(("parallel", "arbitrary")),
        name="attn",
    )(qT, k4, vT4, km, own, adj, *slabs)
    return outs[0], [o.reshape(w.shape) for o, w in zip(outs[1:], side_weights)]


def _post_kernel(a_ref, mc_ref, ga_ref, x_ref, g1_ref, sc_ref, sh_ref, ng_ref,
                 wao_ref, wo_ref, wrh_ref, wrl_ref, br_ref, x1_ref, h2p_ref, route_ref):
    tm = x_ref.shape[1]
    d = x_ref.shape[2]
    sub = tm // POST_SPLIT
    lane = lax.broadcasted_iota(jnp.int32, (sub, LANES), 1)
    ninf = -jnp.inf
    for t in range(POST_SPLIT):
        rows = slice(t * sub, (t + 1) * sub)
        ya = jnp.dot(a_ref[0, rows, :], wao_ref[...], preferred_element_type=f32)
        merged = ga_ref[0, rows, :].astype(f32) * ya + mc_ref[0, rows, :].astype(f32)
        z = jnp.dot(merged.astype(bf16), wo_ref[...], preferred_element_type=f32)
        x1 = x_ref[0, rows, :] + g1_ref[0] * z
        x1_ref[0, rows, :] = x1
        h2 = _ada_norm(x1, ng_ref[...], sc_ref[0], sh_ref[0])
        h_hi = h2.astype(bf16)
        h_hi32 = h_hi.astype(f32)

        bits = pltpu.bitcast(h_hi32, jnp.uint32)
        h2p_ref[rows, :] = (bits[:, :d // 2] >> 16) | (bits[:, d // 2:] & jnp.uint32(0xFFFF0000))

        h_lo = (h2 - h_hi32).astype(bf16)
        logits = (jnp.dot(h_hi, wrh_ref[...], preferred_element_type=f32)
                  + (jnp.dot(h_lo, wrh_ref[...], preferred_element_type=f32)
                     + jnp.dot(h_hi, wrl_ref[...], preferred_element_type=f32))) + br_ref[...]
        gl = jnp.where(lane < N_GROUPS, logits, ninf)
        gmax = jnp.max(gl, axis=1, keepdims=True)
        g_sel = jnp.min(jnp.where(gl == gmax, lane, LANES), axis=1, keepdims=True)
        g_w = 1.0 / jnp.sum(jnp.exp(gl - gmax), axis=1, keepdims=True)
        off = lane - (N_GROUPS + EXPERTS_PER_GROUP * g_sel)
        in_group = (off >= 0) & (off < EXPERTS_PER_GROUP)
        el = jnp.where(in_group, logits, ninf)
        ex = jnp.exp(el - jnp.max(el, axis=1, keepdims=True))
        ep = jnp.where(in_group, ex / jnp.sum(ex, axis=1, keepdims=True), -1.0)
        p1 = jnp.max(ep, axis=1, keepdims=True)
        i1 = jnp.min(jnp.where(ep == p1, lane, LANES), axis=1, keepdims=True)
        ep2 = jnp.where(lane == i1, -1.0, ep)
        p2 = jnp.max(ep2, axis=1, keepdims=True)
        i2 = jnp.min(jnp.where(ep2 == p2, lane, LANES), axis=1, keepdims=True)
        tot = p1 + p2
        w1 = p1 / tot * g_w
        w2 = p2 / tot * g_w
        e1 = (i1 - N_GROUPS).astype(f32)
        e2 = (i2 - N_GROUPS).astype(f32)
        route_ref[rows, :] = jnp.where(lane == 0, e1, jnp.where(lane == 1, e2,
                                       jnp.where(lane == 2, w1, jnp.where(lane == 3, w2, 0.0))))


def _post(a, mc, ga, x, g1, sc2, sh2, ng, w_attn_out, w_o, w_rg, b_rg, w_re, b_re):
    B, S, D = x.shape
    tm = ROW_TILE
    aw = a.shape[2]
    nt = S // tm
    wr = jnp.zeros((D, LANES), f32).at[:, :N_GROUPS].set(w_rg).at[:, N_GROUPS:N_GROUPS + N_EXPERTS].set(w_re)
    br = jnp.zeros((1, LANES), f32).at[0, :N_GROUPS].set(b_rg).at[0, N_GROUPS:N_GROUPS + N_EXPERTS].set(b_re)
    wr_hi = wr.astype(bf16)
    wr_lo = (wr - wr_hi.astype(f32)).astype(bf16)
    const = lambda shape: pl.BlockSpec(shape, lambda b, s: (0,) * len(shape))
    vec = pl.BlockSpec((1, 1, D), lambda b, s: (b, 0, 0))
    row = lambda width: pl.BlockSpec((1, tm, width), lambda b, s: (b, s, 0))
    flat = lambda width: pl.BlockSpec((tm, width), lambda b, s: (b * nt + s, 0))
    return pl.pallas_call(
        _post_kernel,
        grid=(B, nt),
        in_specs=[row(aw), row(D), row(D), row(D), vec, vec, vec, const((1, D)),
                  const((aw, D)), const((D, D)), const((D, LANES)), const((D, LANES)), const((1, LANES))],
        out_specs=(row(D), flat(D // 2), flat(LANES)),
        out_shape=(jax.ShapeDtypeStruct((B, S, D), f32),
                   jax.ShapeDtypeStruct((B * S, D // 2), jnp.uint32),
                   jax.ShapeDtypeStruct((B * S, LANES), f32)),
        compiler_params=_params(("parallel", "parallel")),
        name="post",
    )(a, mc, ga, x, g1.reshape(B, 1, D), sc2.reshape(B, 1, D), sh2.reshape(B, 1, D), ng.reshape(1, D),
      w_attn_out.astype(bf16), w_o.astype(bf16), wr_hi, wr_lo, br)


def _dispatch_tables(route, n_chunks):
    T = route.shape[0]
    tc = T // n_chunks
    na = tc * TOP_K_EXPERTS
    e = route[:, :TOP_K_EXPERTS].astype(jnp.int32).reshape(n_chunks, na)
    w = route[:, TOP_K_EXPERTS:2 * TOP_K_EXPERTS].reshape(n_chunks, na)
    order = jnp.argsort(e, axis=1, stable=True).astype(jnp.int32)
    stok = order // TOP_K_EXPERTS
    sw = jnp.take_along_axis(w, order, axis=1)
    counts = jnp.sum((e[:, :, None] == jnp.arange(N_EXPERTS)[None, None, :]).astype(jnp.int32), axis=1)
    starts = jnp.cumsum(counts, axis=1) - counts
    nblk = (counts + EXPERT_ROWS - 1) // EXPERT_ROWS
    bend = jnp.cumsum(nblk, axis=1)
    bstart = bend - nblk
    total = bend[:, -1:]
    n_blocks = na // EXPERT_ROWS + N_EXPERTS
    bidx = jnp.arange(n_blocks, dtype=jnp.int32)[None, :]
    be = jnp.sum((bidx[:, :, None] >= bend[:, None, :]).astype(jnp.int32), axis=2)
    be = jnp.minimum(be, N_EXPERTS - 1)
    valid = bidx < total
    be = jnp.where(valid, be, jnp.take_along_axis(be, jnp.maximum(total - 1, 0), axis=1))
    local = bidx - jnp.take_along_axis(bstart, be, axis=1)
    bs = jnp.take_along_axis(starts, be, axis=1) + EXPERT_ROWS * local
    bn = jnp.clip(jnp.take_along_axis(counts, be, axis=1) - EXPERT_ROWS * local, 0, EXPERT_ROWS)
    bn = jnp.where(valid, bn, 0)
    bs = jnp.where(valid, bs, 0)
    flat = lambda t: t.reshape(-1).astype(jnp.int32)
    return flat(be), flat(bs), flat(bn), stok.reshape(n_chunks, 1, na), sw.reshape(n_chunks, 1, na), n_blocks


def _moe_kernel(be_ref, bs_ref, bn_ref, stok_ref, sw_ref, h2p_ref, wg_ref, wu_ref, wd_ref,
                x1_ref, g2_ref, fg_ref, o_ref, yacc, xs, outs, *, n_blocks, normalize):
    c = pl.program_id(0)
    b = pl.program_id(1)
    half = xs.shape[2]
    blk = c * n_blocks + jnp.minimum(b, n_blocks - 1)
    n = jnp.where(b < n_blocks, bn_ref[blk], 0)
    st = bs_ref[blk]

    @pl.when(b == 0)
    def _():
        yacc[...] = jnp.zeros(yacc.shape, f32)

    @pl.when((b == 0) & (c == 0))
    def _():
        xs[...] = jnp.zeros(xs.shape, jnp.uint32)

    def for_rows(rows_fn):
        full = n // SUBLANES

        def group(g, carry):
            rows_fn(g, list(range(SUBLANES)), st + g * SUBLANES)
            return carry

        def single(j, carry):
            rows_fn(full, [j], st + full * SUBLANES)
            return carry

        lax.fori_loop(0, full, group, 0)
        lax.fori_loop(0, n - full * SUBLANES, single, 0)

    @pl.when(n > 0)
    def _():
        def gather(g, js, i0):
            for j in js:
                xs[g, pl.ds(j, 1), :] = h2p_ref[pl.ds(stok_ref[0, 0, i0 + j], 1), :]

        for_rows(gather)
        packed = xs[...].reshape(xs.shape[0] * SUBLANES, half)
        x_lo = pltpu.bitcast(packed << 16, f32).astype(bf16)
        x_hi = pltpu.bitcast(packed & jnp.uint32(0xFFFF0000), f32).astype(bf16)

        def up(w_ref):
            return (jnp.dot(x_lo, w_ref[0, :half, :], preferred_element_type=f32)
                    + jnp.dot(x_hi, w_ref[0, half:, :], preferred_element_type=f32))

        hidden = (jax.nn.silu(up(wg_ref)) * up(wu_ref)).astype(bf16)
        outs[...] = jnp.dot(hidden, wd_ref[0], preferred_element_type=f32).reshape(outs.shape)

        def scatter(g, js, i0):
            toks = [stok_ref[0, 0, i0 + j] for j in js]
            acc = [yacc[pl.ds(tok, 1), :] for tok in toks]
            for j, tok, a in zip(js, toks, acc):
                yacc[pl.ds(tok, 1), :] = a + sw_ref[0, 0, i0 + j] * outs[g, pl.ds(j, 1), :]

        for_rows(scatter)

    @pl.when(b >= n_blocks)
    def _():
        rows = o_ref.shape[0]
        start = pl.multiple_of((b - n_blocks) * rows, rows)
        x2 = x1_ref[...] + g2_ref[0] * yacc[pl.ds(start, rows), :]
        if normalize:
            x2 = (x2 * lax.rsqrt(jnp.mean(x2 * x2, axis=-1, keepdims=True) + NORM_EPS)) * fg_ref[...]
        o_ref[...] = x2


def _moe(route, h2p, w_gate, w_up, w_down, x1, g2, fg, normalize):
    T, half = h2p.shape
    B, S, D = x1.shape
    de = w_gate.shape[2]
    n_chunks = MOE_CHUNKS
    tc = T // n_chunks
    fin_rows = ROW_TILE // 2
    n_fin = tc // fin_rows
    assert n_chunks % B == 0 and tc % fin_rows == 0
    be, bs, bn, stok, sw, n_blocks = _dispatch_tables(route, n_chunks)
    na = stok.shape[2]
    wmap = lambda c, b, be_r, bs_r, bn_r: (be_r[c * n_blocks + jnp.minimum(b, n_blocks - 1)], 0, 0)
    smem = lambda: pl.BlockSpec((1, 1, na), lambda c, b, *_: (c, 0, 0), memory_space=pltpu.SMEM)
    tile = pl.BlockSpec((fin_rows, D), lambda c, b, *_: (c * n_fin + jnp.maximum(b - n_blocks, 0), 0))
    grid_spec = pltpu.PrefetchScalarGridSpec(
        num_scalar_prefetch=3,
        grid=(n_chunks, n_blocks + n_fin),
        in_specs=[smem(), smem(),
                  pl.BlockSpec((tc, half), lambda c, b, *_: (c, 0)),
                  pl.BlockSpec((1, D, de), wmap),
                  pl.BlockSpec((1, D, de), wmap),
                  pl.BlockSpec((1, de, D), wmap),
                  tile,
                  pl.BlockSpec((1, 1, D), lambda c, b, *_: (c // (n_chunks // B), 0, 0)),
                  pl.BlockSpec((1, D), lambda c, b, *_: (0, 0))],
        out_specs=tile,
        scratch_shapes=[pltpu.VMEM((tc, D), f32),
                        pltpu.VMEM((EXPERT_ROWS // SUBLANES, SUBLANES, half), jnp.uint32),
                        pltpu.VMEM((EXPERT_ROWS // SUBLANES, SUBLANES, D), f32)],
    )
    out = pl.pallas_call(
        functools.partial(_moe_kernel, n_blocks=n_blocks, normalize=normalize),
        grid_spec=grid_spec,
        out_shape=jax.ShapeDtypeStruct((T, D), f32),
        compiler_params=_params(("arbitrary", "arbitrary")),
        name="moe",
    )(be, bs, bn, stok, sw, h2p, w_gate, w_up, w_down,
      x1.reshape(T, D), g2.reshape(B, 1, D), fg.reshape(1, D))
    return out.reshape(B, S, D)


def kernel(x, c, rel_bias, norm1_g, norm2_g, w_ada, b_ada, w_in, b_gate, conv_w, w_attn_out,
           w_conv_out, w_o, w_router_group, b_router_group, w_router_expert, b_router_expert,
           w_exp_gate, w_exp_up, w_exp_down, final_norm_g):
    B, S, D = x.shape
    depth = w_ada.shape[0]
    assert S % ROW_TILE == 0 and ROW_TILE % MOBA_BLOCK == 0 and (B * S) % (MOE_CHUNKS * EXPERT_ROWS) == 0
    own, adj = _bias_tiles(rel_bias)
    for l in range(depth):
        mod = _mod(c, w_ada[l], b_ada[l])
        sh1, sc1, g1, sh2, sc2, g2 = jnp.split(mod, 6, axis=-1)
        qT, k4, vT4, km, mc, ga = _inproj(x, norm1_g[l], sc1, sh1, w_in[l], b_gate[l], conv_w[l],
                                          w_conv_out[l])
        a, expert_w = _attention(qT, k4, vT4, km, own, adj, [w_exp_gate[l], w_exp_up[l], w_exp_down[l]])
        x1, h2p, route = _post(a, mc, ga, x, g1, sc2, sh2, norm2_g[l], w_attn_out[l], w_o[l],
                               w_router_group[l], b_router_group[l], w_router_expert[l],
                               b_router_expert[l])
        x = _moe(route, h2p, *expert_w, x1, g2, final_norm_g, normalize=(l + 1 == depth))
    return x
```

```python
import functools
import math

import numpy as np
import jax
import jax.numpy as jnp
from jax import lax
from jax.experimental import pallas as pl
from jax.experimental.pallas import tpu as pltpu

f32 = jnp.float32
bf16 = jnp.bfloat16

N_HEADS = 8
HEAD_DIM = 64
ATTN_WIDTH = N_HEADS * HEAD_DIM
MOBA_BLOCK = 256
MOBA_TOPK = 3
CONV_K = 3
N_BUCKETS = 32
MAX_EXACT = N_BUCKETS // 2
MAX_DISTANCE = 128
N_GROUPS = 4
EXPERTS_PER_GROUP = 8
N_EXPERTS = N_GROUPS * EXPERTS_PER_GROUP
TOP_K_EXPERTS = 2
NORM_EPS = 1e-6
NEG_INF = -1e30

LANES = 128
SUBLANES = 8
ROW_TILE = 1024
POST_SPLIT = 2
EXPERT_ROWS = 256
ROW_UNROLL = 8
MOE_CHUNKS = 4
FAR_BLOCKS_PER_ITER = 2
FAR_SUBSTEPS = 2
SUM_ROWS = 16
LOG2E = 1.4426950408889634
VMEM_LIMIT = 56 * 1024 * 1024
HIGHEST = lax.Precision.HIGHEST


def _params(sem, vmem=VMEM_LIMIT):
    return pltpu.CompilerParams(dimension_semantics=sem, vmem_limit_bytes=vmem)


def _mod_kernel(c_ref, w_ref, b_ref, o_ref):
    c = c_ref[...]
    o_ref[...] = jnp.dot(jax.nn.silu(c), w_ref[...], precision=HIGHEST,
                         preferred_element_type=f32) + b_ref[...]


def _mod(c, w_ada, b_ada):
    B, D = c.shape
    n_out = w_ada.shape[1]
    rows = 8
    cp = jnp.zeros((rows, D), f32).at[:B].set(c)
    tn = 1024
    out = pl.pallas_call(
        _mod_kernel,
        grid=(n_out // tn,),
        in_specs=[pl.BlockSpec((rows, D), lambda j: (0, 0)),
                  pl.BlockSpec((D, tn), lambda j: (0, j)),
                  pl.BlockSpec((1, tn), lambda j: (0, j))],
        out_specs=pl.BlockSpec((rows, tn), lambda j: (0, j)),
        out_shape=jax.ShapeDtypeStruct((rows, n_out), f32),
        compiler_params=_params(("arbitrary",)),
        name="mod",
    )(cp, w_ada, b_ada.reshape(1, n_out))
    return out[:B]


def _bucket_thresholds():
    n = np.arange(2 * MOBA_BLOCK)

    def buckets(ft):
        nf = np.maximum(n, 1).astype(ft)
        large = MAX_EXACT + (np.log(nf / ft(MAX_EXACT)) / ft(math.log(MAX_DISTANCE / MAX_EXACT))
                             * ft(N_BUCKETS - MAX_EXACT)).astype(np.int32)
        return np.where(n < MAX_EXACT, n, np.minimum(large, N_BUCKETS - 1))

    b = buckets(np.float32)
    assert np.array_equal(b, buckets(np.float64)) and np.all(np.diff(b) >= 0)
    assert b[-1] == N_BUCKETS - 1 and b[MOBA_BLOCK + 1] == N_BUCKETS - 1
    return [int(np.argmax(b >= j)) for j in range(N_BUCKETS)]


def _bias_kernel(rb_ref, own_ref, adj_ref, *, thresholds):
    h = pl.program_id(0)
    kk = lax.broadcasted_iota(jnp.int32, (MOBA_BLOCK, MOBA_BLOCK), 0)
    qq = lax.broadcasted_iota(jnp.int32, (MOBA_BLOCK, MOBA_BLOCK), 1)
    far = rb_ref[N_BUCKETS - 1, h]

    def table(dist):
        acc = jnp.full(dist.shape, rb_ref[0, h], f32)
        for j in range(1, N_BUCKETS):
            acc = jnp.where(dist >= thresholds[j], rb_ref[j, h], acc)
        return (acc - far) * LOG2E

    own_ref[0] = table(qq - kk)
    adj_ref[0] = table(qq - kk + MOBA_BLOCK)


def _bias_tiles(rel_bias):
    shp = jax.ShapeDtypeStruct((N_HEADS, MOBA_BLOCK, MOBA_BLOCK), f32)
    spec = pl.BlockSpec((1, MOBA_BLOCK, MOBA_BLOCK), lambda h: (h, 0, 0))
    return pl.pallas_call(
        functools.partial(_bias_kernel, thresholds=_bucket_thresholds()),
        grid=(N_HEADS,),
        in_specs=[pl.BlockSpec(memory_space=pltpu.SMEM)],
        out_specs=(spec, spec),
        out_shape=(shp, shp),
        compiler_params=_params(("arbitrary",)),
        name="bias",
    )(rel_bias)


def _ada_norm(x, g, scale, shift):
    y = x * lax.rsqrt(jnp.mean(x * x, axis=-1, keepdims=True) + NORM_EPS)
    return (y * g) * (1.0 + scale) + shift


def _inproj_kernel(x_ref, g_ref, sc_ref, sh_ref, wqT_ref, wk_ref, wvT_ref, wc_ref, wgl_ref,
                   wco_ref, cw_ref, bg_ref,
                   qT_ref, k_ref, vT_ref, km_ref, mc_ref, ga_ref, ubuf):
    tm = x_ref.shape[1]
    cwid = wco_ref.shape[0]
    d = x_ref.shape[2]
    s = pl.program_id(1)
    hb = _ada_norm(x_ref[0], g_ref[...], sc_ref[0], sh_ref[0]).astype(bf16)

    nt = (((1,), (1,)), ((), ()))
    qT_ref[0] = lax.dot_general(wqT_ref[...], hb, nt, preferred_element_type=f32).astype(bf16)
    k = jnp.dot(hb, wk_ref[...], preferred_element_type=f32)
    vT = lax.dot_general(wvT_ref[...], hb, nt, preferred_element_type=f32).astype(bf16)
    for j in range(tm // MOBA_BLOCK):
        rows = slice(j * MOBA_BLOCK, (j + 1) * MOBA_BLOCK)
        k_ref[0, j] = k[rows].astype(bf16)
        vT_ref[0, j] = vT[:, rows]
        km_ref[0, 0, j:j + 1, :] = jnp.mean(k[rows], axis=0, keepdims=True)

    cbx = jnp.dot(hb, wc_ref[...], preferred_element_type=f32)
    u = cbx[:, :cwid] * cbx[:, 2 * cwid:]

    @pl.when(s == 0)
    def _():
        ubuf[0:8, :] = jnp.zeros((8, cwid), f32)

    ubuf[8:8 + tm, :] = u
    cw = cw_ref[...]
    conv = cw[0:1] * ubuf[6:6 + tm, :] + cw[1:2] * ubuf[7:7 + tm, :] + cw[2:3] * u
    ubuf[0:8, :] = ubuf[tm:tm + 8, :]
    cv = (cbx[:, cwid:2 * cwid] * conv).astype(bf16)
    y_conv = jnp.dot(cv, wco_ref[...], preferred_element_type=f32)

    gl = jnp.dot(hb, wgl_ref[...], preferred_element_type=f32) + bg_ref[...]
    ga_ref[0] = jax.nn.sigmoid(gl[:, :d]).astype(bf16)
    mc_ref[0] = (jax.nn.sigmoid(gl[:, d:]) * y_conv).astype(bf16)


def _inproj(x, g, sc, sh, w_in, b_gate, conv_w, w_conv_out):
    B, S, D = x.shape
    tm = ROW_TILE
    nb = S // MOBA_BLOCK
    bpt = tm // MOBA_BLOCK
    aw = ATTN_WIDTH
    cwid = conv_w.shape[1]
    scale = HEAD_DIM ** -0.5 * LOG2E
    wqT = (w_in[:, :aw] * scale).T.astype(bf16)
    wk = w_in[:, aw:2 * aw].astype(bf16)
    wvT = w_in[:, 2 * aw:3 * aw].T.astype(bf16)
    wc = w_in[:, 3 * aw:3 * aw + 3 * cwid].astype(bf16)
    wgl = w_in[:, 3 * aw + 3 * cwid:].astype(bf16)
    wco = w_conv_out.astype(bf16)

    const = lambda shape: pl.BlockSpec(shape, lambda b, s: (0,) * len(shape))
    vec = pl.BlockSpec((1, 1, D), lambda b, s: (b, 0, 0))
    row = lambda width: pl.BlockSpec((1, tm, width), lambda b, s: (b, s, 0))
    outs = pl.pallas_call(
        _inproj_kernel,
        grid=(B, S // tm),
        in_specs=[row(D), const((1, D)), vec, vec,
                  const((aw, D)), const((D, aw)), const((aw, D)), const((D, 3 * cwid)),
                  const((D, 2 * D)), const((cwid, D)), const((CONV_K, cwid)), const((1, 2 * D))],
        out_specs=(pl.BlockSpec((1, aw, tm), lambda b, s: (b, 0, s)),
                   pl.BlockSpec((1, bpt, MOBA_BLOCK, aw), lambda b, s: (b, s, 0, 0)),
                   pl.BlockSpec((1, bpt, aw, MOBA_BLOCK), lambda b, s: (b, s, 0, 0)),
                   pl.BlockSpec((1, 1, bpt, aw), lambda b, s: (b, s, 0, 0)),
                   row(D), row(D)),
        out_shape=(jax.ShapeDtypeStruct((B, aw, S), bf16),
                   jax.ShapeDtypeStruct((B, nb, MOBA_BLOCK, aw), bf16),
                   jax.ShapeDtypeStruct((B, nb, aw, MOBA_BLOCK), bf16),
                   jax.ShapeDtypeStruct((B, S // tm, bpt, aw), f32),
                   jax.ShapeDtypeStruct((B, S, D), bf16),
                   jax.ShapeDtypeStruct((B, S, D), bf16)),
        scratch_shapes=[pltpu.VMEM((tm + 8, cwid), f32)],
        compiler_params=_params(("parallel", "arbitrary")),
        name="inproj",
    )(x, g.reshape(1, D), sc.reshape(B, 1, D), sh.reshape(B, 1, D),
      wqT, wk, wvT, wc, wgl, wco, conv_w, b_gate.reshape(1, 2 * D))
    qT, k4, vT4, km, mc, ga = outs
    return qT, k4, vT4, km.reshape(B, nb, aw), mc, ga


def _attn_kernel(qT_ref, k_ref, vT_ref, km_ref, own_ref, adj_ref, *side_refs):
    n_side = len(side_refs) // 2
    o_ref = side_refs[n_side]
    for src, dst in zip(side_refs[:n_side], side_refs[n_side + 1:]):
        dst[...] = src[...].astype(bf16)

    i = pl.program_id(1)
    bs = MOBA_BLOCK
    nb = km_ref.shape[1]
    n_heads = qT_ref.shape[1] // HEAD_DIM
    heads = range(n_heads)
    blk = lax.broadcasted_iota(jnp.int32, (nb, bs), 0)

    qcat = []
    for h in heads:
        pair, a = divmod(h, LANES // HEAD_DIM)
        qT = qT_ref[0, pair * LANES:(pair + 1) * LANES, :]
        row = lax.broadcasted_iota(jnp.int32, qT.shape, 0)
        qa = jnp.where((row >= a * HEAD_DIM) & (row < (a + 1) * HEAD_DIM), qT, jnp.zeros_like(qT))
        km = km_ref[0, :, pair * LANES:(pair + 1) * LANES]
        km_hi = km.astype(bf16)
        km_mid = (km - km_hi.astype(f32)).astype(bf16)
        km_lo = (km - km_hi.astype(f32) - km_mid.astype(f32)).astype(bf16)
        g = (jnp.dot(km_hi, qa, preferred_element_type=f32)
             + (jnp.dot(km_mid, qa, preferred_element_type=f32) + jnp.dot(km_lo, qa, preferred_element_type=f32)))
        g = jnp.where(blk < i, g, NEG_INF)
        fq = jnp.full((nb, bs), NEG_INF, f32)
        for _ in range(MOBA_TOPK):
            top = jnp.max(g, axis=0, keepdims=True)
            first = jnp.min(jnp.where(g == top, blk, nb), axis=0, keepdims=True)
            pick = blk == first
            fq = jnp.where(pick, 0.0, fq)
            g = jnp.where(pick, -jnp.inf, g)
        fq = jnp.where(blk < i, fq, 0.0).astype(bf16)
        pad = jnp.zeros((LANES - nb, bs), bf16)
        qcat.append(jnp.concatenate([qa, fq, pad], axis=0))

    lane = lax.broadcasted_iota(jnp.int32, (bs, LANES), 1)
    causal = (lax.broadcasted_iota(jnp.int32, (bs, bs), 0) <= lax.broadcasted_iota(jnp.int32, (bs, bs), 1))
    ones_rows = jnp.ones((SUM_ROWS, bs), bf16)

    def step(blocks, state):
        onehots = [jnp.where(lane == n, 1.0, 0.0).astype(bf16) for n, _ in blocks]
        scores = []
        for h in heads:
            pair = h // (LANES // HEAD_DIM)
            cols = slice(pair * LANES, (pair + 1) * LANES)
            scores.append([jnp.dot(jnp.concatenate([k_ref[0, n, :, cols], onehot], axis=1), qcat[h],
                                   preferred_element_type=f32) for (n, _), onehot in zip(blocks, onehots)])
        out = []
        for h in heads:
            sTs = []
            for (n, kind), sT in zip(blocks, scores[h]):
                if kind == "own":
                    sT = jnp.where(causal, own_ref[h] + sT, NEG_INF)
                elif kind == "adj":
                    sT = adj_ref[h] + sT
                sTs.append(sT)
            m_new = None if state is None else state[h][0]
            for sT in sTs:
                top = jnp.max(sT, axis=0, keepdims=True)
                m_new = top if m_new is None else jnp.maximum(m_new, top)
            acc = None if state is None else jnp.exp2(state[h][0] - m_new) * state[h][1]
            for (n, _), sT in zip(blocks, sTs):
                p = jnp.exp2(sT - m_new).astype(bf16)
                v_ext = jnp.concatenate([vT_ref[0, n, h * HEAD_DIM:(h + 1) * HEAD_DIM, :], ones_rows], axis=0)
                pv = jnp.dot(v_ext, p, preferred_element_type=f32)
                acc = pv if acc is None else acc + pv
            out.append((m_new, acc))
        return tuple(out)

    state = lax.cond(i >= 1,
                     lambda: step([(i, "own"), (i - 1, "adj")], None),
                     lambda: step([(i, "own")], None))
    n_far = jnp.maximum(i - 1, 0)
    per = FAR_BLOCKS_PER_ITER
    def far_iter(it, st):
        for u in range(FAR_SUBSTEPS):
            st = step([((it * FAR_SUBSTEPS + u) * per + j, "far") for j in range(per)], st)
        return st

    state = lax.fori_loop(0, n_far // (per * FAR_SUBSTEPS), far_iter, state)
    per = per * FAR_SUBSTEPS
    done = n_far - n_far % per
    size = per // 2
    while size >= 1:
        take = (n_far % per) & size
        state = lax.cond(take > 0,
                         lambda st, d=done, sz=size: step([(d + j, "far") for j in range(sz)], st),
                         lambda st: st, state)
        done = done + take
        size //= 2
    outT = jnp.concatenate([acc[:HEAD_DIM] / acc[HEAD_DIM:HEAD_DIM + 1] for (_, acc) in state], axis=0)
    o_ref[0] = outT.T.astype(bf16)


def _attention(qT, k4, vT4, km, own, adj, side_weights):
    B, aw, S = qT.shape
    nb = S // MOBA_BLOCK
    assert MOBA_TOPK < nb <= LANES
    n_heads = aw // HEAD_DIM
    steps = B * nb
    whole = lambda shape: pl.BlockSpec(shape, lambda b, i: (b,) + (0,) * (len(shape) - 1))
    table = pl.BlockSpec((n_heads, MOBA_BLOCK, MOBA_BLOCK), lambda b, i: (0, 0, 0))
    slabs = [w.reshape(steps, -1, w.shape[-1]) for w in side_weights]
    slab_specs = [pl.BlockSpec((1,) + w.shape[1:], lambda b, i: (b * nb + i, 0, 0)) for w in slabs]
    outs = pl.pallas_call(
        _attn_kernel,
        grid=(B, nb),
        in_specs=[pl.BlockSpec((1, aw, MOBA_BLOCK), lambda b, i: (b, 0, i)),
                  whole((1, nb, MOBA_BLOCK, aw)), whole((1, nb, aw, MOBA_BLOCK)), whole((1, nb, aw)),
                  table, table] + slab_specs,
        out_specs=[pl.BlockSpec((1, MOBA_BLOCK, aw), lambda b, i: (b, i, 0))] + slab_specs,
        out_shape=[jax.ShapeDtypeStruct((B, S, aw), bf16)]
                  + [jax.ShapeDtypeStruct(w.shape, bf16) for w in slabs],
        compiler_params=_params(("parallel", "arbitrary")),
        name="attn",
    )(qT, k4, vT4, km, own, adj, *slabs)
    return outs[0], [o.reshape(w.shape) for o, w in zip(outs[1:], side_weights)]


def _post_kernel(a_ref, mc_ref, ga_ref, x_ref, g1_ref, sc_ref, sh_ref, ng_ref,
                 wao_ref, wo_ref, wrh_ref, wrl_ref, br_ref, x1_ref, h2p_ref, route_ref):
    tm = x_ref.shape[1]
    d = x_ref.shape[2]
    sub = tm // POST_SPLIT
    lane = lax.broadcasted_iota(jnp.int32, (sub, LANES), 1)
    ninf = -jnp.inf
    for t in range(POST_SPLIT):
        rows = slice(t * sub, (t + 1) * sub)
        ya = jnp.dot(a_ref[0, rows, :], wao_ref[...], preferred_element_type=f32)
        merged = ga_ref[0, rows, :].astype(f32) * ya + mc_ref[0, rows, :].astype(f32)
        z = jnp.dot(merged.astype(bf16), wo_ref[...], preferred_element_type=f32)
        x1 = x_ref[0, rows, :] + g1_ref[0] * z
        x1_ref[0, rows, :] = x1
        h2 = _ada_norm(x1, ng_ref[...], sc_ref[0], sh_ref[0])
        h_hi = h2.astype(bf16)
        h_hi32 = h_hi.astype(f32)

        bits = pltpu.bitcast(h_hi32, jnp.uint32)
        packed = (bits[:, :d // 2] >> 16) | (bits[:, d // 2:] & jnp.uint32(0xFFFF0000))
        chunks = packed.shape[1] // LANES
        for j in range(chunks):
            h2p_ref[pl.ds(t * sub * chunks + j, sub, stride=chunks), :] = packed[:, j * LANES:(j + 1) * LANES]

        h_lo = (h2 - h_hi32).astype(bf16)
        logits = (jnp.dot(h_hi, wrh_ref[...], preferred_element_type=f32)
                  + (jnp.dot(h_lo, wrh_ref[...], preferred_element_type=f32)
                     + jnp.dot(h_hi, wrl_ref[...], preferred_element_type=f32))) + br_ref[...]
        gl = jnp.where(lane < N_GROUPS, logits, ninf)
        gmax = jnp.max(gl, axis=1, keepdims=True)
        g_sel = jnp.min(jnp.where(gl == gmax, lane, LANES), axis=1, keepdims=True)
        g_w = 1.0 / jnp.sum(jnp.exp(gl - gmax), axis=1, keepdims=True)
        off = lane - (N_GROUPS + EXPERTS_PER_GROUP * g_sel)
        in_group = (off >= 0) & (off < EXPERTS_PER_GROUP)
        el = jnp.where(in_group, logits, ninf)
        ex = jnp.exp(el - jnp.max(el, axis=1, keepdims=True))
        ep = jnp.where(in_group, ex / jnp.sum(ex, axis=1, keepdims=True), -1.0)
        p1 = jnp.max(ep, axis=1, keepdims=True)
        i1 = jnp.min(jnp.where(ep == p1, lane, LANES), axis=1, keepdims=True)
        ep2 = jnp.where(lane == i1, -1.0, ep)
        p2 = jnp.max(ep2, axis=1, keepdims=True)
        i2 = jnp.min(jnp.where(ep2 == p2, lane, LANES), axis=1, keepdims=True)
        tot = p1 + p2
        w1 = p1 / tot * g_w
        w2 = p2 / tot * g_w
        e1 = (i1 - N_GROUPS).astype(f32)
        e2 = (i2 - N_GROUPS).astype(f32)
        route_ref[rows, :] = jnp.where(lane == 0, e1, jnp.where(lane == 1, e2,
                                       jnp.where(lane == 2, w1, jnp.where(lane == 3, w2, 0.0))))


def _post(a, mc, ga, x, g1, sc2, sh2, ng, w_attn_out, w_o, w_rg, b_rg, w_re, b_re):
    B, S, D = x.shape
    tm = ROW_TILE
    aw = a.shape[2]
    nt = S // tm
    wr = jnp.zeros((D, LANES), f32).at[:, :N_GROUPS].set(w_rg).at[:, N_GROUPS:N_GROUPS + N_EXPERTS].set(w_re)
    br = jnp.zeros((1, LANES), f32).at[0, :N_GROUPS].set(b_rg).at[0, N_GROUPS:N_GROUPS + N_EXPERTS].set(b_re)
    wr_hi = wr.astype(bf16)
    wr_lo = (wr - wr_hi.astype(f32)).astype(bf16)
    const = lambda shape: pl.BlockSpec(shape, lambda b, s: (0,) * len(shape))
    vec = pl.BlockSpec((1, 1, D), lambda b, s: (b, 0, 0))
    row = lambda width: pl.BlockSpec((1, tm, width), lambda b, s: (b, s, 0))
    flat = lambda width, rows_per_token=1: pl.BlockSpec((tm * rows_per_token, width), lambda b, s: (b * nt + s, 0))
    words = D // 2 // LANES
    return pl.pallas_call(
        _post_kernel,
        grid=(B, nt),
        in_specs=[row(aw), row(D), row(D), row(D), vec, vec, vec, const((1, D)),
                  const((aw, D)), const((D, D)), const((D, LANES)), const((D, LANES)), const((1, LANES))],
        out_specs=(row(D), flat(LANES, words), flat(LANES)),
        out_shape=(jax.ShapeDtypeStruct((B, S, D), f32),
                   jax.ShapeDtypeStruct((B * S * words, LANES), jnp.uint32),
                   jax.ShapeDtypeStruct((B * S, LANES), f32)),
        compiler_params=_params(("parallel", "parallel")),
        name="post",
    )(a, mc, ga, x, g1.reshape(B, 1, D), sc2.reshape(B, 1, D), sh2.reshape(B, 1, D), ng.reshape(1, D),
      w_attn_out.astype(bf16), w_o.astype(bf16), wr_hi, wr_lo, br)


def _dispatch_tables(route, n_chunks):
    T = route.shape[0]
    tc = T // n_chunks
    na = tc * TOP_K_EXPERTS
    e = route[:, :TOP_K_EXPERTS].astype(jnp.int32).reshape(n_chunks, na)
    w = route[:, TOP_K_EXPERTS:2 * TOP_K_EXPERTS].reshape(n_chunks, na)
    order = jnp.argsort(e, axis=1, stable=True).astype(jnp.int32)
    stok = order // TOP_K_EXPERTS
    sw = jnp.take_along_axis(w, order, axis=1)
    counts = jnp.sum((e[:, :, None] == jnp.arange(N_EXPERTS)[None, None, :]).astype(jnp.int32), axis=1)
    starts = jnp.cumsum(counts, axis=1) - counts
    nblk = (counts + EXPERT_ROWS - 1) // EXPERT_ROWS
    bend = jnp.cumsum(nblk, axis=1)
    bstart = bend - nblk
    total = bend[:, -1:]
    n_blocks = na // EXPERT_ROWS + N_EXPERTS
    bidx = jnp.arange(n_blocks, dtype=jnp.int32)[None, :]
    be = jnp.sum((bidx[:, :, None] >= bend[:, None, :]).astype(jnp.int32), axis=2)
    be = jnp.minimum(be, N_EXPERTS - 1)
    valid = bidx < total
    be = jnp.where(valid, be, jnp.take_along_axis(be, jnp.maximum(total - 1, 0), axis=1))
    local = bidx - jnp.take_along_axis(bstart, be, axis=1)
    bs = jnp.take_along_axis(starts, be, axis=1) + EXPERT_ROWS * local
    bn = jnp.clip(jnp.take_along_axis(counts, be, axis=1) - EXPERT_ROWS * local, 0, EXPERT_ROWS)
    bn = jnp.where(valid, bn, 0)
    bs = jnp.where(valid, bs, 0)
    flat = lambda t: t.reshape(-1).astype(jnp.int32)
    return flat(be), flat(bs), flat(bn), stok.reshape(n_chunks, 1, na), sw.reshape(n_chunks, 1, na), n_blocks


def _moe_kernel(be_ref, bs_ref, bn_ref, stok_ref, sw_ref, h2p_ref, wg_ref, wu_ref, wd_ref,
                x1_ref, g2_ref, fg_ref, o_ref, yacc, xs, outs, *, n_blocks, normalize):
    c = pl.program_id(0)
    b = pl.program_id(1)
    block_rows = outs.shape[0] // SUBLANES
    words = xs.shape[0] // block_rows
    half = words * LANES
    blk = c * n_blocks + jnp.minimum(b, n_blocks - 1)
    n = jnp.where(b < n_blocks, bn_ref[blk], 0)
    st = bs_ref[blk]

    @pl.when(b == 0)
    def _():
        yacc[...] = jnp.zeros(yacc.shape, f32)

    @pl.when((b == 0) & (c == 0))
    def _():
        xs[...] = jnp.zeros(xs.shape, jnp.uint32)

    def for_rows(rows_fn):
        full = n // ROW_UNROLL

        def group(g, carry):
            rows_fn([g * ROW_UNROLL + j for j in range(ROW_UNROLL)], st + g * ROW_UNROLL)
            return carry

        def single(j, carry):
            rows_fn([full * ROW_UNROLL + j], st + full * ROW_UNROLL + j)
            return carry

        lax.fori_loop(0, full, group, 0)
        lax.fori_loop(0, n - full * ROW_UNROLL, single, 0)

    def token_rows(index, rows_per_token):
        return pl.ds(pl.multiple_of(index * rows_per_token, rows_per_token), rows_per_token)

    @pl.when(n > 0)
    def _():
        def gather(rs, i0):
            for k, r in enumerate(rs):
                xs[token_rows(r, words), :] = h2p_ref[token_rows(stok_ref[0, 0, i0 + k], words), :]

        for_rows(gather)
        lo, hi = [], []
        for j in range(words):
            packed = xs[pl.ds(j, block_rows, stride=words), :]
            lo.append(pltpu.bitcast(packed << 16, f32).astype(bf16))
            hi.append(pltpu.bitcast(packed & jnp.uint32(0xFFFF0000), f32).astype(bf16))
        x_lo = jnp.concatenate(lo, axis=1)
        x_hi = jnp.concatenate(hi, axis=1)

        def up(w_ref):
            return (jnp.dot(x_lo, w_ref[0, :half, :], preferred_element_type=f32)
                    + jnp.dot(x_hi, w_ref[0, half:, :], preferred_element_type=f32))

        hidden = (jax.nn.silu(up(wg_ref)) * up(wu_ref)).astype(bf16)
        out = jnp.dot(hidden, wd_ref[0], preferred_element_type=f32)
        for j in range(SUBLANES):
            outs[pl.ds(j, block_rows, stride=SUBLANES), :] = out[:, j * LANES:(j + 1) * LANES]

        def scatter(rs, i0):
            toks = [stok_ref[0, 0, i0 + k] for k in range(len(rs))]
            acc = [yacc[token_rows(tok, SUBLANES), :] for tok in toks]
            for k, (r, tok, a) in enumerate(zip(rs, toks, acc)):
                yacc[token_rows(tok, SUBLANES), :] = a + sw_ref[0, 0, i0 + k] * outs[token_rows(r, SUBLANES), :]

        for_rows(scatter)

    @pl.when(b >= n_blocks)
    def _():
        rows = o_ref.shape[0]
        start = pl.multiple_of((b - n_blocks) * rows * SUBLANES, rows * SUBLANES)
        y = jnp.concatenate([yacc[pl.ds(start + j, rows, stride=SUBLANES), :] for j in range(SUBLANES)], axis=1)
        x2 = x1_ref[...] + g2_ref[0] * y
        if normalize:
            x2 = (x2 * lax.rsqrt(jnp.mean(x2 * x2, axis=-1, keepdims=True) + NORM_EPS)) * fg_ref[...]
        o_ref[...] = x2


def _moe(route, h2p, w_gate, w_up, w_down, x1, g2, fg, normalize):
    B, S, D = x1.shape
    T = B * S
    words = h2p.shape[0] // T
    assert D == words * 2 * LANES == SUBLANES * LANES
    de = w_gate.shape[2]
    n_chunks = MOE_CHUNKS
    tc = T // n_chunks
    fin_rows = ROW_TILE // 2
    n_fin = tc // fin_rows
    assert n_chunks % B == 0 and tc % fin_rows == 0
    be, bs, bn, stok, sw, n_blocks = _dispatch_tables(route, n_chunks)
    na = stok.shape[2]
    wmap = lambda c, b, be_r, bs_r, bn_r: (be_r[c * n_blocks + jnp.minimum(b, n_blocks - 1)], 0, 0)
    smem = lambda: pl.BlockSpec((1, 1, na), lambda c, b, *_: (c, 0, 0), memory_space=pltpu.SMEM)
    tile = pl.BlockSpec((fin_rows, D), lambda c, b, *_: (c * n_fin + jnp.maximum(b - n_blocks, 0), 0))
    grid_spec = pltpu.PrefetchScalarGridSpec(
        num_scalar_prefetch=3,
        grid=(n_chunks, n_blocks + n_fin),
        in_specs=[smem(), smem(),
                  pl.BlockSpec((tc * words, LANES), lambda c, b, *_: (c, 0)),
                  pl.BlockSpec((1, D, de), wmap),
                  pl.BlockSpec((1, D, de), wmap),
                  pl.BlockSpec((1, de, D), wmap),
                  tile,
                  pl.BlockSpec((1, 1, D), lambda c, b, *_: (c // (n_chunks // B), 0, 0)),
                  pl.BlockSpec((1, D), lambda c, b, *_: (0, 0))],
        out_specs=tile,
        scratch_shapes=[pltpu.VMEM((tc * SUBLANES, LANES), f32),
                        pltpu.VMEM((EXPERT_ROWS * words, LANES), jnp.uint32),
                        pltpu.VMEM((EXPERT_ROWS * SUBLANES, LANES), f32)],
    )
    out = pl.pallas_call(
        functools.partial(_moe_kernel, n_blocks=n_blocks, normalize=normalize),
        grid_spec=grid_spec,
        out_shape=jax.ShapeDtypeStruct((T, D), f32),
        compiler_params=_params(("arbitrary", "arbitrary")),
        name="moe",
    )(be, bs, bn, stok, sw, h2p, w_gate, w_up, w_down,
      x1.reshape(T, D), g2.reshape(B, 1, D), fg.reshape(1, D))
    return out.reshape(B, S, D)


def kernel(x, c, rel_bias, norm1_g, norm2_g, w_ada, b_ada, w_in, b_gate, conv_w, w_attn_out,
           w_conv_out, w_o, w_router_group, b_router_group, w_router_expert, b_router_expert,
           w_exp_gate, w_exp_up, w_exp_down, final_norm_g):
    B, S, D = x.shape
    depth = w_ada.shape[0]
    assert S % ROW_TILE == 0 and ROW_TILE % MOBA_BLOCK == 0 and (B * S) % (MOE_CHUNKS * EXPERT_ROWS) == 0
    own, adj = _bias_tiles(rel_bias)
    for l in range(depth):
        mod = _mod(c, w_ada[l], b_ada[l])
        sh1, sc1, g1, sh2, sc2, g2 = jnp.split(mod, 6, axis=-1)
        qT, k4, vT4, km, mc, ga = _inproj(x, norm1_g[l], sc1, sh1, w_in[l], b_gate[l], conv_w[l],
                                          w_conv_out[l])
        a, expert_w = _attention(qT, k4, vT4, km, own, adj, [w_exp_gate[l], w_exp_up[l], w_exp_down[l]])
        x1, h2p, route = _post(a, mc, ga, x, g1, sc2, sh2, norm2_g[l], w_attn_out[l], w_o[l],
                               w_router_group[l], b_router_group[l], w_router_expert[l],
                               b_router_expert[l])
        x = _moe(route, h2p, *expert_w, x1, g2, final_norm_g, normalize=(l + 1 == depth))
    return x
```

```python
import functools
import math

import numpy as np
import jax
import jax.numpy as jnp
from jax import lax
from jax.experimental import pallas as pl
from jax.experimental.pallas import tpu as pltpu

f32 = jnp.float32
bf16 = jnp.bfloat16

N_HEADS = 8
HEAD_DIM = 64
ATTN_WIDTH = N_HEADS * HEAD_DIM
MOBA_BLOCK = 256
MOBA_TOPK = 3
CONV_K = 3
N_BUCKETS = 32
MAX_EXACT = N_BUCKETS // 2
MAX_DISTANCE = 128
N_GROUPS = 4
EXPERTS_PER_GROUP = 8
N_EXPERTS = N_GROUPS * EXPERTS_PER_GROUP
TOP_K_EXPERTS = 2
NORM_EPS = 1e-6
NEG_INF = -1e30

LANES = 128
SUBLANES = 8
ROW_TILE = 1024
POST_SPLIT = 2
EXPERT_ROWS = 256
TAIL_ROWS = 64
ROW_UNROLL = 8
MOE_CHUNKS = 4
FAR_BLOCKS_PER_ITER = 2
FAR_SUBSTEPS = 2
SUM_ROWS = 16
LOG2E = 1.4426950408889634
VMEM_LIMIT = 56 * 1024 * 1024


def _params(sem, vmem=VMEM_LIMIT):
    return pltpu.CompilerParams(dimension_semantics=sem, vmem_limit_bytes=vmem)


def _mod_kernel(c_ref, w_ref, b_ref, o_ref, *, batch):
    act = jax.nn.silu(c_ref[...]).T
    w = w_ref[...]
    rows = [jnp.sum(act[:, r:r + 1] * w, axis=0, keepdims=True) for r in range(batch)]
    rows.append(jnp.zeros((o_ref.shape[0] - batch, w.shape[1]), f32))
    o_ref[...] = jnp.concatenate(rows, axis=0) + b_ref[...]


def _mod(c, w_ada, b_ada):
    B, D = c.shape
    n_out = w_ada.shape[1]
    rows = 8
    cp = jnp.zeros((rows, D), f32).at[:B].set(c)
    tn = 1024
    out = pl.pallas_call(
        functools.partial(_mod_kernel, batch=B),
        grid=(n_out // tn,),
        in_specs=[pl.BlockSpec((rows, D), lambda j: (0, 0)),
                  pl.BlockSpec((D, tn), lambda j: (0, j)),
                  pl.BlockSpec((1, tn), lambda j: (0, j))],
        out_specs=pl.BlockSpec((rows, tn), lambda j: (0, j)),
        out_shape=jax.ShapeDtypeStruct((rows, n_out), f32),
        compiler_params=_params(("arbitrary",)),
        name="mod",
    )(cp, w_ada, b_ada.reshape(1, n_out))
    return out[:B]


def _bucket_thresholds():
    n = np.arange(2 * MOBA_BLOCK)

    def buckets(ft):
        nf = np.maximum(n, 1).astype(ft)
        large = MAX_EXACT + (np.log(nf / ft(MAX_EXACT)) / ft(math.log(MAX_DISTANCE / MAX_EXACT))
                             * ft(N_BUCKETS - MAX_EXACT)).astype(np.int32)
        return np.where(n < MAX_EXACT, n, np.minimum(large, N_BUCKETS - 1))

    b = buckets(np.float32)
    assert np.array_equal(b, buckets(np.float64)) and np.all(np.diff(b) >= 0)
    assert b[-1] == N_BUCKETS - 1 and b[MOBA_BLOCK + 1] == N_BUCKETS - 1
    return [int(np.argmax(b >= j)) for j in range(N_BUCKETS)]


def _bias_kernel(rb_ref, own_ref, adj_ref, *, thresholds):
    h = pl.program_id(0)
    kk = lax.broadcasted_iota(jnp.int32, (MOBA_BLOCK, MOBA_BLOCK), 0)
    qq = lax.broadcasted_iota(jnp.int32, (MOBA_BLOCK, MOBA_BLOCK), 1)
    far = rb_ref[N_BUCKETS - 1, h]

    def table(dist):
        acc = jnp.full(dist.shape, rb_ref[0, h], f32)
        for j in range(1, N_BUCKETS):
            acc = jnp.where(dist >= thresholds[j], rb_ref[j, h], acc)
        return (acc - far) * LOG2E

    own_ref[0] = table(qq - kk)
    adj_ref[0] = table(qq - kk + MOBA_BLOCK)


def _bias_tiles(rel_bias):
    shp = jax.ShapeDtypeStruct((N_HEADS, MOBA_BLOCK, MOBA_BLOCK), f32)
    spec = pl.BlockSpec((1, MOBA_BLOCK, MOBA_BLOCK), lambda h: (h, 0, 0))
    return pl.pallas_call(
        functools.partial(_bias_kernel, thresholds=_bucket_thresholds()),
        grid=(N_HEADS,),
        in_specs=[pl.BlockSpec(memory_space=pltpu.SMEM)],
        out_specs=(spec, spec),
        out_shape=(shp, shp),
        compiler_params=_params(("arbitrary",)),
        name="bias",
    )(rel_bias)


def _ada_norm(x, g, scale, shift):
    y = x * lax.rsqrt(jnp.mean(x * x, axis=-1, keepdims=True) + NORM_EPS)
    return (y * g) * (1.0 + scale) + shift


def _inproj_kernel(x_ref, g_ref, sc_ref, sh_ref, wqT_ref, wk_ref, wvT_ref, wc_ref, wgl_ref,
                   wco_ref, cw_ref, bg_ref,
                   qT_ref, k_ref, vT_ref, km_ref, mc_ref, ga_ref, ubuf):
    tm = x_ref.shape[1]
    cwid = wco_ref.shape[0]
    d = x_ref.shape[2]
    s = pl.program_id(1)
    hb = _ada_norm(x_ref[0], g_ref[...], sc_ref[0], sh_ref[0]).astype(bf16)

    nt = (((1,), (1,)), ((), ()))
    qT_ref[0] = lax.dot_general(wqT_ref[...], hb, nt, preferred_element_type=f32).astype(bf16)
    k = jnp.dot(hb, wk_ref[...], preferred_element_type=f32)
    vT = lax.dot_general(wvT_ref[...], hb, nt, preferred_element_type=f32).astype(bf16)
    for j in range(tm // MOBA_BLOCK):
        rows = slice(j * MOBA_BLOCK, (j + 1) * MOBA_BLOCK)
        k_ref[0, j] = k[rows].astype(bf16)
        vT_ref[0, j] = vT[:, rows]
        km_ref[0, 0, j:j + 1, :] = jnp.mean(k[rows], axis=0, keepdims=True)

    cbx = jnp.dot(hb, wc_ref[...], preferred_element_type=f32)
    u = cbx[:, :cwid] * cbx[:, 2 * cwid:]

    @pl.when(s == 0)
    def _():
        ubuf[0:8, :] = jnp.zeros((8, cwid), f32)

    ubuf[8:8 + tm, :] = u
    cw = cw_ref[...]
    conv = cw[0:1] * ubuf[6:6 + tm, :] + cw[1:2] * ubuf[7:7 + tm, :] + cw[2:3] * u
    ubuf[0:8, :] = ubuf[tm:tm + 8, :]
    cv = (cbx[:, cwid:2 * cwid] * conv).astype(bf16)
    y_conv = jnp.dot(cv, wco_ref[...], preferred_element_type=f32)

    gl = jnp.dot(hb, wgl_ref[...], preferred_element_type=f32) + bg_ref[...]
    ga_ref[0] = jax.nn.sigmoid(gl[:, :d]).astype(bf16)
    mc_ref[0] = (jax.nn.sigmoid(gl[:, d:]) * y_conv).astype(bf16)


def _inproj(x, g, sc, sh, w_in, b_gate, conv_w, w_conv_out):
    B, S, D = x.shape
    tm = ROW_TILE
    nb = S // MOBA_BLOCK
    bpt = tm // MOBA_BLOCK
    aw = ATTN_WIDTH
    cwid = conv_w.shape[1]
    scale = HEAD_DIM ** -0.5 * LOG2E
    wqT = (w_in[:, :aw] * scale).T.astype(bf16)
    wk = w_in[:, aw:2 * aw].astype(bf16)
    wvT = w_in[:, 2 * aw:3 * aw].T.astype(bf16)
    wc = w_in[:, 3 * aw:3 * aw + 3 * cwid].astype(bf16)
    wgl = w_in[:, 3 * aw + 3 * cwid:].astype(bf16)
    wco = w_conv_out.astype(bf16)

    const = lambda shape: pl.BlockSpec(shape, lambda b, s: (0,) * len(shape))
    vec = pl.BlockSpec((1, 1, D), lambda b, s: (b, 0, 0))
    row = lambda width: pl.BlockSpec((1, tm, width), lambda b, s: (b, s, 0))
    outs = pl.pallas_call(
        _inproj_kernel,
        grid=(B, S // tm),
        in_specs=[row(D), const((1, D)), vec, vec,
                  const((aw, D)), const((D, aw)), const((aw, D)), const((D, 3 * cwid)),
                  const((D, 2 * D)), const((cwid, D)), const((CONV_K, cwid)), const((1, 2 * D))],
        out_specs=(pl.BlockSpec((1, aw, tm), lambda b, s: (b, 0, s)),
                   pl.BlockSpec((1, bpt, MOBA_BLOCK, aw), lambda b, s: (b, s, 0, 0)),
                   pl.BlockSpec((1, bpt, aw, MOBA_BLOCK), lambda b, s: (b, s, 0, 0)),
                   pl.BlockSpec((1, 1, bpt, aw), lambda b, s: (b, s, 0, 0)),
                   row(D), row(D)),
        out_shape=(jax.ShapeDtypeStruct((B, aw, S), bf16),
                   jax.ShapeDtypeStruct((B, nb, MOBA_BLOCK, aw), bf16),
                   jax.ShapeDtypeStruct((B, nb, aw, MOBA_BLOCK), bf16),
                   jax.ShapeDtypeStruct((B, S // tm, bpt, aw), f32),
                   jax.ShapeDtypeStruct((B, S, D), bf16),
                   jax.ShapeDtypeStruct((B, S, D), bf16)),
        scratch_shapes=[pltpu.VMEM((tm + 8, cwid), f32)],
        compiler_params=_params(("parallel", "arbitrary")),
        name="inproj",
    )(x, g.reshape(1, D), sc.reshape(B, 1, D), sh.reshape(B, 1, D),
      wqT, wk, wvT, wc, wgl, wco, conv_w, b_gate.reshape(1, 2 * D))
    qT, k4, vT4, km, mc, ga = outs
    return qT, k4, vT4, km.reshape(B, nb, aw), mc, ga


def _attn_kernel(qT_ref, k_ref, vT_ref, km_ref, own_ref, adj_ref, *side_refs):
    n_side = len(side_refs) // 2
    o_ref = side_refs[n_side]
    for src, dst in zip(side_refs[:n_side], side_refs[n_side + 1:]):
        dst[...] = src[...].astype(bf16)

    i = pl.program_id(1)
    bs = MOBA_BLOCK
    nb = km_ref.shape[1]
    n_heads = qT_ref.shape[1] // HEAD_DIM
    heads = range(n_heads)
    blk = lax.broadcasted_iota(jnp.int32, (nb, bs), 0)

    qcat = []
    for h in heads:
        pair, a = divmod(h, LANES // HEAD_DIM)
        qT = qT_ref[0, pair * LANES:(pair + 1) * LANES, :]
        row = lax.broadcasted_iota(jnp.int32, qT.shape, 0)
        qa = jnp.where((row >= a * HEAD_DIM) & (row < (a + 1) * HEAD_DIM), qT, jnp.zeros_like(qT))
        km = km_ref[0, :, pair * LANES:(pair + 1) * LANES]
        km_hi = km.astype(bf16)
        km_mid = (km - km_hi.astype(f32)).astype(bf16)
        km_lo = (km - km_hi.astype(f32) - km_mid.astype(f32)).astype(bf16)
        g = (jnp.dot(km_hi, qa, preferred_element_type=f32)
             + (jnp.dot(km_mid, qa, preferred_element_type=f32) + jnp.dot(km_lo, qa, preferred_element_type=f32)))
        g = jnp.where(blk < i, g, NEG_INF)
        fq = jnp.full((nb, bs), NEG_INF, f32)
        for _ in range(MOBA_TOPK):
            top = jnp.max(g, axis=0, keepdims=True)
            first = jnp.min(jnp.where(g == top, blk, nb), axis=0, keepdims=True)
            pick = blk == first
            fq = jnp.where(pick, 0.0, fq)
            g = jnp.where(pick, -jnp.inf, g)
        fq = jnp.where(blk < i, fq, 0.0).astype(bf16)
        pad = jnp.zeros((LANES - nb, bs), bf16)
        qcat.append(jnp.concatenate([qa, fq, pad], axis=0))

    lane = lax.broadcasted_iota(jnp.int32, (bs, LANES), 1)
    causal = (lax.broadcasted_iota(jnp.int32, (bs, bs), 0) <= lax.broadcasted_iota(jnp.int32, (bs, bs), 1))
    ones_rows = jnp.ones((SUM_ROWS, bs), bf16)

    def step(blocks, state):
        onehots = [jnp.where(lane == n, 1.0, 0.0).astype(bf16) for n, _ in blocks]
        scores = []
        for h in heads:
            pair = h // (LANES // HEAD_DIM)
            cols = slice(pair * LANES, (pair + 1) * LANES)
            scores.append([jnp.dot(jnp.concatenate([k_ref[0, n, :, cols], onehot], axis=1), qcat[h],
                                   preferred_element_type=f32) for (n, _), onehot in zip(blocks, onehots)])
        out = []
        for h in heads:
            sTs = []
            for (n, kind), sT in zip(blocks, scores[h]):
                if kind == "own":
                    sT = jnp.where(causal, own_ref[h] + sT, NEG_INF)
                elif kind == "adj":
                    sT = adj_ref[h] + sT
                sTs.append(sT)
            m_new = None if state is None else state[h][0]
            for sT in sTs:
                top = jnp.max(sT, axis=0, keepdims=True)
                m_new = top if m_new is None else jnp.maximum(m_new, top)
            acc = None if state is None else jnp.exp2(state[h][0] - m_new) * state[h][1]
            for (n, _), sT in zip(blocks, sTs):
                p = jnp.exp2(sT - m_new).astype(bf16)
                v_ext = jnp.concatenate([vT_ref[0, n, h * HEAD_DIM:(h + 1) * HEAD_DIM, :], ones_rows], axis=0)
                pv = jnp.dot(v_ext, p, preferred_element_type=f32)
                acc = pv if acc is None else acc + pv
            out.append((m_new, acc))
        return tuple(out)

    state = lax.cond(i >= 1,
                     lambda: step([(i, "own"), (i - 1, "adj")], None),
                     lambda: step([(i, "own")], None))
    n_far = jnp.maximum(i - 1, 0)
    per = FAR_BLOCKS_PER_ITER
    def far_iter(it, st):
        for u in range(FAR_SUBSTEPS):
            st = step([((it * FAR_SUBSTEPS + u) * per + j, "far") for j in range(per)], st)
        return st

    state = lax.fori_loop(0, n_far // (per * FAR_SUBSTEPS), far_iter, state)
    per = per * FAR_SUBSTEPS
    done = n_far - n_far % per
    size = per // 2
    while size >= 1:
        take = (n_far % per) & size
        state = lax.cond(take > 0,
                         lambda st, d=done, sz=size: step([(d + j, "far") for j in range(sz)], st),
                         lambda st: st, state)
        done = done + take
        size //= 2
    outT = jnp.concatenate([acc[:HEAD_DIM] / acc[HEAD_DIM:HEAD_DIM + 1] for (_, acc) in state], axis=0)
    o_ref[0] = outT.T.astype(bf16)


def _attention(qT, k4, vT4, km, own, adj, side_weights):
    B, aw, S = qT.shape
    nb = S // MOBA_BLOCK
    assert MOBA_TOPK < nb <= LANES
    n_heads = aw // HEAD_DIM
    steps = B * nb
    whole = lambda shape: pl.BlockSpec(shape, lambda b, i: (b,) + (0,) * (len(shape) - 1))
    table = pl.BlockSpec((n_heads, MOBA_BLOCK, MOBA_BLOCK), lambda b, i: (0, 0, 0))
    slabs = [w.reshape(steps, -1, w.shape[-1]) for w in side_weights]
    slab_specs = [pl.BlockSpec((1,) + w.shape[1:], lambda b, i: (b * nb + i, 0, 0)) for w in slabs]
    outs = pl.pallas_call(
        _attn_kernel,
        grid=(B, nb),
        in_specs=[pl.BlockSpec((1, aw, MOBA_BLOCK), lambda b, i: (b, 0, i)),
                  whole((1, nb, MOBA_BLOCK, aw)), whole((1, nb, aw, MOBA_BLOCK)), whole((1, nb, aw)),
                  table, table] + slab_specs,
        out_specs=[pl.BlockSpec((1, MOBA_BLOCK, aw), lambda b, i: (b, i, 0))] + slab_specs,
        out_shape=[jax.ShapeDtypeStruct((B, S, aw), bf16)]
                  + [jax.ShapeDtypeStruct(w.shape, bf16) for w in slabs],
        compiler_params=_params(("parallel", "arbitrary")),
        name="attn",
    )(qT, k4, vT4, km, own, adj, *slabs)
    return outs[0], [o.reshape(w.shape) for o, w in zip(outs[1:], side_weights)]


def _post_kernel(a_ref, mc_ref, ga_ref, x_ref, g1_ref, sc_ref, sh_ref, ng_ref,
                 wao_ref, wo_ref, wrh_ref, wrl_ref, br_ref, x1_ref, h2p_ref, route_ref):
    tm = x_ref.shape[1]
    d = x_ref.shape[2]
    sub = tm // POST_SPLIT
    lane = lax.broadcasted_iota(jnp.int32, (sub, LANES), 1)
    ninf = -jnp.inf
    for t in range(POST_SPLIT):
        rows = slice(t * sub, (t + 1) * sub)
        ya = jnp.dot(a_ref[0, rows, :], wao_ref[...], preferred_element_type=f32)
        merged = ga_ref[0, rows, :].astype(f32) * ya + mc_ref[0, rows, :].astype(f32)
        z = jnp.dot(merged.astype(bf16), wo_ref[...], preferred_element_type=f32)
        x1 = x_ref[0, rows, :] + g1_ref[0] * z
        x1_ref[0, rows, :] = x1
        h2 = _ada_norm(x1, ng_ref[...], sc_ref[0], sh_ref[0])
        h_hi = h2.astype(bf16)
        h_hi32 = h_hi.astype(f32)

        bits = pltpu.bitcast(h_hi32, jnp.uint32)
        packed = (bits[:, :d // 2] >> 16) | (bits[:, d // 2:] & jnp.uint32(0xFFFF0000))
        chunks = packed.shape[1] // LANES
        for j in range(chunks):
            h2p_ref[pl.ds(t * sub * chunks + j, sub, stride=chunks), :] = packed[:, j * LANES:(j + 1) * LANES]

        h_lo = (h2 - h_hi32).astype(bf16)
        logits = (jnp.dot(h_hi, wrh_ref[...], preferred_element_type=f32)
                  + (jnp.dot(h_lo, wrh_ref[...], preferred_element_type=f32)
                     + jnp.dot(h_hi, wrl_ref[...], preferred_element_type=f32))) + br_ref[...]
        gl = jnp.where(lane < N_GROUPS, logits, ninf)
        gmax = jnp.max(gl, axis=1, keepdims=True)
        g_sel = jnp.min(jnp.where(gl == gmax, lane, LANES), axis=1, keepdims=True)
        g_w = 1.0 / jnp.sum(jnp.exp(gl - gmax), axis=1, keepdims=True)
        off = lane - (N_GROUPS + EXPERTS_PER_GROUP * g_sel)
        in_group = (off >= 0) & (off < EXPERTS_PER_GROUP)
        el = jnp.where(in_group, logits, ninf)
        ex = jnp.exp(el - jnp.max(el, axis=1, keepdims=True))
        ep = jnp.where(in_group, ex / jnp.sum(ex, axis=1, keepdims=True), -1.0)
        p1 = jnp.max(ep, axis=1, keepdims=True)
        i1 = jnp.min(jnp.where(ep == p1, lane, LANES), axis=1, keepdims=True)
        ep2 = jnp.where(lane == i1, -1.0, ep)
        p2 = jnp.max(ep2, axis=1, keepdims=True)
        i2 = jnp.min(jnp.where(ep2 == p2, lane, LANES), axis=1, keepdims=True)
        tot = p1 + p2
        w1 = p1 / tot * g_w
        w2 = p2 / tot * g_w
        e1 = (i1 - N_GROUPS).astype(f32)
        e2 = (i2 - N_GROUPS).astype(f32)
        route = jnp.where(lane == 0, e1, jnp.where(lane == 1, e2,
                          jnp.where(lane == 2, w1, jnp.where(lane == 3, w2, 0.0))))
        route_ref[:, rows] = route.T[:SUBLANES, :]


def _post(a, mc, ga, x, g1, sc2, sh2, ng, w_attn_out, w_o, w_rg, b_rg, w_re, b_re):
    B, S, D = x.shape
    tm = ROW_TILE
    aw = a.shape[2]
    nt = S // tm
    wr = jnp.zeros((D, LANES), f32).at[:, :N_GROUPS].set(w_rg).at[:, N_GROUPS:N_GROUPS + N_EXPERTS].set(w_re)
    br = jnp.zeros((1, LANES), f32).at[0, :N_GROUPS].set(b_rg).at[0, N_GROUPS:N_GROUPS + N_EXPERTS].set(b_re)
    wr_hi = wr.astype(bf16)
    wr_lo = (wr - wr_hi.astype(f32)).astype(bf16)
    const = lambda shape: pl.BlockSpec(shape, lambda b, s: (0,) * len(shape))
    vec = pl.BlockSpec((1, 1, D), lambda b, s: (b, 0, 0))
    row = lambda width: pl.BlockSpec((1, tm, width), lambda b, s: (b, s, 0))
    flat = lambda width, rows_per_token=1: pl.BlockSpec((tm * rows_per_token, width), lambda b, s: (b * nt + s, 0))
    words = D // 2 // LANES
    return pl.pallas_call(
        _post_kernel,
        grid=(B, nt),
        in_specs=[row(aw), row(D), row(D), row(D), vec, vec, vec, const((1, D)),
                  const((aw, D)), const((D, D)), const((D, LANES)), const((D, LANES)), const((1, LANES))],
        out_specs=(row(D), flat(LANES, words), pl.BlockSpec((SUBLANES, tm), lambda b, s: (0, b * nt + s))),
        out_shape=(jax.ShapeDtypeStruct((B, S, D), f32),
                   jax.ShapeDtypeStruct((B * S * words, LANES), jnp.uint32),
                   jax.ShapeDtypeStruct((SUBLANES, B * S), f32)),
        compiler_params=_params(("parallel", "parallel")),
        name="post",
    )(a, mc, ga, x, g1.reshape(B, 1, D), sc2.reshape(B, 1, D), sh2.reshape(B, 1, D), ng.reshape(1, D),
      w_attn_out.astype(bf16), w_o.astype(bf16), wr_hi, wr_lo, br)


def _dispatch_tables(route, n_chunks):
    T = route.shape[1]
    tc = T // n_chunks
    na = tc * TOP_K_EXPERTS
    per_chunk = lambda rows: rows.reshape(TOP_K_EXPERTS, n_chunks, tc).transpose(1, 0, 2).reshape(n_chunks, na)
    e = per_chunk(route[:TOP_K_EXPERTS]).astype(jnp.int32)
    w = per_chunk(route[TOP_K_EXPERTS:2 * TOP_K_EXPERTS])
    order = jnp.argsort(e, axis=1, stable=True).astype(jnp.int32)
    stok = order % tc
    sw = jnp.take_along_axis(w, order, axis=1)
    counts = jnp.sum((e[:, :, None] == jnp.arange(N_EXPERTS)[None, None, :]).astype(jnp.int32), axis=1)
    starts = jnp.cumsum(counts, axis=1) - counts
    nblk = (counts + EXPERT_ROWS - 1) // EXPERT_ROWS
    bend = jnp.cumsum(nblk, axis=1)
    bstart = bend - nblk
    total = bend[:, -1:]
    n_blocks = na // EXPERT_ROWS + N_EXPERTS
    bidx = jnp.arange(n_blocks, dtype=jnp.int32)[None, :]
    be = jnp.sum((bidx[:, :, None] >= bend[:, None, :]).astype(jnp.int32), axis=2)
    be = jnp.minimum(be, N_EXPERTS - 1)
    valid = bidx < total
    be = jnp.where(valid, be, jnp.take_along_axis(be, jnp.maximum(total - 1, 0), axis=1))
    local = bidx - jnp.take_along_axis(bstart, be, axis=1)
    bs = jnp.take_along_axis(starts, be, axis=1) + EXPERT_ROWS * local
    bn = jnp.clip(jnp.take_along_axis(counts, be, axis=1) - EXPERT_ROWS * local, 0, EXPERT_ROWS)
    bn = jnp.where(valid, bn, 0)
    bs = jnp.where(valid, bs, 0)
    flat = lambda t: t.reshape(-1).astype(jnp.int32)
    return flat(be), flat(bs), flat(bn), stok.reshape(n_chunks, 1, na), sw.reshape(n_chunks, 1, na), n_blocks


def _moe_kernel(be_ref, bs_ref, bn_ref, stok_ref, sw_ref, h2p_ref, wg_ref, wu_ref, wd_ref,
                x1_ref, g2_ref, fg_ref, o_ref, yacc, xs, outs, *, n_blocks, normalize):
    c = pl.program_id(0)
    b = pl.program_id(1)
    block_rows = outs.shape[0] // SUBLANES
    words = xs.shape[0] // block_rows
    half = words * LANES
    blk = c * n_blocks + jnp.minimum(b, n_blocks - 1)
    n = jnp.where(b < n_blocks, bn_ref[blk], 0)
    st = bs_ref[blk]

    @pl.when(b == 0)
    def _():
        yacc[...] = jnp.zeros(yacc.shape, f32)

    @pl.when((b == 0) & (c == 0))
    def _():
        xs[...] = jnp.zeros(xs.shape, jnp.uint32)

    def for_rows(rows_fn):
        full = n // ROW_UNROLL

        def group(g, carry):
            rows_fn([g * ROW_UNROLL + j for j in range(ROW_UNROLL)], st + g * ROW_UNROLL)
            return carry

        def single(j, carry):
            rows_fn([full * ROW_UNROLL + j], st + full * ROW_UNROLL + j)
            return carry

        lax.fori_loop(0, full, group, 0)
        lax.fori_loop(0, n - full * ROW_UNROLL, single, 0)

    def token_rows(index, rows_per_token):
        return pl.ds(pl.multiple_of(index * rows_per_token, rows_per_token), rows_per_token)

    @pl.when(n > 0)
    def _():
        def gather(rs, i0):
            for k, r in enumerate(rs):
                xs[token_rows(r, words), :] = h2p_ref[token_rows(stok_ref[0, 0, i0 + k], words), :]

        for_rows(gather)

    def ffn(rows):
        lo, hi = [], []
        for j in range(words):
            packed = xs[pl.ds(j, rows, stride=words), :]
            lo.append(pltpu.bitcast(packed << 16, f32).astype(bf16))
            hi.append(pltpu.bitcast(packed & jnp.uint32(0xFFFF0000), f32).astype(bf16))
        x_lo = jnp.concatenate(lo, axis=1)
        x_hi = jnp.concatenate(hi, axis=1)

        def up(w_ref):
            return (jnp.dot(x_lo, w_ref[0, :half, :], preferred_element_type=f32)
                    + jnp.dot(x_hi, w_ref[0, half:, :], preferred_element_type=f32))

        hidden = (jax.nn.silu(up(wg_ref)) * up(wu_ref)).astype(bf16)
        out = jnp.dot(hidden, wd_ref[0], preferred_element_type=f32)
        for j in range(SUBLANES):
            outs[pl.ds(j, rows, stride=SUBLANES), :] = out[:, j * LANES:(j + 1) * LANES]

    pl.when(n > TAIL_ROWS)(lambda: ffn(block_rows))
    pl.when((n > 0) & (n <= TAIL_ROWS))(lambda: ffn(TAIL_ROWS))

    @pl.when(n > 0)
    def _():
        def scatter(rs, i0):
            toks = [stok_ref[0, 0, i0 + k] for k in range(len(rs))]
            acc = [yacc[token_rows(tok, SUBLANES), :] for tok in toks]
            for k, (r, tok, a) in enumerate(zip(rs, toks, acc)):
                yacc[token_rows(tok, SUBLANES), :] = a + sw_ref[0, 0, i0 + k] * outs[token_rows(r, SUBLANES), :]

        for_rows(scatter)

    @pl.when(b >= n_blocks)
    def _():
        rows = o_ref.shape[0]
        start = pl.multiple_of((b - n_blocks) * rows * SUBLANES, rows * SUBLANES)
        y = jnp.concatenate([yacc[pl.ds(start + j, rows, stride=SUBLANES), :] for j in range(SUBLANES)], axis=1)
        x2 = x1_ref[...] + g2_ref[0] * y
        if normalize:
            x2 = (x2 * lax.rsqrt(jnp.mean(x2 * x2, axis=-1, keepdims=True) + NORM_EPS)) * fg_ref[...]
        o_ref[...] = x2


def _moe(route, h2p, w_gate, w_up, w_down, x1, g2, fg, normalize):
    B, S, D = x1.shape
    T = B * S
    words = h2p.shape[0] // T
    assert D == words * 2 * LANES == SUBLANES * LANES
    de = w_gate.shape[2]
    n_chunks = MOE_CHUNKS
    tc = T // n_chunks
    fin_rows = ROW_TILE // 2
    n_fin = tc // fin_rows
    assert n_chunks % B == 0 and tc % fin_rows == 0
    be, bs, bn, stok, sw, n_blocks = _dispatch_tables(route, n_chunks)
    na = stok.shape[2]
    wmap = lambda c, b, be_r, bs_r, bn_r: (be_r[c * n_blocks + jnp.minimum(b, n_blocks - 1)], 0, 0)
    smem = lambda: pl.BlockSpec((1, 1, na), lambda c, b, *_: (c, 0, 0), memory_space=pltpu.SMEM)
    tile = pl.BlockSpec((fin_rows, D), lambda c, b, *_: (c * n_fin + jnp.maximum(b - n_blocks, 0), 0))
    grid_spec = pltpu.PrefetchScalarGridSpec(
        num_scalar_prefetch=3,
        grid=(n_chunks, n_blocks + n_fin),
        in_specs=[smem(), smem(),
                  pl.BlockSpec((tc * words, LANES), lambda c, b, *_: (c, 0)),
                  pl.BlockSpec((1, D, de), wmap),
                  pl.BlockSpec((1, D, de), wmap),
                  pl.BlockSpec((1, de, D), wmap),
                  tile,
                  pl.BlockSpec((1, 1, D), lambda c, b, *_: (c // (n_chunks // B), 0, 0)),
                  pl.BlockSpec((1, D), lambda c, b, *_: (0, 0))],
        out_specs=tile,
        scratch_shapes=[pltpu.VMEM((tc * SUBLANES, LANES), f32),
                        pltpu.VMEM((EXPERT_ROWS * words, LANES), jnp.uint32),
                        pltpu.VMEM((EXPERT_ROWS * SUBLANES, LANES), f32)],
    )
    out = pl.pallas_call(
        functools.partial(_moe_kernel, n_blocks=n_blocks, normalize=normalize),
        grid_spec=grid_spec,
        out_shape=jax.ShapeDtypeStruct((T, D), f32),
        compiler_params=_params(("arbitrary", "arbitrary")),
        name="moe",
    )(be, bs, bn, stok, sw, h2p, w_gate, w_up, w_down,
      x1.reshape(T, D), g2.reshape(B, 1, D), fg.reshape(1, D))
    return out.reshape(B, S, D)


def kernel(x, c, rel_bias, norm1_g, norm2_g, w_ada, b_ada, w_in, b_gate, conv_w, w_attn_out,
           w_conv_out, w_o, w_router_group, b_router_group, w_router_expert, b_router_expert,
           w_exp_gate, w_exp_up, w_exp_down, final_norm_g):
    B, S, D = x.shape
    depth = w_ada.shape[0]
    assert S % ROW_TILE == 0 and ROW_TILE % MOBA_BLOCK == 0 and (B * S) % (MOE_CHUNKS * EXPERT_ROWS) == 0
    own, adj = _bias_tiles(rel_bias)
    for l in range(depth):
        mod = _mod(c, w_ada[l], b_ada[l])
        sh1, sc1, g1, sh2, sc2, g2 = jnp.split(mod, 6, axis=-1)
        qT, k4, vT4, km, mc, ga = _inproj(x, norm1_g[l], sc1, sh1, w_in[l], b_gate[l], conv_w[l],
                                          w_conv_out[l])
        a, expert_w = _attention(qT, k4, vT4, km, own, adj, [w_exp_gate[l], w_exp_up[l], w_exp_down[l]])
        x1, h2p, route = _post(a, mc, ga, x, g1, sc2, sh2, norm2_g[l], w_attn_out[l], w_o[l],
                               w_router_group[l], b_router_group[l], w_router_expert[l],
                               b_router_expert[l])
        x = _moe(route, h2p, *expert_w, x1, g2, final_norm_g, normalize=(l + 1 == depth))
    return x
```

```python
import functools
import math

import numpy as np
import jax
import jax.numpy as jnp
from jax import lax
from jax.experimental import pallas as pl
from jax.experimental.pallas import tpu as pltpu

f32 = jnp.float32
bf16 = jnp.bfloat16

N_HEADS = 8
HEAD_DIM = 64
ATTN_WIDTH = N_HEADS * HEAD_DIM
MOBA_BLOCK = 256
MOBA_TOPK = 3
CONV_K = 3
N_BUCKETS = 32
MAX_EXACT = N_BUCKETS // 2
MAX_DISTANCE = 128
N_GROUPS = 4
EXPERTS_PER_GROUP = 8
N_EXPERTS = N_GROUPS * EXPERTS_PER_GROUP
TOP_K_EXPERTS = 2
NORM_EPS = 1e-6
NEG_INF = -1e30

LANES = 128
SUBLANES = 8
ROW_TILE = 1024
POST_SPLIT = 2
EXPERT_ROWS = 256
ROW_UNROLL = 8
MOE_CHUNKS = 4
FAR_BLOCKS_PER_ITER = 2
FAR_SUBSTEPS = 2
SUM_ROWS = 16
LOG2E = 1.4426950408889634
VMEM_LIMIT = 56 * 1024 * 1024


def _params(sem, vmem=VMEM_LIMIT):
    return pltpu.CompilerParams(dimension_semantics=sem, vmem_limit_bytes=vmem)


def _mod_kernel(c_ref, w_ref, b_ref, o_ref, *, batch):
    act = jax.nn.silu(c_ref[...]).T
    w = w_ref[...]
    rows = [jnp.sum(act[:, r:r + 1] * w, axis=0, keepdims=True) for r in range(batch)]
    rows.append(jnp.zeros((o_ref.shape[0] - batch, w.shape[1]), f32))
    o_ref[...] = jnp.concatenate(rows, axis=0) + b_ref[...]


def _mod(c, w_ada, b_ada):
    B, D = c.shape
    n_out = w_ada.shape[1]
    rows = 8
    cp = jnp.zeros((rows, D), f32).at[:B].set(c)
    tn = 1024
    out = pl.pallas_call(
        functools.partial(_mod_kernel, batch=B),
        grid=(n_out // tn,),
        in_specs=[pl.BlockSpec((rows, D), lambda j: (0, 0)),
                  pl.BlockSpec((D, tn), lambda j: (0, j)),
                  pl.BlockSpec((1, tn), lambda j: (0, j))],
        out_specs=pl.BlockSpec((rows, tn), lambda j: (0, j)),
        out_shape=jax.ShapeDtypeStruct((rows, n_out), f32),
        compiler_params=_params(("arbitrary",)),
        name="mod",
    )(cp, w_ada, b_ada.reshape(1, n_out))
    return out[:B]


def _bucket_thresholds():
    n = np.arange(2 * MOBA_BLOCK)

    def buckets(ft):
        nf = np.maximum(n, 1).astype(ft)
        large = MAX_EXACT + (np.log(nf / ft(MAX_EXACT)) / ft(math.log(MAX_DISTANCE / MAX_EXACT))
                             * ft(N_BUCKETS - MAX_EXACT)).astype(np.int32)
        return np.where(n < MAX_EXACT, n, np.minimum(large, N_BUCKETS - 1))

    b = buckets(np.float32)
    assert np.array_equal(b, buckets(np.float64)) and np.all(np.diff(b) >= 0)
    assert b[-1] == N_BUCKETS - 1 and b[MOBA_BLOCK + 1] == N_BUCKETS - 1
    return [int(np.argmax(b >= j)) for j in range(N_BUCKETS)]


def _bias_kernel(rb_ref, own_ref, adj_ref, *, thresholds):
    h = pl.program_id(0)
    kk = lax.broadcasted_iota(jnp.int32, (MOBA_BLOCK, MOBA_BLOCK), 0)
    qq = lax.broadcasted_iota(jnp.int32, (MOBA_BLOCK, MOBA_BLOCK), 1)
    far = rb_ref[N_BUCKETS - 1, h]

    def table(dist):
        acc = jnp.full(dist.shape, rb_ref[0, h], f32)
        for j in range(1, N_BUCKETS):
            acc = jnp.where(dist >= thresholds[j], rb_ref[j, h], acc)
        return (acc - far) * LOG2E

    own_ref[0] = table(qq - kk)
    adj_ref[0] = table(qq - kk + MOBA_BLOCK)


def _bias_tiles(rel_bias):
    shp = jax.ShapeDtypeStruct((N_HEADS, MOBA_BLOCK, MOBA_BLOCK), f32)
    spec = pl.BlockSpec((1, MOBA_BLOCK, MOBA_BLOCK), lambda h: (h, 0, 0))
    return pl.pallas_call(
        functools.partial(_bias_kernel, thresholds=_bucket_thresholds()),
        grid=(N_HEADS,),
        in_specs=[pl.BlockSpec(memory_space=pltpu.SMEM)],
        out_specs=(spec, spec),
        out_shape=(shp, shp),
        compiler_params=_params(("arbitrary",)),
        name="bias",
    )(rel_bias)


def _ada_norm(x, g, scale, shift):
    y = x * lax.rsqrt(jnp.mean(x * x, axis=-1, keepdims=True) + NORM_EPS)
    return (y * g) * (1.0 + scale) + shift


def _inproj_kernel(x_ref, g_ref, sc_ref, sh_ref, wqT_ref, wk_ref, wvT_ref, wc_ref, wgl_ref,
                   wco_ref, cw_ref, bg_ref,
                   qT_ref, k_ref, vT_ref, km_ref, mc_ref, ga_ref, ubuf):
    tm = x_ref.shape[1]
    cwid = wco_ref.shape[0]
    d = x_ref.shape[2]
    s = pl.program_id(1)
    hb = _ada_norm(x_ref[0], g_ref[...], sc_ref[0], sh_ref[0]).astype(bf16)

    nt = (((1,), (1,)), ((), ()))
    qT_ref[0] = lax.dot_general(wqT_ref[...], hb, nt, preferred_element_type=f32).astype(bf16)
    k = jnp.dot(hb, wk_ref[...], preferred_element_type=f32)
    vT = lax.dot_general(wvT_ref[...], hb, nt, preferred_element_type=f32).astype(bf16)
    for j in range(tm // MOBA_BLOCK):
        rows = slice(j * MOBA_BLOCK, (j + 1) * MOBA_BLOCK)
        k_ref[0, j] = k[rows].astype(bf16)
        vT_ref[0, j] = vT[:, rows]
        km_ref[0, 0, j:j + 1, :] = jnp.mean(k[rows], axis=0, keepdims=True)

    cbx = jnp.dot(hb, wc_ref[...], preferred_element_type=f32)
    u = cbx[:, :cwid] * cbx[:, 2 * cwid:]

    @pl.when(s == 0)
    def _():
        ubuf[0:8, :] = jnp.zeros((8, cwid), f32)

    ubuf[8:8 + tm, :] = u
    cw = cw_ref[...]
    conv = cw[0:1] * ubuf[6:6 + tm, :] + cw[1:2] * ubuf[7:7 + tm, :] + cw[2:3] * u
    ubuf[0:8, :] = ubuf[tm:tm + 8, :]
    cv = (cbx[:, cwid:2 * cwid] * conv).astype(bf16)
    y_conv = jnp.dot(cv, wco_ref[...], preferred_element_type=f32)

    gl = jnp.dot(hb, wgl_ref[...], preferred_element_type=f32) + bg_ref[...]
    ga_ref[0] = jax.nn.sigmoid(gl[:, :d]).astype(bf16)
    mc_ref[0] = (jax.nn.sigmoid(gl[:, d:]) * y_conv).astype(bf16)


def _inproj(x, g, sc, sh, w_in, b_gate, conv_w, w_conv_out):
    B, S, D = x.shape
    tm = ROW_TILE
    nb = S // MOBA_BLOCK
    bpt = tm // MOBA_BLOCK
    aw = ATTN_WIDTH
    cwid = conv_w.shape[1]
    scale = HEAD_DIM ** -0.5 * LOG2E
    wqT = (w_in[:, :aw] * scale).T.astype(bf16)
    wk = w_in[:, aw:2 * aw].astype(bf16)
    wvT = w_in[:, 2 * aw:3 * aw].T.astype(bf16)
    wc = w_in[:, 3 * aw:3 * aw + 3 * cwid].astype(bf16)
    wgl = w_in[:, 3 * aw + 3 * cwid:].astype(bf16)
    wco = w_conv_out.astype(bf16)

    const = lambda shape: pl.BlockSpec(shape, lambda b, s: (0,) * len(shape))
    vec = pl.BlockSpec((1, 1, D), lambda b, s: (b, 0, 0))
    row = lambda width: pl.BlockSpec((1, tm, width), lambda b, s: (b, s, 0))
    outs = pl.pallas_call(
        _inproj_kernel,
        grid=(B, S // tm),
        in_specs=[row(D), const((1, D)), vec, vec,
                  const((aw, D)), const((D, aw)), const((aw, D)), const((D, 3 * cwid)),
                  const((D, 2 * D)), const((cwid, D)), const((CONV_K, cwid)), const((1, 2 * D))],
        out_specs=(pl.BlockSpec((1, aw, tm), lambda b, s: (b, 0, s)),
                   pl.BlockSpec((1, bpt, MOBA_BLOCK, aw), lambda b, s: (b, s, 0, 0)),
                   pl.BlockSpec((1, bpt, aw, MOBA_BLOCK), lambda b, s: (b, s, 0, 0)),
                   pl.BlockSpec((1, 1, bpt, aw), lambda b, s: (b, s, 0, 0)),
                   row(D), row(D)),
        out_shape=(jax.ShapeDtypeStruct((B, aw, S), bf16),
                   jax.ShapeDtypeStruct((B, nb, MOBA_BLOCK, aw), bf16),
                   jax.ShapeDtypeStruct((B, nb, aw, MOBA_BLOCK), bf16),
                   jax.ShapeDtypeStruct((B, S // tm, bpt, aw), f32),
                   jax.ShapeDtypeStruct((B, S, D), bf16),
                   jax.ShapeDtypeStruct((B, S, D), bf16)),
        scratch_shapes=[pltpu.VMEM((tm + 8, cwid), f32)],
        compiler_params=_params(("parallel", "arbitrary")),
        name="inproj",
    )(x, g.reshape(1, D), sc.reshape(B, 1, D), sh.reshape(B, 1, D),
      wqT, wk, wvT, wc, wgl, wco, conv_w, b_gate.reshape(1, 2 * D))
    qT, k4, vT4, km, mc, ga = outs
    return qT, k4, vT4, km.reshape(B, nb, aw), mc, ga


def _attn_kernel(qT_ref, k_ref, vT_ref, km_ref, own_ref, adj_ref, *side_refs):
    n_side = len(side_refs) // 2
    o_ref = side_refs[n_side]
    for src, dst in zip(side_refs[:n_side], side_refs[n_side + 1:]):
        dst[...] = src[...].astype(bf16)

    i = pl.program_id(1)
    bs = MOBA_BLOCK
    nb = km_ref.shape[1]
    n_heads = qT_ref.shape[1] // HEAD_DIM
    heads = range(n_heads)
    blk = lax.broadcasted_iota(jnp.int32, (nb, bs), 0)

    qcat = []
    for h in heads:
        pair, a = divmod(h, LANES // HEAD_DIM)
        qT = qT_ref[0, pair * LANES:(pair + 1) * LANES, :]
        row = lax.broadcasted_iota(jnp.int32, qT.shape, 0)
        qa = jnp.where((row >= a * HEAD_DIM) & (row < (a + 1) * HEAD_DIM), qT, jnp.zeros_like(qT))
        km = km_ref[0, :, pair * LANES:(pair + 1) * LANES]
        km_hi = km.astype(bf16)
        km_mid = (km - km_hi.astype(f32)).astype(bf16)
        km_lo = (km - km_hi.astype(f32) - km_mid.astype(f32)).astype(bf16)
        g = (jnp.dot(km_hi, qa, preferred_element_type=f32)
             + (jnp.dot(km_mid, qa, preferred_element_type=f32) + jnp.dot(km_lo, qa, preferred_element_type=f32)))
        g = jnp.where(blk < i, g, NEG_INF)
        fq = jnp.full((nb, bs), NEG_INF, f32)
        for _ in range(MOBA_TOPK):
            top = jnp.max(g, axis=0, keepdims=True)
            first = jnp.min(jnp.where(g == top, blk, nb), axis=0, keepdims=True)
            pick = blk == first
            fq = jnp.where(pick, 0.0, fq)
            g = jnp.where(pick, -jnp.inf, g)
        fq = jnp.where(blk < i, fq, 0.0).astype(bf16)
        pad = jnp.zeros((LANES - nb, bs), bf16)
        qcat.append(jnp.concatenate([qa, fq, pad], axis=0))

    lane = lax.broadcasted_iota(jnp.int32, (bs, LANES), 1)
    causal = (lax.broadcasted_iota(jnp.int32, (bs, bs), 0) <= lax.broadcasted_iota(jnp.int32, (bs, bs), 1))
    ones_rows = jnp.ones((SUM_ROWS, bs), bf16)

    def step(blocks, state):
        onehots = [jnp.where(lane == n, 1.0, 0.0).astype(bf16) for n, _ in blocks]
        scores = []
        for h in heads:
            pair = h // (LANES // HEAD_DIM)
            cols = slice(pair * LANES, (pair + 1) * LANES)
            scores.append([jnp.dot(jnp.concatenate([k_ref[0, n, :, cols], onehot], axis=1), qcat[h],
                                   preferred_element_type=f32) for (n, _), onehot in zip(blocks, onehots)])
        out = []
        for h in heads:
            sTs = []
            for (n, kind), sT in zip(blocks, scores[h]):
                if kind == "own":
                    sT = jnp.where(causal, own_ref[h] + sT, NEG_INF)
                elif kind == "adj":
                    sT = adj_ref[h] + sT
                sTs.append(sT)
            m_new = None if state is None else state[h][0]
            for sT in sTs:
                top = jnp.max(sT, axis=0, keepdims=True)
                m_new = top if m_new is None else jnp.maximum(m_new, top)
            acc = None if state is None else jnp.exp2(state[h][0] - m_new) * state[h][1]
            for (n, _), sT in zip(blocks, sTs):
                p = jnp.exp2(sT - m_new).astype(bf16)
                v_ext = jnp.concatenate([vT_ref[0, n, h * HEAD_DIM:(h + 1) * HEAD_DIM, :], ones_rows], axis=0)
                pv = jnp.dot(v_ext, p, preferred_element_type=f32)
                acc = pv if acc is None else acc + pv
            out.append((m_new, acc))
        return tuple(out)

    state = lax.cond(i >= 1,
                     lambda: step([(i, "own"), (i - 1, "adj")], None),
                     lambda: step([(i, "own")], None))
    n_far = jnp.maximum(i - 1, 0)
    per = FAR_BLOCKS_PER_ITER
    def far_iter(it, st):
        for u in range(FAR_SUBSTEPS):
            st = step([((it * FAR_SUBSTEPS + u) * per + j, "far") for j in range(per)], st)
        return st

    state = lax.fori_loop(0, n_far // (per * FAR_SUBSTEPS), far_iter, state)
    per = per * FAR_SUBSTEPS
    done = n_far - n_far % per
    size = per // 2
    while size >= 1:
        take = (n_far % per) & size
        state = lax.cond(take > 0,
                         lambda st, d=done, sz=size: step([(d + j, "far") for j in range(sz)], st),
                         lambda st: st, state)
        done = done + take
        size //= 2
    outT = jnp.concatenate([acc[:HEAD_DIM] / acc[HEAD_DIM:HEAD_DIM + 1] for (_, acc) in state], axis=0)
    o_ref[0] = outT.T.astype(bf16)


def _attention(qT, k4, vT4, km, own, adj, side_weights):
    B, aw, S = qT.shape
    nb = S // MOBA_BLOCK
    assert MOBA_TOPK < nb <= LANES
    n_heads = aw // HEAD_DIM
    steps = B * nb
    whole = lambda shape: pl.BlockSpec(shape, lambda b, i: (b,) + (0,) * (len(shape) - 1))
    table = pl.BlockSpec((n_heads, MOBA_BLOCK, MOBA_BLOCK), lambda b, i: (0, 0, 0))
    slabs = [w.reshape(steps, -1, w.shape[-1]) for w in side_weights]
    slab_specs = [pl.BlockSpec((1,) + w.shape[1:], lambda b, i: (b * nb + i, 0, 0)) for w in slabs]
    outs = pl.pallas_call(
        _attn_kernel,
        grid=(B, nb),
        in_specs=[pl.BlockSpec((1, aw, MOBA_BLOCK), lambda b, i: (b, 0, i)),
                  whole((1, nb, MOBA_BLOCK, aw)), whole((1, nb, aw, MOBA_BLOCK)), whole((1, nb, aw)),
                  table, table] + slab_specs,
        out_specs=[pl.BlockSpec((1, MOBA_BLOCK, aw), lambda b, i: (b, i, 0))] + slab_specs,
        out_shape=[jax.ShapeDtypeStruct((B, S, aw), bf16)]
                  + [jax.ShapeDtypeStruct(w.shape, bf16) for w in slabs],
        compiler_params=_params(("parallel", "arbitrary")),
        name="attn",
    )(qT, k4, vT4, km, own, adj, *slabs)
    return outs[0], [o.reshape(w.shape) for o, w in zip(outs[1:], side_weights)]


def _post_kernel(a_ref, mc_ref, ga_ref, x_ref, g1_ref, sc_ref, sh_ref, ng_ref,
                 wao_ref, wo_ref, wrh_ref, wrl_ref, br_ref, x1_ref, h2p_ref, route_ref):
    tm = x_ref.shape[1]
    d = x_ref.shape[2]
    sub = tm // POST_SPLIT
    lane = lax.broadcasted_iota(jnp.int32, (sub, LANES), 1)
    ninf = -jnp.inf
    for t in range(POST_SPLIT):
        rows = slice(t * sub, (t + 1) * sub)
        ya = jnp.dot(a_ref[0, rows, :], wao_ref[...], preferred_element_type=f32)
        merged = ga_ref[0, rows, :].astype(f32) * ya + mc_ref[0, rows, :].astype(f32)
        z = jnp.dot(merged.astype(bf16), wo_ref[...], preferred_element_type=f32)
        x1 = x_ref[0, rows, :] + g1_ref[0] * z
        x1_ref[0, rows, :] = x1
        h2 = _ada_norm(x1, ng_ref[...], sc_ref[0], sh_ref[0])
        h_hi = h2.astype(bf16)
        h_hi32 = h_hi.astype(f32)

        bits = pltpu.bitcast(h_hi32, jnp.uint32)
        packed = (bits[:, :d // 2] >> 16) | (bits[:, d // 2:] & jnp.uint32(0xFFFF0000))
        chunks = packed.shape[1] // LANES
        for j in range(chunks):
            h2p_ref[pl.ds(t * sub * chunks + j, sub, stride=chunks), :] = packed[:, j * LANES:(j + 1) * LANES]

        h_lo = (h2 - h_hi32).astype(bf16)
        logits = (jnp.dot(h_hi, wrh_ref[...], preferred_element_type=f32)
                  + (jnp.dot(h_lo, wrh_ref[...], preferred_element_type=f32)
                     + jnp.dot(h_hi, wrl_ref[...], preferred_element_type=f32))) + br_ref[...]
        gl = jnp.where(lane < N_GROUPS, logits, ninf)
        gmax = jnp.max(gl, axis=1, keepdims=True)
        g_sel = jnp.min(jnp.where(gl == gmax, lane, LANES), axis=1, keepdims=True)
        g_w = 1.0 / jnp.sum(jnp.exp(gl - gmax), axis=1, keepdims=True)
        off = lane - (N_GROUPS + EXPERTS_PER_GROUP * g_sel)
        in_group = (off >= 0) & (off < EXPERTS_PER_GROUP)
        el = jnp.where(in_group, logits, ninf)
        ex = jnp.exp(el - jnp.max(el, axis=1, keepdims=True))
        ep = jnp.where(in_group, ex / jnp.sum(ex, axis=1, keepdims=True), -1.0)
        p1 = jnp.max(ep, axis=1, keepdims=True)
        i1 = jnp.min(jnp.where(ep == p1, lane, LANES), axis=1, keepdims=True)
        ep2 = jnp.where(lane == i1, -1.0, ep)
        p2 = jnp.max(ep2, axis=1, keepdims=True)
        i2 = jnp.min(jnp.where(ep2 == p2, lane, LANES), axis=1, keepdims=True)
        tot = p1 + p2
        w1 = p1 / tot * g_w
        w2 = p2 / tot * g_w
        e1 = (i1 - N_GROUPS).astype(f32)
        e2 = (i2 - N_GROUPS).astype(f32)
        route = jnp.where(lane == 0, e1, jnp.where(lane == 1, e2,
                          jnp.where(lane == 2, w1, jnp.where(lane == 3, w2, 0.0))))
        route_ref[:, rows] = route.T[:SUBLANES, :]


def _post(a, mc, ga, x, g1, sc2, sh2, ng, w_attn_out, w_o, w_rg, b_rg, w_re, b_re):
    B, S, D = x.shape
    tm = ROW_TILE
    aw = a.shape[2]
    nt = S // tm
    wr = jnp.zeros((D, LANES), f32).at[:, :N_GROUPS].set(w_rg).at[:, N_GROUPS:N_GROUPS + N_EXPERTS].set(w_re)
    br = jnp.zeros((1, LANES), f32).at[0, :N_GROUPS].set(b_rg).at[0, N_GROUPS:N_GROUPS + N_EXPERTS].set(b_re)
    wr_hi = wr.astype(bf16)
    wr_lo = (wr - wr_hi.astype(f32)).astype(bf16)
    const = lambda shape: pl.BlockSpec(shape, lambda b, s: (0,) * len(shape))
    vec = pl.BlockSpec((1, 1, D), lambda b, s: (b, 0, 0))
    row = lambda width: pl.BlockSpec((1, tm, width), lambda b, s: (b, s, 0))
    flat = lambda width, rows_per_token=1: pl.BlockSpec((tm * rows_per_token, width), lambda b, s: (b * nt + s, 0))
    words = D // 2 // LANES
    return pl.pallas_call(
        _post_kernel,
        grid=(B, nt),
        in_specs=[row(aw), row(D), row(D), row(D), vec, vec, vec, const((1, D)),
                  const((aw, D)), const((D, D)), const((D, LANES)), const((D, LANES)), const((1, LANES))],
        out_specs=(row(D), flat(LANES, words), pl.BlockSpec((SUBLANES, tm), lambda b, s: (0, b * nt + s))),
        out_shape=(jax.ShapeDtypeStruct((B, S, D), f32),
                   jax.ShapeDtypeStruct((B * S * words, LANES), jnp.uint32),
                   jax.ShapeDtypeStruct((SUBLANES, B * S), f32)),
        compiler_params=_params(("parallel", "parallel")),
        name="post",
    )(a, mc, ga, x, g1.reshape(B, 1, D), sc2.reshape(B, 1, D), sh2.reshape(B, 1, D), ng.reshape(1, D),
      w_attn_out.astype(bf16), w_o.astype(bf16), wr_hi, wr_lo, br)


def _dispatch_tables(route, n_chunks):
    T = route.shape[1]
    tc = T // n_chunks
    na = tc * TOP_K_EXPERTS
    per_chunk = lambda rows: rows.reshape(TOP_K_EXPERTS, n_chunks, tc).transpose(1, 0, 2).reshape(n_chunks, na)
    e = per_chunk(route[:TOP_K_EXPERTS]).astype(jnp.int32)
    w = per_chunk(route[TOP_K_EXPERTS:2 * TOP_K_EXPERTS])
    order = jnp.argsort(e, axis=1, stable=True).astype(jnp.int32)
    stok = order % tc
    sw = jnp.take_along_axis(w, order, axis=1)
    counts = jnp.sum((e[:, None, :] == jnp.arange(N_EXPERTS)[None, :, None]).astype(jnp.int32), axis=2)
    starts = jnp.cumsum(counts, axis=1) - counts
    flat = lambda t: t.reshape(-1).astype(jnp.int32)
    return flat(starts), flat(counts), stok.reshape(n_chunks, 1, na), sw.reshape(n_chunks, 1, na)


def _moe_kernel(start_ref, count_ref, stok_ref, sw_ref, h2p_ref, wg_ref, wu_ref, wd_ref,
                x1_ref, g2_ref, fg_ref, o_ref, yacc, xs, outs, *, normalize):
    c = pl.program_id(0)
    b = pl.program_id(1)
    block_rows = outs.shape[0] // SUBLANES
    words = xs.shape[0] // block_rows
    half = words * LANES
    run = c * N_EXPERTS + jnp.minimum(b, N_EXPERTS - 1)
    count = jnp.where(b < N_EXPERTS, count_ref[run], 0)
    first = start_ref[run]

    @pl.when(b == 0)
    def _():
        yacc[...] = jnp.zeros(yacc.shape, f32)

    @pl.when((b == 0) & (c == 0))
    def _():
        xs[...] = jnp.zeros(xs.shape, jnp.uint32)

    def token_rows(index, rows_per_token):
        return pl.ds(pl.multiple_of(index * rows_per_token, rows_per_token), rows_per_token)

    def block(blk, carry):
        st = first + blk * block_rows
        n = jnp.minimum(count - blk * block_rows, block_rows)

        def for_rows(rows_fn):
            full = n // ROW_UNROLL

            def group(g, carry):
                rows_fn([g * ROW_UNROLL + j for j in range(ROW_UNROLL)], st + g * ROW_UNROLL)
                return carry

            def single(j, carry):
                rows_fn([full * ROW_UNROLL + j], st + full * ROW_UNROLL + j)
                return carry

            lax.fori_loop(0, full, group, 0)
            lax.fori_loop(0, n - full * ROW_UNROLL, single, 0)

        def gather(rs, i0):
            for k, r in enumerate(rs):
                xs[token_rows(r, words), :] = h2p_ref[token_rows(stok_ref[0, 0, i0 + k], words), :]

        for_rows(gather)
        lo, hi = [], []
        for j in range(words):
            packed = xs[pl.ds(j, block_rows, stride=words), :]
            lo.append(pltpu.bitcast(packed << 16, f32).astype(bf16))
            hi.append(pltpu.bitcast(packed & jnp.uint32(0xFFFF0000), f32).astype(bf16))
        x_lo = jnp.concatenate(lo, axis=1)
        x_hi = jnp.concatenate(hi, axis=1)

        def up(w_ref):
            return (jnp.dot(x_lo, w_ref[0, :half, :], preferred_element_type=f32)
                    + jnp.dot(x_hi, w_ref[0, half:, :], preferred_element_type=f32))

        hidden = (jax.nn.silu(up(wg_ref)) * up(wu_ref)).astype(bf16)
        out = jnp.dot(hidden, wd_ref[0], preferred_element_type=f32)
        for j in range(SUBLANES):
            outs[pl.ds(j, block_rows, stride=SUBLANES), :] = out[:, j * LANES:(j + 1) * LANES]

        def scatter(rs, i0):
            toks = [stok_ref[0, 0, i0 + k] for k in range(len(rs))]
            acc = [yacc[token_rows(tok, SUBLANES), :] for tok in toks]
            for k, (r, tok, a) in enumerate(zip(rs, toks, acc)):
                yacc[token_rows(tok, SUBLANES), :] = a + sw_ref[0, 0, i0 + k] * outs[token_rows(r, SUBLANES), :]

        for_rows(scatter)
        return carry

    lax.fori_loop(0, (count + block_rows - 1) // block_rows, block, 0)

    @pl.when(b >= N_EXPERTS)
    def _():
        rows = o_ref.shape[0]
        start = pl.multiple_of((b - N_EXPERTS) * rows * SUBLANES, rows * SUBLANES)
        y = jnp.concatenate([yacc[pl.ds(start + j, rows, stride=SUBLANES), :] for j in range(SUBLANES)], axis=1)
        x2 = x1_ref[...] + g2_ref[0] * y
        if normalize:
            x2 = (x2 * lax.rsqrt(jnp.mean(x2 * x2, axis=-1, keepdims=True) + NORM_EPS)) * fg_ref[...]
        o_ref[...] = x2


def _moe(route, h2p, w_gate, w_up, w_down, x1, g2, fg, normalize):
    B, S, D = x1.shape
    T = B * S
    words = h2p.shape[0] // T
    assert D == words * 2 * LANES == SUBLANES * LANES
    de = w_gate.shape[2]
    n_chunks = MOE_CHUNKS
    tc = T // n_chunks
    fin_rows = ROW_TILE // 2
    n_fin = tc // fin_rows
    assert n_chunks % B == 0 and tc % fin_rows == 0 and w_gate.shape[0] == N_EXPERTS
    starts, counts, stok, sw = _dispatch_tables(route, n_chunks)
    na = stok.shape[2]
    wmap = lambda c, b, *_: (jnp.minimum(b, N_EXPERTS - 1), 0, 0)
    smem = lambda: pl.BlockSpec((1, 1, na), lambda c, b, *_: (c, 0, 0), memory_space=pltpu.SMEM)
    tile = pl.BlockSpec((fin_rows, D), lambda c, b, *_: (c * n_fin + jnp.maximum(b - N_EXPERTS, 0), 0))
    grid_spec = pltpu.PrefetchScalarGridSpec(
        num_scalar_prefetch=2,
        grid=(n_chunks, N_EXPERTS + n_fin),
        in_specs=[smem(), smem(),
                  pl.BlockSpec((tc * words, LANES), lambda c, b, *_: (c, 0)),
                  pl.BlockSpec((1, D, de), wmap),
                  pl.BlockSpec((1, D, de), wmap),
                  pl.BlockSpec((1, de, D), wmap),
                  tile,
                  pl.BlockSpec((1, 1, D), lambda c, b, *_: (c // (n_chunks // B), 0, 0)),
                  pl.BlockSpec((1, D), lambda c, b, *_: (0, 0))],
        out_specs=tile,
        scratch_shapes=[pltpu.VMEM((tc * SUBLANES, LANES), f32),
                        pltpu.VMEM((EXPERT_ROWS * words, LANES), jnp.uint32),
                        pltpu.VMEM((EXPERT_ROWS * SUBLANES, LANES), f32)],
    )
    out = pl.pallas_call(
        functools.partial(_moe_kernel, normalize=normalize),
        grid_spec=grid_spec,
        out_shape=jax.ShapeDtypeStruct((T, D), f32),
        compiler_params=_params(("arbitrary", "arbitrary")),
        name="moe",
    )(starts, counts, stok, sw, h2p, w_gate, w_up, w_down,
      x1.reshape(T, D), g2.reshape(B, 1, D), fg.reshape(1, D))
    return out.reshape(B, S, D)


def kernel(x, c, rel_bias, norm1_g, norm2_g, w_ada, b_ada, w_in, b_gate, conv_w, w_attn_out,
           w_conv_out, w_o, w_router_group, b_router_group, w_router_expert, b_router_expert,
           w_exp_gate, w_exp_up, w_exp_down, final_norm_g):
    B, S, D = x.shape
    depth = w_ada.shape[0]
    assert S % ROW_TILE == 0 and ROW_TILE % MOBA_BLOCK == 0 and (B * S) % (MOE_CHUNKS * ROW_TILE) == 0
    own, adj = _bias_tiles(rel_bias)
    for l in range(depth):
        mod = _mod(c, w_ada[l], b_ada[l])
        sh1, sc1, g1, sh2, sc2, g2 = jnp.split(mod, 6, axis=-1)
        qT, k4, vT4, km, mc, ga = _inproj(x, norm1_g[l], sc1, sh1, w_in[l], b_gate[l], conv_w[l],
                                          w_conv_out[l])
        a, expert_w = _attention(qT, k4, vT4, km, own, adj, [w_exp_gate[l], w_exp_up[l], w_exp_down[l]])
        x1, h2p, route = _post(a, mc, ga, x, g1, sc2, sh2, norm2_g[l], w_attn_out[l], w_o[l],
                               w_router_group[l], b_router_group[l], w_router_expert[l],
                               b_router_expert[l])
        x = _moe(route, h2p, *expert_w, x1, g2, final_norm_g, normalize=(l + 1 == depth))
    return x
```

```python
import functools
import math

import numpy as np
import jax
import jax.numpy as jnp
from jax import lax
from jax.experimental import pallas as pl
from jax.experimental.pallas import tpu as pltpu

f32 = jnp.float32
bf16 = jnp.bfloat16

N_HEADS = 8
HEAD_DIM = 64
ATTN_WIDTH = N_HEADS * HEAD_DIM
MOBA_BLOCK = 256
MOBA_TOPK = 3
CONV_K = 3
N_BUCKETS = 32
MAX_EXACT = N_BUCKETS // 2
MAX_DISTANCE = 128
N_GROUPS = 4
EXPERTS_PER_GROUP = 8
N_EXPERTS = N_GROUPS * EXPERTS_PER_GROUP
TOP_K_EXPERTS = 2
NORM_EPS = 1e-6
NEG_INF = -1e30

LANES = 128
SUBLANES = 8
ROW_TILE = 1024
POST_SPLIT = 2
EXPERT_ROWS = 320
ROW_UNROLL = 8
MOE_CHUNKS = 4
FAR_BLOCKS_PER_ITER = 2
FAR_SUBSTEPS = 2
SUM_ROWS = 16
LOG2E = 1.4426950408889634
VMEM_LIMIT = 56 * 1024 * 1024


def _params(sem, vmem=VMEM_LIMIT):
    return pltpu.CompilerParams(dimension_semantics=sem, vmem_limit_bytes=vmem)


def _mod_kernel(c_ref, w_ref, b_ref, o_ref, *, batch):
    act = jax.nn.silu(c_ref[...]).T
    w = w_ref[...]
    rows = [jnp.sum(act[:, r:r + 1] * w, axis=0, keepdims=True) for r in range(batch)]
    rows.append(jnp.zeros((o_ref.shape[0] - batch, w.shape[1]), f32))
    o_ref[...] = jnp.concatenate(rows, axis=0) + b_ref[...]


def _mod(c, w_ada, b_ada):
    B, D = c.shape
    n_out = w_ada.shape[1]
    rows = 8
    cp = jnp.zeros((rows, D), f32).at[:B].set(c)
    tn = 1024
    out = pl.pallas_call(
        functools.partial(_mod_kernel, batch=B),
        grid=(n_out // tn,),
        in_specs=[pl.BlockSpec((rows, D), lambda j: (0, 0)),
                  pl.BlockSpec((D, tn), lambda j: (0, j)),
                  pl.BlockSpec((1, tn), lambda j: (0, j))],
        out_specs=pl.BlockSpec((rows, tn), lambda j: (0, j)),
        out_shape=jax.ShapeDtypeStruct((rows, n_out), f32),
        compiler_params=_params(("arbitrary",)),
        name="mod",
    )(cp, w_ada, b_ada.reshape(1, n_out))
    return out[:B]


def _bucket_thresholds():
    n = np.arange(2 * MOBA_BLOCK)

    def buckets(ft):
        nf = np.maximum(n, 1).astype(ft)
        large = MAX_EXACT + (np.log(nf / ft(MAX_EXACT)) / ft(math.log(MAX_DISTANCE / MAX_EXACT))
                             * ft(N_BUCKETS - MAX_EXACT)).astype(np.int32)
        return np.where(n < MAX_EXACT, n, np.minimum(large, N_BUCKETS - 1))

    b = buckets(np.float32)
    assert np.array_equal(b, buckets(np.float64)) and np.all(np.diff(b) >= 0)
    assert b[-1] == N_BUCKETS - 1 and b[MOBA_BLOCK + 1] == N_BUCKETS - 1
    return [int(np.argmax(b >= j)) for j in range(N_BUCKETS)]


def _bias_kernel(rb_ref, own_ref, adj_ref, *, thresholds):
    h = pl.program_id(0)
    kk = lax.broadcasted_iota(jnp.int32, (MOBA_BLOCK, MOBA_BLOCK), 0)
    qq = lax.broadcasted_iota(jnp.int32, (MOBA_BLOCK, MOBA_BLOCK), 1)
    far = rb_ref[N_BUCKETS - 1, h]

    def table(dist):
        acc = jnp.full(dist.shape, rb_ref[0, h], f32)
        for j in range(1, N_BUCKETS):
            acc = jnp.where(dist >= thresholds[j], rb_ref[j, h], acc)
        return (acc - far) * LOG2E

    own_ref[0] = table(qq - kk)
    adj_ref[0] = table(qq - kk + MOBA_BLOCK)


def _bias_tiles(rel_bias):
    shp = jax.ShapeDtypeStruct((N_HEADS, MOBA_BLOCK, MOBA_BLOCK), f32)
    spec = pl.BlockSpec((1, MOBA_BLOCK, MOBA_BLOCK), lambda h: (h, 0, 0))
    return pl.pallas_call(
        functools.partial(_bias_kernel, thresholds=_bucket_thresholds()),
        grid=(N_HEADS,),
        in_specs=[pl.BlockSpec(memory_space=pltpu.SMEM)],
        out_specs=(spec, spec),
        out_shape=(shp, shp),
        compiler_params=_params(("arbitrary",)),
        name="bias",
    )(rel_bias)


def _ada_norm(x, g, scale, shift):
    y = x * lax.rsqrt(jnp.mean(x * x, axis=-1, keepdims=True) + NORM_EPS)
    return (y * g) * (1.0 + scale) + shift


def _inproj_kernel(x_ref, g_ref, sc_ref, sh_ref, wqT_ref, wk_ref, wvT_ref, wc_ref, wgl_ref,
                   wco_ref, cw_ref, bg_ref,
                   qT_ref, k_ref, vT_ref, km_ref, mc_ref, ga_ref, ubuf):
    tm = x_ref.shape[1]
    cwid = wco_ref.shape[0]
    d = x_ref.shape[2]
    s = pl.program_id(1)
    hb = _ada_norm(x_ref[0], g_ref[...], sc_ref[0], sh_ref[0]).astype(bf16)

    nt = (((1,), (1,)), ((), ()))
    qT_ref[0] = lax.dot_general(wqT_ref[...], hb, nt, preferred_element_type=f32).astype(bf16)
    k = jnp.dot(hb, wk_ref[...], preferred_element_type=f32)
    vT = lax.dot_general(wvT_ref[...], hb, nt, preferred_element_type=f32).astype(bf16)
    for j in range(tm // MOBA_BLOCK):
        rows = slice(j * MOBA_BLOCK, (j + 1) * MOBA_BLOCK)
        k_ref[0, j] = k[rows].astype(bf16)
        vT_ref[0, j] = vT[:, rows]
        km_ref[0, 0, j:j + 1, :] = jnp.mean(k[rows], axis=0, keepdims=True)

    cbx = jnp.dot(hb, wc_ref[...], preferred_element_type=f32)
    u = cbx[:, :cwid] * cbx[:, 2 * cwid:]

    @pl.when(s == 0)
    def _():
        ubuf[0:8, :] = jnp.zeros((8, cwid), f32)

    ubuf[8:8 + tm, :] = u
    cw = cw_ref[...]
    conv = cw[0:1] * ubuf[6:6 + tm, :] + cw[1:2] * ubuf[7:7 + tm, :] + cw[2:3] * u
    ubuf[0:8, :] = ubuf[tm:tm + 8, :]
    cv = (cbx[:, cwid:2 * cwid] * conv).astype(bf16)
    y_conv = jnp.dot(cv, wco_ref[...], preferred_element_type=f32)

    gl = jnp.dot(hb, wgl_ref[...], preferred_element_type=f32) + bg_ref[...]
    ga_ref[0] = jax.nn.sigmoid(gl[:, :d]).astype(bf16)
    mc_ref[0] = (jax.nn.sigmoid(gl[:, d:]) * y_conv).astype(bf16)


def _inproj(x, g, sc, sh, w_in, b_gate, conv_w, w_conv_out):
    B, S, D = x.shape
    tm = ROW_TILE
    nb = S // MOBA_BLOCK
    bpt = tm // MOBA_BLOCK
    aw = ATTN_WIDTH
    cwid = conv_w.shape[1]
    scale = HEAD_DIM ** -0.5 * LOG2E
    wqT = (w_in[:, :aw] * scale).T.astype(bf16)
    wk = w_in[:, aw:2 * aw].astype(bf16)
    wvT = w_in[:, 2 * aw:3 * aw].T.astype(bf16)
    wc = w_in[:, 3 * aw:3 * aw + 3 * cwid].astype(bf16)
    wgl = w_in[:, 3 * aw + 3 * cwid:].astype(bf16)
    wco = w_conv_out.astype(bf16)

    const = lambda shape: pl.BlockSpec(shape, lambda b, s: (0,) * len(shape))
    vec = pl.BlockSpec((1, 1, D), lambda b, s: (b, 0, 0))
    row = lambda width: pl.BlockSpec((1, tm, width), lambda b, s: (b, s, 0))
    outs = pl.pallas_call(
        _inproj_kernel,
        grid=(B, S // tm),
        in_specs=[row(D), const((1, D)), vec, vec,
                  const((aw, D)), const((D, aw)), const((aw, D)), const((D, 3 * cwid)),
                  const((D, 2 * D)), const((cwid, D)), const((CONV_K, cwid)), const((1, 2 * D))],
        out_specs=(pl.BlockSpec((1, aw, tm), lambda b, s: (b, 0, s)),
                   pl.BlockSpec((1, bpt, MOBA_BLOCK, aw), lambda b, s: (b, s, 0, 0)),
                   pl.BlockSpec((1, bpt, aw, MOBA_BLOCK), lambda b, s: (b, s, 0, 0)),
                   pl.BlockSpec((1, 1, bpt, aw), lambda b, s: (b, s, 0, 0)),
                   row(D), row(D)),
        out_shape=(jax.ShapeDtypeStruct((B, aw, S), bf16),
                   jax.ShapeDtypeStruct((B, nb, MOBA_BLOCK, aw), bf16),
                   jax.ShapeDtypeStruct((B, nb, aw, MOBA_BLOCK), bf16),
                   jax.ShapeDtypeStruct((B, S // tm, bpt, aw), f32),
                   jax.ShapeDtypeStruct((B, S, D), bf16),
                   jax.ShapeDtypeStruct((B, S, D), bf16)),
        scratch_shapes=[pltpu.VMEM((tm + 8, cwid), f32)],
        compiler_params=_params(("parallel", "arbitrary")),
        name="inproj",
    )(x, g.reshape(1, D), sc.reshape(B, 1, D), sh.reshape(B, 1, D),
      wqT, wk, wvT, wc, wgl, wco, conv_w, b_gate.reshape(1, 2 * D))
    qT, k4, vT4, km, mc, ga = outs
    return qT, k4, vT4, km.reshape(B, nb, aw), mc, ga


def _attn_kernel(qT_ref, k_ref, vT_ref, km_ref, own_ref, adj_ref, *side_refs):
    n_side = len(side_refs) // 2
    o_ref = side_refs[n_side]
    for src, dst in zip(side_refs[:n_side], side_refs[n_side + 1:]):
        dst[...] = src[...].astype(bf16)

    i = pl.program_id(1)
    bs = MOBA_BLOCK
    nb = km_ref.shape[1]
    n_heads = qT_ref.shape[1] // HEAD_DIM
    heads = range(n_heads)
    blk = lax.broadcasted_iota(jnp.int32, (nb, bs), 0)

    qcat = []
    for h in heads:
        pair, a = divmod(h, LANES // HEAD_DIM)
        qT = qT_ref[0, pair * LANES:(pair + 1) * LANES, :]
        row = lax.broadcasted_iota(jnp.int32, qT.shape, 0)
        qa = jnp.where((row >= a * HEAD_DIM) & (row < (a + 1) * HEAD_DIM), qT, jnp.zeros_like(qT))
        km = km_ref[0, :, pair * LANES:(pair + 1) * LANES]
        km_hi = km.astype(bf16)
        km_mid = (km - km_hi.astype(f32)).astype(bf16)
        km_lo = (km - km_hi.astype(f32) - km_mid.astype(f32)).astype(bf16)
        g = (jnp.dot(km_hi, qa, preferred_element_type=f32)
             + (jnp.dot(km_mid, qa, preferred_element_type=f32) + jnp.dot(km_lo, qa, preferred_element_type=f32)))
        g = jnp.where(blk < i, g, NEG_INF)
        fq = jnp.full((nb, bs), NEG_INF, f32)
        for _ in range(MOBA_TOPK):
            top = jnp.max(g, axis=0, keepdims=True)
            first = jnp.min(jnp.where(g == top, blk, nb), axis=0, keepdims=True)
            pick = blk == first
            fq = jnp.where(pick, 0.0, fq)
            g = jnp.where(pick, -jnp.inf, g)
        fq = jnp.where(blk < i, fq, 0.0).astype(bf16)
        pad = jnp.zeros((LANES - nb, bs), bf16)
        qcat.append(jnp.concatenate([qa, fq, pad], axis=0))

    lane = lax.broadcasted_iota(jnp.int32, (bs, LANES), 1)
    causal = (lax.broadcasted_iota(jnp.int32, (bs, bs), 0) <= lax.broadcasted_iota(jnp.int32, (bs, bs), 1))
    ones_rows = jnp.ones((SUM_ROWS, bs), bf16)

    def step(blocks, state):
        onehots = [jnp.where(lane == n, 1.0, 0.0).astype(bf16) for n, _ in blocks]
        scores = []
        for h in heads:
            pair = h // (LANES // HEAD_DIM)
            cols = slice(pair * LANES, (pair + 1) * LANES)
            scores.append([jnp.dot(jnp.concatenate([k_ref[0, n, :, cols], onehot], axis=1), qcat[h],
                                   preferred_element_type=f32) for (n, _), onehot in zip(blocks, onehots)])
        out = []
        for h in heads:
            sTs = []
            for (n, kind), sT in zip(blocks, scores[h]):
                if kind == "own":
                    sT = jnp.where(causal, own_ref[h] + sT, NEG_INF)
                elif kind == "adj":
                    sT = adj_ref[h] + sT
                sTs.append(sT)
            m_new = None if state is None else state[h][0]
            for sT in sTs:
                top = jnp.max(sT, axis=0, keepdims=True)
                m_new = top if m_new is None else jnp.maximum(m_new, top)
            acc = None if state is None else jnp.exp2(state[h][0] - m_new) * state[h][1]
            for (n, _), sT in zip(blocks, sTs):
                p = jnp.exp2(sT - m_new).astype(bf16)
                v_ext = jnp.concatenate([vT_ref[0, n, h * HEAD_DIM:(h + 1) * HEAD_DIM, :], ones_rows], axis=0)
                pv = jnp.dot(v_ext, p, preferred_element_type=f32)
                acc = pv if acc is None else acc + pv
            out.append((m_new, acc))
        return tuple(out)

    state = lax.cond(i >= 1,
                     lambda: step([(i, "own"), (i - 1, "adj")], None),
                     lambda: step([(i, "own")], None))
    n_far = jnp.maximum(i - 1, 0)
    per = FAR_BLOCKS_PER_ITER
    def far_iter(it, st):
        for u in range(FAR_SUBSTEPS):
            st = step([((it * FAR_SUBSTEPS + u) * per + j, "far") for j in range(per)], st)
        return st

    state = lax.fori_loop(0, n_far // (per * FAR_SUBSTEPS), far_iter, state)
    per = per * FAR_SUBSTEPS
    done = n_far - n_far % per
    size = per // 2
    while size >= 1:
        take = (n_far % per) & size
        state = lax.cond(take > 0,
                         lambda st, d=done, sz=size: step([(d + j, "far") for j in range(sz)], st),
                         lambda st: st, state)
        done = done + take
        size //= 2
    outT = jnp.concatenate([acc[:HEAD_DIM] / acc[HEAD_DIM:HEAD_DIM + 1] for (_, acc) in state], axis=0)
    o_ref[0] = outT.T.astype(bf16)


def _attention(qT, k4, vT4, km, own, adj, side_weights):
    B, aw, S = qT.shape
    nb = S // MOBA_BLOCK
    assert MOBA_TOPK < nb <= LANES
    n_heads = aw // HEAD_DIM
    steps = B * nb
    whole = lambda shape: pl.BlockSpec(shape, lambda b, i: (b,) + (0,) * (len(shape) - 1))
    table = pl.BlockSpec((n_heads, MOBA_BLOCK, MOBA_BLOCK), lambda b, i: (0, 0, 0))
    slabs = [w.reshape(steps, -1, w.shape[-1]) for w in side_weights]
    slab_specs = [pl.BlockSpec((1,) + w.shape[1:], lambda b, i: (b * nb + i, 0, 0)) for w in slabs]
    outs = pl.pallas_call(
        _attn_kernel,
        grid=(B, nb),
        in_specs=[pl.BlockSpec((1, aw, MOBA_BLOCK), lambda b, i: (b, 0, i)),
                  whole((1, nb, MOBA_BLOCK, aw)), whole((1, nb, aw, MOBA_BLOCK)), whole((1, nb, aw)),
                  table, table] + slab_specs,
        out_specs=[pl.BlockSpec((1, MOBA_BLOCK, aw), lambda b, i: (b, i, 0))] + slab_specs,
        out_shape=[jax.ShapeDtypeStruct((B, S, aw), bf16)]
                  + [jax.ShapeDtypeStruct(w.shape, bf16) for w in slabs],
        compiler_params=_params(("parallel", "arbitrary")),
        name="attn",
    )(qT, k4, vT4, km, own, adj, *slabs)
    return outs[0], [o.reshape(w.shape) for o, w in zip(outs[1:], side_weights)]


def _post_kernel(a_ref, mc_ref, ga_ref, x_ref, g1_ref, sc_ref, sh_ref, ng_ref,
                 wao_ref, wo_ref, wrh_ref, wrl_ref, br_ref, x1_ref, h2p_ref, route_ref):
    tm = x_ref.shape[1]
    d = x_ref.shape[2]
    sub = tm // POST_SPLIT
    lane = lax.broadcasted_iota(jnp.int32, (sub, LANES), 1)
    ninf = -jnp.inf
    for t in range(POST_SPLIT):
        rows = slice(t * sub, (t + 1) * sub)
        ya = jnp.dot(a_ref[0, rows, :], wao_ref[...], preferred_element_type=f32)
        merged = ga_ref[0, rows, :].astype(f32) * ya + mc_ref[0, rows, :].astype(f32)
        z = jnp.dot(merged.astype(bf16), wo_ref[...], preferred_element_type=f32)
        x1 = x_ref[0, rows, :] + g1_ref[0] * z
        x1_ref[0, rows, :] = x1
        h2 = _ada_norm(x1, ng_ref[...], sc_ref[0], sh_ref[0])
        h_hi = h2.astype(bf16)
        h_hi32 = h_hi.astype(f32)

        bits = pltpu.bitcast(h_hi32, jnp.uint32)
        packed = (bits[:, :d // 2] >> 16) | (bits[:, d // 2:] & jnp.uint32(0xFFFF0000))
        chunks = packed.shape[1] // LANES
        for j in range(chunks):
            h2p_ref[pl.ds(t * sub * chunks + j, sub, stride=chunks), :] = packed[:, j * LANES:(j + 1) * LANES]

        h_lo = (h2 - h_hi32).astype(bf16)
        logits = (jnp.dot(h_hi, wrh_ref[...], preferred_element_type=f32)
                  + (jnp.dot(h_lo, wrh_ref[...], preferred_element_type=f32)
                     + jnp.dot(h_hi, wrl_ref[...], preferred_element_type=f32))) + br_ref[...]
        gl = jnp.where(lane < N_GROUPS, logits, ninf)
        gmax = jnp.max(gl, axis=1, keepdims=True)
        g_sel = jnp.min(jnp.where(gl == gmax, lane, LANES), axis=1, keepdims=True)
        g_w = 1.0 / jnp.sum(jnp.exp(gl - gmax), axis=1, keepdims=True)
        off = lane - (N_GROUPS + EXPERTS_PER_GROUP * g_sel)
        in_group = (off >= 0) & (off < EXPERTS_PER_GROUP)
        el = jnp.where(in_group, logits, ninf)
        ex = jnp.exp(el - jnp.max(el, axis=1, keepdims=True))
        ep = jnp.where(in_group, ex / jnp.sum(ex, axis=1, keepdims=True), -1.0)
        p1 = jnp.max(ep, axis=1, keepdims=True)
        i1 = jnp.min(jnp.where(ep == p1, lane, LANES), axis=1, keepdims=True)
        ep2 = jnp.where(lane == i1, -1.0, ep)
        p2 = jnp.max(ep2, axis=1, keepdims=True)
        i2 = jnp.min(jnp.where(ep2 == p2, lane, LANES), axis=1, keepdims=True)
        tot = p1 + p2
        w1 = p1 / tot * g_w
        w2 = p2 / tot * g_w
        e1 = (i1 - N_GROUPS).astype(f32)
        e2 = (i2 - N_GROUPS).astype(f32)
        route = jnp.where(lane == 0, e1, jnp.where(lane == 1, e2,
                          jnp.where(lane == 2, w1, jnp.where(lane == 3, w2, 0.0))))
        route_ref[:, rows] = route.T[:SUBLANES, :]


def _post(a, mc, ga, x, g1, sc2, sh2, ng, w_attn_out, w_o, w_rg, b_rg, w_re, b_re):
    B, S, D = x.shape
    tm = ROW_TILE
    aw = a.shape[2]
    nt = S // tm
    wr = jnp.zeros((D, LANES), f32).at[:, :N_GROUPS].set(w_rg).at[:, N_GROUPS:N_GROUPS + N_EXPERTS].set(w_re)
    br = jnp.zeros((1, LANES), f32).at[0, :N_GROUPS].set(b_rg).at[0, N_GROUPS:N_GROUPS + N_EXPERTS].set(b_re)
    wr_hi = wr.astype(bf16)
    wr_lo = (wr - wr_hi.astype(f32)).astype(bf16)
    const = lambda shape: pl.BlockSpec(shape, lambda b, s: (0,) * len(shape))
    vec = pl.BlockSpec((1, 1, D), lambda b, s: (b, 0, 0))
    row = lambda width: pl.BlockSpec((1, tm, width), lambda b, s: (b, s, 0))
    flat = lambda width, rows_per_token=1: pl.BlockSpec((tm * rows_per_token, width), lambda b, s: (b * nt + s, 0))
    words = D // 2 // LANES
    return pl.pallas_call(
        _post_kernel,
        grid=(B, nt),
        in_specs=[row(aw), row(D), row(D), row(D), vec, vec, vec, const((1, D)),
                  const((aw, D)), const((D, D)), const((D, LANES)), const((D, LANES)), const((1, LANES))],
        out_specs=(row(D), flat(LANES, words), pl.BlockSpec((SUBLANES, tm), lambda b, s: (0, b * nt + s))),
        out_shape=(jax.ShapeDtypeStruct((B, S, D), f32),
                   jax.ShapeDtypeStruct((B * S * words, LANES), jnp.uint32),
                   jax.ShapeDtypeStruct((SUBLANES, B * S), f32)),
        compiler_params=_params(("parallel", "parallel")),
        name="post",
    )(a, mc, ga, x, g1.reshape(B, 1, D), sc2.reshape(B, 1, D), sh2.reshape(B, 1, D), ng.reshape(1, D),
      w_attn_out.astype(bf16), w_o.astype(bf16), wr_hi, wr_lo, br)


def _dispatch_tables(route, n_chunks):
    T = route.shape[1]
    tc = T // n_chunks
    na = tc * TOP_K_EXPERTS
    per_chunk = lambda rows: rows.reshape(TOP_K_EXPERTS, n_chunks, tc).transpose(1, 0, 2).reshape(n_chunks, na)
    e = per_chunk(route[:TOP_K_EXPERTS]).astype(jnp.int32)
    w = per_chunk(route[TOP_K_EXPERTS:2 * TOP_K_EXPERTS])
    order = jnp.argsort(e, axis=1, stable=True).astype(jnp.int32)
    stok = order % tc
    sw = jnp.take_along_axis(w, order, axis=1)
    counts = jnp.sum((e[:, None, :] == jnp.arange(N_EXPERTS)[None, :, None]).astype(jnp.int32), axis=2)
    starts = jnp.cumsum(counts, axis=1) - counts
    flat = lambda t: t.reshape(-1).astype(jnp.int32)
    return flat(starts), flat(counts), stok.reshape(n_chunks, 1, na), sw.reshape(n_chunks, 1, na)


def _moe_kernel(start_ref, count_ref, stok_ref, sw_ref, h2p_ref, wg_ref, wu_ref, wd_ref,
                x1_ref, g2_ref, fg_ref, o_ref, yacc, xs, outs, *, normalize):
    c = pl.program_id(0)
    b = pl.program_id(1)
    block_rows = outs.shape[0] // SUBLANES
    words = xs.shape[0] // block_rows
    half = words * LANES
    run = c * N_EXPERTS + jnp.minimum(b, N_EXPERTS - 1)
    count = jnp.where(b < N_EXPERTS, count_ref[run], 0)
    first = start_ref[run]

    @pl.when(b == 0)
    def _():
        yacc[...] = jnp.zeros(yacc.shape, f32)

    @pl.when((b == 0) & (c == 0))
    def _():
        xs[...] = jnp.zeros(xs.shape, jnp.uint32)

    def token_rows(index, rows_per_token):
        return pl.ds(pl.multiple_of(index * rows_per_token, rows_per_token), rows_per_token)

    def block(blk, carry):
        st = first + blk * block_rows
        n = jnp.minimum(count - blk * block_rows, block_rows)

        def for_rows(rows_fn):
            full = n // ROW_UNROLL

            def group(g, carry):
                rows_fn([g * ROW_UNROLL + j for j in range(ROW_UNROLL)], st + g * ROW_UNROLL)
                return carry

            def single(j, carry):
                rows_fn([full * ROW_UNROLL + j], st + full * ROW_UNROLL + j)
                return carry

            lax.fori_loop(0, full, group, 0)
            lax.fori_loop(0, n - full * ROW_UNROLL, single, 0)

        def gather(rs, i0):
            for k, r in enumerate(rs):
                xs[token_rows(r, words), :] = h2p_ref[token_rows(stok_ref[0, 0, i0 + k], words), :]

        for_rows(gather)
        lo, hi = [], []
        for j in range(words):
            packed = xs[pl.ds(j, block_rows, stride=words), :]
            lo.append(pltpu.bitcast(packed << 16, f32).astype(bf16))
            hi.append(pltpu.bitcast(packed & jnp.uint32(0xFFFF0000), f32).astype(bf16))
        x_lo = jnp.concatenate(lo, axis=1)
        x_hi = jnp.concatenate(hi, axis=1)

        def up(w_ref):
            return (jnp.dot(x_lo, w_ref[0, :half, :], preferred_element_type=f32)
                    + jnp.dot(x_hi, w_ref[0, half:, :], preferred_element_type=f32))

        hidden = (jax.nn.silu(up(wg_ref)) * up(wu_ref)).astype(bf16)
        out = jnp.dot(hidden, wd_ref[0], preferred_element_type=f32)
        for j in range(SUBLANES):
            outs[pl.ds(j, block_rows, stride=SUBLANES), :] = out[:, j * LANES:(j + 1) * LANES]

        def scatter(rs, i0):
            toks = [stok_ref[0, 0, i0 + k] for k in range(len(rs))]
            acc = [yacc[token_rows(tok, SUBLANES), :] for tok in toks]
            for k, (r, tok, a) in enumerate(zip(rs, toks, acc)):
                yacc[token_rows(tok, SUBLANES), :] = a + sw_ref[0, 0, i0 + k] * outs[token_rows(r, SUBLANES), :]

        for_rows(scatter)
        return carry

    lax.fori_loop(0, (count + block_rows - 1) // block_rows, block, 0)

    @pl.when(b >= N_EXPERTS)
    def _():
        rows = o_ref.shape[0]
        start = pl.multiple_of((b - N_EXPERTS) * rows * SUBLANES, rows * SUBLANES)
        y = jnp.concatenate([yacc[pl.ds(start + j, rows, stride=SUBLANES), :] for j in range(SUBLANES)], axis=1)
        x2 = x1_ref[...] + g2_ref[0] * y
        if normalize:
            x2 = (x2 * lax.rsqrt(jnp.mean(x2 * x2, axis=-1, keepdims=True) + NORM_EPS)) * fg_ref[...]
        o_ref[...] = x2


def _moe(route, h2p, w_gate, w_up, w_down, x1, g2, fg, normalize):
    B, S, D = x1.shape
    T = B * S
    words = h2p.shape[0] // T
    assert D == words * 2 * LANES == SUBLANES * LANES
    de = w_gate.shape[2]
    n_chunks = MOE_CHUNKS
    tc = T // n_chunks
    fin_rows = ROW_TILE // 2
    n_fin = tc // fin_rows
    assert n_chunks % B == 0 and tc % fin_rows == 0 and w_gate.shape[0] == N_EXPERTS
    starts, counts, stok, sw = _dispatch_tables(route, n_chunks)
    na = stok.shape[2]
    wmap = lambda c, b, *_: (jnp.minimum(b, N_EXPERTS - 1), 0, 0)
    smem = lambda: pl.BlockSpec((1, 1, na), lambda c, b, *_: (c, 0, 0), memory_space=pltpu.SMEM)
    tile = pl.BlockSpec((fin_rows, D), lambda c, b, *_: (c * n_fin + jnp.maximum(b - N_EXPERTS, 0), 0))
    grid_spec = pltpu.PrefetchScalarGridSpec(
        num_scalar_prefetch=2,
        grid=(n_chunks, N_EXPERTS + n_fin),
        in_specs=[smem(), smem(),
                  pl.BlockSpec((tc * words, LANES), lambda c, b, *_: (c, 0)),
                  pl.BlockSpec((1, D, de), wmap),
                  pl.BlockSpec((1, D, de), wmap),
                  pl.BlockSpec((1, de, D), wmap),
                  tile,
                  pl.BlockSpec((1, 1, D), lambda c, b, *_: (c // (n_chunks // B), 0, 0)),
                  pl.BlockSpec((1, D), lambda c, b, *_: (0, 0))],
        out_specs=tile,
        scratch_shapes=[pltpu.VMEM((tc * SUBLANES, LANES), f32),
                        pltpu.VMEM((EXPERT_ROWS * words, LANES), jnp.uint32),
                        pltpu.VMEM((EXPERT_ROWS * SUBLANES, LANES), f32)],
    )
    out = pl.pallas_call(
        functools.partial(_moe_kernel, normalize=normalize),
        grid_spec=grid_spec,
        out_shape=jax.ShapeDtypeStruct((T, D), f32),
        compiler_params=_params(("arbitrary", "arbitrary")),
        name="moe",
    )(starts, counts, stok, sw, h2p, w_gate, w_up, w_down,
      x1.reshape(T, D), g2.reshape(B, 1, D), fg.reshape(1, D))
    return out.reshape(B, S, D)


def kernel(x, c, rel_bias, norm1_g, norm2_g, w_ada, b_ada, w_in, b_gate, conv_w, w_attn_out,
           w_conv_out, w_o, w_router_group, b_router_group, w_router_expert, b_router_expert,
           w_exp_gate, w_exp_up, w_exp_down, final_norm_g):
    B, S, D = x.shape
    depth = w_ada.shape[0]
    assert S % ROW_TILE == 0 and ROW_TILE % MOBA_BLOCK == 0 and (B * S) % (MOE_CHUNKS * ROW_TILE) == 0
    own, adj = _bias_tiles(rel_bias)
    for l in range(depth):
        mod = _mod(c, w_ada[l], b_ada[l])
        sh1, sc1, g1, sh2, sc2, g2 = jnp.split(mod, 6, axis=-1)
        qT, k4, vT4, km, mc, ga = _inproj(x, norm1_g[l], sc1, sh1, w_in[l], b_gate[l], conv_w[l],
                                          w_conv_out[l])
        a, expert_w = _attention(qT, k4, vT4, km, own, adj, [w_exp_gate[l], w_exp_up[l], w_exp_down[l]])
        x1, h2p, route = _post(a, mc, ga, x, g1, sc2, sh2, norm2_g[l], w_attn_out[l], w_o[l],
                               w_router_group[l], b_router_group[l], w_router_expert[l],
                               b_router_expert[l])
        x = _moe(route, h2p, *expert_w, x1, g2, final_norm_g, normalize=(l + 1 == depth))
    return x
```

```python
import functools
import math

import numpy as np
import jax
import jax.numpy as jnp
from jax import lax
from jax.experimental import pallas as pl
from jax.experimental.pallas import tpu as pltpu

f32 = jnp.float32
bf16 = jnp.bfloat16

N_HEADS = 8
HEAD_DIM = 64
ATTN_WIDTH = N_HEADS * HEAD_DIM
MOBA_BLOCK = 256
MOBA_TOPK = 3
CONV_K = 3
N_BUCKETS = 32
MAX_EXACT = N_BUCKETS // 2
MAX_DISTANCE = 128
N_GROUPS = 4
EXPERTS_PER_GROUP = 8
N_EXPERTS = N_GROUPS * EXPERTS_PER_GROUP
TOP_K_EXPERTS = 2
NORM_EPS = 1e-6
NEG_INF = -1e30

LANES = 128
SUBLANES = 8
ROW_TILE = 1024
POST_SPLIT = 2
EXPERT_ROWS = 320
ROW_UNROLL = 8
MOE_CHUNKS = 4
FAR_BLOCKS_PER_ITER = 2
FAR_SUBSTEPS = 2
SUM_ROWS = 16
LOG2E = 1.4426950408889634
VMEM_LIMIT = 56 * 1024 * 1024


def _params(sem, vmem=VMEM_LIMIT):
    return pltpu.CompilerParams(dimension_semantics=sem, vmem_limit_bytes=vmem)


def _mod_kernel(c_ref, w_ref, b_ref, o_ref, *, batch):
    act = jax.nn.silu(c_ref[...]).T
    w = w_ref[...]
    rows = [jnp.sum(act[:, r:r + 1] * w, axis=0, keepdims=True) for r in range(batch)]
    rows.append(jnp.zeros((o_ref.shape[0] - batch, w.shape[1]), f32))
    o_ref[...] = jnp.concatenate(rows, axis=0) + b_ref[...]


def _mod(c, w_ada, b_ada):
    B, D = c.shape
    n_out = w_ada.shape[1]
    rows = 8
    cp = jnp.zeros((rows, D), f32).at[:B].set(c)
    tn = 1024
    out = pl.pallas_call(
        functools.partial(_mod_kernel, batch=B),
        grid=(n_out // tn,),
        in_specs=[pl.BlockSpec((rows, D), lambda j: (0, 0)),
                  pl.BlockSpec((D, tn), lambda j: (0, j)),
                  pl.BlockSpec((1, tn), lambda j: (0, j))],
        out_specs=pl.BlockSpec((rows, tn), lambda j: (0, j)),
        out_shape=jax.ShapeDtypeStruct((rows, n_out), f32),
        compiler_params=_params(("arbitrary",)),
        name="mod",
    )(cp, w_ada, b_ada.reshape(1, n_out))
    return out[:B]


def _bucket_thresholds():
    n = np.arange(2 * MOBA_BLOCK)

    def buckets(ft):
        nf = np.maximum(n, 1).astype(ft)
        large = MAX_EXACT + (np.log(nf / ft(MAX_EXACT)) / ft(math.log(MAX_DISTANCE / MAX_EXACT))
                             * ft(N_BUCKETS - MAX_EXACT)).astype(np.int32)
        return np.where(n < MAX_EXACT, n, np.minimum(large, N_BUCKETS - 1))

    b = buckets(np.float32)
    assert np.array_equal(b, buckets(np.float64)) and np.all(np.diff(b) >= 0)
    assert b[-1] == N_BUCKETS - 1 and b[MOBA_BLOCK + 1] == N_BUCKETS - 1
    return [int(np.argmax(b >= j)) for j in range(N_BUCKETS)]


def _bias_kernel(rb_ref, own_ref, adj_ref, *, thresholds):
    h = pl.program_id(0)
    kk = lax.broadcasted_iota(jnp.int32, (MOBA_BLOCK, MOBA_BLOCK), 0)
    qq = lax.broadcasted_iota(jnp.int32, (MOBA_BLOCK, MOBA_BLOCK), 1)
    far = rb_ref[N_BUCKETS - 1, h]

    def table(dist):
        acc = jnp.full(dist.shape, rb_ref[0, h], f32)
        for j in range(1, N_BUCKETS):
            acc = jnp.where(dist >= thresholds[j], rb_ref[j, h], acc)
        return (acc - far) * LOG2E

    own_ref[0] = table(qq - kk)
    adj_ref[0] = table(qq - kk + MOBA_BLOCK)


def _bias_tiles(rel_bias):
    shp = jax.ShapeDtypeStruct((N_HEADS, MOBA_BLOCK, MOBA_BLOCK), f32)
    spec = pl.BlockSpec((1, MOBA_BLOCK, MOBA_BLOCK), lambda h: (h, 0, 0))
    return pl.pallas_call(
        functools.partial(_bias_kernel, thresholds=_bucket_thresholds()),
        grid=(N_HEADS,),
        in_specs=[pl.BlockSpec(memory_space=pltpu.SMEM)],
        out_specs=(spec, spec),
        out_shape=(shp, shp),
        compiler_params=_params(("arbitrary",)),
        name="bias",
    )(rel_bias)


def _ada_norm(x, g, scale, shift):
    y = x * lax.rsqrt(jnp.mean(x * x, axis=-1, keepdims=True) + NORM_EPS)
    return (y * g) * (1.0 + scale) + shift


def _inproj_kernel(x_ref, g_ref, sc_ref, sh_ref, wqT_ref, wk_ref, wvT_ref, wc_ref, wgl_ref,
                   wco_ref, cw_ref, bg_ref,
                   qT_ref, k_ref, vT_ref, km_ref, mc_ref, ga_ref, ubuf):
    tm = x_ref.shape[1]
    cwid = wco_ref.shape[0]
    d = x_ref.shape[2]
    s = pl.program_id(1)
    hb = _ada_norm(x_ref[0], g_ref[...], sc_ref[0], sh_ref[0]).astype(bf16)

    nt = (((1,), (1,)), ((), ()))
    qT_ref[0] = lax.dot_general(wqT_ref[...], hb, nt, preferred_element_type=f32).astype(bf16)
    k = jnp.dot(hb, wk_ref[...], preferred_element_type=f32)
    vT = lax.dot_general(wvT_ref[...], hb, nt, preferred_element_type=f32).astype(bf16)
    for j in range(tm // MOBA_BLOCK):
        rows = slice(j * MOBA_BLOCK, (j + 1) * MOBA_BLOCK)
        k_ref[0, j] = k[rows].astype(bf16)
        vT_ref[0, j] = vT[:, rows]
        km_ref[0, 0, j:j + 1, :] = jnp.mean(k[rows], axis=0, keepdims=True)

    cbx = jnp.dot(hb, wc_ref[...], preferred_element_type=f32)
    u = cbx[:, :cwid] * cbx[:, 2 * cwid:]

    @pl.when(s == 0)
    def _():
        ubuf[0:8, :] = jnp.zeros((8, cwid), f32)

    ubuf[8:8 + tm, :] = u
    cw = cw_ref[...]
    conv = cw[0:1] * ubuf[6:6 + tm, :] + cw[1:2] * ubuf[7:7 + tm, :] + cw[2:3] * u
    ubuf[0:8, :] = ubuf[tm:tm + 8, :]
    cv = (cbx[:, cwid:2 * cwid] * conv).astype(bf16)
    y_conv = jnp.dot(cv, wco_ref[...], preferred_element_type=f32)

    gl = jnp.dot(hb, wgl_ref[...], preferred_element_type=f32) + bg_ref[...]
    ga_ref[0] = jax.nn.sigmoid(gl[:, :d]).astype(bf16)
    mc_ref[0] = (jax.nn.sigmoid(gl[:, d:]) * y_conv).astype(bf16)


def _inproj(x, g, sc, sh, w_in, b_gate, conv_w, w_conv_out):
    B, S, D = x.shape
    tm = ROW_TILE
    nb = S // MOBA_BLOCK
    bpt = tm // MOBA_BLOCK
    aw = ATTN_WIDTH
    cwid = conv_w.shape[1]
    scale = HEAD_DIM ** -0.5 * LOG2E
    wqT = (w_in[:, :aw] * scale).T.astype(bf16)
    wk = w_in[:, aw:2 * aw].astype(bf16)
    wvT = w_in[:, 2 * aw:3 * aw].T.astype(bf16)
    wc = w_in[:, 3 * aw:3 * aw + 3 * cwid].astype(bf16)
    wgl = w_in[:, 3 * aw + 3 * cwid:].astype(bf16)
    wco = w_conv_out.astype(bf16)

    const = lambda shape: pl.BlockSpec(shape, lambda b, s: (0,) * len(shape))
    vec = pl.BlockSpec((1, 1, D), lambda b, s: (b, 0, 0))
    row = lambda width: pl.BlockSpec((1, tm, width), lambda b, s: (b, s, 0))
    outs = pl.pallas_call(
        _inproj_kernel,
        grid=(B, S // tm),
        in_specs=[row(D), const((1, D)), vec, vec,
                  const((aw, D)), const((D, aw)), const((aw, D)), const((D, 3 * cwid)),
                  const((D, 2 * D)), const((cwid, D)), const((CONV_K, cwid)), const((1, 2 * D))],
        out_specs=(pl.BlockSpec((1, aw, tm), lambda b, s: (b, 0, s)),
                   pl.BlockSpec((1, bpt, MOBA_BLOCK, aw), lambda b, s: (b, s, 0, 0)),
                   pl.BlockSpec((1, bpt, aw, MOBA_BLOCK), lambda b, s: (b, s, 0, 0)),
                   pl.BlockSpec((1, 1, bpt, aw), lambda b, s: (b, s, 0, 0)),
                   row(D), row(D)),
        out_shape=(jax.ShapeDtypeStruct((B, aw, S), bf16),
                   jax.ShapeDtypeStruct((B, nb, MOBA_BLOCK, aw), bf16),
                   jax.ShapeDtypeStruct((B, nb, aw, MOBA_BLOCK), bf16),
                   jax.ShapeDtypeStruct((B, S // tm, bpt, aw), f32),
                   jax.ShapeDtypeStruct((B, S, D), bf16),
                   jax.ShapeDtypeStruct((B, S, D), bf16)),
        scratch_shapes=[pltpu.VMEM((tm + 8, cwid), f32)],
        compiler_params=_params(("parallel", "arbitrary")),
        name="inproj",
    )(x, g.reshape(1, D), sc.reshape(B, 1, D), sh.reshape(B, 1, D),
      wqT, wk, wvT, wc, wgl, wco, conv_w, b_gate.reshape(1, 2 * D))
    qT, k4, vT4, km, mc, ga = outs
    return qT, k4, vT4, km.reshape(B, nb, aw), mc, ga


def _attn_kernel(qT_ref, k_ref, vT_ref, km_ref, own_ref, adj_ref, *side_refs):
    n_side = len(side_refs) // 2
    o_ref = side_refs[n_side]
    for src, dst in zip(side_refs[:n_side], side_refs[n_side + 1:]):
        dst[...] = src[...].astype(bf16)

    i = pl.program_id(1)
    bs = MOBA_BLOCK
    nb = km_ref.shape[1]
    n_heads = qT_ref.shape[1] // HEAD_DIM
    heads = range(n_heads)
    blk = lax.broadcasted_iota(jnp.int32, (nb, bs), 0)

    qcat = []
    for h in heads:
        pair, a = divmod(h, LANES // HEAD_DIM)
        qT = qT_ref[0, pair * LANES:(pair + 1) * LANES, :]
        row = lax.broadcasted_iota(jnp.int32, qT.shape, 0)
        qa = jnp.where((row >= a * HEAD_DIM) & (row < (a + 1) * HEAD_DIM), qT, jnp.zeros_like(qT))
        km = km_ref[0, :, pair * LANES:(pair + 1) * LANES]
        km_hi = km.astype(bf16)
        km_mid = (km - km_hi.astype(f32)).astype(bf16)
        km_lo = (km - km_hi.astype(f32) - km_mid.astype(f32)).astype(bf16)
        g = (jnp.dot(km_hi, qa, preferred_element_type=f32)
             + (jnp.dot(km_mid, qa, preferred_element_type=f32) + jnp.dot(km_lo, qa, preferred_element_type=f32)))
        g = jnp.where(blk < i, g, NEG_INF)
        fq = jnp.full((nb, bs), NEG_INF, f32)
        for _ in range(MOBA_TOPK):
            top = jnp.max(g, axis=0, keepdims=True)
            first = jnp.min(jnp.where(g == top, blk, nb), axis=0, keepdims=True)
            pick = blk == first
            fq = jnp.where(pick, 0.0, fq)
            g = jnp.where(pick, -jnp.inf, g)
        fq = jnp.where(blk < i, fq, 0.0).astype(bf16)
        pad = jnp.zeros((LANES - nb, bs), bf16)
        qcat.append(jnp.concatenate([qa, fq, pad], axis=0))

    lane = lax.broadcasted_iota(jnp.int32, (bs, LANES), 1)
    causal = (lax.broadcasted_iota(jnp.int32, (bs, bs), 0) <= lax.broadcasted_iota(jnp.int32, (bs, bs), 1))
    ones_rows = jnp.ones((SUM_ROWS, bs), bf16)

    def step(blocks, state):
        onehots = [jnp.where(lane == n, 1.0, 0.0).astype(bf16) for n, _ in blocks]
        scores = []
        for h in heads:
            pair = h // (LANES // HEAD_DIM)
            cols = slice(pair * LANES, (pair + 1) * LANES)
            scores.append([jnp.dot(jnp.concatenate([k_ref[0, n, :, cols], onehot], axis=1), qcat[h],
                                   preferred_element_type=f32) for (n, _), onehot in zip(blocks, onehots)])
        out = []
        for h in heads:
            sTs = []
            for (n, kind), sT in zip(blocks, scores[h]):
                if kind == "own":
                    sT = jnp.where(causal, own_ref[h] + sT, NEG_INF)
                elif kind == "adj":
                    sT = adj_ref[h] + sT
                sTs.append(sT)
            m_new = None if state is None else state[h][0]
            for sT in sTs:
                top = jnp.max(sT, axis=0, keepdims=True)
                m_new = top if m_new is None else jnp.maximum(m_new, top)
            acc = None if state is None else jnp.exp2(state[h][0] - m_new) * state[h][1]
            for (n, _), sT in zip(blocks, sTs):
                p = jnp.exp2(sT - m_new).astype(bf16)
                v_ext = jnp.concatenate([vT_ref[0, n, h * HEAD_DIM:(h + 1) * HEAD_DIM, :], ones_rows], axis=0)
                pv = jnp.dot(v_ext, p, preferred_element_type=f32)
                acc = pv if acc is None else acc + pv
            out.append((m_new, acc))
        return tuple(out)

    state = lax.cond(i >= 1,
                     lambda: step([(i, "own"), (i - 1, "adj")], None),
                     lambda: step([(i, "own")], None))
    n_far = jnp.maximum(i - 1, 0)
    per = FAR_BLOCKS_PER_ITER
    def far_iter(it, st):
        for u in range(FAR_SUBSTEPS):
            st = step([((it * FAR_SUBSTEPS + u) * per + j, "far") for j in range(per)], st)
        return st

    state = lax.fori_loop(0, n_far // (per * FAR_SUBSTEPS), far_iter, state)
    per = per * FAR_SUBSTEPS
    done = n_far - n_far % per
    size = per // 2
    while size >= 1:
        take = (n_far % per) & size
        state = lax.cond(take > 0,
                         lambda st, d=done, sz=size: step([(d + j, "far") for j in range(sz)], st),
                         lambda st: st, state)
        done = done + take
        size //= 2
    outT = jnp.concatenate([acc[:HEAD_DIM] / acc[HEAD_DIM:HEAD_DIM + 1] for (_, acc) in state], axis=0)
    o_ref[0] = outT.T.astype(bf16)


def _attention(qT, k4, vT4, km, own, adj, side_weights):
    B, aw, S = qT.shape
    nb = S // MOBA_BLOCK
    assert MOBA_TOPK < nb <= LANES
    n_heads = aw // HEAD_DIM
    steps = B * nb
    whole = lambda shape: pl.BlockSpec(shape, lambda b, i: (b,) + (0,) * (len(shape) - 1))
    table = pl.BlockSpec((n_heads, MOBA_BLOCK, MOBA_BLOCK), lambda b, i: (0, 0, 0))
    slabs = [w.reshape(steps, -1, w.shape[-1]) for w in side_weights]
    slab_specs = [pl.BlockSpec((1,) + w.shape[1:], lambda b, i: (b * nb + i, 0, 0)) for w in slabs]
    outs = pl.pallas_call(
        _attn_kernel,
        grid=(B, nb),
        in_specs=[pl.BlockSpec((1, aw, MOBA_BLOCK), lambda b, i: (b, 0, i)),
                  whole((1, nb, MOBA_BLOCK, aw)), whole((1, nb, aw, MOBA_BLOCK)), whole((1, nb, aw)),
                  table, table] + slab_specs,
        out_specs=[pl.BlockSpec((1, MOBA_BLOCK, aw), lambda b, i: (b, i, 0))] + slab_specs,
        out_shape=[jax.ShapeDtypeStruct((B, S, aw), bf16)]
                  + [jax.ShapeDtypeStruct(w.shape, bf16) for w in slabs],
        compiler_params=_params(("parallel", "arbitrary")),
        name="attn",
    )(qT, k4, vT4, km, own, adj, *slabs)
    return outs[0], [o.reshape(w.shape) for o, w in zip(outs[1:], side_weights)]


def _post_kernel(a_ref, mc_ref, ga_ref, x_ref, g1_ref, sc_ref, sh_ref, ng_ref,
                 wao_ref, wo_ref, wr_ref, br_ref, x1_ref, h2p_ref, route_ref):
    tm = x_ref.shape[1]
    d = x_ref.shape[2]
    sub = tm // POST_SPLIT
    lane = lax.broadcasted_iota(jnp.int32, (sub, LANES), 1)
    ninf = -jnp.inf
    for t in range(POST_SPLIT):
        rows = slice(t * sub, (t + 1) * sub)
        ya = jnp.dot(a_ref[0, rows, :], wao_ref[...], preferred_element_type=f32)
        merged = ga_ref[0, rows, :].astype(f32) * ya + mc_ref[0, rows, :].astype(f32)
        z = jnp.dot(merged.astype(bf16), wo_ref[...], preferred_element_type=f32)
        x1 = x_ref[0, rows, :] + g1_ref[0] * z
        x1_ref[0, rows, :] = x1
        h2 = _ada_norm(x1, ng_ref[...], sc_ref[0], sh_ref[0])
        h_hi = h2.astype(bf16)
        h_hi32 = h_hi.astype(f32)

        bits = pltpu.bitcast(h_hi32, jnp.uint32)
        packed = (bits[:, :d // 2] >> 16) | (bits[:, d // 2:] & jnp.uint32(0xFFFF0000))
        chunks = packed.shape[1] // LANES
        for j in range(chunks):
            h2p_ref[pl.ds(t * sub * chunks + j, sub, stride=chunks), :] = packed[:, j * LANES:(j + 1) * LANES]

        h_lo = (h2 - h_hi32).astype(bf16)
        hi_both = jnp.dot(h_hi, wr_ref[...], preferred_element_type=f32)
        logits = (hi_both[:, :LANES]
                  + (jnp.dot(h_lo, wr_ref[:, :LANES], preferred_element_type=f32) + hi_both[:, LANES:])) + br_ref[...]
        gl = jnp.where(lane < N_GROUPS, logits, ninf)
        gmax = jnp.max(gl, axis=1, keepdims=True)
        g_sel = jnp.min(jnp.where(gl == gmax, lane, LANES), axis=1, keepdims=True)
        g_w = 1.0 / jnp.sum(jnp.exp(gl - gmax), axis=1, keepdims=True)
        off = lane - (N_GROUPS + EXPERTS_PER_GROUP * g_sel)
        in_group = (off >= 0) & (off < EXPERTS_PER_GROUP)
        el = jnp.where(in_group, logits, ninf)
        ex = jnp.exp(el - jnp.max(el, axis=1, keepdims=True))
        ep = jnp.where(in_group, ex / jnp.sum(ex, axis=1, keepdims=True), -1.0)
        p1 = jnp.max(ep, axis=1, keepdims=True)
        i1 = jnp.min(jnp.where(ep == p1, lane, LANES), axis=1, keepdims=True)
        ep2 = jnp.where(lane == i1, -1.0, ep)
        p2 = jnp.max(ep2, axis=1, keepdims=True)
        i2 = jnp.min(jnp.where(ep2 == p2, lane, LANES), axis=1, keepdims=True)
        tot = p1 + p2
        w1 = p1 / tot * g_w
        w2 = p2 / tot * g_w
        e1 = (i1 - N_GROUPS).astype(f32)
        e2 = (i2 - N_GROUPS).astype(f32)
        route = jnp.where(lane == 0, e1, jnp.where(lane == 1, e2,
                          jnp.where(lane == 2, w1, jnp.where(lane == 3, w2, 0.0))))
        route_ref[:, rows] = route.T[:SUBLANES, :]


def _post(a, mc, ga, x, g1, sc2, sh2, ng, w_attn_out, w_o, w_rg, b_rg, w_re, b_re):
    B, S, D = x.shape
    tm = ROW_TILE
    aw = a.shape[2]
    nt = S // tm
    wr = jnp.zeros((D, LANES), f32).at[:, :N_GROUPS].set(w_rg).at[:, N_GROUPS:N_GROUPS + N_EXPERTS].set(w_re)
    br = jnp.zeros((1, LANES), f32).at[0, :N_GROUPS].set(b_rg).at[0, N_GROUPS:N_GROUPS + N_EXPERTS].set(b_re)
    wr_hi = wr.astype(bf16)
    wr_split = jnp.concatenate([wr_hi, (wr - wr_hi.astype(f32)).astype(bf16)], axis=1)
    const = lambda shape: pl.BlockSpec(shape, lambda b, s: (0,) * len(shape))
    vec = pl.BlockSpec((1, 1, D), lambda b, s: (b, 0, 0))
    row = lambda width: pl.BlockSpec((1, tm, width), lambda b, s: (b, s, 0))
    flat = lambda width, rows_per_token=1: pl.BlockSpec((tm * rows_per_token, width), lambda b, s: (b * nt + s, 0))
    words = D // 2 // LANES
    return pl.pallas_call(
        _post_kernel,
        grid=(B, nt),
        in_specs=[row(aw), row(D), row(D), row(D), vec, vec, vec, const((1, D)),
                  const((aw, D)), const((D, D)), const((D, 2 * LANES)), const((1, LANES))],
        out_specs=(row(D), flat(LANES, words), pl.BlockSpec((SUBLANES, tm), lambda b, s: (0, b * nt + s))),
        out_shape=(jax.ShapeDtypeStruct((B, S, D), f32),
                   jax.ShapeDtypeStruct((B * S * words, LANES), jnp.uint32),
                   jax.ShapeDtypeStruct((SUBLANES, B * S), f32)),
        compiler_params=_params(("parallel", "parallel")),
        name="post",
    )(a, mc, ga, x, g1.reshape(B, 1, D), sc2.reshape(B, 1, D), sh2.reshape(B, 1, D), ng.reshape(1, D),
      w_attn_out.astype(bf16), w_o.astype(bf16), wr_split, br)


def _dispatch_tables(route, n_chunks):
    T = route.shape[1]
    tc = T // n_chunks
    na = tc * TOP_K_EXPERTS
    per_chunk = lambda rows: rows.reshape(TOP_K_EXPERTS, n_chunks, tc).transpose(1, 0, 2).reshape(n_chunks, na)
    e = per_chunk(route[:TOP_K_EXPERTS]).astype(jnp.int32)
    w = per_chunk(route[TOP_K_EXPERTS:2 * TOP_K_EXPERTS])
    order = jnp.argsort(e, axis=1, stable=True).astype(jnp.int32)
    stok = order % tc
    sw = jnp.take_along_axis(w, order, axis=1)
    counts = jnp.sum((e[:, None, :] == jnp.arange(N_EXPERTS)[None, :, None]).astype(jnp.int32), axis=2)
    starts = jnp.cumsum(counts, axis=1) - counts
    flat = lambda t: t.reshape(-1).astype(jnp.int32)
    return flat(starts), flat(counts), stok.reshape(n_chunks, 1, na), sw.reshape(n_chunks, 1, na)


def _moe_kernel(start_ref, count_ref, stok_ref, sw_ref, h2p_ref, wg_ref, wu_ref, wd_ref,
                x1_ref, g2_ref, fg_ref, o_ref, yacc, xs, outs, *, normalize):
    c = pl.program_id(0)
    b = pl.program_id(1)
    block_rows = outs.shape[0] // SUBLANES
    words = xs.shape[0] // block_rows
    half = words * LANES
    run = c * N_EXPERTS + jnp.minimum(b, N_EXPERTS - 1)
    count = jnp.where(b < N_EXPERTS, count_ref[run], 0)
    first = start_ref[run]

    @pl.when(b == 0)
    def _():
        yacc[...] = jnp.zeros(yacc.shape, f32)

    @pl.when((b == 0) & (c == 0))
    def _():
        xs[...] = jnp.zeros(xs.shape, jnp.uint32)

    def token_rows(index, rows_per_token):
        return pl.ds(pl.multiple_of(index * rows_per_token, rows_per_token), rows_per_token)

    def block(blk, carry):
        st = first + blk * block_rows
        n = jnp.minimum(count - blk * block_rows, block_rows)

        def for_rows(rows_fn):
            full = n // ROW_UNROLL

            def group(g, carry):
                rows_fn([g * ROW_UNROLL + j for j in range(ROW_UNROLL)], st + g * ROW_UNROLL)
                return carry

            def single(j, carry):
                rows_fn([full * ROW_UNROLL + j], st + full * ROW_UNROLL + j)
                return carry

            lax.fori_loop(0, full, group, 0)
            lax.fori_loop(0, n - full * ROW_UNROLL, single, 0)

        def gather(rs, i0):
            for k, r in enumerate(rs):
                xs[token_rows(r, words), :] = h2p_ref[token_rows(stok_ref[0, 0, i0 + k], words), :]

        for_rows(gather)
        lo, hi = [], []
        for j in range(words):
            packed = xs[pl.ds(j, block_rows, stride=words), :]
            lo.append(pltpu.bitcast(packed << 16, f32).astype(bf16))
            hi.append(pltpu.bitcast(packed & jnp.uint32(0xFFFF0000), f32).astype(bf16))
        x_lo = jnp.concatenate(lo, axis=1)
        x_hi = jnp.concatenate(hi, axis=1)

        def up(w_ref):
            return (jnp.dot(x_lo, w_ref[0, :half, :], preferred_element_type=f32)
                    + jnp.dot(x_hi, w_ref[0, half:, :], preferred_element_type=f32))

        hidden = (jax.nn.silu(up(wg_ref)) * up(wu_ref)).astype(bf16)
        out = jnp.dot(hidden, wd_ref[0], preferred_element_type=f32)
        for j in range(SUBLANES):
            outs[pl.ds(j, block_rows, stride=SUBLANES), :] = out[:, j * LANES:(j + 1) * LANES]

        def scatter(rs, i0):
            toks = [stok_ref[0, 0, i0 + k] for k in range(len(rs))]
            acc = [yacc[token_rows(tok, SUBLANES), :] for tok in toks]
            for k, (r, tok, a) in enumerate(zip(rs, toks, acc)):
                yacc[token_rows(tok, SUBLANES), :] = a + sw_ref[0, 0, i0 + k] * outs[token_rows(r, SUBLANES), :]

        for_rows(scatter)
        return carry

    lax.fori_loop(0, (count + block_rows - 1) // block_rows, block, 0)

    @pl.when(b >= N_EXPERTS)
    def _():
        rows = o_ref.shape[0]
        start = pl.multiple_of((b - N_EXPERTS) * rows * SUBLANES, rows * SUBLANES)
        y = jnp.concatenate([yacc[pl.ds(start + j, rows, stride=SUBLANES), :] for j in range(SUBLANES)], axis=1)
        x2 = x1_ref[...] + g2_ref[0] * y
        if normalize:
            x2 = (x2 * lax.rsqrt(jnp.mean(x2 * x2, axis=-1, keepdims=True) + NORM_EPS)) * fg_ref[...]
        o_ref[...] = x2


def _moe(route, h2p, w_gate, w_up, w_down, x1, g2, fg, normalize):
    B, S, D = x1.shape
    T = B * S
    words = h2p.shape[0] // T
    assert D == words * 2 * LANES == SUBLANES * LANES
    de = w_gate.shape[2]
    n_chunks = MOE_CHUNKS
    tc = T // n_chunks
    fin_rows = ROW_TILE // 2
    n_fin = tc // fin_rows
    assert n_chunks % B == 0 and tc % fin_rows == 0 and w_gate.shape[0] == N_EXPERTS
    starts, counts, stok, sw = _dispatch_tables(route, n_chunks)
    na = stok.shape[2]
    wmap = lambda c, b, *_: (jnp.minimum(b, N_EXPERTS - 1), 0, 0)
    smem = lambda: pl.BlockSpec((1, 1, na), lambda c, b, *_: (c, 0, 0), memory_space=pltpu.SMEM)
    tile = pl.BlockSpec((fin_rows, D), lambda c, b, *_: (c * n_fin + jnp.maximum(b - N_EXPERTS, 0), 0))
    grid_spec = pltpu.PrefetchScalarGridSpec(
        num_scalar_prefetch=2,
        grid=(n_chunks, N_EXPERTS + n_fin),
        in_specs=[smem(), smem(),
                  pl.BlockSpec((tc * words, LANES), lambda c, b, *_: (c, 0)),
                  pl.BlockSpec((1, D, de), wmap),
                  pl.BlockSpec((1, D, de), wmap),
                  pl.BlockSpec((1, de, D), wmap),
                  tile,
                  pl.BlockSpec((1, 1, D), lambda c, b, *_: (c // (n_chunks // B), 0, 0)),
                  pl.BlockSpec((1, D), lambda c, b, *_: (0, 0))],
        out_specs=tile,
        scratch_shapes=[pltpu.VMEM((tc * SUBLANES, LANES), f32),
                        pltpu.VMEM((EXPERT_ROWS * words, LANES), jnp.uint32),
                        pltpu.VMEM((EXPERT_ROWS * SUBLANES, LANES), f32)],
    )
    out = pl.pallas_call(
        functools.partial(_moe_kernel, normalize=normalize),
        grid_spec=grid_spec,
        out_shape=jax.ShapeDtypeStruct((T, D), f32),
        compiler_params=_params(("arbitrary", "arbitrary")),
        name="moe",
    )(starts, counts, stok, sw, h2p, w_gate, w_up, w_down,
      x1.reshape(T, D), g2.reshape(B, 1, D), fg.reshape(1, D))
    return out.reshape(B, S, D)


def kernel(x, c, rel_bias, norm1_g, norm2_g, w_ada, b_ada, w_in, b_gate, conv_w, w_attn_out,
           w_conv_out, w_o, w_router_group, b_router_group, w_router_expert, b_router_expert,
           w_exp_gate, w_exp_up, w_exp_down, final_norm_g):
    B, S, D = x.shape
    depth = w_ada.shape[0]
    assert S % ROW_TILE == 0 and ROW_TILE % MOBA_BLOCK == 0
    own, adj = _bias_tiles(rel_bias)
    for l in range(depth):
        mod = _mod(c, w_ada[l], b_ada[l])
        sh1, sc1, g1, sh2, sc2, g2 = jnp.split(mod, 6, axis=-1)
        qT, k4, vT4, km, mc, ga = _inproj(x, norm1_g[l], sc1, sh1, w_in[l], b_gate[l], conv_w[l],
                                          w_conv_out[l])
        a, expert_w = _attention(qT, k4, vT4, km, own, adj, [w_exp_gate[l], w_exp_up[l], w_exp_down[l]])
        x1, h2p, route = _post(a, mc, ga, x, g1, sc2, sh2, norm2_g[l], w_attn_out[l], w_o[l],
                               w_router_group[l], b_router_group[l], w_router_expert[l],
                               b_router_expert[l])
        x = _moe(route, h2p, *expert_w, x1, g2, final_norm_g, normalize=(l + 1 == depth))
    return x
```

```python
import functools
import math

import numpy as np
import jax
import jax.numpy as jnp
from jax import lax
from jax.experimental import pallas as pl
from jax.experimental.pallas import tpu as pltpu

f32 = jnp.float32
bf16 = jnp.bfloat16

N_HEADS = 8
HEAD_DIM = 64
ATTN_WIDTH = N_HEADS * HEAD_DIM
MOBA_BLOCK = 256
MOBA_TOPK = 3
CONV_K = 3
N_BUCKETS = 32
MAX_EXACT = N_BUCKETS // 2
MAX_DISTANCE = 128
N_GROUPS = 4
EXPERTS_PER_GROUP = 8
N_EXPERTS = N_GROUPS * EXPERTS_PER_GROUP
TOP_K_EXPERTS = 2
NORM_EPS = 1e-6
NEG_INF = -1e30

LANES = 128
SUBLANES = 8
MOD_COLS = 1024
ROW_TILE = 1024
POST_SPLIT = 2
EXPERT_ROWS = 320
ROW_UNROLL = 8
MOE_CHUNKS = 4
FAR_BLOCKS_PER_ITER = 2
FAR_SUBSTEPS = 2
SUM_ROWS = 16
LOG2E = 1.4426950408889634
VMEM_LIMIT = 56 * 1024 * 1024


def _params(sem, vmem=VMEM_LIMIT):
    return pltpu.CompilerParams(dimension_semantics=sem, vmem_limit_bytes=vmem)


def _mod_kernel(c_ref, w_ref, b_ref, o_ref, *, batch):
    act = jax.nn.silu(c_ref[...]).T
    w = w_ref[...]
    rows = [jnp.sum(act[:, r:r + 1] * w, axis=0, keepdims=True) for r in range(batch)]
    rows.append(jnp.zeros((o_ref.shape[0] - batch, w.shape[1]), f32))
    o_ref[...] = jnp.concatenate(rows, axis=0) + b_ref[...]


def _mod(c, w_ada, b_ada):
    B, D = c.shape
    n_out = w_ada.shape[1]
    rows = SUBLANES
    cp = jnp.zeros((rows, D), f32).at[:B].set(c)
    tn = MOD_COLS
    out = pl.pallas_call(
        functools.partial(_mod_kernel, batch=B),
        grid=(n_out // tn,),
        in_specs=[pl.BlockSpec((rows, D), lambda j: (0, 0)),
                  pl.BlockSpec((D, tn), lambda j: (0, j)),
                  pl.BlockSpec((1, tn), lambda j: (0, j))],
        out_specs=pl.BlockSpec((rows, tn), lambda j: (0, j)),
        out_shape=jax.ShapeDtypeStruct((rows, n_out), f32),
        compiler_params=_params(("arbitrary",)),
        name="mod",
    )(cp, w_ada, b_ada.reshape(1, n_out))
    return out[:B]


def _bucket_thresholds():
    n = np.arange(2 * MOBA_BLOCK)

    def buckets(ft):
        nf = np.maximum(n, 1).astype(ft)
        large = MAX_EXACT + (np.log(nf / ft(MAX_EXACT)) / ft(math.log(MAX_DISTANCE / MAX_EXACT))
                             * ft(N_BUCKETS - MAX_EXACT)).astype(np.int32)
        return np.where(n < MAX_EXACT, n, np.minimum(large, N_BUCKETS - 1))

    b = buckets(np.float32)
    assert np.array_equal(b, buckets(np.float64)) and np.all(np.diff(b) >= 0)
    assert b[-1] == N_BUCKETS - 1 and b[MOBA_BLOCK + 1] == N_BUCKETS - 1
    return [int(np.argmax(b >= j)) for j in range(N_BUCKETS)]


def _bias_kernel(rb_ref, own_ref, adj_ref, *, thresholds):
    h = pl.program_id(0)
    kk = lax.broadcasted_iota(jnp.int32, (MOBA_BLOCK, MOBA_BLOCK), 0)
    qq = lax.broadcasted_iota(jnp.int32, (MOBA_BLOCK, MOBA_BLOCK), 1)
    far = rb_ref[N_BUCKETS - 1, h]

    def table(dist):
        acc = jnp.full(dist.shape, rb_ref[0, h], f32)
        for j in range(1, N_BUCKETS):
            acc = jnp.where(dist >= thresholds[j], rb_ref[j, h], acc)
        return (acc - far) * LOG2E

    own_ref[0] = table(qq - kk)
    adj_ref[0] = table(qq - kk + MOBA_BLOCK)


def _bias_tiles(rel_bias):
    shp = jax.ShapeDtypeStruct((N_HEADS, MOBA_BLOCK, MOBA_BLOCK), f32)
    spec = pl.BlockSpec((1, MOBA_BLOCK, MOBA_BLOCK), lambda h: (h, 0, 0))
    return pl.pallas_call(
        functools.partial(_bias_kernel, thresholds=_bucket_thresholds()),
        grid=(N_HEADS,),
        in_specs=[pl.BlockSpec(memory_space=pltpu.SMEM)],
        out_specs=(spec, spec),
        out_shape=(shp, shp),
        compiler_params=_params(("arbitrary",)),
        name="bias",
    )(rel_bias)


def _ada_norm(x, g, scale, shift):
    y = x * lax.rsqrt(jnp.mean(x * x, axis=-1, keepdims=True) + NORM_EPS)
    return (y * g) * (1.0 + scale) + shift


def _inproj_kernel(x_ref, g_ref, sc_ref, sh_ref, wqT_ref, wk_ref, wvT_ref, wc_ref, wgl_ref,
                   wco_ref, cw_ref, bg_ref,
                   qT_ref, k_ref, vT_ref, km_ref, mc_ref, ga_ref, ubuf):
    tm = x_ref.shape[1]
    cwid = wco_ref.shape[0]
    d = x_ref.shape[2]
    s = pl.program_id(1)
    hb = _ada_norm(x_ref[0], g_ref[...], sc_ref[0], sh_ref[0]).astype(bf16)

    nt = (((1,), (1,)), ((), ()))
    qT_ref[0] = lax.dot_general(wqT_ref[...], hb, nt, preferred_element_type=f32).astype(bf16)
    k = jnp.dot(hb, wk_ref[...], preferred_element_type=f32)
    vT = lax.dot_general(wvT_ref[...], hb, nt, preferred_element_type=f32).astype(bf16)
    for j in range(tm // MOBA_BLOCK):
        rows = slice(j * MOBA_BLOCK, (j + 1) * MOBA_BLOCK)
        k_ref[0, j] = k[rows].astype(bf16)
        vT_ref[0, j] = vT[:, rows]
        km_ref[0, 0, j:j + 1, :] = jnp.mean(k[rows], axis=0, keepdims=True)

    cbx = jnp.dot(hb, wc_ref[...], preferred_element_type=f32)
    u = cbx[:, :cwid] * cbx[:, 2 * cwid:]

    halo = SUBLANES

    @pl.when(s == 0)
    def _():
        ubuf[0:halo, :] = jnp.zeros((halo, cwid), f32)

    ubuf[halo:halo + tm, :] = u
    cw = cw_ref[...]
    taps = [ubuf[halo - back:halo - back + tm, :] for back in range(CONV_K - 1, 0, -1)] + [u]
    conv = cw[0:1] * taps[0]
    for j in range(1, CONV_K):
        conv = conv + cw[j:j + 1] * taps[j]
    ubuf[0:halo, :] = ubuf[tm:tm + halo, :]
    cv = (cbx[:, cwid:2 * cwid] * conv).astype(bf16)
    y_conv = jnp.dot(cv, wco_ref[...], preferred_element_type=f32)

    gl = jnp.dot(hb, wgl_ref[...], preferred_element_type=f32) + bg_ref[...]
    ga_ref[0] = jax.nn.sigmoid(gl[:, :d]).astype(bf16)
    mc_ref[0] = (jax.nn.sigmoid(gl[:, d:]) * y_conv).astype(bf16)


def _inproj(x, g, sc, sh, w_in, b_gate, conv_w, w_conv_out):
    B, S, D = x.shape
    tm = ROW_TILE
    nb = S // MOBA_BLOCK
    bpt = tm // MOBA_BLOCK
    aw = ATTN_WIDTH
    cwid = conv_w.shape[1]
    scale = HEAD_DIM ** -0.5 * LOG2E
    wqT = (w_in[:, :aw] * scale).T.astype(bf16)
    wk = w_in[:, aw:2 * aw].astype(bf16)
    wvT = w_in[:, 2 * aw:3 * aw].T.astype(bf16)
    wc = w_in[:, 3 * aw:3 * aw + 3 * cwid].astype(bf16)
    wgl = w_in[:, 3 * aw + 3 * cwid:].astype(bf16)
    wco = w_conv_out.astype(bf16)

    const = lambda shape: pl.BlockSpec(shape, lambda b, s: (0,) * len(shape))
    vec = pl.BlockSpec((1, 1, D), lambda b, s: (b, 0, 0))
    row = lambda width: pl.BlockSpec((1, tm, width), lambda b, s: (b, s, 0))
    outs = pl.pallas_call(
        _inproj_kernel,
        grid=(B, S // tm),
        in_specs=[row(D), const((1, D)), vec, vec,
                  const((aw, D)), const((D, aw)), const((aw, D)), const((D, 3 * cwid)),
                  const((D, 2 * D)), const((cwid, D)), const((CONV_K, cwid)), const((1, 2 * D))],
        out_specs=(pl.BlockSpec((1, aw, tm), lambda b, s: (b, 0, s)),
                   pl.BlockSpec((1, bpt, MOBA_BLOCK, aw), lambda b, s: (b, s, 0, 0)),
                   pl.BlockSpec((1, bpt, aw, MOBA_BLOCK), lambda b, s: (b, s, 0, 0)),
                   pl.BlockSpec((1, 1, bpt, aw), lambda b, s: (b, s, 0, 0)),
                   row(D), row(D)),
        out_shape=(jax.ShapeDtypeStruct((B, aw, S), bf16),
                   jax.ShapeDtypeStruct((B, nb, MOBA_BLOCK, aw), bf16),
                   jax.ShapeDtypeStruct((B, nb, aw, MOBA_BLOCK), bf16),
                   jax.ShapeDtypeStruct((B, S // tm, bpt, aw), f32),
                   jax.ShapeDtypeStruct((B, S, D), bf16),
                   jax.ShapeDtypeStruct((B, S, D), bf16)),
        scratch_shapes=[pltpu.VMEM((tm + SUBLANES, cwid), f32)],
        compiler_params=_params(("parallel", "arbitrary")),
        name="inproj",
    )(x, g.reshape(1, D), sc.reshape(B, 1, D), sh.reshape(B, 1, D),
      wqT, wk, wvT, wc, wgl, wco, conv_w, b_gate.reshape(1, 2 * D))
    qT, k4, vT4, km, mc, ga = outs
    return qT, k4, vT4, km.reshape(B, nb, aw), mc, ga


def _attn_kernel(qT_ref, k_ref, vT_ref, km_ref, own_ref, adj_ref, *side_refs):
    n_side = len(side_refs) // 2
    o_ref = side_refs[n_side]
    for src, dst in zip(side_refs[:n_side], side_refs[n_side + 1:]):
        dst[...] = src[...].astype(bf16)

    i = pl.program_id(1)
    bs = MOBA_BLOCK
    nb = km_ref.shape[1]
    n_heads = qT_ref.shape[1] // HEAD_DIM
    heads = range(n_heads)
    blk = lax.broadcasted_iota(jnp.int32, (nb, bs), 0)

    qcat = []
    for h in heads:
        pair, a = divmod(h, LANES // HEAD_DIM)
        qT = qT_ref[0, pair * LANES:(pair + 1) * LANES, :]
        row = lax.broadcasted_iota(jnp.int32, qT.shape, 0)
        qa = jnp.where((row >= a * HEAD_DIM) & (row < (a + 1) * HEAD_DIM), qT, jnp.zeros_like(qT))
        km = km_ref[0, :, pair * LANES:(pair + 1) * LANES]
        km_hi = km.astype(bf16)
        km_mid = (km - km_hi.astype(f32)).astype(bf16)
        km_lo = (km - km_hi.astype(f32) - km_mid.astype(f32)).astype(bf16)
        g = (jnp.dot(km_hi, qa, preferred_element_type=f32)
             + (jnp.dot(km_mid, qa, preferred_element_type=f32) + jnp.dot(km_lo, qa, preferred_element_type=f32)))
        g = jnp.where(blk < i, g, NEG_INF)
        fq = jnp.full((nb, bs), NEG_INF, f32)
        for _ in range(MOBA_TOPK):
            top = jnp.max(g, axis=0, keepdims=True)
            first = jnp.min(jnp.where(g == top, blk, nb), axis=0, keepdims=True)
            pick = blk == first
            fq = jnp.where(pick, 0.0, fq)
            g = jnp.where(pick, -jnp.inf, g)
        fq = jnp.where(blk < i, fq, 0.0).astype(bf16)
        pad = jnp.zeros((LANES - nb, bs), bf16)
        qcat.append(jnp.concatenate([qa, fq, pad], axis=0))

    lane = lax.broadcasted_iota(jnp.int32, (bs, LANES), 1)
    causal = (lax.broadcasted_iota(jnp.int32, (bs, bs), 0) <= lax.broadcasted_iota(jnp.int32, (bs, bs), 1))
    ones_rows = jnp.ones((SUM_ROWS, bs), bf16)

    def step(blocks, state):
        onehots = [jnp.where(lane == n, 1.0, 0.0).astype(bf16) for n, _ in blocks]
        scores = []
        for h in heads:
            pair = h // (LANES // HEAD_DIM)
            cols = slice(pair * LANES, (pair + 1) * LANES)
            scores.append([jnp.dot(jnp.concatenate([k_ref[0, n, :, cols], onehot], axis=1), qcat[h],
                                   preferred_element_type=f32) for (n, _), onehot in zip(blocks, onehots)])
        out = []
        for h in heads:
            sTs = []
            for (n, kind), sT in zip(blocks, scores[h]):
                if kind == "own":
                    sT = jnp.where(causal, own_ref[h] + sT, NEG_INF)
                elif kind == "adj":
                    sT = adj_ref[h] + sT
                sTs.append(sT)
            m_new = None if state is None else state[h][0]
            for sT in sTs:
                top = jnp.max(sT, axis=0, keepdims=True)
                m_new = top if m_new is None else jnp.maximum(m_new, top)
            acc = None if state is None else jnp.exp2(state[h][0] - m_new) * state[h][1]
            for (n, _), sT in zip(blocks, sTs):
                p = jnp.exp2(sT - m_new).astype(bf16)
                v_ext = jnp.concatenate([vT_ref[0, n, h * HEAD_DIM:(h + 1) * HEAD_DIM, :], ones_rows], axis=0)
                pv = jnp.dot(v_ext, p, preferred_element_type=f32)
                acc = pv if acc is None else acc + pv
            out.append((m_new, acc))
        return tuple(out)

    state = lax.cond(i >= 1,
                     lambda: step([(i, "own"), (i - 1, "adj")], None),
                     lambda: step([(i, "own")], None))
    n_far = jnp.maximum(i - 1, 0)
    per = FAR_BLOCKS_PER_ITER
    def far_iter(it, st):
        for u in range(FAR_SUBSTEPS):
            st = step([((it * FAR_SUBSTEPS + u) * per + j, "far") for j in range(per)], st)
        return st

    state = lax.fori_loop(0, n_far // (per * FAR_SUBSTEPS), far_iter, state)
    per = per * FAR_SUBSTEPS
    done = n_far - n_far % per
    size = per // 2
    while size >= 1:
        take = (n_far % per) & size
        state = lax.cond(take > 0,
                         lambda st, d=done, sz=size: step([(d + j, "far") for j in range(sz)], st),
                         lambda st: st, state)
        done = done + take
        size //= 2
    outT = jnp.concatenate([acc[:HEAD_DIM] / acc[HEAD_DIM:HEAD_DIM + 1] for (_, acc) in state], axis=0)
    o_ref[0] = outT.T.astype(bf16)


def _attention(qT, k4, vT4, km, own, adj, side_weights):
    B, aw, S = qT.shape
    nb = S // MOBA_BLOCK
    assert MOBA_TOPK < nb <= LANES
    n_heads = aw // HEAD_DIM
    steps = B * nb
    whole = lambda shape: pl.BlockSpec(shape, lambda b, i: (b,) + (0,) * (len(shape) - 1))
    table = pl.BlockSpec((n_heads, MOBA_BLOCK, MOBA_BLOCK), lambda b, i: (0, 0, 0))
    slabs = [w.reshape(steps, -1, w.shape[-1]) for w in side_weights]
    slab_specs = [pl.BlockSpec((1,) + w.shape[1:], lambda b, i: (b * nb + i, 0, 0)) for w in slabs]
    outs = pl.pallas_call(
        _attn_kernel,
        grid=(B, nb),
        in_specs=[pl.BlockSpec((1, aw, MOBA_BLOCK), lambda b, i: (b, 0, i)),
                  whole((1, nb, MOBA_BLOCK, aw)), whole((1, nb, aw, MOBA_BLOCK)), whole((1, nb, aw)),
                  table, table] + slab_specs,
        out_specs=[pl.BlockSpec((1, MOBA_BLOCK, aw), lambda b, i: (b, i, 0))] + slab_specs,
        out_shape=[jax.ShapeDtypeStruct((B, S, aw), bf16)]
                  + [jax.ShapeDtypeStruct(w.shape, bf16) for w in slabs],
        compiler_params=_params(("parallel", "arbitrary")),
        name="attn",
    )(qT, k4, vT4, km, own, adj, *slabs)
    return outs[0], [o.reshape(w.shape) for o, w in zip(outs[1:], side_weights)]


def _post_kernel(a_ref, mc_ref, ga_ref, x_ref, g1_ref, sc_ref, sh_ref, ng_ref,
                 wao_ref, wo_ref, wr_ref, br_ref, x1_ref, h2p_ref, route_ref):
    tm = x_ref.shape[1]
    d = x_ref.shape[2]
    sub = tm // POST_SPLIT
    lane = lax.broadcasted_iota(jnp.int32, (sub, LANES), 1)
    ninf = -jnp.inf
    for t in range(POST_SPLIT):
        rows = slice(t * sub, (t + 1) * sub)
        ya = jnp.dot(a_ref[0, rows, :], wao_ref[...], preferred_element_type=f32)
        merged = ga_ref[0, rows, :].astype(f32) * ya + mc_ref[0, rows, :].astype(f32)
        z = jnp.dot(merged.astype(bf16), wo_ref[...], preferred_element_type=f32)
        x1 = x_ref[0, rows, :] + g1_ref[0] * z
        x1_ref[0, rows, :] = x1
        h2 = _ada_norm(x1, ng_ref[...], sc_ref[0], sh_ref[0])
        h_hi = h2.astype(bf16)
        h_hi32 = h_hi.astype(f32)

        bits = pltpu.bitcast(h_hi32, jnp.uint32)
        packed = (bits[:, :d // 2] >> 16) | (bits[:, d // 2:] & jnp.uint32(0xFFFF0000))
        chunks = packed.shape[1] // LANES
        for j in range(chunks):
            h2p_ref[pl.ds(t * sub * chunks + j, sub, stride=chunks), :] = packed[:, j * LANES:(j + 1) * LANES]

        h_lo = (h2 - h_hi32).astype(bf16)
        hi_both = jnp.dot(h_hi, wr_ref[...], preferred_element_type=f32)
        logits = (hi_both[:, :LANES]
                  + (jnp.dot(h_lo, wr_ref[:, :LANES], preferred_element_type=f32) + hi_both[:, LANES:])) + br_ref[...]
        gl = jnp.where(lane < N_GROUPS, logits, ninf)
        gmax = jnp.max(gl, axis=1, keepdims=True)
        g_sel = jnp.min(jnp.where(gl == gmax, lane, LANES), axis=1, keepdims=True)
        g_w = 1.0 / jnp.sum(jnp.exp(gl - gmax), axis=1, keepdims=True)
        off = lane - (N_GROUPS + EXPERTS_PER_GROUP * g_sel)
        in_group = (off >= 0) & (off < EXPERTS_PER_GROUP)
        el = jnp.where(in_group, logits, ninf)
        ex = jnp.exp(el - jnp.max(el, axis=1, keepdims=True))
        ep = jnp.where(in_group, ex / jnp.sum(ex, axis=1, keepdims=True), -1.0)
        p1 = jnp.max(ep, axis=1, keepdims=True)
        i1 = jnp.min(jnp.where(ep == p1, lane, LANES), axis=1, keepdims=True)
        ep2 = jnp.where(lane == i1, -1.0, ep)
        p2 = jnp.max(ep2, axis=1, keepdims=True)
        i2 = jnp.min(jnp.where(ep2 == p2, lane, LANES), axis=1, keepdims=True)
        tot = p1 + p2
        w1 = p1 / tot * g_w
        w2 = p2 / tot * g_w
        e1 = (i1 - N_GROUPS).astype(f32)
        e2 = (i2 - N_GROUPS).astype(f32)
        route = jnp.where(lane == 0, e1, jnp.where(lane == 1, e2,
                          jnp.where(lane == 2, w1, jnp.where(lane == 3, w2, 0.0))))
        route_ref[:, rows] = route.T[:SUBLANES, :]


def _post(a, mc, ga, x, g1, sc2, sh2, ng, w_attn_out, w_o, w_rg, b_rg, w_re, b_re):
    B, S, D = x.shape
    tm = ROW_TILE
    aw = a.shape[2]
    nt = S // tm
    wr = jnp.zeros((D, LANES), f32).at[:, :N_GROUPS].set(w_rg).at[:, N_GROUPS:N_GROUPS + N_EXPERTS].set(w_re)
    br = jnp.zeros((1, LANES), f32).at[0, :N_GROUPS].set(b_rg).at[0, N_GROUPS:N_GROUPS + N_EXPERTS].set(b_re)
    wr_hi = wr.astype(bf16)
    wr_split = jnp.concatenate([wr_hi, (wr - wr_hi.astype(f32)).astype(bf16)], axis=1)
    const = lambda shape: pl.BlockSpec(shape, lambda b, s: (0,) * len(shape))
    vec = pl.BlockSpec((1, 1, D), lambda b, s: (b, 0, 0))
    row = lambda width: pl.BlockSpec((1, tm, width), lambda b, s: (b, s, 0))
    flat = lambda width, rows_per_token=1: pl.BlockSpec((tm * rows_per_token, width), lambda b, s: (b * nt + s, 0))
    words = D // 2 // LANES
    return pl.pallas_call(
        _post_kernel,
        grid=(B, nt),
        in_specs=[row(aw), row(D), row(D), row(D), vec, vec, vec, const((1, D)),
                  const((aw, D)), const((D, D)), const((D, 2 * LANES)), const((1, LANES))],
        out_specs=(row(D), flat(LANES, words), pl.BlockSpec((SUBLANES, tm), lambda b, s: (0, b * nt + s))),
        out_shape=(jax.ShapeDtypeStruct((B, S, D), f32),
                   jax.ShapeDtypeStruct((B * S * words, LANES), jnp.uint32),
                   jax.ShapeDtypeStruct((SUBLANES, B * S), f32)),
        compiler_params=_params(("parallel", "parallel")),
        name="post",
    )(a, mc, ga, x, g1.reshape(B, 1, D), sc2.reshape(B, 1, D), sh2.reshape(B, 1, D), ng.reshape(1, D),
      w_attn_out.astype(bf16), w_o.astype(bf16), wr_split, br)


def _dispatch_tables(route, n_chunks):
    T = route.shape[1]
    tc = T // n_chunks
    na = tc * TOP_K_EXPERTS
    per_chunk = lambda rows: rows.reshape(TOP_K_EXPERTS, n_chunks, tc).transpose(1, 0, 2).reshape(n_chunks, na)
    e = per_chunk(route[:TOP_K_EXPERTS]).astype(jnp.int32)
    w = per_chunk(route[TOP_K_EXPERTS:2 * TOP_K_EXPERTS])
    tok = jnp.broadcast_to(jnp.arange(na, dtype=jnp.int32) % tc, e.shape)
    _, stok, sw = lax.sort((e, tok, w), dimension=1, num_keys=1, is_stable=True)
    counts = jnp.sum((e[:, None, :] == jnp.arange(N_EXPERTS)[None, :, None]).astype(jnp.int32), axis=2)
    starts = jnp.cumsum(counts, axis=1) - counts
    flat = lambda t: t.reshape(-1).astype(jnp.int32)
    return flat(starts), flat(counts), stok.reshape(n_chunks, 1, na), sw.reshape(n_chunks, 1, na)


def _moe_kernel(start_ref, count_ref, stok_ref, sw_ref, h2p_ref, wg_ref, wu_ref, wd_ref,
                x1_ref, g2_ref, fg_ref, o_ref, yacc, xs, outs, *, normalize):
    c = pl.program_id(0)
    b = pl.program_id(1)
    block_rows = outs.shape[0] // SUBLANES
    words = xs.shape[0] // block_rows
    half = words * LANES
    run = c * N_EXPERTS + jnp.minimum(b, N_EXPERTS - 1)
    count = jnp.where(b < N_EXPERTS, count_ref[run], 0)
    first = start_ref[run]

    @pl.when(b == 0)
    def _():
        yacc[...] = jnp.zeros(yacc.shape, f32)

    @pl.when((b == 0) & (c == 0))
    def _():
        xs[...] = jnp.zeros(xs.shape, jnp.uint32)

    def token_rows(index, rows_per_token):
        return pl.ds(pl.multiple_of(index * rows_per_token, rows_per_token), rows_per_token)

    def block(blk, carry):
        st = first + blk * block_rows
        n = jnp.minimum(count - blk * block_rows, block_rows)

        def for_rows(rows_fn):
            full = n // ROW_UNROLL

            def group(g, carry):
                rows_fn([g * ROW_UNROLL + j for j in range(ROW_UNROLL)], st + g * ROW_UNROLL)
                return carry

            def single(j, carry):
                rows_fn([full * ROW_UNROLL + j], st + full * ROW_UNROLL + j)
                return carry

            lax.fori_loop(0, full, group, 0)
            lax.fori_loop(0, n - full * ROW_UNROLL, single, 0)

        def gather(rs, i0):
            for k, r in enumerate(rs):
                xs[token_rows(r, words), :] = h2p_ref[token_rows(stok_ref[0, 0, i0 + k], words), :]

        for_rows(gather)
        lo, hi = [], []
        for j in range(words):
            packed = xs[pl.ds(j, block_rows, stride=words), :]
            lo.append(pltpu.bitcast(packed << 16, f32).astype(bf16))
            hi.append(pltpu.bitcast(packed & jnp.uint32(0xFFFF0000), f32).astype(bf16))
        x_lo = jnp.concatenate(lo, axis=1)
        x_hi = jnp.concatenate(hi, axis=1)

        def up(w_ref):
            return (jnp.dot(x_lo, w_ref[0, :half, :], preferred_element_type=f32)
                    + jnp.dot(x_hi, w_ref[0, half:, :], preferred_element_type=f32))

        hidden = (jax.nn.silu(up(wg_ref)) * up(wu_ref)).astype(bf16)
        out = jnp.dot(hidden, wd_ref[0], preferred_element_type=f32)
        for j in range(SUBLANES):
            outs[pl.ds(j, block_rows, stride=SUBLANES), :] = out[:, j * LANES:(j + 1) * LANES]

        def scatter(rs, i0):
            toks = [stok_ref[0, 0, i0 + k] for k in range(len(rs))]
            acc = [yacc[token_rows(tok, SUBLANES), :] for tok in toks]
            for k, (r, tok, a) in enumerate(zip(rs, toks, acc)):
                yacc[token_rows(tok, SUBLANES), :] = a + sw_ref[0, 0, i0 + k] * outs[token_rows(r, SUBLANES), :]

        for_rows(scatter)
        return carry

    lax.fori_loop(0, (count + block_rows - 1) // block_rows, block, 0)

    @pl.when(b >= N_EXPERTS)
    def _():
        rows = o_ref.shape[0]
        start = pl.multiple_of((b - N_EXPERTS) * rows * SUBLANES, rows * SUBLANES)
        y = jnp.concatenate([yacc[pl.ds(start + j, rows, stride=SUBLANES), :] for j in range(SUBLANES)], axis=1)
        x2 = x1_ref[...] + g2_ref[0] * y
        if normalize:
            x2 = (x2 * lax.rsqrt(jnp.mean(x2 * x2, axis=-1, keepdims=True) + NORM_EPS)) * fg_ref[...]
        o_ref[...] = x2


def _moe(route, h2p, w_gate, w_up, w_down, x1, g2, fg, normalize):
    B, S, D = x1.shape
    T = B * S
    words = h2p.shape[0] // T
    assert D == words * 2 * LANES == SUBLANES * LANES
    de = w_gate.shape[2]
    n_chunks = MOE_CHUNKS
    tc = T // n_chunks
    fin_rows = ROW_TILE // 2
    n_fin = tc // fin_rows
    assert n_chunks % B == 0 and tc % fin_rows == 0 and w_gate.shape[0] == N_EXPERTS
    starts, counts, stok, sw = _dispatch_tables(route, n_chunks)
    na = stok.shape[2]
    wmap = lambda c, b, *_: (jnp.minimum(b, N_EXPERTS - 1), 0, 0)
    smem = lambda: pl.BlockSpec((1, 1, na), lambda c, b, *_: (c, 0, 0), memory_space=pltpu.SMEM)
    tile = pl.BlockSpec((fin_rows, D), lambda c, b, *_: (c * n_fin + jnp.maximum(b - N_EXPERTS, 0), 0))
    grid_spec = pltpu.PrefetchScalarGridSpec(
        num_scalar_prefetch=2,
        grid=(n_chunks, N_EXPERTS + n_fin),
        in_specs=[smem(), smem(),
                  pl.BlockSpec((tc * words, LANES), lambda c, b, *_: (c, 0)),
                  pl.BlockSpec((1, D, de), wmap),
                  pl.BlockSpec((1, D, de), wmap),
                  pl.BlockSpec((1, de, D), wmap),
                  tile,
                  pl.BlockSpec((1, 1, D), lambda c, b, *_: (c // (n_chunks // B), 0, 0)),
                  pl.BlockSpec((1, D), lambda c, b, *_: (0, 0))],
        out_specs=tile,
        scratch_shapes=[pltpu.VMEM((tc * SUBLANES, LANES), f32),
                        pltpu.VMEM((EXPERT_ROWS * words, LANES), jnp.uint32),
                        pltpu.VMEM((EXPERT_ROWS * SUBLANES, LANES), f32)],
    )
    out = pl.pallas_call(
        functools.partial(_moe_kernel, normalize=normalize),
        grid_spec=grid_spec,
        out_shape=jax.ShapeDtypeStruct((T, D), f32),
        compiler_params=_params(("arbitrary", "arbitrary")),
        name="moe",
    )(starts, counts, stok, sw, h2p, w_gate, w_up, w_down,
      x1.reshape(T, D), g2.reshape(B, 1, D), fg.reshape(1, D))
    return out.reshape(B, S, D)


def kernel(x, c, rel_bias, norm1_g, norm2_g, w_ada, b_ada, w_in, b_gate, conv_w, w_attn_out,
           w_conv_out, w_o, w_router_group, b_router_group, w_router_expert, b_router_expert,
           w_exp_gate, w_exp_up, w_exp_down, final_norm_g):
    B, S, D = x.shape
    depth = w_ada.shape[0]
    assert S % ROW_TILE == 0 and ROW_TILE % MOBA_BLOCK == 0
    own, adj = _bias_tiles(rel_bias)
    for l in range(depth):
        mod = _mod(c, w_ada[l], b_ada[l])
        sh1, sc1, g1, sh2, sc2, g2 = jnp.split(mod, 6, axis=-1)
        qT, k4, vT4, km, mc, ga = _inproj(x, norm1_g[l], sc1, sh1, w_in[l], b_gate[l], conv_w[l],
                                          w_conv_out[l])
        a, expert_w = _attention(qT, k4, vT4, km, own, adj, [w_exp_gate[l], w_exp_up[l], w_exp_down[l]])
        x1, h2p, route = _post(a, mc, ga, x, g1, sc2, sh2, norm2_g[l], w_attn_out[l], w_o[l],
                               w_router_group[l], b_router_group[l], w_router_expert[l],
                               b_router_expert[l])
        x = _moe(route, h2p, *expert_w, x1, g2, final_norm_g, normalize=(l + 1 == depth))
    return x
```

```python
import functools
import math

import numpy as np
import jax
import jax.numpy as jnp
from jax import lax
from jax.experimental import pallas as pl
from jax.experimental.pallas import tpu as pltpu

f32 = jnp.float32
bf16 = jnp.bfloat16

N_HEADS = 8
HEAD_DIM = 64
ATTN_WIDTH = N_HEADS * HEAD_DIM
MOBA_BLOCK = 256
MOBA_TOPK = 3
CONV_K = 3
N_BUCKETS = 32
MAX_EXACT = N_BUCKETS // 2
MAX_DISTANCE = 128
N_GROUPS = 4
EXPERTS_PER_GROUP = 8
N_EXPERTS = N_GROUPS * EXPERTS_PER_GROUP
TOP_K_EXPERTS = 2
NORM_EPS = 1e-6
NEG_INF = -1e30

LANES = 128
SUBLANES = 8
MOD_COLS = 1024
ROW_TILE = 1024
POST_SPLIT = 2
EXPERT_ROWS = 320
ROW_UNROLL = 8
MOE_CHUNKS = 4
FAR_BLOCKS_PER_ITER = 2
FAR_SUBSTEPS = 2
SUM_ROWS = 16
LOG2E = 1.4426950408889634
VMEM_LIMIT = 56 * 1024 * 1024


def _params(sem, vmem=VMEM_LIMIT):
    return pltpu.CompilerParams(dimension_semantics=sem, vmem_limit_bytes=vmem)


def _mod_kernel(c_ref, w_ref, b_ref, o_ref, *, batch):
    act = jax.nn.silu(c_ref[...]).T
    w = w_ref[...]
    rows = [jnp.sum(act[:, r:r + 1] * w, axis=0, keepdims=True) for r in range(batch)]
    rows.append(jnp.zeros((o_ref.shape[0] - batch, w.shape[1]), f32))
    o_ref[...] = jnp.concatenate(rows, axis=0) + b_ref[...]


def _mod(c, w_ada, b_ada):
    B, D = c.shape
    n_out = w_ada.shape[1]
    rows = SUBLANES
    cp = jnp.zeros((rows, D), f32).at[:B].set(c)
    tn = MOD_COLS
    out = pl.pallas_call(
        functools.partial(_mod_kernel, batch=B),
        grid=(n_out // tn,),
        in_specs=[pl.BlockSpec((rows, D), lambda j: (0, 0)),
                  pl.BlockSpec((D, tn), lambda j: (0, j)),
                  pl.BlockSpec((1, tn), lambda j: (0, j))],
        out_specs=pl.BlockSpec((rows, tn), lambda j: (0, j)),
        out_shape=jax.ShapeDtypeStruct((rows, n_out), f32),
        compiler_params=_params(("arbitrary",)),
        name="mod",
    )(cp, w_ada, b_ada.reshape(1, n_out))
    return out[:B]


def _bucket_thresholds():
    n = np.arange(2 * MOBA_BLOCK)

    def buckets(ft):
        nf = np.maximum(n, 1).astype(ft)
        large = MAX_EXACT + (np.log(nf / ft(MAX_EXACT)) / ft(math.log(MAX_DISTANCE / MAX_EXACT))
                             * ft(N_BUCKETS - MAX_EXACT)).astype(np.int32)
        return np.where(n < MAX_EXACT, n, np.minimum(large, N_BUCKETS - 1))

    b = buckets(np.float32)
    assert np.array_equal(b, buckets(np.float64)) and np.all(np.diff(b) >= 0)
    assert b[-1] == N_BUCKETS - 1 and b[MOBA_BLOCK + 1] == N_BUCKETS - 1
    return [int(np.argmax(b >= j)) for j in range(N_BUCKETS)]


def _bias_kernel(rb_ref, own_ref, adj_ref, *, thresholds):
    h = pl.program_id(0)
    kk = lax.broadcasted_iota(jnp.int32, (MOBA_BLOCK, MOBA_BLOCK), 0)
    qq = lax.broadcasted_iota(jnp.int32, (MOBA_BLOCK, MOBA_BLOCK), 1)
    far = rb_ref[N_BUCKETS - 1, h]

    def table(dist):
        acc = jnp.full(dist.shape, rb_ref[0, h], f32)
        for j in range(1, N_BUCKETS):
            acc = jnp.where(dist >= thresholds[j], rb_ref[j, h], acc)
        return (acc - far) * LOG2E

    own_ref[0] = table(qq - kk)
    adj_ref[0] = table(qq - kk + MOBA_BLOCK)


def _bias_tiles(rel_bias):
    shp = jax.ShapeDtypeStruct((N_HEADS, MOBA_BLOCK, MOBA_BLOCK), f32)
    spec = pl.BlockSpec((1, MOBA_BLOCK, MOBA_BLOCK), lambda h: (h, 0, 0))
    return pl.pallas_call(
        functools.partial(_bias_kernel, thresholds=_bucket_thresholds()),
        grid=(N_HEADS,),
        in_specs=[pl.BlockSpec(memory_space=pltpu.SMEM)],
        out_specs=(spec, spec),
        out_shape=(shp, shp),
        compiler_params=_params(("arbitrary",)),
        name="bias",
    )(rel_bias)


def _ada_norm(x, g, scale, shift):
    y = x * lax.rsqrt(jnp.mean(x * x, axis=-1, keepdims=True) + NORM_EPS)
    return (y * g) * (1.0 + scale) + shift


def _inproj_kernel(x_ref, g_ref, sc_ref, sh_ref, wqT_ref, wk_ref, wvT_ref, wc_ref, wgl_ref,
                   wco_ref, cw_ref, bg_ref,
                   qT_ref, k_ref, vT_ref, km_ref, mc_ref, ga_ref, ubuf):
    tm = x_ref.shape[1]
    cwid = wco_ref.shape[0]
    d = x_ref.shape[2]
    s = pl.program_id(1)
    hb = _ada_norm(x_ref[0], g_ref[...], sc_ref[0], sh_ref[0]).astype(bf16)

    nt = (((1,), (1,)), ((), ()))
    qT_ref[0] = lax.dot_general(wqT_ref[...], hb, nt, preferred_element_type=f32).astype(bf16)
    k = jnp.dot(hb, wk_ref[...], preferred_element_type=f32)
    vT = lax.dot_general(wvT_ref[...], hb, nt, preferred_element_type=f32).astype(bf16)
    for j in range(tm // MOBA_BLOCK):
        rows = slice(j * MOBA_BLOCK, (j + 1) * MOBA_BLOCK)
        k_ref[0, j] = k[rows].astype(bf16)
        vT_ref[0, j] = vT[:, rows]
        km_ref[0, 0, j:j + 1, :] = jnp.mean(k[rows], axis=0, keepdims=True)

    cbx = jnp.dot(hb, wc_ref[...], preferred_element_type=f32)
    u = cbx[:, :cwid] * cbx[:, 2 * cwid:]

    halo = SUBLANES

    @pl.when(s == 0)
    def _():
        ubuf[0:halo, :] = jnp.zeros((halo, cwid), f32)

    ubuf[halo:halo + tm, :] = u
    cw = cw_ref[...]
    taps = [ubuf[halo - back:halo - back + tm, :] for back in range(CONV_K - 1, 0, -1)] + [u]
    conv = cw[0:1] * taps[0]
    for j in range(1, CONV_K):
        conv = conv + cw[j:j + 1] * taps[j]
    ubuf[0:halo, :] = ubuf[tm:tm + halo, :]
    cv = (cbx[:, cwid:2 * cwid] * conv).astype(bf16)
    y_conv = jnp.dot(cv, wco_ref[...], preferred_element_type=f32)

    gl = jnp.dot(hb, wgl_ref[...], preferred_element_type=f32) + bg_ref[...]
    ga_ref[0] = jax.nn.sigmoid(gl[:, :d]).astype(bf16)
    mc_ref[0] = (jax.nn.sigmoid(gl[:, d:]) * y_conv).astype(bf16)


def _inproj(x, g, sc, sh, w_in, b_gate, conv_w, w_conv_out):
    B, S, D = x.shape
    tm = ROW_TILE
    nb = S // MOBA_BLOCK
    bpt = tm // MOBA_BLOCK
    aw = ATTN_WIDTH
    cwid = conv_w.shape[1]
    scale = HEAD_DIM ** -0.5 * LOG2E
    wqT = (w_in[:, :aw] * scale).T.astype(bf16)
    wk = w_in[:, aw:2 * aw].astype(bf16)
    wvT = w_in[:, 2 * aw:3 * aw].T.astype(bf16)
    wc = w_in[:, 3 * aw:3 * aw + 3 * cwid].astype(bf16)
    wgl = w_in[:, 3 * aw + 3 * cwid:].astype(bf16)
    wco = w_conv_out.astype(bf16)

    const = lambda shape: pl.BlockSpec(shape, lambda b, s: (0,) * len(shape))
    vec = pl.BlockSpec((1, 1, D), lambda b, s: (b, 0, 0))
    row = lambda width: pl.BlockSpec((1, tm, width), lambda b, s: (b, s, 0))
    outs = pl.pallas_call(
        _inproj_kernel,
        grid=(B, S // tm),
        in_specs=[row(D), const((1, D)), vec, vec,
                  const((aw, D)), const((D, aw)), const((aw, D)), const((D, 3 * cwid)),
                  const((D, 2 * D)), const((cwid, D)), const((CONV_K, cwid)), const((1, 2 * D))],
        out_specs=(pl.BlockSpec((1, aw, tm), lambda b, s: (b, 0, s)),
                   pl.BlockSpec((1, bpt, MOBA_BLOCK, aw), lambda b, s: (b, s, 0, 0)),
                   pl.BlockSpec((1, bpt, aw, MOBA_BLOCK), lambda b, s: (b, s, 0, 0)),
                   pl.BlockSpec((1, 1, bpt, aw), lambda b, s: (b, s, 0, 0)),
                   row(D), row(D)),
        out_shape=(jax.ShapeDtypeStruct((B, aw, S), bf16),
                   jax.ShapeDtypeStruct((B, nb, MOBA_BLOCK, aw), bf16),
                   jax.ShapeDtypeStruct((B, nb, aw, MOBA_BLOCK), bf16),
                   jax.ShapeDtypeStruct((B, S // tm, bpt, aw), f32),
                   jax.ShapeDtypeStruct((B, S, D), bf16),
                   jax.ShapeDtypeStruct((B, S, D), bf16)),
        scratch_shapes=[pltpu.VMEM((tm + SUBLANES, cwid), f32)],
        compiler_params=_params(("parallel", "arbitrary")),
        name="inproj",
    )(x, g.reshape(1, D), sc.reshape(B, 1, D), sh.reshape(B, 1, D),
      wqT, wk, wvT, wc, wgl, wco, conv_w, b_gate.reshape(1, 2 * D))
    qT, k4, vT4, km, mc, ga = outs
    return qT, k4, vT4, km.reshape(B, nb, aw), mc, ga


def _attn_kernel(qT_ref, k_ref, vT_ref, km_ref, own_ref, adj_ref, *side_refs):
    n_side = len(side_refs) // 2
    o_ref = side_refs[n_side]
    for src, dst in zip(side_refs[:n_side], side_refs[n_side + 1:]):
        dst[...] = src[...].astype(bf16)

    i = pl.program_id(1)
    bs = MOBA_BLOCK
    nb = km_ref.shape[1]
    n_heads = qT_ref.shape[1] // HEAD_DIM
    heads = range(n_heads)
    blk = lax.broadcasted_iota(jnp.int32, (nb, bs), 0)

    qcat = []
    for h in heads:
        pair, a = divmod(h, LANES // HEAD_DIM)
        qT = qT_ref[0, pair * LANES:(pair + 1) * LANES, :]
        row = lax.broadcasted_iota(jnp.int32, qT.shape, 0)
        qa = jnp.where((row >= a * HEAD_DIM) & (row < (a + 1) * HEAD_DIM), qT, jnp.zeros_like(qT))
        km = km_ref[0, :, pair * LANES:(pair + 1) * LANES]
        km_hi = km.astype(bf16)
        km_mid = (km - km_hi.astype(f32)).astype(bf16)
        km_lo = (km - km_hi.astype(f32) - km_mid.astype(f32)).astype(bf16)
        g = (jnp.dot(km_hi, qa, preferred_element_type=f32)
             + (jnp.dot(km_mid, qa, preferred_element_type=f32) + jnp.dot(km_lo, qa, preferred_element_type=f32)))
        g = jnp.where(blk < i, g, NEG_INF)
        fq = jnp.full((nb, bs), NEG_INF, f32)
        for _ in range(MOBA_TOPK):
            top = jnp.max(g, axis=0, keepdims=True)
            first = jnp.min(jnp.where(g == top, blk, nb), axis=0, keepdims=True)
            pick = blk == first
            fq = jnp.where(pick, 0.0, fq)
            g = jnp.where(pick, -jnp.inf, g)
        fq = jnp.where(blk < i, fq, 0.0).astype(bf16)
        pad = jnp.zeros((LANES - nb, bs), bf16)
        qcat.append(jnp.concatenate([qa, fq, pad], axis=0))

    lane = lax.broadcasted_iota(jnp.int32, (bs, LANES), 1)
    causal = (lax.broadcasted_iota(jnp.int32, (bs, bs), 0) <= lax.broadcasted_iota(jnp.int32, (bs, bs), 1))
    ones_rows = jnp.ones((SUM_ROWS, bs), bf16)

    def qk(blocks, h):
        pair = h // (LANES // HEAD_DIM)
        cols = slice(pair * LANES, (pair + 1) * LANES)
        return [jnp.dot(jnp.concatenate([k_ref[0, n, :, cols], jnp.where(lane == n, 1.0, 0.0).astype(bf16)], axis=1),
                        qcat[h], preferred_element_type=f32) for n, _ in blocks]

    def step(blocks, state, scores=None, next_blocks=None):
        if scores is None:
            scores = [qk(blocks, h) for h in heads]
        next_scores = []
        out = []
        for h in heads:
            sTs = []
            for (n, kind), sT in zip(blocks, scores[h]):
                if kind == "own":
                    sT = jnp.where(causal, own_ref[h] + sT, NEG_INF)
                elif kind == "adj":
                    sT = adj_ref[h] + sT
                sTs.append(sT)
            m_new = None if state is None else state[h][0]
            for sT in sTs:
                top = jnp.max(sT, axis=0, keepdims=True)
                m_new = top if m_new is None else jnp.maximum(m_new, top)
            acc = None if state is None else jnp.exp2(state[h][0] - m_new) * state[h][1]
            for (n, _), sT in zip(blocks, sTs):
                p = jnp.exp2(sT - m_new).astype(bf16)
                v_ext = jnp.concatenate([vT_ref[0, n, h * HEAD_DIM:(h + 1) * HEAD_DIM, :], ones_rows], axis=0)
                pv = jnp.dot(v_ext, p, preferred_element_type=f32)
                acc = pv if acc is None else acc + pv
            out.append((m_new, acc))
            if next_blocks is not None:
                next_scores.append(qk(next_blocks, h))
        return tuple(out), next_scores

    state = lax.cond(i >= 1,
                     lambda: step([(i, "own"), (i - 1, "adj")], None)[0],
                     lambda: step([(i, "own")], None)[0])
    n_far = jnp.maximum(i - 1, 0)
    per = FAR_BLOCKS_PER_ITER
    def far_iter(it, st):
        blocks_of = lambda u: [((it * FAR_SUBSTEPS + u) * per + j, "far") for j in range(per)]
        scores = None
        for u in range(FAR_SUBSTEPS):
            st, scores = step(blocks_of(u), st, scores, blocks_of(u + 1) if u + 1 < FAR_SUBSTEPS else None)
        return st

    state = lax.fori_loop(0, n_far // (per * FAR_SUBSTEPS), far_iter, state)
    per = per * FAR_SUBSTEPS
    done = n_far - n_far % per
    size = per // 2
    while size >= 1:
        take = (n_far % per) & size
        state = lax.cond(take > 0,
                         lambda st, d=done, sz=size: step([(d + j, "far") for j in range(sz)], st)[0],
                         lambda st: st, state)
        done = done + take
        size //= 2
    outT = jnp.concatenate([acc[:HEAD_DIM] / acc[HEAD_DIM:HEAD_DIM + 1] for (_, acc) in state], axis=0)
    o_ref[0] = outT.T.astype(bf16)


def _attention(qT, k4, vT4, km, own, adj, side_weights):
    B, aw, S = qT.shape
    nb = S // MOBA_BLOCK
    assert MOBA_TOPK < nb <= LANES
    n_heads = aw // HEAD_DIM
    steps = B * nb
    whole = lambda shape: pl.BlockSpec(shape, lambda b, i: (b,) + (0,) * (len(shape) - 1))
    table = pl.BlockSpec((n_heads, MOBA_BLOCK, MOBA_BLOCK), lambda b, i: (0, 0, 0))
    slabs = [w.reshape(steps, -1, w.shape[-1]) for w in side_weights]
    slab_specs = [pl.BlockSpec((1,) + w.shape[1:], lambda b, i: (b * nb + i, 0, 0)) for w in slabs]
    outs = pl.pallas_call(
        _attn_kernel,
        grid=(B, nb),
        in_specs=[pl.BlockSpec((1, aw, MOBA_BLOCK), lambda b, i: (b, 0, i)),
                  whole((1, nb, MOBA_BLOCK, aw)), whole((1, nb, aw, MOBA_BLOCK)), whole((1, nb, aw)),
                  table, table] + slab_specs,
        out_specs=[pl.BlockSpec((1, MOBA_BLOCK, aw), lambda b, i: (b, i, 0))] + slab_specs,
        out_shape=[jax.ShapeDtypeStruct((B, S, aw), bf16)]
                  + [jax.ShapeDtypeStruct(w.shape, bf16) for w in slabs],
        compiler_params=_params(("parallel", "arbitrary")),
        name="attn",
    )(qT, k4, vT4, km, own, adj, *slabs)
    return outs[0], [o.reshape(w.shape) for o, w in zip(outs[1:], side_weights)]


def _post_kernel(a_ref, mc_ref, ga_ref, x_ref, g1_ref, sc_ref, sh_ref, ng_ref,
                 wao_ref, wo_ref, wr_ref, br_ref, x1_ref, h2p_ref, route_ref):
    tm = x_ref.shape[1]
    d = x_ref.shape[2]
    sub = tm // POST_SPLIT
    lane = lax.broadcasted_iota(jnp.int32, (sub, LANES), 1)
    ninf = -jnp.inf
    for t in range(POST_SPLIT):
        rows = slice(t * sub, (t + 1) * sub)
        ya = jnp.dot(a_ref[0, rows, :], wao_ref[...], preferred_element_type=f32)
        merged = ga_ref[0, rows, :].astype(f32) * ya + mc_ref[0, rows, :].astype(f32)
        z = jnp.dot(merged.astype(bf16), wo_ref[...], preferred_element_type=f32)
        x1 = x_ref[0, rows, :] + g1_ref[0] * z
        x1_ref[0, rows, :] = x1
        h2 = _ada_norm(x1, ng_ref[...], sc_ref[0], sh_ref[0])
        h_hi = h2.astype(bf16)
        h_hi32 = h_hi.astype(f32)

        bits = pltpu.bitcast(h_hi32, jnp.uint32)
        packed = (bits[:, :d // 2] >> 16) | (bits[:, d // 2:] & jnp.uint32(0xFFFF0000))
        chunks = packed.shape[1] // LANES
        for j in range(chunks):
            h2p_ref[pl.ds(t * sub * chunks + j, sub, stride=chunks), :] = packed[:, j * LANES:(j + 1) * LANES]

        h_lo = (h2 - h_hi32).astype(bf16)
        hi_both = jnp.dot(h_hi, wr_ref[...], preferred_element_type=f32)
        logits = (hi_both[:, :LANES]
                  + (jnp.dot(h_lo, wr_ref[:, :LANES], preferred_element_type=f32) + hi_both[:, LANES:])) + br_ref[...]
        gl = jnp.where(lane < N_GROUPS, logits, ninf)
        gmax = jnp.max(gl, axis=1, keepdims=True)
        g_sel = jnp.min(jnp.where(gl == gmax, lane, LANES), axis=1, keepdims=True)
        g_w = 1.0 / jnp.sum(jnp.exp(gl - gmax), axis=1, keepdims=True)
        off = lane - (N_GROUPS + EXPERTS_PER_GROUP * g_sel)
        in_group = (off >= 0) & (off < EXPERTS_PER_GROUP)
        el = jnp.where(in_group, logits, ninf)
        ex = jnp.exp(el - jnp.max(el, axis=1, keepdims=True))
        ep = jnp.where(in_group, ex / jnp.sum(ex, axis=1, keepdims=True), -1.0)
        p1 = jnp.max(ep, axis=1, keepdims=True)
        i1 = jnp.min(jnp.where(ep == p1, lane, LANES), axis=1, keepdims=True)
        ep2 = jnp.where(lane == i1, -1.0, ep)
        p2 = jnp.max(ep2, axis=1, keepdims=True)
        i2 = jnp.min(jnp.where(ep2 == p2, lane, LANES), axis=1, keepdims=True)
        tot = p1 + p2
        w1 = p1 / tot * g_w
        w2 = p2 / tot * g_w
        e1 = (i1 - N_GROUPS).astype(f32)
        e2 = (i2 - N_GROUPS).astype(f32)
        route = jnp.where(lane == 0, e1, jnp.where(lane == 1, e2,
                          jnp.where(lane == 2, w1, jnp.where(lane == 3, w2, 0.0))))
        route_ref[:, rows] = route.T[:SUBLANES, :]


def _post(a, mc, ga, x, g1, sc2, sh2, ng, w_attn_out, w_o, w_rg, b_rg, w_re, b_re):
    B, S, D = x.shape
    tm = ROW_TILE
    aw = a.shape[2]
    nt = S // tm
    wr = jnp.zeros((D, LANES), f32).at[:, :N_GROUPS].set(w_rg).at[:, N_GROUPS:N_GROUPS + N_EXPERTS].set(w_re)
    br = jnp.zeros((1, LANES), f32).at[0, :N_GROUPS].set(b_rg).at[0, N_GROUPS:N_GROUPS + N_EXPERTS].set(b_re)
    wr_hi = wr.astype(bf16)
    wr_split = jnp.concatenate([wr_hi, (wr - wr_hi.astype(f32)).astype(bf16)], axis=1)
    const = lambda shape: pl.BlockSpec(shape, lambda b, s: (0,) * len(shape))
    vec = pl.BlockSpec((1, 1, D), lambda b, s: (b, 0, 0))
    row = lambda width: pl.BlockSpec((1, tm, width), lambda b, s: (b, s, 0))
    flat = lambda width, rows_per_token=1: pl.BlockSpec((tm * rows_per_token, width), lambda b, s: (b * nt + s, 0))
    words = D // 2 // LANES
    return pl.pallas_call(
        _post_kernel,
        grid=(B, nt),
        in_specs=[row(aw), row(D), row(D), row(D), vec, vec, vec, const((1, D)),
                  const((aw, D)), const((D, D)), const((D, 2 * LANES)), const((1, LANES))],
        out_specs=(row(D), flat(LANES, words), pl.BlockSpec((SUBLANES, tm), lambda b, s: (0, b * nt + s))),
        out_shape=(jax.ShapeDtypeStruct((B, S, D), f32),
                   jax.ShapeDtypeStruct((B * S * words, LANES), jnp.uint32),
                   jax.ShapeDtypeStruct((SUBLANES, B * S), f32)),
        compiler_params=_params(("parallel", "parallel")),
        name="post",
    )(a, mc, ga, x, g1.reshape(B, 1, D), sc2.reshape(B, 1, D), sh2.reshape(B, 1, D), ng.reshape(1, D),
      w_attn_out.astype(bf16), w_o.astype(bf16), wr_split, br)


def _dispatch_tables(route, n_chunks):
    T = route.shape[1]
    tc = T // n_chunks
    na = tc * TOP_K_EXPERTS
    per_chunk = lambda rows: rows.reshape(TOP_K_EXPERTS, n_chunks, tc).transpose(1, 0, 2).reshape(n_chunks, na)
    e = per_chunk(route[:TOP_K_EXPERTS]).astype(jnp.int32)
    w = per_chunk(route[TOP_K_EXPERTS:2 * TOP_K_EXPERTS])
    tok = jnp.broadcast_to(jnp.arange(na, dtype=jnp.int32) % tc, e.shape)
    _, stok, sw = lax.sort((e, tok, w), dimension=1, num_keys=1, is_stable=True)
    counts = jnp.sum((e[:, None, :] == jnp.arange(N_EXPERTS)[None, :, None]).astype(jnp.int32), axis=2)
    starts = jnp.cumsum(counts, axis=1) - counts
    flat = lambda t: t.reshape(-1).astype(jnp.int32)
    return flat(starts), flat(counts), stok.reshape(n_chunks, 1, na), sw.reshape(n_chunks, 1, na)


def _moe_kernel(start_ref, count_ref, stok_ref, sw_ref, h2p_ref, wg_ref, wu_ref, wd_ref,
                x1_ref, g2_ref, fg_ref, o_ref, yacc, xs, outs, *, normalize):
    c = pl.program_id(0)
    b = pl.program_id(1)
    block_rows = outs.shape[0] // SUBLANES
    words = xs.shape[0] // block_rows
    half = words * LANES
    run = c * N_EXPERTS + jnp.minimum(b, N_EXPERTS - 1)
    count = jnp.where(b < N_EXPERTS, count_ref[run], 0)
    first = start_ref[run]

    @pl.when(b == 0)
    def _():
        yacc[...] = jnp.zeros(yacc.shape, f32)

    @pl.when((b == 0) & (c == 0))
    def _():
        xs[...] = jnp.zeros(xs.shape, jnp.uint32)

    def token_rows(index, rows_per_token):
        return pl.ds(pl.multiple_of(index * rows_per_token, rows_per_token), rows_per_token)

    def block(blk, carry):
        st = first + blk * block_rows
        n = jnp.minimum(count - blk * block_rows, block_rows)

        def for_rows(rows_fn):
            full = n // ROW_UNROLL

            def group(g, carry):
                rows_fn([g * ROW_UNROLL + j for j in range(ROW_UNROLL)], st + g * ROW_UNROLL)
                return carry

            def single(j, carry):
                rows_fn([full * ROW_UNROLL + j], st + full * ROW_UNROLL + j)
                return carry

            lax.fori_loop(0, full, group, 0)
            lax.fori_loop(0, n - full * ROW_UNROLL, single, 0)

        def gather(rs, i0):
            for k, r in enumerate(rs):
                xs[token_rows(r, words), :] = h2p_ref[token_rows(stok_ref[0, 0, i0 + k], words), :]

        for_rows(gather)
        lo, hi = [], []
        for j in range(words):
            packed = xs[pl.ds(j, block_rows, stride=words), :]
            lo.append(pltpu.bitcast(packed << 16, f32).astype(bf16))
            hi.append(pltpu.bitcast(packed & jnp.uint32(0xFFFF0000), f32).astype(bf16))
        x_lo = jnp.concatenate(lo, axis=1)
        x_hi = jnp.concatenate(hi, axis=1)

        def up(w_ref):
            return (jnp.dot(x_lo, w_ref[0, :half, :], preferred_element_type=f32)
                    + jnp.dot(x_hi, w_ref[0, half:, :], preferred_element_type=f32))

        hidden = (jax.nn.silu(up(wg_ref)) * up(wu_ref)).astype(bf16)
        out = jnp.dot(hidden, wd_ref[0], preferred_element_type=f32)
        for j in range(SUBLANES):
            outs[pl.ds(j, block_rows, stride=SUBLANES), :] = out[:, j * LANES:(j + 1) * LANES]

        def scatter(rs, i0):
            toks = [stok_ref[0, 0, i0 + k] for k in range(len(rs))]
            acc = [yacc[token_rows(tok, SUBLANES), :] for tok in toks]
            for k, (r, tok, a) in enumerate(zip(rs, toks, acc)):
                yacc[token_rows(tok, SUBLANES), :] = a + sw_ref[0, 0, i0 + k] * outs[token_rows(r, SUBLANES), :]

        for_rows(scatter)
        return carry

    lax.fori_loop(0, (count + block_rows - 1) // block_rows, block, 0)

    @pl.when(b >= N_EXPERTS)
    def _():
        rows = o_ref.shape[0]
        start = pl.multiple_of((b - N_EXPERTS) * rows * SUBLANES, rows * SUBLANES)
        y = jnp.concatenate([yacc[pl.ds(start + j, rows, stride=SUBLANES), :] for j in range(SUBLANES)], axis=1)
        x2 = x1_ref[...] + g2_ref[0] * y
        if normalize:
            x2 = (x2 * lax.rsqrt(jnp.mean(x2 * x2, axis=-1, keepdims=True) + NORM_EPS)) * fg_ref[...]
        o_ref[...] = x2


def _moe(route, h2p, w_gate, w_up, w_down, x1, g2, fg, normalize):
    B, S, D = x1.shape
    T = B * S
    words = h2p.shape[0] // T
    assert D == words * 2 * LANES == SUBLANES * LANES
    de = w_gate.shape[2]
    n_chunks = MOE_CHUNKS
    tc = T // n_chunks
    fin_rows = ROW_TILE // 2
    n_fin = tc // fin_rows
    assert n_chunks % B == 0 and tc % fin_rows == 0 and w_gate.shape[0] == N_EXPERTS
    starts, counts, stok, sw = _dispatch_tables(route, n_chunks)
    na = stok.shape[2]
    wmap = lambda c, b, *_: (jnp.minimum(b, N_EXPERTS - 1), 0, 0)
    smem = lambda: pl.BlockSpec((1, 1, na), lambda c, b, *_: (c, 0, 0), memory_space=pltpu.SMEM)
    tile = pl.BlockSpec((fin_rows, D), lambda c, b, *_: (c * n_fin + jnp.maximum(b - N_EXPERTS, 0), 0))
    grid_spec = pltpu.PrefetchScalarGridSpec(
        num_scalar_prefetch=2,
        grid=(n_chunks, N_EXPERTS + n_fin),
        in_specs=[smem(), smem(),
                  pl.BlockSpec((tc * words, LANES), lambda c, b, *_: (c, 0)),
                  pl.BlockSpec((1, D, de), wmap),
                  pl.BlockSpec((1, D, de), wmap),
                  pl.BlockSpec((1, de, D), wmap),
                  tile,
                  pl.BlockSpec((1, 1, D), lambda c, b, *_: (c // (n_chunks // B), 0, 0)),
                  pl.BlockSpec((1, D), lambda c, b, *_: (0, 0))],
        out_specs=tile,
        scratch_shapes=[pltpu.VMEM((tc * SUBLANES, LANES), f32),
                        pltpu.VMEM((EXPERT_ROWS * words, LANES), jnp.uint32),
                        pltpu.VMEM((EXPERT_ROWS * SUBLANES, LANES), f32)],
    )
    out = pl.pallas_call(
        functools.partial(_moe_kernel, normalize=normalize),
        grid_spec=grid_spec,
        out_shape=jax.ShapeDtypeStruct((T, D), f32),
        compiler_params=_params(("arbitrary", "arbitrary")),
        name="moe",
    )(starts, counts, stok, sw, h2p, w_gate, w_up, w_down,
      x1.reshape(T, D), g2.reshape(B, 1, D), fg.reshape(1, D))
    return out.reshape(B, S, D)


def kernel(x, c, rel_bias, norm1_g, norm2_g, w_ada, b_ada, w_in, b_gate, conv_w, w_attn_out,
           w_conv_out, w_o, w_router_group, b_router_group, w_router_expert, b_router_expert,
           w_exp_gate, w_exp_up, w_exp_down, final_norm_g):
    B, S, D = x.shape
    depth = w_ada.shape[0]
    assert S % ROW_TILE == 0 and ROW_TILE % MOBA_BLOCK == 0
    own, adj = _bias_tiles(rel_bias)
    for l in range(depth):
        mod = _mod(c, w_ada[l], b_ada[l])
        sh1, sc1, g1, sh2, sc2, g2 = jnp.split(mod, 6, axis=-1)
        qT, k4, vT4, km, mc, ga = _inproj(x, norm1_g[l], sc1, sh1, w_in[l], b_gate[l], conv_w[l],
                                          w_conv_out[l])
        a, expert_w = _attention(qT, k4, vT4, km, own, adj, [w_exp_gate[l], w_exp_up[l], w_exp_down[l]])
        x1, h2p, route = _post(a, mc, ga, x, g1, sc2, sh2, norm2_g[l], w_attn_out[l], w_o[l],
                               w_router_group[l], b_router_group[l], w_router_expert[l],
                               b_router_expert[l])
        x = _moe(route, h2p, *expert_w, x1, g2, final_norm_g, normalize=(l + 1 == depth))
    return x
```

```python
import functools
import math

import numpy as np
import jax
import jax.numpy as jnp
from jax import lax
from jax.experimental import pallas as pl
from jax.experimental.pallas import tpu as pltpu

f32 = jnp.float32
bf16 = jnp.bfloat16

N_HEADS = 8
HEAD_DIM = 64
ATTN_WIDTH = N_HEADS * HEAD_DIM
MOBA_BLOCK = 256
MOBA_TOPK = 3
CONV_K = 3
N_BUCKETS = 32
MAX_EXACT = N_BUCKETS // 2
MAX_DISTANCE = 128
N_GROUPS = 4
EXPERTS_PER_GROUP = 8
N_EXPERTS = N_GROUPS * EXPERTS_PER_GROUP
TOP_K_EXPERTS = 2
NORM_EPS = 1e-6
NEG_INF = -1e30

LANES = 128
SUBLANES = 8
MOD_COLS = 1024
ROW_TILE = 1024
POST_SPLIT = 2
EXPERT_ROWS = 320
ROW_UNROLL = 8
MOE_CHUNKS = 4
FAR_BLOCKS_PER_ITER = 2
FAR_SUBSTEPS = 4
SUM_ROWS = 16
LOG2E = 1.4426950408889634
VMEM_LIMIT = 56 * 1024 * 1024


def _params(sem, vmem=VMEM_LIMIT):
    return pltpu.CompilerParams(dimension_semantics=sem, vmem_limit_bytes=vmem)


def _mod_kernel(c_ref, w_ref, b_ref, o_ref, *, batch):
    act = jax.nn.silu(c_ref[...]).T
    w = w_ref[...]
    rows = [jnp.sum(act[:, r:r + 1] * w, axis=0, keepdims=True) for r in range(batch)]
    rows.append(jnp.zeros((o_ref.shape[0] - batch, w.shape[1]), f32))
    o_ref[...] = jnp.concatenate(rows, axis=0) + b_ref[...]


def _mod(c, w_ada, b_ada):
    B, D = c.shape
    n_out = w_ada.shape[1]
    rows = SUBLANES
    cp = jnp.zeros((rows, D), f32).at[:B].set(c)
    tn = MOD_COLS
    out = pl.pallas_call(
        functools.partial(_mod_kernel, batch=B),
        grid=(n_out // tn,),
        in_specs=[pl.BlockSpec((rows, D), lambda j: (0, 0)),
                  pl.BlockSpec((D, tn), lambda j: (0, j)),
                  pl.BlockSpec((1, tn), lambda j: (0, j))],
        out_specs=pl.BlockSpec((rows, tn), lambda j: (0, j)),
        out_shape=jax.ShapeDtypeStruct((rows, n_out), f32),
        compiler_params=_params(("arbitrary",)),
        name="mod",
    )(cp, w_ada, b_ada.reshape(1, n_out))
    return out[:B]


def _bucket_thresholds():
    n = np.arange(2 * MOBA_BLOCK)

    def buckets(ft):
        nf = np.maximum(n, 1).astype(ft)
        large = MAX_EXACT + (np.log(nf / ft(MAX_EXACT)) / ft(math.log(MAX_DISTANCE / MAX_EXACT))
                             * ft(N_BUCKETS - MAX_EXACT)).astype(np.int32)
        return np.where(n < MAX_EXACT, n, np.minimum(large, N_BUCKETS - 1))

    b = buckets(np.float32)
    assert np.array_equal(b, buckets(np.float64)) and np.all(np.diff(b) >= 0)
    assert b[-1] == N_BUCKETS - 1 and b[MOBA_BLOCK + 1] == N_BUCKETS - 1
    return [int(np.argmax(b >= j)) for j in range(N_BUCKETS)]


def _bias_kernel(rb_ref, own_ref, adj_ref, *, thresholds):
    h = pl.program_id(0)
    kk = lax.broadcasted_iota(jnp.int32, (MOBA_BLOCK, MOBA_BLOCK), 0)
    qq = lax.broadcasted_iota(jnp.int32, (MOBA_BLOCK, MOBA_BLOCK), 1)
    far = rb_ref[N_BUCKETS - 1, h]

    def table(dist):
        acc = jnp.full(dist.shape, rb_ref[0, h], f32)
        for j in range(1, N_BUCKETS):
            acc = jnp.where(dist >= thresholds[j], rb_ref[j, h], acc)
        return (acc - far) * LOG2E

    own_ref[0] = table(qq - kk)
    adj_ref[0] = table(qq - kk + MOBA_BLOCK)


def _bias_tiles(rel_bias):
    shp = jax.ShapeDtypeStruct((N_HEADS, MOBA_BLOCK, MOBA_BLOCK), f32)
    spec = pl.BlockSpec((1, MOBA_BLOCK, MOBA_BLOCK), lambda h: (h, 0, 0))
    return pl.pallas_call(
        functools.partial(_bias_kernel, thresholds=_bucket_thresholds()),
        grid=(N_HEADS,),
        in_specs=[pl.BlockSpec(memory_space=pltpu.SMEM)],
        out_specs=(spec, spec),
        out_shape=(shp, shp),
        compiler_params=_params(("arbitrary",)),
        name="bias",
    )(rel_bias)


def _ada_norm(x, g, scale, shift):
    y = x * lax.rsqrt(jnp.mean(x * x, axis=-1, keepdims=True) + NORM_EPS)
    return (y * g) * (1.0 + scale) + shift


def _inproj_kernel(x_ref, g_ref, sc_ref, sh_ref, wqT_ref, wk_ref, wvT_ref, wc_ref, wgl_ref,
                   wco_ref, cw_ref, bg_ref,
                   qT_ref, k_ref, vT_ref, km_ref, mc_ref, ga_ref, ubuf):
    tm = x_ref.shape[1]
    cwid = wco_ref.shape[0]
    d = x_ref.shape[2]
    s = pl.program_id(1)
    hb = _ada_norm(x_ref[0], g_ref[...], sc_ref[0], sh_ref[0]).astype(bf16)

    nt = (((1,), (1,)), ((), ()))
    qT_ref[0] = lax.dot_general(wqT_ref[...], hb, nt, preferred_element_type=f32).astype(bf16)
    k = jnp.dot(hb, wk_ref[...], preferred_element_type=f32)
    vT = lax.dot_general(wvT_ref[...], hb, nt, preferred_element_type=f32).astype(bf16)
    for j in range(tm // MOBA_BLOCK):
        rows = slice(j * MOBA_BLOCK, (j + 1) * MOBA_BLOCK)
        k_ref[0, j] = k[rows].astype(bf16)
        vT_ref[0, j] = vT[:, rows]
        km_ref[0, 0, j:j + 1, :] = jnp.mean(k[rows], axis=0, keepdims=True)

    cbx = jnp.dot(hb, wc_ref[...], preferred_element_type=f32)
    u = cbx[:, :cwid] * cbx[:, 2 * cwid:]

    halo = SUBLANES

    @pl.when(s == 0)
    def _():
        ubuf[0:halo, :] = jnp.zeros((halo, cwid), f32)

    ubuf[halo:halo + tm, :] = u
    cw = cw_ref[...]
    taps = [ubuf[halo - back:halo - back + tm, :] for back in range(CONV_K - 1, 0, -1)] + [u]
    conv = cw[0:1] * taps[0]
    for j in range(1, CONV_K):
        conv = conv + cw[j:j + 1] * taps[j]
    ubuf[0:halo, :] = ubuf[tm:tm + halo, :]
    cv = (cbx[:, cwid:2 * cwid] * conv).astype(bf16)
    y_conv = jnp.dot(cv, wco_ref[...], preferred_element_type=f32)

    gl = jnp.dot(hb, wgl_ref[...], preferred_element_type=f32) + bg_ref[...]
    ga_ref[0] = jax.nn.sigmoid(gl[:, :d]).astype(bf16)
    mc_ref[0] = (jax.nn.sigmoid(gl[:, d:]) * y_conv).astype(bf16)


def _inproj(x, g, sc, sh, w_in, b_gate, conv_w, w_conv_out):
    B, S, D = x.shape
    tm = ROW_TILE
    nb = S // MOBA_BLOCK
    bpt = tm // MOBA_BLOCK
    aw = ATTN_WIDTH
    cwid = conv_w.shape[1]
    scale = HEAD_DIM ** -0.5 * LOG2E
    wqT = (w_in[:, :aw] * scale).T.astype(bf16)
    wk = w_in[:, aw:2 * aw].astype(bf16)
    wvT = w_in[:, 2 * aw:3 * aw].T.astype(bf16)
    wc = w_in[:, 3 * aw:3 * aw + 3 * cwid].astype(bf16)
    wgl = w_in[:, 3 * aw + 3 * cwid:].astype(bf16)
    wco = w_conv_out.astype(bf16)

    const = lambda shape: pl.BlockSpec(shape, lambda b, s: (0,) * len(shape))
    vec = pl.BlockSpec((1, 1, D), lambda b, s: (b, 0, 0))
    row = lambda width: pl.BlockSpec((1, tm, width), lambda b, s: (b, s, 0))
    outs = pl.pallas_call(
        _inproj_kernel,
        grid=(B, S // tm),
        in_specs=[row(D), const((1, D)), vec, vec,
                  const((aw, D)), const((D, aw)), const((aw, D)), const((D, 3 * cwid)),
                  const((D, 2 * D)), const((cwid, D)), const((CONV_K, cwid)), const((1, 2 * D))],
        out_specs=(pl.BlockSpec((1, aw, tm), lambda b, s: (b, 0, s)),
                   pl.BlockSpec((1, bpt, MOBA_BLOCK, aw), lambda b, s: (b, s, 0, 0)),
                   pl.BlockSpec((1, bpt, aw, MOBA_BLOCK), lambda b, s: (b, s, 0, 0)),
                   pl.BlockSpec((1, 1, bpt, aw), lambda b, s: (b, s, 0, 0)),
                   row(D), row(D)),
        out_shape=(jax.ShapeDtypeStruct((B, aw, S), bf16),
                   jax.ShapeDtypeStruct((B, nb, MOBA_BLOCK, aw), bf16),
                   jax.ShapeDtypeStruct((B, nb, aw, MOBA_BLOCK), bf16),
                   jax.ShapeDtypeStruct((B, S // tm, bpt, aw), f32),
                   jax.ShapeDtypeStruct((B, S, D), bf16),
                   jax.ShapeDtypeStruct((B, S, D), bf16)),
        scratch_shapes=[pltpu.VMEM((tm + SUBLANES, cwid), f32)],
        compiler_params=_params(("parallel", "arbitrary")),
        name="inproj",
    )(x, g.reshape(1, D), sc.reshape(B, 1, D), sh.reshape(B, 1, D),
      wqT, wk, wvT, wc, wgl, wco, conv_w, b_gate.reshape(1, 2 * D))
    qT, k4, vT4, km, mc, ga = outs
    return qT, k4, vT4, km.reshape(B, nb, aw), mc, ga


def _attn_kernel(qT_ref, k_ref, vT_ref, km_ref, own_ref, adj_ref, *side_refs):
    n_side = len(side_refs) // 2
    o_ref = side_refs[n_side]
    for src, dst in zip(side_refs[:n_side], side_refs[n_side + 1:]):
        dst[...] = src[...].astype(bf16)

    i = pl.program_id(1)
    bs = MOBA_BLOCK
    nb = km_ref.shape[1]
    n_heads = qT_ref.shape[1] // HEAD_DIM
    heads = range(n_heads)
    blk = lax.broadcasted_iota(jnp.int32, (nb, bs), 0)

    qcat = []
    for h in heads:
        pair, a = divmod(h, LANES // HEAD_DIM)
        qT = qT_ref[0, pair * LANES:(pair + 1) * LANES, :]
        row = lax.broadcasted_iota(jnp.int32, qT.shape, 0)
        qa = jnp.where((row >= a * HEAD_DIM) & (row < (a + 1) * HEAD_DIM), qT, jnp.zeros_like(qT))
        km = km_ref[0, :, pair * LANES:(pair + 1) * LANES]
        km_hi = km.astype(bf16)
        km_mid = (km - km_hi.astype(f32)).astype(bf16)
        km_lo = (km - km_hi.astype(f32) - km_mid.astype(f32)).astype(bf16)
        g = (jnp.dot(km_hi, qa, preferred_element_type=f32)
             + (jnp.dot(km_mid, qa, preferred_element_type=f32) + jnp.dot(km_lo, qa, preferred_element_type=f32)))
        g = jnp.where(blk < i, g, NEG_INF)
        fq = jnp.full((nb, bs), NEG_INF, f32)
        for _ in range(MOBA_TOPK):
            top = jnp.max(g, axis=0, keepdims=True)
            first = jnp.min(jnp.where(g == top, blk, nb), axis=0, keepdims=True)
            pick = blk == first
            fq = jnp.where(pick, 0.0, fq)
            g = jnp.where(pick, -jnp.inf, g)
        fq = jnp.where(blk < i, fq, 0.0).astype(bf16)
        pad = jnp.zeros((LANES - nb, bs), bf16)
        qcat.append(jnp.concatenate([qa, fq, pad], axis=0))

    lane = lax.broadcasted_iota(jnp.int32, (bs, LANES), 1)
    causal = (lax.broadcasted_iota(jnp.int32, (bs, bs), 0) <= lax.broadcasted_iota(jnp.int32, (bs, bs), 1))
    ones_rows = jnp.ones((SUM_ROWS, bs), bf16)

    def qk(blocks, h):
        pair = h // (LANES // HEAD_DIM)
        cols = slice(pair * LANES, (pair + 1) * LANES)
        return [jnp.dot(jnp.concatenate([k_ref[0, n, :, cols], jnp.where(lane == n, 1.0, 0.0).astype(bf16)], axis=1),
                        qcat[h], preferred_element_type=f32) for n, _ in blocks]

    def step(blocks, state, scores=None, next_blocks=None):
        if scores is None:
            scores = [qk(blocks, h) for h in heads]
        next_scores = []
        out = []
        for h in heads:
            sTs = []
            for (n, kind), sT in zip(blocks, scores[h]):
                if kind == "own":
                    sT = jnp.where(causal, own_ref[h] + sT, NEG_INF)
                elif kind == "adj":
                    sT = adj_ref[h] + sT
                sTs.append(sT)
            m_new = None if state is None else state[h][0]
            for sT in sTs:
                top = jnp.max(sT, axis=0, keepdims=True)
                m_new = top if m_new is None else jnp.maximum(m_new, top)
            acc = None if state is None else jnp.exp2(state[h][0] - m_new) * state[h][1]
            for (n, _), sT in zip(blocks, sTs):
                p = jnp.exp2(sT - m_new).astype(bf16)
                v_ext = jnp.concatenate([vT_ref[0, n, h * HEAD_DIM:(h + 1) * HEAD_DIM, :], ones_rows], axis=0)
                pv = jnp.dot(v_ext, p, preferred_element_type=f32)
                acc = pv if acc is None else acc + pv
            out.append((m_new, acc))
            if next_blocks is not None:
                next_scores.append(qk(next_blocks, h))
        return tuple(out), next_scores

    state = lax.cond(i >= 1,
                     lambda: step([(i, "own"), (i - 1, "adj")], None)[0],
                     lambda: step([(i, "own")], None)[0])
    n_far = jnp.maximum(i - 1, 0)
    per = FAR_BLOCKS_PER_ITER

    def chained(first, n_sub, st):
        blocks_of = lambda u: [(first + u * per + j, "far") for j in range(per)]
        scores = None
        for u in range(n_sub):
            st, scores = step(blocks_of(u), st, scores, blocks_of(u + 1) if u + 1 < n_sub else None)
        return st

    span = per * FAR_SUBSTEPS
    state = lax.fori_loop(0, n_far // span, lambda it, st: chained(it * span, FAR_SUBSTEPS, st), state)
    done = n_far - n_far % span
    size = span // 2
    while size >= 1:
        take = (n_far % span) & size
        if size >= per:
            region = lambda st, d=done, sz=size: chained(d, sz // per, st)
        else:
            region = lambda st, d=done, sz=size: step([(d + j, "far") for j in range(sz)], st)[0]
        state = lax.cond(take > 0, region, lambda st: st, state)
        done = done + take
        size //= 2
    outT = jnp.concatenate([acc[:HEAD_DIM] / acc[HEAD_DIM:HEAD_DIM + 1] for (_, acc) in state], axis=0)
    o_ref[0] = outT.T.astype(bf16)


def _attention(qT, k4, vT4, km, own, adj, side_weights):
    B, aw, S = qT.shape
    nb = S // MOBA_BLOCK
    assert MOBA_TOPK < nb <= LANES
    n_heads = aw // HEAD_DIM
    steps = B * nb
    whole = lambda shape: pl.BlockSpec(shape, lambda b, i: (b,) + (0,) * (len(shape) - 1))
    table = pl.BlockSpec((n_heads, MOBA_BLOCK, MOBA_BLOCK), lambda b, i: (0, 0, 0))
    slabs = [w.reshape(steps, -1, w.shape[-1]) for w in side_weights]
    slab_specs = [pl.BlockSpec((1,) + w.shape[1:], lambda b, i: (b * nb + i, 0, 0)) for w in slabs]
    outs = pl.pallas_call(
        _attn_kernel,
        grid=(B, nb),
        in_specs=[pl.BlockSpec((1, aw, MOBA_BLOCK), lambda b, i: (b, 0, i)),
                  whole((1, nb, MOBA_BLOCK, aw)), whole((1, nb, aw, MOBA_BLOCK)), whole((1, nb, aw)),
                  table, table] + slab_specs,
        out_specs=[pl.BlockSpec((1, MOBA_BLOCK, aw), lambda b, i: (b, i, 0))] + slab_specs,
        out_shape=[jax.ShapeDtypeStruct((B, S, aw), bf16)]
                  + [jax.ShapeDtypeStruct(w.shape, bf16) for w in slabs],
        compiler_params=_params(("parallel", "arbitrary")),
        name="attn",
    )(qT, k4, vT4, km, own, adj, *slabs)
    return outs[0], [o.reshape(w.shape) for o, w in zip(outs[1:], side_weights)]


def _post_kernel(a_ref, mc_ref, ga_ref, x_ref, g1_ref, sc_ref, sh_ref, ng_ref,
                 wao_ref, wo_ref, wr_ref, br_ref, x1_ref, h2p_ref, route_ref):
    tm = x_ref.shape[1]
    d = x_ref.shape[2]
    sub = tm // POST_SPLIT
    lane = lax.broadcasted_iota(jnp.int32, (sub, LANES), 1)
    ninf = -jnp.inf
    for t in range(POST_SPLIT):
        rows = slice(t * sub, (t + 1) * sub)
        ya = jnp.dot(a_ref[0, rows, :], wao_ref[...], preferred_element_type=f32)
        merged = ga_ref[0, rows, :].astype(f32) * ya + mc_ref[0, rows, :].astype(f32)
        z = jnp.dot(merged.astype(bf16), wo_ref[...], preferred_element_type=f32)
        x1 = x_ref[0, rows, :] + g1_ref[0] * z
        x1_ref[0, rows, :] = x1
        h2 = _ada_norm(x1, ng_ref[...], sc_ref[0], sh_ref[0])
        h_hi = h2.astype(bf16)
        h_hi32 = h_hi.astype(f32)

        bits = pltpu.bitcast(h_hi32, jnp.uint32)
        packed = (bits[:, :d // 2] >> 16) | (bits[:, d // 2:] & jnp.uint32(0xFFFF0000))
        chunks = packed.shape[1] // LANES
        for j in range(chunks):
            h2p_ref[pl.ds(t * sub * chunks + j, sub, stride=chunks), :] = packed[:, j * LANES:(j + 1) * LANES]

        h_lo = (h2 - h_hi32).astype(bf16)
        hi_both = jnp.dot(h_hi, wr_ref[...], preferred_element_type=f32)
        logits = (hi_both[:, :LANES]
                  + (jnp.dot(h_lo, wr_ref[:, :LANES], preferred_element_type=f32) + hi_both[:, LANES:])) + br_ref[...]
        gl = jnp.where(lane < N_GROUPS, logits, ninf)
        gmax = jnp.max(gl, axis=1, keepdims=True)
        g_sel = jnp.min(jnp.where(gl == gmax, lane, LANES), axis=1, keepdims=True)
        g_w = 1.0 / jnp.sum(jnp.exp(gl - gmax), axis=1, keepdims=True)
        off = lane - (N_GROUPS + EXPERTS_PER_GROUP * g_sel)
        in_group = (off >= 0) & (off < EXPERTS_PER_GROUP)
        el = jnp.where(in_group, logits, ninf)
        ex = jnp.exp(el - jnp.max(el, axis=1, keepdims=True))
        ep = jnp.where(in_group, ex / jnp.sum(ex, axis=1, keepdims=True), -1.0)
        p1 = jnp.max(ep, axis=1, keepdims=True)
        i1 = jnp.min(jnp.where(ep == p1, lane, LANES), axis=1, keepdims=True)
        ep2 = jnp.where(lane == i1, -1.0, ep)
        p2 = jnp.max(ep2, axis=1, keepdims=True)
        i2 = jnp.min(jnp.where(ep2 == p2, lane, LANES), axis=1, keepdims=True)
        tot = p1 + p2
        w1 = p1 / tot * g_w
        w2 = p2 / tot * g_w
        e1 = (i1 - N_GROUPS).astype(f32)
        e2 = (i2 - N_GROUPS).astype(f32)
        route = jnp.where(lane == 0, e1, jnp.where(lane == 1, e2,
                          jnp.where(lane == 2, w1, jnp.where(lane == 3, w2, 0.0))))
        route_ref[:, rows] = route.T[:SUBLANES, :]


def _post(a, mc, ga, x, g1, sc2, sh2, ng, w_attn_out, w_o, w_rg, b_rg, w_re, b_re):
    B, S, D = x.shape
    tm = ROW_TILE
    aw = a.shape[2]
    nt = S // tm
    wr = jnp.zeros((D, LANES), f32).at[:, :N_GROUPS].set(w_rg).at[:, N_GROUPS:N_GROUPS + N_EXPERTS].set(w_re)
    br = jnp.zeros((1, LANES), f32).at[0, :N_GROUPS].set(b_rg).at[0, N_GROUPS:N_GROUPS + N_EXPERTS].set(b_re)
    wr_hi = wr.astype(bf16)
    wr_split = jnp.concatenate([wr_hi, (wr - wr_hi.astype(f32)).astype(bf16)], axis=1)
    const = lambda shape: pl.BlockSpec(shape, lambda b, s: (0,) * len(shape))
    vec = pl.BlockSpec((1, 1, D), lambda b, s: (b, 0, 0))
    row = lambda width: pl.BlockSpec((1, tm, width), lambda b, s: (b, s, 0))
    flat = lambda width, rows_per_token=1: pl.BlockSpec((tm * rows_per_token, width), lambda b, s: (b * nt + s, 0))
    words = D // 2 // LANES
    return pl.pallas_call(
        _post_kernel,
        grid=(B, nt),
        in_specs=[row(aw), row(D), row(D), row(D), vec, vec, vec, const((1, D)),
                  const((aw, D)), const((D, D)), const((D, 2 * LANES)), const((1, LANES))],
        out_specs=(row(D), flat(LANES, words), pl.BlockSpec((SUBLANES, tm), lambda b, s: (0, b * nt + s))),
        out_shape=(jax.ShapeDtypeStruct((B, S, D), f32),
                   jax.ShapeDtypeStruct((B * S * words, LANES), jnp.uint32),
                   jax.ShapeDtypeStruct((SUBLANES, B * S), f32)),
        compiler_params=_params(("parallel", "parallel")),
        name="post",
    )(a, mc, ga, x, g1.reshape(B, 1, D), sc2.reshape(B, 1, D), sh2.reshape(B, 1, D), ng.reshape(1, D),
      w_attn_out.astype(bf16), w_o.astype(bf16), wr_split, br)


def _dispatch_tables(route, n_chunks):
    T = route.shape[1]
    tc = T // n_chunks
    na = tc * TOP_K_EXPERTS
    per_chunk = lambda rows: rows.reshape(TOP_K_EXPERTS, n_chunks, tc).transpose(1, 0, 2).reshape(n_chunks, na)
    e = per_chunk(route[:TOP_K_EXPERTS]).astype(jnp.int32)
    w = per_chunk(route[TOP_K_EXPERTS:2 * TOP_K_EXPERTS])
    tok = jnp.broadcast_to(jnp.arange(na, dtype=jnp.int32) % tc, e.shape)
    _, stok, sw = lax.sort((e, tok, w), dimension=1, num_keys=1, is_stable=True)
    counts = jnp.sum((e[:, None, :] == jnp.arange(N_EXPERTS)[None, :, None]).astype(jnp.int32), axis=2)
    starts = jnp.cumsum(counts, axis=1) - counts
    flat = lambda t: t.reshape(-1).astype(jnp.int32)
    return flat(starts), flat(counts), stok.reshape(n_chunks, 1, na), sw.reshape(n_chunks, 1, na)


def _moe_kernel(start_ref, count_ref, stok_ref, sw_ref, h2p_ref, wg_ref, wu_ref, wd_ref,
                x1_ref, g2_ref, fg_ref, o_ref, yacc, xs, outs, *, normalize):
    c = pl.program_id(0)
    b = pl.program_id(1)
    block_rows = outs.shape[0] // SUBLANES
    words = xs.shape[0] // block_rows
    half = words * LANES
    run = c * N_EXPERTS + jnp.minimum(b, N_EXPERTS - 1)
    count = jnp.where(b < N_EXPERTS, count_ref[run], 0)
    first = start_ref[run]

    @pl.when(b == 0)
    def _():
        yacc[...] = jnp.zeros(yacc.shape, f32)

    @pl.when((b == 0) & (c == 0))
    def _():
        xs[...] = jnp.zeros(xs.shape, jnp.uint32)

    def token_rows(index, rows_per_token):
        return pl.ds(pl.multiple_of(index * rows_per_token, rows_per_token), rows_per_token)

    def block(blk, carry):
        st = first + blk * block_rows
        n = jnp.minimum(count - blk * block_rows, block_rows)

        def for_rows(rows_fn):
            full = n // ROW_UNROLL

            def group(g, carry):
                rows_fn([g * ROW_UNROLL + j for j in range(ROW_UNROLL)], st + g * ROW_UNROLL)
                return carry

            def single(j, carry):
                rows_fn([full * ROW_UNROLL + j], st + full * ROW_UNROLL + j)
                return carry

            lax.fori_loop(0, full, group, 0)
            lax.fori_loop(0, n - full * ROW_UNROLL, single, 0)

        def gather(rs, i0):
            for k, r in enumerate(rs):
                xs[token_rows(r, words), :] = h2p_ref[token_rows(stok_ref[0, 0, i0 + k], words), :]

        for_rows(gather)
        lo, hi = [], []
        for j in range(words):
            packed = xs[pl.ds(j, block_rows, stride=words), :]
            lo.append(pltpu.bitcast(packed << 16, f32).astype(bf16))
            hi.append(pltpu.bitcast(packed & jnp.uint32(0xFFFF0000), f32).astype(bf16))
        x_lo = jnp.concatenate(lo, axis=1)
        x_hi = jnp.concatenate(hi, axis=1)

        def up(w_ref):
            return (jnp.dot(x_lo, w_ref[0, :half, :], preferred_element_type=f32)
                    + jnp.dot(x_hi, w_ref[0, half:, :], preferred_element_type=f32))

        hidden = (jax.nn.silu(up(wg_ref)) * up(wu_ref)).astype(bf16)
        out = jnp.dot(hidden, wd_ref[0], preferred_element_type=f32)
        for j in range(SUBLANES):
            outs[pl.ds(j, block_rows, stride=SUBLANES), :] = out[:, j * LANES:(j + 1) * LANES]

        def scatter(rs, i0):
            toks = [stok_ref[0, 0, i0 + k] for k in range(len(rs))]
            acc = [yacc[token_rows(tok, SUBLANES), :] for tok in toks]
            for k, (r, tok, a) in enumerate(zip(rs, toks, acc)):
                yacc[token_rows(tok, SUBLANES), :] = a + sw_ref[0, 0, i0 + k] * outs[token_rows(r, SUBLANES), :]

        for_rows(scatter)
        return carry

    lax.fori_loop(0, (count + block_rows - 1) // block_rows, block, 0)

    @pl.when(b >= N_EXPERTS)
    def _():
        rows = o_ref.shape[0]
        start = pl.multiple_of((b - N_EXPERTS) * rows * SUBLANES, rows * SUBLANES)
        y = jnp.concatenate([yacc[pl.ds(start + j, rows, stride=SUBLANES), :] for j in range(SUBLANES)], axis=1)
        x2 = x1_ref[...] + g2_ref[0] * y
        if normalize:
            x2 = (x2 * lax.rsqrt(jnp.mean(x2 * x2, axis=-1, keepdims=True) + NORM_EPS)) * fg_ref[...]
        o_ref[...] = x2


def _moe(route, h2p, w_gate, w_up, w_down, x1, g2, fg, normalize):
    B, S, D = x1.shape
    T = B * S
    words = h2p.shape[0] // T
    assert D == words * 2 * LANES == SUBLANES * LANES
    de = w_gate.shape[2]
    n_chunks = MOE_CHUNKS
    tc = T // n_chunks
    fin_rows = ROW_TILE // 2
    n_fin = tc // fin_rows
    assert n_chunks % B == 0 and tc % fin_rows == 0 and w_gate.shape[0] == N_EXPERTS
    starts, counts, stok, sw = _dispatch_tables(route, n_chunks)
    na = stok.shape[2]
    wmap = lambda c, b, *_: (jnp.minimum(b, N_EXPERTS - 1), 0, 0)
    smem = lambda: pl.BlockSpec((1, 1, na), lambda c, b, *_: (c, 0, 0), memory_space=pltpu.SMEM)
    tile = pl.BlockSpec((fin_rows, D), lambda c, b, *_: (c * n_fin + jnp.maximum(b - N_EXPERTS, 0), 0))
    grid_spec = pltpu.PrefetchScalarGridSpec(
        num_scalar_prefetch=2,
        grid=(n_chunks, N_EXPERTS + n_fin),
        in_specs=[smem(), smem(),
                  pl.BlockSpec((tc * words, LANES), lambda c, b, *_: (c, 0)),
                  pl.BlockSpec((1, D, de), wmap),
                  pl.BlockSpec((1, D, de), wmap),
                  pl.BlockSpec((1, de, D), wmap),
                  tile,
                  pl.BlockSpec((1, 1, D), lambda c, b, *_: (c // (n_chunks // B), 0, 0)),
                  pl.BlockSpec((1, D), lambda c, b, *_: (0, 0))],
        out_specs=tile,
        scratch_shapes=[pltpu.VMEM((tc * SUBLANES, LANES), f32),
                        pltpu.VMEM((EXPERT_ROWS * words, LANES), jnp.uint32),
                        pltpu.VMEM((EXPERT_ROWS * SUBLANES, LANES), f32)],
    )
    out = pl.pallas_call(
        functools.partial(_moe_kernel, normalize=normalize),
        grid_spec=grid_spec,
        out_shape=jax.ShapeDtypeStruct((T, D), f32),
        compiler_params=_params(("arbitrary", "arbitrary")),
        name="moe",
    )(starts, counts, stok, sw, h2p, w_gate, w_up, w_down,
      x1.reshape(T, D), g2.reshape(B, 1, D), fg.reshape(1, D))
    return out.reshape(B, S, D)


def kernel(x, c, rel_bias, norm1_g, norm2_g, w_ada, b_ada, w_in, b_gate, conv_w, w_attn_out,
           w_conv_out, w_o, w_router_group, b_router_group, w_router_expert, b_router_expert,
           w_exp_gate, w_exp_up, w_exp_down, final_norm_g):
    B, S, D = x.shape
    depth = w_ada.shape[0]
    assert S % ROW_TILE == 0 and ROW_TILE % MOBA_BLOCK == 0
    own, adj = _bias_tiles(rel_bias)
    for l in range(depth):
        mod = _mod(c, w_ada[l], b_ada[l])
        sh1, sc1, g1, sh2, sc2, g2 = jnp.split(mod, 6, axis=-1)
        qT, k4, vT4, km, mc, ga = _inproj(x, norm1_g[l], sc1, sh1, w_in[l], b_gate[l], conv_w[l],
                                          w_conv_out[l])
        a, expert_w = _attention(qT, k4, vT4, km, own, adj, [w_exp_gate[l], w_exp_up[l], w_exp_down[l]])
        x1, h2p, route = _post(a, mc, ga, x, g1, sc2, sh2, norm2_g[l], w_attn_out[l], w_o[l],
                               w_router_group[l], b_router_group[l], w_router_expert[l],
                               b_router_expert[l])
        x = _moe(route, h2p, *expert_w, x1, g2, final_norm_g, normalize=(l + 1 == depth))
    return x
```

```python
import functools
import math

import numpy as np
import jax
import jax.numpy as jnp
from jax import lax
from jax.experimental import pallas as pl
from jax.experimental.pallas import tpu as pltpu

f32 = jnp.float32
bf16 = jnp.bfloat16

N_HEADS = 8
HEAD_DIM = 64
ATTN_WIDTH = N_HEADS * HEAD_DIM
MOBA_BLOCK = 256
MOBA_TOPK = 3
CONV_K = 3
N_BUCKETS = 32
MAX_EXACT = N_BUCKETS // 2
MAX_DISTANCE = 128
N_GROUPS = 4
EXPERTS_PER_GROUP = 8
N_EXPERTS = N_GROUPS * EXPERTS_PER_GROUP
TOP_K_EXPERTS = 2
NORM_EPS = 1e-6
NEG_INF = -1e30

LANES = 128
SUBLANES = 8
MOD_COLS = 1024
ROW_TILE = 1024
POST_SPLIT = 2
EXPERT_ROWS = 320
ROW_UNROLL = 8
MOE_CHUNKS = 4
FAR_BLOCKS_PER_ITER = 2
FAR_SUBSTEPS = 4
SUM_ROWS = 16
LOG2E = 1.4426950408889634
VMEM_LIMIT = 56 * 1024 * 1024


def _params(sem, vmem=VMEM_LIMIT):
    return pltpu.CompilerParams(dimension_semantics=sem, vmem_limit_bytes=vmem)


def _mod_kernel(c_ref, w_ref, b_ref, o_ref, *, batch):
    act = jax.nn.silu(c_ref[...]).T
    w = w_ref[...]
    rows = [jnp.sum(act[:, r:r + 1] * w, axis=0, keepdims=True) for r in range(batch)]
    rows.append(jnp.zeros((o_ref.shape[0] - batch, w.shape[1]), f32))
    o_ref[...] = jnp.concatenate(rows, axis=0) + b_ref[...]


def _mod(c, w_ada, b_ada):
    B, D = c.shape
    n_out = w_ada.shape[1]
    rows = SUBLANES
    cp = jnp.zeros((rows, D), f32).at[:B].set(c)
    tn = MOD_COLS
    out = pl.pallas_call(
        functools.partial(_mod_kernel, batch=B),
        grid=(n_out // tn,),
        in_specs=[pl.BlockSpec((rows, D), lambda j: (0, 0)),
                  pl.BlockSpec((D, tn), lambda j: (0, j)),
                  pl.BlockSpec((1, tn), lambda j: (0, j))],
        out_specs=pl.BlockSpec((rows, tn), lambda j: (0, j)),
        out_shape=jax.ShapeDtypeStruct((rows, n_out), f32),
        compiler_params=_params(("arbitrary",)),
        name="mod",
    )(cp, w_ada, b_ada.reshape(1, n_out))
    return out[:B]


def _bucket_thresholds():
    n = np.arange(2 * MOBA_BLOCK)

    def buckets(ft):
        nf = np.maximum(n, 1).astype(ft)
        large = MAX_EXACT + (np.log(nf / ft(MAX_EXACT)) / ft(math.log(MAX_DISTANCE / MAX_EXACT))
                             * ft(N_BUCKETS - MAX_EXACT)).astype(np.int32)
        return np.where(n < MAX_EXACT, n, np.minimum(large, N_BUCKETS - 1))

    b = buckets(np.float32)
    assert np.array_equal(b, buckets(np.float64)) and np.all(np.diff(b) >= 0)
    assert b[-1] == N_BUCKETS - 1 and b[MOBA_BLOCK + 1] == N_BUCKETS - 1
    return [int(np.argmax(b >= j)) for j in range(N_BUCKETS)]


def _bias_kernel(rb_ref, own_ref, adj_ref, *, thresholds):
    h = pl.program_id(0)
    kk = lax.broadcasted_iota(jnp.int32, (MOBA_BLOCK, MOBA_BLOCK), 0)
    qq = lax.broadcasted_iota(jnp.int32, (MOBA_BLOCK, MOBA_BLOCK), 1)
    far = rb_ref[N_BUCKETS - 1, h]

    def table(dist):
        acc = jnp.full(dist.shape, rb_ref[0, h], f32)
        for j in range(1, N_BUCKETS):
            acc = jnp.where(dist >= thresholds[j], rb_ref[j, h], acc)
        return (acc - far) * LOG2E

    own_ref[0] = table(qq - kk)
    adj_ref[0] = table(qq - kk + MOBA_BLOCK)


def _bias_tiles(rel_bias):
    shp = jax.ShapeDtypeStruct((N_HEADS, MOBA_BLOCK, MOBA_BLOCK), f32)
    spec = pl.BlockSpec((1, MOBA_BLOCK, MOBA_BLOCK), lambda h: (h, 0, 0))
    return pl.pallas_call(
        functools.partial(_bias_kernel, thresholds=_bucket_thresholds()),
        grid=(N_HEADS,),
        in_specs=[pl.BlockSpec(memory_space=pltpu.SMEM)],
        out_specs=(spec, spec),
        out_shape=(shp, shp),
        compiler_params=_params(("arbitrary",)),
        name="bias",
    )(rel_bias)


def _ada_norm(x, g, scale, shift):
    y = x * lax.rsqrt(jnp.mean(x * x, axis=-1, keepdims=True) + NORM_EPS)
    return (y * g) * (1.0 + scale) + shift


def _inproj_kernel(x_ref, g_ref, sc_ref, sh_ref, wqT_ref, wk_ref, wvT_ref, wc_ref, wgl_ref,
                   wco_ref, cw_ref, bg_ref,
                   qT_ref, k_ref, vT_ref, km_ref, mc_ref, ga_ref, ubuf):
    tm = x_ref.shape[1]
    cwid = wco_ref.shape[0]
    d = x_ref.shape[2]
    s = pl.program_id(1)
    hb = _ada_norm(x_ref[0], g_ref[...], sc_ref[0], sh_ref[0]).astype(bf16)

    nt = (((1,), (1,)), ((), ()))
    qT_ref[0] = lax.dot_general(wqT_ref[...], hb, nt, preferred_element_type=f32).astype(bf16)
    k = jnp.dot(hb, wk_ref[...], preferred_element_type=f32)
    vT = lax.dot_general(wvT_ref[...], hb, nt, preferred_element_type=f32).astype(bf16)
    for j in range(tm // MOBA_BLOCK):
        rows = slice(j * MOBA_BLOCK, (j + 1) * MOBA_BLOCK)
        k_ref[0, j] = k[rows].astype(bf16)
        vT_ref[0, j] = vT[:, rows]
        km_ref[0, 0, j:j + 1, :] = jnp.mean(k[rows], axis=0, keepdims=True)

    cbx = jnp.dot(hb, wc_ref[...], preferred_element_type=f32)
    u = cbx[:, :cwid] * cbx[:, 2 * cwid:]

    halo = SUBLANES

    @pl.when(s == 0)
    def _():
        ubuf[0:halo, :] = jnp.zeros((halo, cwid), f32)

    ubuf[halo:halo + tm, :] = u
    cw = cw_ref[...]
    taps = [ubuf[halo - back:halo - back + tm, :] for back in range(CONV_K - 1, 0, -1)] + [u]
    conv = cw[0:1] * taps[0]
    for j in range(1, CONV_K):
        conv = conv + cw[j:j + 1] * taps[j]
    ubuf[0:halo, :] = ubuf[tm:tm + halo, :]
    cv = (cbx[:, cwid:2 * cwid] * conv).astype(bf16)
    y_conv = jnp.dot(cv, wco_ref[...], preferred_element_type=f32)

    gl = jnp.dot(hb, wgl_ref[...], preferred_element_type=f32) + bg_ref[...]
    ga_ref[0] = jax.nn.sigmoid(gl[:, :d]).astype(bf16)
    mc_ref[0] = (jax.nn.sigmoid(gl[:, d:]) * y_conv).astype(bf16)


def _inproj(x, g, sc, sh, w_in, b_gate, conv_w, w_conv_out):
    B, S, D = x.shape
    tm = ROW_TILE
    nb = S // MOBA_BLOCK
    bpt = tm // MOBA_BLOCK
    aw = ATTN_WIDTH
    cwid = conv_w.shape[1]
    scale = HEAD_DIM ** -0.5 * LOG2E
    wqT = (w_in[:, :aw] * scale).T.astype(bf16)
    wk = w_in[:, aw:2 * aw].astype(bf16)
    wvT = w_in[:, 2 * aw:3 * aw].T.astype(bf16)
    wc = w_in[:, 3 * aw:3 * aw + 3 * cwid].astype(bf16)
    wgl = w_in[:, 3 * aw + 3 * cwid:].astype(bf16)
    wco = w_conv_out.astype(bf16)

    const = lambda shape: pl.BlockSpec(shape, lambda b, s: (0,) * len(shape))
    vec = pl.BlockSpec((1, 1, D), lambda b, s: (b, 0, 0))
    row = lambda width: pl.BlockSpec((1, tm, width), lambda b, s: (b, s, 0))
    outs = pl.pallas_call(
        _inproj_kernel,
        grid=(B, S // tm),
        in_specs=[row(D), const((1, D)), vec, vec,
                  const((aw, D)), const((D, aw)), const((aw, D)), const((D, 3 * cwid)),
                  const((D, 2 * D)), const((cwid, D)), const((CONV_K, cwid)), const((1, 2 * D))],
        out_specs=(pl.BlockSpec((1, aw, tm), lambda b, s: (b, 0, s)),
                   pl.BlockSpec((1, bpt, MOBA_BLOCK, aw), lambda b, s: (b, s, 0, 0)),
                   pl.BlockSpec((1, bpt, aw, MOBA_BLOCK), lambda b, s: (b, s, 0, 0)),
                   pl.BlockSpec((1, 1, bpt, aw), lambda b, s: (b, s, 0, 0)),
                   row(D), row(D)),
        out_shape=(jax.ShapeDtypeStruct((B, aw, S), bf16),
                   jax.ShapeDtypeStruct((B, nb, MOBA_BLOCK, aw), bf16),
                   jax.ShapeDtypeStruct((B, nb, aw, MOBA_BLOCK), bf16),
                   jax.ShapeDtypeStruct((B, S // tm, bpt, aw), f32),
                   jax.ShapeDtypeStruct((B, S, D), bf16),
                   jax.ShapeDtypeStruct((B, S, D), bf16)),
        scratch_shapes=[pltpu.VMEM((tm + SUBLANES, cwid), f32)],
        compiler_params=_params(("parallel", "arbitrary")),
        name="inproj",
    )(x, g.reshape(1, D), sc.reshape(B, 1, D), sh.reshape(B, 1, D),
      wqT, wk, wvT, wc, wgl, wco, conv_w, b_gate.reshape(1, 2 * D))
    qT, k4, vT4, km, mc, ga = outs
    return qT, k4, vT4, km.reshape(B, nb, aw), mc, ga


def _attn_kernel(qT_ref, k_ref, vT_ref, km_ref, own_ref, adj_ref, *side_refs):
    n_side = len(side_refs) // 2
    o_ref = side_refs[n_side]
    for src, dst in zip(side_refs[:n_side], side_refs[n_side + 1:]):
        dst[...] = src[...].astype(bf16)

    i = pl.program_id(1)
    bs = MOBA_BLOCK
    nb = km_ref.shape[1]
    n_heads = qT_ref.shape[1] // HEAD_DIM
    heads = range(n_heads)
    blk = lax.broadcasted_iota(jnp.int32, (nb, bs), 0)

    qcat = []
    for h in heads:
        pair, a = divmod(h, LANES // HEAD_DIM)
        qT = qT_ref[0, pair * LANES:(pair + 1) * LANES, :]
        row = lax.broadcasted_iota(jnp.int32, qT.shape, 0)
        qa = jnp.where((row >= a * HEAD_DIM) & (row < (a + 1) * HEAD_DIM), qT, jnp.zeros_like(qT))
        km = km_ref[0, :, pair * LANES:(pair + 1) * LANES]
        km_hi = km.astype(bf16)
        km_mid = (km - km_hi.astype(f32)).astype(bf16)
        km_lo = (km - km_hi.astype(f32) - km_mid.astype(f32)).astype(bf16)
        g = (jnp.dot(km_hi, qa, preferred_element_type=f32)
             + (jnp.dot(km_mid, qa, preferred_element_type=f32) + jnp.dot(km_lo, qa, preferred_element_type=f32)))
        g = jnp.where(blk < i, g, NEG_INF)
        fq = jnp.full((nb, bs), NEG_INF, f32)
        for _ in range(MOBA_TOPK):
            top = jnp.max(g, axis=0, keepdims=True)
            first = jnp.min(jnp.where(g == top, blk, nb), axis=0, keepdims=True)
            pick = blk == first
            fq = jnp.where(pick, 0.0, fq)
            g = jnp.where(pick, -jnp.inf, g)
        fq = jnp.where(blk < i, fq, 0.0).astype(bf16)
        pad = jnp.zeros((LANES - nb, bs), bf16)
        qcat.append(jnp.concatenate([qa, fq, pad], axis=0))

    lane = lax.broadcasted_iota(jnp.int32, (bs, LANES), 1)
    causal = (lax.broadcasted_iota(jnp.int32, (bs, bs), 0) <= lax.broadcasted_iota(jnp.int32, (bs, bs), 1))
    ones_rows = jnp.ones((SUM_ROWS, bs), bf16)

    def qk(blocks, h):
        pair = h // (LANES // HEAD_DIM)
        cols = slice(pair * LANES, (pair + 1) * LANES)
        return [jnp.dot(jnp.concatenate([k_ref[0, n, :, cols], jnp.where(lane == n, 1.0, 0.0).astype(bf16)], axis=1),
                        qcat[h], preferred_element_type=f32) for n, _ in blocks]

    def step(blocks, state, scores=None, next_blocks=None):
        if scores is None:
            scores = [qk(blocks, h) for h in heads]
        next_scores = []
        out = []
        for h in heads:
            sTs = []
            for (n, kind), sT in zip(blocks, scores[h]):
                if kind == "own":
                    sT = jnp.where(causal, own_ref[h] + sT, NEG_INF)
                elif kind == "adj":
                    sT = adj_ref[h] + sT
                sTs.append(sT)
            m_new = None if state is None else state[h][0]
            for sT in sTs:
                top = jnp.max(sT, axis=0, keepdims=True)
                m_new = top if m_new is None else jnp.maximum(m_new, top)
            acc = None if state is None else jnp.exp2(state[h][0] - m_new) * state[h][1]
            for (n, _), sT in zip(blocks, sTs):
                p = jnp.exp2(sT - m_new).astype(bf16)
                v_ext = jnp.concatenate([vT_ref[0, n, h * HEAD_DIM:(h + 1) * HEAD_DIM, :], ones_rows], axis=0)
                pv = jnp.dot(v_ext, p, preferred_element_type=f32)
                acc = pv if acc is None else acc + pv
            out.append((m_new, acc))
            if next_blocks is not None:
                next_scores.append(qk(next_blocks, h))
        return tuple(out), next_scores

    per = FAR_BLOCKS_PER_ITER

    def chained(first, n_sub, st, scores=None):
        blocks_of = lambda u: [(first + u * per + j, "far") for j in range(per)]
        for u in range(n_sub):
            st, scores = step(blocks_of(u), st, scores, blocks_of(u + 1) if u + 1 < n_sub else None)
        return st

    def opening_with_far():
        st, scores = step([(i, "own"), (i - 1, "adj")], None, None, [(j, "far") for j in range(per)])
        return chained(0, FAR_SUBSTEPS - 1, st, scores)

    span = per * FAR_SUBSTEPS
    joined = i - 1 >= span - per
    state = lax.cond(joined, opening_with_far,
                     lambda: lax.cond(i >= 1,
                                      lambda: step([(i, "own"), (i - 1, "adj")], None)[0],
                                      lambda: step([(i, "own")], None)[0]))
    base = jnp.where(joined, span - per, 0)
    n_far = jnp.maximum(i - 1, 0) - base
    state = lax.fori_loop(0, n_far // span, lambda it, st: chained(base + it * span, FAR_SUBSTEPS, st), state)
    done = base + n_far - n_far % span
    size = span // 2
    while size >= 1:
        take = (n_far % span) & size
        if size >= per:
            region = lambda st, d=done, sz=size: chained(d, sz // per, st)
        else:
            region = lambda st, d=done, sz=size: step([(d + j, "far") for j in range(sz)], st)[0]
        state = lax.cond(take > 0, region, lambda st: st, state)
        done = done + take
        size //= 2
    outT = jnp.concatenate([acc[:HEAD_DIM] / acc[HEAD_DIM:HEAD_DIM + 1] for (_, acc) in state], axis=0)
    o_ref[0] = outT.T.astype(bf16)


def _attention(qT, k4, vT4, km, own, adj, side_weights):
    B, aw, S = qT.shape
    nb = S // MOBA_BLOCK
    assert MOBA_TOPK < nb <= LANES
    n_heads = aw // HEAD_DIM
    steps = B * nb
    whole = lambda shape: pl.BlockSpec(shape, lambda b, i: (b,) + (0,) * (len(shape) - 1))
    table = pl.BlockSpec((n_heads, MOBA_BLOCK, MOBA_BLOCK), lambda b, i: (0, 0, 0))
    slabs = [w.reshape(steps, -1, w.shape[-1]) for w in side_weights]
    slab_specs = [pl.BlockSpec((1,) + w.shape[1:], lambda b, i: (b * nb + i, 0, 0)) for w in slabs]
    outs = pl.pallas_call(
        _attn_kernel,
        grid=(B, nb),
        in_specs=[pl.BlockSpec((1, aw, MOBA_BLOCK), lambda b, i: (b, 0, i)),
                  whole((1, nb, MOBA_BLOCK, aw)), whole((1, nb, aw, MOBA_BLOCK)), whole((1, nb, aw)),
                  table, table] + slab_specs,
        out_specs=[pl.BlockSpec((1, MOBA_BLOCK, aw), lambda b, i: (b, i, 0))] + slab_specs,
        out_shape=[jax.ShapeDtypeStruct((B, S, aw), bf16)]
                  + [jax.ShapeDtypeStruct(w.shape, bf16) for w in slabs],
        compiler_params=_params(("parallel", "arbitrary")),
        name="attn",
    )(qT, k4, vT4, km, own, adj, *slabs)
    return outs[0], [o.reshape(w.shape) for o, w in zip(outs[1:], side_weights)]


def _post_kernel(a_ref, mc_ref, ga_ref, x_ref, g1_ref, sc_ref, sh_ref, ng_ref,
                 wao_ref, wo_ref, wr_ref, br_ref, x1_ref, h2p_ref, route_ref):
    tm = x_ref.shape[1]
    d = x_ref.shape[2]
    sub = tm // POST_SPLIT
    lane = lax.broadcasted_iota(jnp.int32, (sub, LANES), 1)
    ninf = -jnp.inf
    for t in range(POST_SPLIT):
        rows = slice(t * sub, (t + 1) * sub)
        ya = jnp.dot(a_ref[0, rows, :], wao_ref[...], preferred_element_type=f32)
        merged = ga_ref[0, rows, :].astype(f32) * ya + mc_ref[0, rows, :].astype(f32)
        z = jnp.dot(merged.astype(bf16), wo_ref[...], preferred_element_type=f32)
        x1 = x_ref[0, rows, :] + g1_ref[0] * z
        x1_ref[0, rows, :] = x1
        h2 = _ada_norm(x1, ng_ref[...], sc_ref[0], sh_ref[0])
        h_hi = h2.astype(bf16)
        h_hi32 = h_hi.astype(f32)

        bits = pltpu.bitcast(h_hi32, jnp.uint32)
        packed = (bits[:, :d // 2] >> 16) | (bits[:, d // 2:] & jnp.uint32(0xFFFF0000))
        chunks = packed.shape[1] // LANES
        for j in range(chunks):
            h2p_ref[pl.ds(t * sub * chunks + j, sub, stride=chunks), :] = packed[:, j * LANES:(j + 1) * LANES]

        h_lo = (h2 - h_hi32).astype(bf16)
        hi_both = jnp.dot(h_hi, wr_ref[...], preferred_element_type=f32)
        logits = (hi_both[:, :LANES]
                  + (jnp.dot(h_lo, wr_ref[:, :LANES], preferred_element_type=f32) + hi_both[:, LANES:])) + br_ref[...]
        gl = jnp.where(lane < N_GROUPS, logits, ninf)
        gmax = jnp.max(gl, axis=1, keepdims=True)
        g_sel = jnp.min(jnp.where(gl == gmax, lane, LANES), axis=1, keepdims=True)
        g_w = 1.0 / jnp.sum(jnp.exp(gl - gmax), axis=1, keepdims=True)
        off = lane - (N_GROUPS + EXPERTS_PER_GROUP * g_sel)
        in_group = (off >= 0) & (off < EXPERTS_PER_GROUP)
        el = jnp.where(in_group, logits, ninf)
        ex = jnp.exp(el - jnp.max(el, axis=1, keepdims=True))
        ep = jnp.where(in_group, ex / jnp.sum(ex, axis=1, keepdims=True), -1.0)
        p1 = jnp.max(ep, axis=1, keepdims=True)
        i1 = jnp.min(jnp.where(ep == p1, lane, LANES), axis=1, keepdims=True)
        ep2 = jnp.where(lane == i1, -1.0, ep)
        p2 = jnp.max(ep2, axis=1, keepdims=True)
        i2 = jnp.min(jnp.where(ep2 == p2, lane, LANES), axis=1, keepdims=True)
        tot = p1 + p2
        w1 = p1 / tot * g_w
        w2 = p2 / tot * g_w
        e1 = (i1 - N_GROUPS).astype(f32)
        e2 = (i2 - N_GROUPS).astype(f32)
        route = jnp.where(lane == 0, e1, jnp.where(lane == 1, e2,
                          jnp.where(lane == 2, w1, jnp.where(lane == 3, w2, 0.0))))
        route_ref[:, rows] = route.T[:SUBLANES, :]


def _post(a, mc, ga, x, g1, sc2, sh2, ng, w_attn_out, w_o, w_rg, b_rg, w_re, b_re):
    B, S, D = x.shape
    tm = ROW_TILE
    aw = a.shape[2]
    nt = S // tm
    wr = jnp.zeros((D, LANES), f32).at[:, :N_GROUPS].set(w_rg).at[:, N_GROUPS:N_GROUPS + N_EXPERTS].set(w_re)
    br = jnp.zeros((1, LANES), f32).at[0, :N_GROUPS].set(b_rg).at[0, N_GROUPS:N_GROUPS + N_EXPERTS].set(b_re)
    wr_hi = wr.astype(bf16)
    wr_split = jnp.concatenate([wr_hi, (wr - wr_hi.astype(f32)).astype(bf16)], axis=1)
    const = lambda shape: pl.BlockSpec(shape, lambda b, s: (0,) * len(shape))
    vec = pl.BlockSpec((1, 1, D), lambda b, s: (b, 0, 0))
    row = lambda width: pl.BlockSpec((1, tm, width), lambda b, s: (b, s, 0))
    flat = lambda width, rows_per_token=1: pl.BlockSpec((tm * rows_per_token, width), lambda b, s: (b * nt + s, 0))
    words = D // 2 // LANES
    return pl.pallas_call(
        _post_kernel,
        grid=(B, nt),
        in_specs=[row(aw), row(D), row(D), row(D), vec, vec, vec, const((1, D)),
                  const((aw, D)), const((D, D)), const((D, 2 * LANES)), const((1, LANES))],
        out_specs=(row(D), flat(LANES, words), pl.BlockSpec((SUBLANES, tm), lambda b, s: (0, b * nt + s))),
        out_shape=(jax.ShapeDtypeStruct((B, S, D), f32),
                   jax.ShapeDtypeStruct((B * S * words, LANES), jnp.uint32),
                   jax.ShapeDtypeStruct((SUBLANES, B * S), f32)),
        compiler_params=_params(("parallel", "parallel")),
        name="post",
    )(a, mc, ga, x, g1.reshape(B, 1, D), sc2.reshape(B, 1, D), sh2.reshape(B, 1, D), ng.reshape(1, D),
      w_attn_out.astype(bf16), w_o.astype(bf16), wr_split, br)


def _dispatch_tables(route, n_chunks):
    T = route.shape[1]
    tc = T // n_chunks
    na = tc * TOP_K_EXPERTS
    per_chunk = lambda rows: rows.reshape(TOP_K_EXPERTS, n_chunks, tc).transpose(1, 0, 2).reshape(n_chunks, na)
    e = per_chunk(route[:TOP_K_EXPERTS]).astype(jnp.int32)
    w = per_chunk(route[TOP_K_EXPERTS:2 * TOP_K_EXPERTS])
    tok = jnp.broadcast_to(jnp.arange(na, dtype=jnp.int32) % tc, e.shape)
    _, stok, sw = lax.sort((e, tok, w), dimension=1, num_keys=1, is_stable=True)
    counts = jnp.sum((e[:, None, :] == jnp.arange(N_EXPERTS)[None, :, None]).astype(jnp.int32), axis=2)
    starts = jnp.cumsum(counts, axis=1) - counts
    flat = lambda t: t.reshape(-1).astype(jnp.int32)
    return flat(starts), flat(counts), stok.reshape(n_chunks, 1, na), sw.reshape(n_chunks, 1, na)


def _moe_kernel(start_ref, count_ref, stok_ref, sw_ref, h2p_ref, wg_ref, wu_ref, wd_ref,
                x1_ref, g2_ref, fg_ref, o_ref, yacc, xs, outs, *, normalize):
    c = pl.program_id(0)
    b = pl.program_id(1)
    block_rows = outs.shape[0] // SUBLANES
    words = xs.shape[0] // block_rows
    half = words * LANES
    run = c * N_EXPERTS + jnp.minimum(b, N_EXPERTS - 1)
    count = jnp.where(b < N_EXPERTS, count_ref[run], 0)
    first = start_ref[run]

    @pl.when(b == 0)
    def _():
        yacc[...] = jnp.zeros(yacc.shape, f32)

    @pl.when((b == 0) & (c == 0))
    def _():
        xs[...] = jnp.zeros(xs.shape, jnp.uint32)

    def token_rows(index, rows_per_token):
        return pl.ds(pl.multiple_of(index * rows_per_token, rows_per_token), rows_per_token)

    def block(blk, carry):
        st = first + blk * block_rows
        n = jnp.minimum(count - blk * block_rows, block_rows)

        def for_rows(rows_fn):
            full = n // ROW_UNROLL

            def group(g, carry):
                rows_fn([g * ROW_UNROLL + j for j in range(ROW_UNROLL)], st + g * ROW_UNROLL)
                return carry

            def single(j, carry):
                rows_fn([full * ROW_UNROLL + j], st + full * ROW_UNROLL + j)
                return carry

            lax.fori_loop(0, full, group, 0)
            lax.fori_loop(0, n - full * ROW_UNROLL, single, 0)

        def gather(rs, i0):
            for k, r in enumerate(rs):
                xs[token_rows(r, words), :] = h2p_ref[token_rows(stok_ref[0, 0, i0 + k], words), :]

        for_rows(gather)
        lo, hi = [], []
        for j in range(words):
            packed = xs[pl.ds(j, block_rows, stride=words), :]
            lo.append(pltpu.bitcast(packed << 16, f32).astype(bf16))
            hi.append(pltpu.bitcast(packed & jnp.uint32(0xFFFF0000), f32).astype(bf16))
        x_lo = jnp.concatenate(lo, axis=1)
        x_hi = jnp.concatenate(hi, axis=1)

        def up(w_ref):
            return (jnp.dot(x_lo, w_ref[0, :half, :], preferred_element_type=f32)
                    + jnp.dot(x_hi, w_ref[0, half:, :], preferred_element_type=f32))

        hidden = (jax.nn.silu(up(wg_ref)) * up(wu_ref)).astype(bf16)
        out = jnp.dot(hidden, wd_ref[0], preferred_element_type=f32)
        for j in range(SUBLANES):
            outs[pl.ds(j, block_rows, stride=SUBLANES), :] = out[:, j * LANES:(j + 1) * LANES]

        def scatter(rs, i0):
            toks = [stok_ref[0, 0, i0 + k] for k in range(len(rs))]
            acc = [yacc[token_rows(tok, SUBLANES), :] for tok in toks]
            for k, (r, tok, a) in enumerate(zip(rs, toks, acc)):
                yacc[token_rows(tok, SUBLANES), :] = a + sw_ref[0, 0, i0 + k] * outs[token_rows(r, SUBLANES), :]

        for_rows(scatter)
        return carry

    lax.fori_loop(0, (count + block_rows - 1) // block_rows, block, 0)

    @pl.when(b >= N_EXPERTS)
    def _():
        rows = o_ref.shape[0]
        start = pl.multiple_of((b - N_EXPERTS) * rows * SUBLANES, rows * SUBLANES)
        y = jnp.concatenate([yacc[pl.ds(start + j, rows, stride=SUBLANES), :] for j in range(SUBLANES)], axis=1)
        x2 = x1_ref[...] + g2_ref[0] * y
        if normalize:
            x2 = (x2 * lax.rsqrt(jnp.mean(x2 * x2, axis=-1, keepdims=True) + NORM_EPS)) * fg_ref[...]
        o_ref[...] = x2


def _moe(route, h2p, w_gate, w_up, w_down, x1, g2, fg, normalize):
    B, S, D = x1.shape
    T = B * S
    words = h2p.shape[0] // T
    assert D == words * 2 * LANES == SUBLANES * LANES
    de = w_gate.shape[2]
    n_chunks = MOE_CHUNKS
    tc = T // n_chunks
    fin_rows = ROW_TILE // 2
    n_fin = tc // fin_rows
    assert n_chunks % B == 0 and tc % fin_rows == 0 and w_gate.shape[0] == N_EXPERTS
    starts, counts, stok, sw = _dispatch_tables(route, n_chunks)
    na = stok.shape[2]
    wmap = lambda c, b, *_: (jnp.minimum(b, N_EXPERTS - 1), 0, 0)
    smem = lambda: pl.BlockSpec((1, 1, na), lambda c, b, *_: (c, 0, 0), memory_space=pltpu.SMEM)
    tile = pl.BlockSpec((fin_rows, D), lambda c, b, *_: (c * n_fin + jnp.maximum(b - N_EXPERTS, 0), 0))
    grid_spec = pltpu.PrefetchScalarGridSpec(
        num_scalar_prefetch=2,
        grid=(n_chunks, N_EXPERTS + n_fin),
        in_specs=[smem(), smem(),
                  pl.BlockSpec((tc * words, LANES), lambda c, b, *_: (c, 0)),
                  pl.BlockSpec((1, D, de), wmap),
                  pl.BlockSpec((1, D, de), wmap),
                  pl.BlockSpec((1, de, D), wmap),
                  tile,
                  pl.BlockSpec((1, 1, D), lambda c, b, *_: (c // (n_chunks // B), 0, 0)),
                  pl.BlockSpec((1, D), lambda c, b, *_: (0, 0))],
        out_specs=tile,
        scratch_shapes=[pltpu.VMEM((tc * SUBLANES, LANES), f32),
                        pltpu.VMEM((EXPERT_ROWS * words, LANES), jnp.uint32),
                        pltpu.VMEM((EXPERT_ROWS * SUBLANES, LANES), f32)],
    )
    out = pl.pallas_call(
        functools.partial(_moe_kernel, normalize=normalize),
        grid_spec=grid_spec,
        out_shape=jax.ShapeDtypeStruct((T, D), f32),
        compiler_params=_params(("arbitrary", "arbitrary")),
        name="moe",
    )(starts, counts, stok, sw, h2p, w_gate, w_up, w_down,
      x1.reshape(T, D), g2.reshape(B, 1, D), fg.reshape(1, D))
    return out.reshape(B, S, D)


def kernel(x, c, rel_bias, norm1_g, norm2_g, w_ada, b_ada, w_in, b_gate, conv_w, w_attn_out,
           w_conv_out, w_o, w_router_group, b_router_group, w_router_expert, b_router_expert,
           w_exp_gate, w_exp_up, w_exp_down, final_norm_g):
    B, S, D = x.shape
    depth = w_ada.shape[0]
    assert S % ROW_TILE == 0 and ROW_TILE % MOBA_BLOCK == 0
    own, adj = _bias_tiles(rel_bias)
    for l in range(depth):
        mod = _mod(c, w_ada[l], b_ada[l])
        sh1, sc1, g1, sh2, sc2, g2 = jnp.split(mod, 6, axis=-1)
        qT, k4, vT4, km, mc, ga = _inproj(x, norm1_g[l], sc1, sh1, w_in[l], b_gate[l], conv_w[l],
                                          w_conv_out[l])
        a, expert_w = _attention(qT, k4, vT4, km, own, adj, [w_exp_gate[l], w_exp_up[l], w_exp_down[l]])
        x1, h2p, route = _post(a, mc, ga, x, g1, sc2, sh2, norm2_g[l], w_attn_out[l], w_o[l],
                               w_router_group[l], b_router_group[l], w_router_expert[l],
                               b_router_expert[l])
        x = _moe(route, h2p, *expert_w, x1, g2, final_norm_g, normalize=(l + 1 == depth))
    return x
```

```python
import functools
import math

import numpy as np
import jax
import jax.numpy as jnp
from jax import lax
from jax.experimental import pallas as pl
from jax.experimental.pallas import tpu as pltpu

f32 = jnp.float32
bf16 = jnp.bfloat16

N_HEADS = 8
HEAD_DIM = 64
ATTN_WIDTH = N_HEADS * HEAD_DIM
MOBA_BLOCK = 256
MOBA_TOPK = 3
CONV_K = 3
N_BUCKETS = 32
MAX_EXACT = N_BUCKETS // 2
MAX_DISTANCE = 128
N_GROUPS = 4
EXPERTS_PER_GROUP = 8
N_EXPERTS = N_GROUPS * EXPERTS_PER_GROUP
TOP_K_EXPERTS = 2
NORM_EPS = 1e-6
NEG_INF = -1e30

LANES = 128
SUBLANES = 8
MOD_COLS = 1024
ROW_TILE = 1024
POST_SPLIT = 2
EXPERT_ROWS = 288
ROW_UNROLL = 8
MOE_CHUNKS = 4
FAR_BLOCKS_PER_ITER = 2
FAR_SUBSTEPS = 4
SUM_ROWS = 16
LOG2E = 1.4426950408889634
VMEM_LIMIT = 56 * 1024 * 1024


def _params(sem, vmem=VMEM_LIMIT):
    return pltpu.CompilerParams(dimension_semantics=sem, vmem_limit_bytes=vmem)


def _mod_kernel(c_ref, w_ref, b_ref, o_ref, *, batch):
    act = jax.nn.silu(c_ref[...]).T
    w = w_ref[...]
    rows = [jnp.sum(act[:, r:r + 1] * w, axis=0, keepdims=True) for r in range(batch)]
    rows.append(jnp.zeros((o_ref.shape[0] - batch, w.shape[1]), f32))
    o_ref[...] = jnp.concatenate(rows, axis=0) + b_ref[...]


def _mod(c, w_ada, b_ada):
    B, D = c.shape
    n_out = w_ada.shape[1]
    rows = SUBLANES
    cp = jnp.zeros((rows, D), f32).at[:B].set(c)
    tn = MOD_COLS
    out = pl.pallas_call(
        functools.partial(_mod_kernel, batch=B),
        grid=(n_out // tn,),
        in_specs=[pl.BlockSpec((rows, D), lambda j: (0, 0)),
                  pl.BlockSpec((D, tn), lambda j: (0, j)),
                  pl.BlockSpec((1, tn), lambda j: (0, j))],
        out_specs=pl.BlockSpec((rows, tn), lambda j: (0, j)),
        out_shape=jax.ShapeDtypeStruct((rows, n_out), f32),
        compiler_params=_params(("arbitrary",)),
        name="mod",
    )(cp, w_ada, b_ada.reshape(1, n_out))
    return out[:B]


def _bucket_thresholds():
    n = np.arange(2 * MOBA_BLOCK)

    def buckets(ft):
        nf = np.maximum(n, 1).astype(ft)
        large = MAX_EXACT + (np.log(nf / ft(MAX_EXACT)) / ft(math.log(MAX_DISTANCE / MAX_EXACT))
                             * ft(N_BUCKETS - MAX_EXACT)).astype(np.int32)
        return np.where(n < MAX_EXACT, n, np.minimum(large, N_BUCKETS - 1))

    b = buckets(np.float32)
    assert np.array_equal(b, buckets(np.float64)) and np.all(np.diff(b) >= 0)
    assert b[-1] == N_BUCKETS - 1 and b[MOBA_BLOCK + 1] == N_BUCKETS - 1
    return [int(np.argmax(b >= j)) for j in range(N_BUCKETS)]


def _bias_kernel(rb_ref, own_ref, adj_ref, *, thresholds):
    h = pl.program_id(0)
    kk = lax.broadcasted_iota(jnp.int32, (MOBA_BLOCK, MOBA_BLOCK), 0)
    qq = lax.broadcasted_iota(jnp.int32, (MOBA_BLOCK, MOBA_BLOCK), 1)
    far = rb_ref[N_BUCKETS - 1, h]

    def table(dist):
        acc = jnp.full(dist.shape, rb_ref[0, h], f32)
        for j in range(1, N_BUCKETS):
            acc = jnp.where(dist >= thresholds[j], rb_ref[j, h], acc)
        return (acc - far) * LOG2E

    own_ref[0] = table(qq - kk)
    adj_ref[0] = table(qq - kk + MOBA_BLOCK)


def _bias_tiles(rel_bias):
    shp = jax.ShapeDtypeStruct((N_HEADS, MOBA_BLOCK, MOBA_BLOCK), f32)
    spec = pl.BlockSpec((1, MOBA_BLOCK, MOBA_BLOCK), lambda h: (h, 0, 0))
    return pl.pallas_call(
        functools.partial(_bias_kernel, thresholds=_bucket_thresholds()),
        grid=(N_HEADS,),
        in_specs=[pl.BlockSpec(memory_space=pltpu.SMEM)],
        out_specs=(spec, spec),
        out_shape=(shp, shp),
        compiler_params=_params(("arbitrary",)),
        name="bias",
    )(rel_bias)


def _ada_norm(x, g, scale, shift):
    y = x * lax.rsqrt(jnp.mean(x * x, axis=-1, keepdims=True) + NORM_EPS)
    return (y * g) * (1.0 + scale) + shift


def _inproj_kernel(x_ref, g_ref, sc_ref, sh_ref, wqT_ref, wk_ref, wvT_ref, wc_ref, wgl_ref,
                   wco_ref, cw_ref, bg_ref,
                   qT_ref, k_ref, vT_ref, km_ref, mc_ref, ga_ref, ubuf):
    tm = x_ref.shape[1]
    cwid = wco_ref.shape[0]
    d = x_ref.shape[2]
    s = pl.program_id(1)
    hb = _ada_norm(x_ref[0], g_ref[...], sc_ref[0], sh_ref[0]).astype(bf16)

    nt = (((1,), (1,)), ((), ()))
    qT_ref[0] = lax.dot_general(wqT_ref[...], hb, nt, preferred_element_type=f32).astype(bf16)
    k = jnp.dot(hb, wk_ref[...], preferred_element_type=f32)
    vT = lax.dot_general(wvT_ref[...], hb, nt, preferred_element_type=f32).astype(bf16)
    for j in range(tm // MOBA_BLOCK):
        rows = slice(j * MOBA_BLOCK, (j + 1) * MOBA_BLOCK)
        k_ref[0, j] = k[rows].astype(bf16)
        vT_ref[0, j] = vT[:, rows]
        km_ref[0, 0, j:j + 1, :] = jnp.mean(k[rows], axis=0, keepdims=True)

    cbx = jnp.dot(hb, wc_ref[...], preferred_element_type=f32)
    u = cbx[:, :cwid] * cbx[:, 2 * cwid:]

    halo = SUBLANES

    @pl.when(s == 0)
    def _():
        ubuf[0:halo, :] = jnp.zeros((halo, cwid), f32)

    ubuf[halo:halo + tm, :] = u
    cw = cw_ref[...]
    taps = [ubuf[halo - back:halo - back + tm, :] for back in range(CONV_K - 1, 0, -1)] + [u]
    conv = cw[0:1] * taps[0]
    for j in range(1, CONV_K):
        conv = conv + cw[j:j + 1] * taps[j]
    ubuf[0:halo, :] = ubuf[tm:tm + halo, :]
    cv = (cbx[:, cwid:2 * cwid] * conv).astype(bf16)
    y_conv = jnp.dot(cv, wco_ref[...], preferred_element_type=f32)

    gl = jnp.dot(hb, wgl_ref[...], preferred_element_type=f32) + bg_ref[...]
    ga_ref[0] = jax.nn.sigmoid(gl[:, :d]).astype(bf16)
    mc_ref[0] = (jax.nn.sigmoid(gl[:, d:]) * y_conv).astype(bf16)


def _inproj(x, g, sc, sh, w_in, b_gate, conv_w, w_conv_out):
    B, S, D = x.shape
    tm = ROW_TILE
    nb = S // MOBA_BLOCK
    bpt = tm // MOBA_BLOCK
    aw = ATTN_WIDTH
    cwid = conv_w.shape[1]
    scale = HEAD_DIM ** -0.5 * LOG2E
    wqT = (w_in[:, :aw] * scale).T.astype(bf16)
    wk = w_in[:, aw:2 * aw].astype(bf16)
    wvT = w_in[:, 2 * aw:3 * aw].T.astype(bf16)
    wc = w_in[:, 3 * aw:3 * aw + 3 * cwid].astype(bf16)
    wgl = w_in[:, 3 * aw + 3 * cwid:].astype(bf16)
    wco = w_conv_out.astype(bf16)

    const = lambda shape: pl.BlockSpec(shape, lambda b, s: (0,) * len(shape))
    vec = pl.BlockSpec((1, 1, D), lambda b, s: (b, 0, 0))
    row = lambda width: pl.BlockSpec((1, tm, width), lambda b, s: (b, s, 0))
    outs = pl.pallas_call(
        _inproj_kernel,
        grid=(B, S // tm),
        in_specs=[row(D), const((1, D)), vec, vec,
                  const((aw, D)), const((D, aw)), const((aw, D)), const((D, 3 * cwid)),
                  const((D, 2 * D)), const((cwid, D)), const((CONV_K, cwid)), const((1, 2 * D))],
        out_specs=(pl.BlockSpec((1, aw, tm), lambda b, s: (b, 0, s)),
                   pl.BlockSpec((1, bpt, MOBA_BLOCK, aw), lambda b, s: (b, s, 0, 0)),
                   pl.BlockSpec((1, bpt, aw, MOBA_BLOCK), lambda b, s: (b, s, 0, 0)),
                   pl.BlockSpec((1, 1, bpt, aw), lambda b, s: (b, s, 0, 0)),
                   row(D), row(D)),
        out_shape=(jax.ShapeDtypeStruct((B, aw, S), bf16),
                   jax.ShapeDtypeStruct((B, nb, MOBA_BLOCK, aw), bf16),
                   jax.ShapeDtypeStruct((B, nb, aw, MOBA_BLOCK), bf16),
                   jax.ShapeDtypeStruct((B, S // tm, bpt, aw), f32),
                   jax.ShapeDtypeStruct((B, S, D), bf16),
                   jax.ShapeDtypeStruct((B, S, D), bf16)),
        scratch_shapes=[pltpu.VMEM((tm + SUBLANES, cwid), f32)],
        compiler_params=_params(("parallel", "arbitrary")),
        name="inproj",
    )(x, g.reshape(1, D), sc.reshape(B, 1, D), sh.reshape(B, 1, D),
      wqT, wk, wvT, wc, wgl, wco, conv_w, b_gate.reshape(1, 2 * D))
    qT, k4, vT4, km, mc, ga = outs
    return qT, k4, vT4, km.reshape(B, nb, aw), mc, ga


def _attn_kernel(qT_ref, k_ref, vT_ref, km_ref, own_ref, adj_ref, *side_refs):
    n_side = len(side_refs) // 2
    o_ref = side_refs[n_side]
    for src, dst in zip(side_refs[:n_side], side_refs[n_side + 1:]):
        dst[...] = src[...].astype(bf16)

    i = pl.program_id(1)
    bs = MOBA_BLOCK
    nb = km_ref.shape[1]
    n_heads = qT_ref.shape[1] // HEAD_DIM
    heads = range(n_heads)
    blk = lax.broadcasted_iota(jnp.int32, (nb, bs), 0)

    qcat = []
    for h in heads:
        pair, a = divmod(h, LANES // HEAD_DIM)
        qT = qT_ref[0, pair * LANES:(pair + 1) * LANES, :]
        row = lax.broadcasted_iota(jnp.int32, qT.shape, 0)
        qa = jnp.where((row >= a * HEAD_DIM) & (row < (a + 1) * HEAD_DIM), qT, jnp.zeros_like(qT))
        km = km_ref[0, :, pair * LANES:(pair + 1) * LANES]
        km_hi = km.astype(bf16)
        km_mid = (km - km_hi.astype(f32)).astype(bf16)
        km_lo = (km - km_hi.astype(f32) - km_mid.astype(f32)).astype(bf16)
        g = (jnp.dot(km_hi, qa, preferred_element_type=f32)
             + (jnp.dot(km_mid, qa, preferred_element_type=f32) + jnp.dot(km_lo, qa, preferred_element_type=f32)))
        g = jnp.where(blk < i, g, NEG_INF)
        fq = jnp.full((nb, bs), NEG_INF, f32)
        for _ in range(MOBA_TOPK):
            top = jnp.max(g, axis=0, keepdims=True)
            first = jnp.min(jnp.where(g == top, blk, nb), axis=0, keepdims=True)
            pick = blk == first
            fq = jnp.where(pick, 0.0, fq)
            g = jnp.where(pick, -jnp.inf, g)
        fq = jnp.where(blk < i, fq, 0.0).astype(bf16)
        pad = jnp.zeros((LANES - nb, bs), bf16)
        qcat.append(jnp.concatenate([qa, fq, pad], axis=0))

    lane = lax.broadcasted_iota(jnp.int32, (bs, LANES), 1)
    causal = (lax.broadcasted_iota(jnp.int32, (bs, bs), 0) <= lax.broadcasted_iota(jnp.int32, (bs, bs), 1))
    ones_rows = jnp.ones((SUM_ROWS, bs), bf16)

    def qk(blocks, h):
        pair = h // (LANES // HEAD_DIM)
        cols = slice(pair * LANES, (pair + 1) * LANES)
        return [jnp.dot(jnp.concatenate([k_ref[0, n, :, cols], jnp.where(lane == n, 1.0, 0.0).astype(bf16)], axis=1),
                        qcat[h], preferred_element_type=f32) for n, _ in blocks]

    def step(blocks, state, scores=None, next_blocks=None):
        if scores is None:
            scores = [qk(blocks, h) for h in heads]
        next_scores = []
        out = []
        for h in heads:
            sTs = []
            for (n, kind), sT in zip(blocks, scores[h]):
                if kind == "own":
                    sT = jnp.where(causal, own_ref[h] + sT, NEG_INF)
                elif kind == "adj":
                    sT = adj_ref[h] + sT
                sTs.append(sT)
            m_new = None if state is None else state[h][0]
            for sT in sTs:
                top = jnp.max(sT, axis=0, keepdims=True)
                m_new = top if m_new is None else jnp.maximum(m_new, top)
            acc = None if state is None else jnp.exp2(state[h][0] - m_new) * state[h][1]
            for (n, _), sT in zip(blocks, sTs):
                p = jnp.exp2(sT - m_new).astype(bf16)
                v_ext = jnp.concatenate([vT_ref[0, n, h * HEAD_DIM:(h + 1) * HEAD_DIM, :], ones_rows], axis=0)
                pv = jnp.dot(v_ext, p, preferred_element_type=f32)
                acc = pv if acc is None else acc + pv
            out.append((m_new, acc))
            if next_blocks is not None:
                next_scores.append(qk(next_blocks, h))
        return tuple(out), next_scores

    per = FAR_BLOCKS_PER_ITER

    def chained(first, n_sub, st, scores=None):
        blocks_of = lambda u: [(first + u * per + j, "far") for j in range(per)]
        for u in range(n_sub):
            st, scores = step(blocks_of(u), st, scores, blocks_of(u + 1) if u + 1 < n_sub else None)
        return st

    def opening_with_far():
        st, scores = step([(i, "own"), (i - 1, "adj")], None, None, [(j, "far") for j in range(per)])
        return chained(0, FAR_SUBSTEPS - 1, st, scores)

    span = per * FAR_SUBSTEPS
    joined = i - 1 >= span - per
    state = lax.cond(joined, opening_with_far,
                     lambda: lax.cond(i >= 1,
                                      lambda: step([(i, "own"), (i - 1, "adj")], None)[0],
                                      lambda: step([(i, "own")], None)[0]))
    base = jnp.where(joined, span - per, 0)
    n_far = jnp.maximum(i - 1, 0) - base
    state = lax.fori_loop(0, n_far // span, lambda it, st: chained(base + it * span, FAR_SUBSTEPS, st), state)
    done = base + n_far - n_far % span
    size = span // 2
    while size >= 1:
        take = (n_far % span) & size
        if size >= per:
            region = lambda st, d=done, sz=size: chained(d, sz // per, st)
        else:
            region = lambda st, d=done, sz=size: step([(d + j, "far") for j in range(sz)], st)[0]
        state = lax.cond(take > 0, region, lambda st: st, state)
        done = done + take
        size //= 2
    outT = jnp.concatenate([acc[:HEAD_DIM] / acc[HEAD_DIM:HEAD_DIM + 1] for (_, acc) in state], axis=0)
    o_ref[0] = outT.T.astype(bf16)


def _attention(qT, k4, vT4, km, own, adj, side_weights):
    B, aw, S = qT.shape
    nb = S // MOBA_BLOCK
    assert MOBA_TOPK < nb <= LANES
    n_heads = aw // HEAD_DIM
    steps = B * nb
    whole = lambda shape: pl.BlockSpec(shape, lambda b, i: (b,) + (0,) * (len(shape) - 1))
    table = pl.BlockSpec((n_heads, MOBA_BLOCK, MOBA_BLOCK), lambda b, i: (0, 0, 0))
    slabs = [w.reshape(steps, -1, w.shape[-1]) for w in side_weights]
    slab_specs = [pl.BlockSpec((1,) + w.shape[1:], lambda b, i: (b * nb + i, 0, 0)) for w in slabs]
    outs = pl.pallas_call(
        _attn_kernel,
        grid=(B, nb),
        in_specs=[pl.BlockSpec((1, aw, MOBA_BLOCK), lambda b, i: (b, 0, i)),
                  whole((1, nb, MOBA_BLOCK, aw)), whole((1, nb, aw, MOBA_BLOCK)), whole((1, nb, aw)),
                  table, table] + slab_specs,
        out_specs=[pl.BlockSpec((1, MOBA_BLOCK, aw), lambda b, i: (b, i, 0))] + slab_specs,
        out_shape=[jax.ShapeDtypeStruct((B, S, aw), bf16)]
                  + [jax.ShapeDtypeStruct(w.shape, bf16) for w in slabs],
        compiler_params=_params(("parallel", "arbitrary")),
        name="attn",
    )(qT, k4, vT4, km, own, adj, *slabs)
    return outs[0], [o.reshape(w.shape) for o, w in zip(outs[1:], side_weights)]


def _post_kernel(a_ref, mc_ref, ga_ref, x_ref, g1_ref, sc_ref, sh_ref, ng_ref,
                 wao_ref, wo_ref, wr_ref, br_ref, x1_ref, h2p_ref, route_ref):
    tm = x_ref.shape[1]
    d = x_ref.shape[2]
    sub = tm // POST_SPLIT
    lane = lax.broadcasted_iota(jnp.int32, (sub, LANES), 1)
    ninf = -jnp.inf
    for t in range(POST_SPLIT):
        rows = slice(t * sub, (t + 1) * sub)
        ya = jnp.dot(a_ref[0, rows, :], wao_ref[...], preferred_element_type=f32)
        merged = ga_ref[0, rows, :].astype(f32) * ya + mc_ref[0, rows, :].astype(f32)
        z = jnp.dot(merged.astype(bf16), wo_ref[...], preferred_element_type=f32)
        x1 = x_ref[0, rows, :] + g1_ref[0] * z
        x1_ref[0, rows, :] = x1
        h2 = _ada_norm(x1, ng_ref[...], sc_ref[0], sh_ref[0])
        h_hi = h2.astype(bf16)
        h_hi32 = h_hi.astype(f32)

        bits = pltpu.bitcast(h_hi32, jnp.uint32)
        packed = (bits[:, :d // 2] >> 16) | (bits[:, d // 2:] & jnp.uint32(0xFFFF0000))
        chunks = packed.shape[1] // LANES
        for j in range(chunks):
            h2p_ref[pl.ds(t * sub * chunks + j, sub, stride=chunks), :] = packed[:, j * LANES:(j + 1) * LANES]

        h_lo = (h2 - h_hi32).astype(bf16)
        hi_both = jnp.dot(h_hi, wr_ref[...], preferred_element_type=f32)
        logits = (hi_both[:, :LANES]
                  + (jnp.dot(h_lo, wr_ref[:, :LANES], preferred_element_type=f32) + hi_both[:, LANES:])) + br_ref[...]
        gl = jnp.where(lane < N_GROUPS, logits, ninf)
        gmax = jnp.max(gl, axis=1, keepdims=True)
        g_sel = jnp.min(jnp.where(gl == gmax, lane, LANES), axis=1, keepdims=True)
        g_w = 1.0 / jnp.sum(jnp.exp(gl - gmax), axis=1, keepdims=True)
        off = lane - (N_GROUPS + EXPERTS_PER_GROUP * g_sel)
        in_group = (off >= 0) & (off < EXPERTS_PER_GROUP)
        el = jnp.where(in_group, logits, ninf)
        ex = jnp.exp(el - jnp.max(el, axis=1, keepdims=True))
        ep = jnp.where(in_group, ex / jnp.sum(ex, axis=1, keepdims=True), -1.0)
        p1 = jnp.max(ep, axis=1, keepdims=True)
        i1 = jnp.min(jnp.where(ep == p1, lane, LANES), axis=1, keepdims=True)
        ep2 = jnp.where(lane == i1, -1.0, ep)
        p2 = jnp.max(ep2, axis=1, keepdims=True)
        i2 = jnp.min(jnp.where(ep2 == p2, lane, LANES), axis=1, keepdims=True)
        tot = p1 + p2
        w1 = p1 / tot * g_w
        w2 = p2 / tot * g_w
        e1 = (i1 - N_GROUPS).astype(f32)
        e2 = (i2 - N_GROUPS).astype(f32)
        route = jnp.where(lane == 0, e1, jnp.where(lane == 1, e2,
                          jnp.where(lane == 2, w1, jnp.where(lane == 3, w2, 0.0))))
        route_ref[:, rows] = route.T[:SUBLANES, :]


def _post(a, mc, ga, x, g1, sc2, sh2, ng, w_attn_out, w_o, w_rg, b_rg, w_re, b_re):
    B, S, D = x.shape
    tm = ROW_TILE
    aw = a.shape[2]
    nt = S // tm
    wr = jnp.zeros((D, LANES), f32).at[:, :N_GROUPS].set(w_rg).at[:, N_GROUPS:N_GROUPS + N_EXPERTS].set(w_re)
    br = jnp.zeros((1, LANES), f32).at[0, :N_GROUPS].set(b_rg).at[0, N_GROUPS:N_GROUPS + N_EXPERTS].set(b_re)
    wr_hi = wr.astype(bf16)
    wr_split = jnp.concatenate([wr_hi, (wr - wr_hi.astype(f32)).astype(bf16)], axis=1)
    const = lambda shape: pl.BlockSpec(shape, lambda b, s: (0,) * len(shape))
    vec = pl.BlockSpec((1, 1, D), lambda b, s: (b, 0, 0))
    row = lambda width: pl.BlockSpec((1, tm, width), lambda b, s: (b, s, 0))
    flat = lambda width, rows_per_token=1: pl.BlockSpec((tm * rows_per_token, width), lambda b, s: (b * nt + s, 0))
    words = D // 2 // LANES
    return pl.pallas_call(
        _post_kernel,
        grid=(B, nt),
        in_specs=[row(aw), row(D), row(D), row(D), vec, vec, vec, const((1, D)),
                  const((aw, D)), const((D, D)), const((D, 2 * LANES)), const((1, LANES))],
        out_specs=(row(D), flat(LANES, words), pl.BlockSpec((SUBLANES, tm), lambda b, s: (0, b * nt + s))),
        out_shape=(jax.ShapeDtypeStruct((B, S, D), f32),
                   jax.ShapeDtypeStruct((B * S * words, LANES), jnp.uint32),
                   jax.ShapeDtypeStruct((SUBLANES, B * S), f32)),
        compiler_params=_params(("parallel", "parallel")),
        name="post",
    )(a, mc, ga, x, g1.reshape(B, 1, D), sc2.reshape(B, 1, D), sh2.reshape(B, 1, D), ng.reshape(1, D),
      w_attn_out.astype(bf16), w_o.astype(bf16), wr_split, br)


def _dispatch_tables(route, n_chunks):
    T = route.shape[1]
    tc = T // n_chunks
    na = tc * TOP_K_EXPERTS
    per_chunk = lambda rows: rows.reshape(TOP_K_EXPERTS, n_chunks, tc).transpose(1, 0, 2).reshape(n_chunks, na)
    e = per_chunk(route[:TOP_K_EXPERTS]).astype(jnp.int32)
    w = per_chunk(route[TOP_K_EXPERTS:2 * TOP_K_EXPERTS])
    tok = jnp.broadcast_to(jnp.arange(na, dtype=jnp.int32) % tc, e.shape)
    _, stok, sw = lax.sort((e, tok, w), dimension=1, num_keys=1, is_stable=True)
    counts = jnp.sum((e[:, None, :] == jnp.arange(N_EXPERTS)[None, :, None]).astype(jnp.int32), axis=2)
    starts = jnp.cumsum(counts, axis=1) - counts
    flat = lambda t: t.reshape(-1).astype(jnp.int32)
    return flat(starts), flat(counts), stok.reshape(n_chunks, 1, na), sw.reshape(n_chunks, 1, na)


def _moe_kernel(start_ref, count_ref, stok_ref, sw_ref, h2p_ref, wg_ref, wu_ref, wd_ref,
                x1_ref, g2_ref, fg_ref, o_ref, yacc, xs, outs, *, normalize):
    c = pl.program_id(0)
    b = pl.program_id(1)
    block_rows = outs.shape[0] // SUBLANES
    words = xs.shape[0] // block_rows
    half = words * LANES
    run = c * N_EXPERTS + jnp.minimum(b, N_EXPERTS - 1)
    count = jnp.where(b < N_EXPERTS, count_ref[run], 0)
    first = start_ref[run]

    @pl.when(b == 0)
    def _():
        yacc[...] = jnp.zeros(yacc.shape, f32)

    @pl.when((b == 0) & (c == 0))
    def _():
        xs[...] = jnp.zeros(xs.shape, jnp.uint32)

    def token_rows(index, rows_per_token):
        return pl.ds(pl.multiple_of(index * rows_per_token, rows_per_token), rows_per_token)

    def block(blk, carry):
        st = first + blk * block_rows
        n = jnp.minimum(count - blk * block_rows, block_rows)

        def for_rows(rows_fn):
            full = n // ROW_UNROLL

            def group(g, carry):
                rows_fn([g * ROW_UNROLL + j for j in range(ROW_UNROLL)], st + g * ROW_UNROLL)
                return carry

            def single(j, carry):
                rows_fn([full * ROW_UNROLL + j], st + full * ROW_UNROLL + j)
                return carry

            lax.fori_loop(0, full, group, 0)
            lax.fori_loop(0, n - full * ROW_UNROLL, single, 0)

        def gather(rs, i0):
            for k, r in enumerate(rs):
                xs[token_rows(r, words), :] = h2p_ref[token_rows(stok_ref[0, 0, i0 + k], words), :]

        for_rows(gather)
        lo, hi = [], []
        for j in range(words):
            packed = xs[pl.ds(j, block_rows, stride=words), :]
            lo.append(pltpu.bitcast(packed << 16, f32).astype(bf16))
            hi.append(pltpu.bitcast(packed & jnp.uint32(0xFFFF0000), f32).astype(bf16))
        x_lo = jnp.concatenate(lo, axis=1)
        x_hi = jnp.concatenate(hi, axis=1)

        def up(w_ref):
            return (jnp.dot(x_lo, w_ref[0, :half, :], preferred_element_type=f32)
                    + jnp.dot(x_hi, w_ref[0, half:, :], preferred_element_type=f32))

        hidden = (jax.nn.silu(up(wg_ref)) * up(wu_ref)).astype(bf16)
        out = jnp.dot(hidden, wd_ref[0], preferred_element_type=f32)
        for j in range(SUBLANES):
            outs[pl.ds(j, block_rows, stride=SUBLANES), :] = out[:, j * LANES:(j + 1) * LANES]

        def scatter(rs, i0):
            toks = [stok_ref[0, 0, i0 + k] for k in range(len(rs))]
            acc = [yacc[token_rows(tok, SUBLANES), :] for tok in toks]
            for k, (r, tok, a) in enumerate(zip(rs, toks, acc)):
                yacc[token_rows(tok, SUBLANES), :] = a + sw_ref[0, 0, i0 + k] * outs[token_rows(r, SUBLANES), :]

        for_rows(scatter)
        return carry

    lax.fori_loop(0, (count + block_rows - 1) // block_rows, block, 0)

    @pl.when(b >= N_EXPERTS)
    def _():
        rows = o_ref.shape[0]
        start = pl.multiple_of((b - N_EXPERTS) * rows * SUBLANES, rows * SUBLANES)
        y = jnp.concatenate([yacc[pl.ds(start + j, rows, stride=SUBLANES), :] for j in range(SUBLANES)], axis=1)
        x2 = x1_ref[...] + g2_ref[0] * y
        if normalize:
            x2 = (x2 * lax.rsqrt(jnp.mean(x2 * x2, axis=-1, keepdims=True) + NORM_EPS)) * fg_ref[...]
        o_ref[...] = x2


def _moe(route, h2p, w_gate, w_up, w_down, x1, g2, fg, normalize):
    B, S, D = x1.shape
    T = B * S
    words = h2p.shape[0] // T
    assert D == words * 2 * LANES == SUBLANES * LANES
    de = w_gate.shape[2]
    n_chunks = MOE_CHUNKS
    tc = T // n_chunks
    fin_rows = ROW_TILE // 2
    n_fin = tc // fin_rows
    assert n_chunks % B == 0 and tc % fin_rows == 0 and w_gate.shape[0] == N_EXPERTS
    starts, counts, stok, sw = _dispatch_tables(route, n_chunks)
    na = stok.shape[2]
    wmap = lambda c, b, *_: (jnp.minimum(b, N_EXPERTS - 1), 0, 0)
    smem = lambda: pl.BlockSpec((1, 1, na), lambda c, b, *_: (c, 0, 0), memory_space=pltpu.SMEM)
    tile = pl.BlockSpec((fin_rows, D), lambda c, b, *_: (c * n_fin + jnp.maximum(b - N_EXPERTS, 0), 0))
    grid_spec = pltpu.PrefetchScalarGridSpec(
        num_scalar_prefetch=2,
        grid=(n_chunks, N_EXPERTS + n_fin),
        in_specs=[smem(), smem(),
                  pl.BlockSpec((tc * words, LANES), lambda c, b, *_: (c, 0)),
                  pl.BlockSpec((1, D, de), wmap),
                  pl.BlockSpec((1, D, de), wmap),
                  pl.BlockSpec((1, de, D), wmap),
                  tile,
                  pl.BlockSpec((1, 1, D), lambda c, b, *_: (c // (n_chunks // B), 0, 0)),
                  pl.BlockSpec((1, D), lambda c, b, *_: (0, 0))],
        out_specs=tile,
        scratch_shapes=[pltpu.VMEM((tc * SUBLANES, LANES), f32),
                        pltpu.VMEM((EXPERT_ROWS * words, LANES), jnp.uint32),
                        pltpu.VMEM((EXPERT_ROWS * SUBLANES, LANES), f32)],
    )
    out = pl.pallas_call(
        functools.partial(_moe_kernel, normalize=normalize),
        grid_spec=grid_spec,
        out_shape=jax.ShapeDtypeStruct((T, D), f32),
        compiler_params=_params(("arbitrary", "arbitrary")),
        name="moe",
    )(starts, counts, stok, sw, h2p, w_gate, w_up, w_down,
      x1.reshape(T, D), g2.reshape(B, 1, D), fg.reshape(1, D))
    return out.reshape(B, S, D)


def kernel(x, c, rel_bias, norm1_g, norm2_g, w_ada, b_ada, w_in, b_gate, conv_w, w_attn_out,
           w_conv_out, w_o, w_router_group, b_router_group, w_router_expert, b_router_expert,
           w_exp_gate, w_exp_up, w_exp_down, final_norm_g):
    B, S, D = x.shape
    depth = w_ada.shape[0]
    assert S % ROW_TILE == 0 and ROW_TILE % MOBA_BLOCK == 0
    own, adj = _bias_tiles(rel_bias)
    for l in range(depth):
        mod = _mod(c, w_ada[l], b_ada[l])
        sh1, sc1, g1, sh2, sc2, g2 = jnp.split(mod, 6, axis=-1)
        qT, k4, vT4, km, mc, ga = _inproj(x, norm1_g[l], sc1, sh1, w_in[l], b_gate[l], conv_w[l],
                                          w_conv_out[l])
        a, expert_w = _attention(qT, k4, vT4, km, own, adj, [w_exp_gate[l], w_exp_up[l], w_exp_down[l]])
        x1, h2p, route = _post(a, mc, ga, x, g1, sc2, sh2, norm2_g[l], w_attn_out[l], w_o[l],
                               w_router_group[l], b_router_group[l], w_router_expert[l],
                               b_router_expert[l])
        x = _moe(route, h2p, *expert_w, x1, g2, final_norm_g, normalize=(l + 1 == depth))
    return x
```

```python
import functools
import math

import numpy as np
import jax
import jax.numpy as jnp
from jax import lax
from jax.experimental import pallas as pl
from jax.experimental.pallas import tpu as pltpu

f32 = jnp.float32
bf16 = jnp.bfloat16

N_HEADS = 8
HEAD_DIM = 64
ATTN_WIDTH = N_HEADS * HEAD_DIM
MOBA_BLOCK = 256
MOBA_TOPK = 3
CONV_K = 3
N_BUCKETS = 32
MAX_EXACT = N_BUCKETS // 2
MAX_DISTANCE = 128
N_GROUPS = 4
EXPERTS_PER_GROUP = 8
N_EXPERTS = N_GROUPS * EXPERTS_PER_GROUP
TOP_K_EXPERTS = 2
NORM_EPS = 1e-6
NEG_INF = -1e30

LANES = 128
SUBLANES = 8
MOD_COLS = 1024
ROW_TILE = 1024
POST_SPLIT = 2
EXPERT_ROWS = 288
ROW_UNROLL = 8
MOE_CHUNKS = 4
FAR_BLOCKS_PER_ITER = 2
FAR_SUBSTEPS = 4
SUM_ROWS = 16
LOG2E = 1.4426950408889634
VMEM_LIMIT = 56 * 1024 * 1024


def _params(sem, vmem=VMEM_LIMIT):
    return pltpu.CompilerParams(dimension_semantics=sem, vmem_limit_bytes=vmem)


def _mod_kernel(c_ref, w_ref, b_ref, o_ref, *, batch):
    act = jax.nn.silu(c_ref[...]).T
    w = w_ref[...]
    rows = [jnp.sum(act[:, r:r + 1] * w, axis=0, keepdims=True) for r in range(batch)]
    rows.append(jnp.zeros((o_ref.shape[0] - batch, w.shape[1]), f32))
    o_ref[...] = jnp.concatenate(rows, axis=0) + b_ref[...]


def _mod(c, w_ada, b_ada):
    B, D = c.shape
    n_out = w_ada.shape[1]
    rows = SUBLANES
    cp = jnp.zeros((rows, D), f32).at[:B].set(c)
    tn = MOD_COLS
    out = pl.pallas_call(
        functools.partial(_mod_kernel, batch=B),
        grid=(n_out // tn,),
        in_specs=[pl.BlockSpec((rows, D), lambda j: (0, 0)),
                  pl.BlockSpec((D, tn), lambda j: (0, j)),
                  pl.BlockSpec((1, tn), lambda j: (0, j))],
        out_specs=pl.BlockSpec((rows, tn), lambda j: (0, j)),
        out_shape=jax.ShapeDtypeStruct((rows, n_out), f32),
        compiler_params=_params(("arbitrary",)),
        name="mod",
    )(cp, w_ada, b_ada.reshape(1, n_out))
    return out[:B]


def _bucket_thresholds():
    n = np.arange(2 * MOBA_BLOCK)

    def buckets(ft):
        nf = np.maximum(n, 1).astype(ft)
        large = MAX_EXACT + (np.log(nf / ft(MAX_EXACT)) / ft(math.log(MAX_DISTANCE / MAX_EXACT))
                             * ft(N_BUCKETS - MAX_EXACT)).astype(np.int32)
        return np.where(n < MAX_EXACT, n, np.minimum(large, N_BUCKETS - 1))

    b = buckets(np.float32)
    assert np.array_equal(b, buckets(np.float64)) and np.all(np.diff(b) >= 0)
    assert b[-1] == N_BUCKETS - 1 and b[MOBA_BLOCK + 1] == N_BUCKETS - 1
    return [int(np.argmax(b >= j)) for j in range(N_BUCKETS)]


def _bias_kernel(rb_ref, own_ref, adj_ref, *, thresholds):
    h = pl.program_id(0)
    kk = lax.broadcasted_iota(jnp.int32, (MOBA_BLOCK, MOBA_BLOCK), 0)
    qq = lax.broadcasted_iota(jnp.int32, (MOBA_BLOCK, MOBA_BLOCK), 1)
    far = rb_ref[N_BUCKETS - 1, h]

    def table(dist):
        acc = jnp.full(dist.shape, rb_ref[0, h], f32)
        for j in range(1, N_BUCKETS):
            acc = jnp.where(dist >= thresholds[j], rb_ref[j, h], acc)
        return (acc - far) * LOG2E

    own_ref[0] = table(qq - kk)
    adj_ref[0] = table(qq - kk + MOBA_BLOCK)


def _bias_tiles(rel_bias):
    shp = jax.ShapeDtypeStruct((N_HEADS, MOBA_BLOCK, MOBA_BLOCK), f32)
    spec = pl.BlockSpec((1, MOBA_BLOCK, MOBA_BLOCK), lambda h: (h, 0, 0))
    return pl.pallas_call(
        functools.partial(_bias_kernel, thresholds=_bucket_thresholds()),
        grid=(N_HEADS,),
        in_specs=[pl.BlockSpec(memory_space=pltpu.SMEM)],
        out_specs=(spec, spec),
        out_shape=(shp, shp),
        compiler_params=_params(("arbitrary",)),
        name="bias",
    )(rel_bias)


def _ada_norm(x, g, scale, shift):
    y = x * lax.rsqrt(jnp.mean(x * x, axis=-1, keepdims=True) + NORM_EPS)
    return (y * g) * (1.0 + scale) + shift


def _inproj_kernel(x_ref, g_ref, sc_ref, sh_ref, wqT_ref, wk_ref, wvT_ref, wc_ref, wgl_ref,
                   wco_ref, cw_ref, bg_ref,
                   qT_ref, k_ref, vT_ref, km_ref, mc_ref, ga_ref, ubuf):
    tm = x_ref.shape[1]
    cwid = wco_ref.shape[0]
    d = x_ref.shape[2]
    s = pl.program_id(1)
    hb = _ada_norm(x_ref[0], g_ref[...], sc_ref[0], sh_ref[0]).astype(bf16)

    nt = (((1,), (1,)), ((), ()))
    qT_ref[0] = lax.dot_general(wqT_ref[...], hb, nt, preferred_element_type=f32).astype(bf16)
    k = jnp.dot(hb, wk_ref[...], preferred_element_type=f32)
    vT = lax.dot_general(wvT_ref[...], hb, nt, preferred_element_type=f32).astype(bf16)
    for j in range(tm // MOBA_BLOCK):
        rows = slice(j * MOBA_BLOCK, (j + 1) * MOBA_BLOCK)
        k_ref[0, j] = k[rows].astype(bf16)
        vT_ref[0, j] = vT[:, rows]
        km_ref[0, 0, j:j + 1, :] = jnp.mean(k[rows], axis=0, keepdims=True)

    cbx = jnp.dot(hb, wc_ref[...], preferred_element_type=f32)
    u = cbx[:, :cwid] * cbx[:, 2 * cwid:]

    halo = SUBLANES

    @pl.when(s == 0)
    def _():
        ubuf[0:halo, :] = jnp.zeros((halo, cwid), f32)

    ubuf[halo:halo + tm, :] = u
    cw = cw_ref[...]
    taps = [ubuf[halo - back:halo - back + tm, :] for back in range(CONV_K - 1, 0, -1)] + [u]
    conv = cw[0:1] * taps[0]
    for j in range(1, CONV_K):
        conv = conv + cw[j:j + 1] * taps[j]
    ubuf[0:halo, :] = ubuf[tm:tm + halo, :]
    cv = (cbx[:, cwid:2 * cwid] * conv).astype(bf16)
    y_conv = jnp.dot(cv, wco_ref[...], preferred_element_type=f32)

    gl = jnp.dot(hb, wgl_ref[...], preferred_element_type=f32) + bg_ref[...]
    ga_ref[0] = jax.nn.sigmoid(gl[:, :d]).astype(bf16)
    mc_ref[0] = (jax.nn.sigmoid(gl[:, d:]) * y_conv).astype(bf16)


def _inproj(x, g, sc, sh, w_in, b_gate, conv_w, w_conv_out):
    B, S, D = x.shape
    tm = ROW_TILE
    nb = S // MOBA_BLOCK
    bpt = tm // MOBA_BLOCK
    aw = ATTN_WIDTH
    cwid = conv_w.shape[1]
    scale = HEAD_DIM ** -0.5 * LOG2E
    wqT = (w_in[:, :aw] * scale).T.astype(bf16)
    wk = w_in[:, aw:2 * aw].astype(bf16)
    wvT = w_in[:, 2 * aw:3 * aw].T.astype(bf16)
    wc = w_in[:, 3 * aw:3 * aw + 3 * cwid].astype(bf16)
    wgl = w_in[:, 3 * aw + 3 * cwid:].astype(bf16)
    wco = w_conv_out.astype(bf16)

    const = lambda shape: pl.BlockSpec(shape, lambda b, s: (0,) * len(shape))
    vec = pl.BlockSpec((1, 1, D), lambda b, s: (b, 0, 0))
    row = lambda width: pl.BlockSpec((1, tm, width), lambda b, s: (b, s, 0))
    outs = pl.pallas_call(
        _inproj_kernel,
        grid=(B, S // tm),
        in_specs=[row(D), const((1, D)), vec, vec,
                  const((aw, D)), const((D, aw)), const((aw, D)), const((D, 3 * cwid)),
                  const((D, 2 * D)), const((cwid, D)), const((CONV_K, cwid)), const((1, 2 * D))],
        out_specs=(pl.BlockSpec((1, aw, tm), lambda b, s: (b, 0, s)),
                   pl.BlockSpec((1, bpt, MOBA_BLOCK, aw), lambda b, s: (b, s, 0, 0)),
                   pl.BlockSpec((1, bpt, aw, MOBA_BLOCK), lambda b, s: (b, s, 0, 0)),
                   pl.BlockSpec((1, 1, bpt, aw), lambda b, s: (b, s, 0, 0)),
                   row(D), row(D)),
        out_shape=(jax.ShapeDtypeStruct((B, aw, S), bf16),
                   jax.ShapeDtypeStruct((B, nb, MOBA_BLOCK, aw), bf16),
                   jax.ShapeDtypeStruct((B, nb, aw, MOBA_BLOCK), bf16),
                   jax.ShapeDtypeStruct((B, S // tm, bpt, aw), f32),
                   jax.ShapeDtypeStruct((B, S, D), bf16),
                   jax.ShapeDtypeStruct((B, S, D), bf16)),
        scratch_shapes=[pltpu.VMEM((tm + SUBLANES, cwid), f32)],
        compiler_params=_params(("parallel", "arbitrary")),
        name="inproj",
    )(x, g.reshape(1, D), sc.reshape(B, 1, D), sh.reshape(B, 1, D),
      wqT, wk, wvT, wc, wgl, wco, conv_w, b_gate.reshape(1, 2 * D))
    qT, k4, vT4, km, mc, ga = outs
    return qT, k4, vT4, km.reshape(B, nb, aw), mc, ga


def _attn_kernel(qT_ref, k_ref, vT_ref, km_ref, own_ref, adj_ref, *side_refs):
    n_side = len(side_refs) // 2
    o_ref = side_refs[n_side]
    for src, dst in zip(side_refs[:n_side], side_refs[n_side + 1:]):
        dst[...] = src[...].astype(bf16)

    i = pl.program_id(1)
    bs = MOBA_BLOCK
    nb = km_ref.shape[1]
    n_heads = qT_ref.shape[1] // HEAD_DIM
    heads = range(n_heads)
    blk = lax.broadcasted_iota(jnp.int32, (nb, bs), 0)

    qcat = []
    for h in heads:
        pair, a = divmod(h, LANES // HEAD_DIM)
        qT = qT_ref[0, pair * LANES:(pair + 1) * LANES, :]
        row = lax.broadcasted_iota(jnp.int32, qT.shape, 0)
        qa = jnp.where((row >= a * HEAD_DIM) & (row < (a + 1) * HEAD_DIM), qT, jnp.zeros_like(qT))
        km = km_ref[0, :, pair * LANES:(pair + 1) * LANES]
        km_hi = km.astype(bf16)
        km_mid = (km - km_hi.astype(f32)).astype(bf16)
        km_lo = (km - km_hi.astype(f32) - km_mid.astype(f32)).astype(bf16)
        g = (jnp.dot(km_hi, qa, preferred_element_type=f32)
             + (jnp.dot(km_mid, qa, preferred_element_type=f32) + jnp.dot(km_lo, qa, preferred_element_type=f32)))
        g = jnp.where(blk < i, g, NEG_INF)
        fq = jnp.full((nb, bs), NEG_INF, f32)
        for _ in range(MOBA_TOPK):
            top = jnp.max(g, axis=0, keepdims=True)
            first = jnp.min(jnp.where(g == top, blk, nb), axis=0, keepdims=True)
            pick = blk == first
            fq = jnp.where(pick, 0.0, fq)
            g = jnp.where(pick, -jnp.inf, g)
        fq = jnp.where(blk < i, fq, 0.0).astype(bf16)
        pad = jnp.zeros((LANES - nb, bs), bf16)
        qcat.append(jnp.concatenate([qa, fq, pad], axis=0))

    lane = lax.broadcasted_iota(jnp.int32, (bs, LANES), 1)
    causal = (lax.broadcasted_iota(jnp.int32, (bs, bs), 0) <= lax.broadcasted_iota(jnp.int32, (bs, bs), 1))
    ones_rows = jnp.ones((SUM_ROWS, bs), bf16)

    def qk(blocks, h):
        pair = h // (LANES // HEAD_DIM)
        cols = slice(pair * LANES, (pair + 1) * LANES)
        return [jnp.dot(jnp.concatenate([k_ref[0, n, :, cols], jnp.where(lane == n, 1.0, 0.0).astype(bf16)], axis=1),
                        qcat[h], preferred_element_type=f32) for n, _ in blocks]

    def step(blocks, state, scores=None, next_blocks=None):
        if scores is None:
            scores = [qk(blocks, h) for h in heads]
        next_scores = []
        out = []
        for h in heads:
            sTs = []
            for (n, kind), sT in zip(blocks, scores[h]):
                if kind == "own":
                    sT = jnp.where(causal, own_ref[h] + sT, NEG_INF)
                elif kind == "adj":
                    sT = adj_ref[h] + sT
                sTs.append(sT)
            m_new = None if state is None else state[h][0]
            for sT in sTs:
                top = jnp.max(sT, axis=0, keepdims=True)
                m_new = top if m_new is None else jnp.maximum(m_new, top)
            acc = None if state is None else jnp.exp2(state[h][0] - m_new) * state[h][1]
            if next_blocks is not None:
                next_scores.append(qk(next_blocks, h))
            for (n, _), sT in zip(blocks, sTs):
                p = jnp.exp2(sT - m_new).astype(bf16)
                v_ext = jnp.concatenate([vT_ref[0, n, h * HEAD_DIM:(h + 1) * HEAD_DIM, :], ones_rows], axis=0)
                pv = jnp.dot(v_ext, p, preferred_element_type=f32)
                acc = pv if acc is None else acc + pv
            out.append((m_new, acc))
        return tuple(out), next_scores

    per = FAR_BLOCKS_PER_ITER

    def chained(first, n_sub, st, scores=None):
        blocks_of = lambda u: [(first + u * per + j, "far") for j in range(per)]
        for u in range(n_sub):
            st, scores = step(blocks_of(u), st, scores, blocks_of(u + 1) if u + 1 < n_sub else None)
        return st

    def opening_with_far():
        st, scores = step([(i, "own"), (i - 1, "adj")], None, None, [(j, "far") for j in range(per)])
        return chained(0, FAR_SUBSTEPS - 1, st, scores)

    span = per * FAR_SUBSTEPS
    joined = i - 1 >= span - per
    state = lax.cond(joined, opening_with_far,
                     lambda: lax.cond(i >= 1,
                                      lambda: step([(i, "own"), (i - 1, "adj")], None)[0],
                                      lambda: step([(i, "own")], None)[0]))
    base = jnp.where(joined, span - per, 0)
    n_far = jnp.maximum(i - 1, 0) - base
    state = lax.fori_loop(0, n_far // span, lambda it, st: chained(base + it * span, FAR_SUBSTEPS, st), state)
    done = base + n_far - n_far % span
    size = span // 2
    while size >= 1:
        take = (n_far % span) & size
        if size >= per:
            region = lambda st, d=done, sz=size: chained(d, sz // per, st)
        else:
            region = lambda st, d=done, sz=size: step([(d + j, "far") for j in range(sz)], st)[0]
        state = lax.cond(take > 0, region, lambda st: st, state)
        done = done + take
        size //= 2
    outT = jnp.concatenate([acc[:HEAD_DIM] / acc[HEAD_DIM:HEAD_DIM + 1] for (_, acc) in state], axis=0)
    o_ref[0] = outT.T.astype(bf16)


def _attention(qT, k4, vT4, km, own, adj, side_weights):
    B, aw, S = qT.shape
    nb = S // MOBA_BLOCK
    assert MOBA_TOPK < nb <= LANES
    n_heads = aw // HEAD_DIM
    steps = B * nb
    whole = lambda shape: pl.BlockSpec(shape, lambda b, i: (b,) + (0,) * (len(shape) - 1))
    table = pl.BlockSpec((n_heads, MOBA_BLOCK, MOBA_BLOCK), lambda b, i: (0, 0, 0))
    slabs = [w.reshape(steps, -1, w.shape[-1]) for w in side_weights]
    slab_specs = [pl.BlockSpec((1,) + w.shape[1:], lambda b, i: (b * nb + i, 0, 0)) for w in slabs]
    outs = pl.pallas_call(
        _attn_kernel,
        grid=(B, nb),
        in_specs=[pl.BlockSpec((1, aw, MOBA_BLOCK), lambda b, i: (b, 0, i)),
                  whole((1, nb, MOBA_BLOCK, aw)), whole((1, nb, aw, MOBA_BLOCK)), whole((1, nb, aw)),
                  table, table] + slab_specs,
        out_specs=[pl.BlockSpec((1, MOBA_BLOCK, aw), lambda b, i: (b, i, 0))] + slab_specs,
        out_shape=[jax.ShapeDtypeStruct((B, S, aw), bf16)]
                  + [jax.ShapeDtypeStruct(w.shape, bf16) for w in slabs],
        compiler_params=_params(("parallel", "arbitrary")),
        name="attn",
    )(qT, k4, vT4, km, own, adj, *slabs)
    return outs[0], [o.reshape(w.shape) for o, w in zip(outs[1:], side_weights)]


def _post_kernel(a_ref, mc_ref, ga_ref, x_ref, g1_ref, sc_ref, sh_ref, ng_ref,
                 wao_ref, wo_ref, wr_ref, br_ref, x1_ref, h2p_ref, route_ref):
    tm = x_ref.shape[1]
    d = x_ref.shape[2]
    sub = tm // POST_SPLIT
    lane = lax.broadcasted_iota(jnp.int32, (sub, LANES), 1)
    ninf = -jnp.inf
    for t in range(POST_SPLIT):
        rows = slice(t * sub, (t + 1) * sub)
        ya = jnp.dot(a_ref[0, rows, :], wao_ref[...], preferred_element_type=f32)
        merged = ga_ref[0, rows, :].astype(f32) * ya + mc_ref[0, rows, :].astype(f32)
        z = jnp.dot(merged.astype(bf16), wo_ref[...], preferred_element_type=f32)
        x1 = x_ref[0, rows, :] + g1_ref[0] * z
        x1_ref[0, rows, :] = x1
        h2 = _ada_norm(x1, ng_ref[...], sc_ref[0], sh_ref[0])
        h_hi = h2.astype(bf16)
        h_hi32 = h_hi.astype(f32)

        bits = pltpu.bitcast(h_hi32, jnp.uint32)
        packed = (bits[:, :d // 2] >> 16) | (bits[:, d // 2:] & jnp.uint32(0xFFFF0000))
        chunks = packed.shape[1] // LANES
        for j in range(chunks):
            h2p_ref[pl.ds(t * sub * chunks + j, sub, stride=chunks), :] = packed[:, j * LANES:(j + 1) * LANES]

        h_lo = (h2 - h_hi32).astype(bf16)
        hi_both = jnp.dot(h_hi, wr_ref[...], preferred_element_type=f32)
        logits = (hi_both[:, :LANES]
                  + (jnp.dot(h_lo, wr_ref[:, :LANES], preferred_element_type=f32) + hi_both[:, LANES:])) + br_ref[...]
        gl = jnp.where(lane < N_GROUPS, logits, ninf)
        gmax = jnp.max(gl, axis=1, keepdims=True)
        g_sel = jnp.min(jnp.where(gl == gmax, lane, LANES), axis=1, keepdims=True)
        g_w = 1.0 / jnp.sum(jnp.exp(gl - gmax), axis=1, keepdims=True)
        off = lane - (N_GROUPS + EXPERTS_PER_GROUP * g_sel)
        in_group = (off >= 0) & (off < EXPERTS_PER_GROUP)
        el = jnp.where(in_group, logits, ninf)
        ex = jnp.exp(el - jnp.max(el, axis=1, keepdims=True))
        ep = jnp.where(in_group, ex / jnp.sum(ex, axis=1, keepdims=True), -1.0)
        p1 = jnp.max(ep, axis=1, keepdims=True)
        i1 = jnp.min(jnp.where(ep == p1, lane, LANES), axis=1, keepdims=True)
        ep2 = jnp.where(lane == i1, -1.0, ep)
        p2 = jnp.max(ep2, axis=1, keepdims=True)
        i2 = jnp.min(jnp.where(ep2 == p2, lane, LANES), axis=1, keepdims=True)
        tot = p1 + p2
        w1 = p1 / tot * g_w
        w2 = p2 / tot * g_w
        e1 = (i1 - N_GROUPS).astype(f32)
        e2 = (i2 - N_GROUPS).astype(f32)
        route = jnp.where(lane == 0, e1, jnp.where(lane == 1, e2,
                          jnp.where(lane == 2, w1, jnp.where(lane == 3, w2, 0.0))))
        route_ref[:, rows] = route.T[:SUBLANES, :]


def _post(a, mc, ga, x, g1, sc2, sh2, ng, w_attn_out, w_o, w_rg, b_rg, w_re, b_re):
    B, S, D = x.shape
    tm = ROW_TILE
    aw = a.shape[2]
    nt = S // tm
    wr = jnp.zeros((D, LANES), f32).at[:, :N_GROUPS].set(w_rg).at[:, N_GROUPS:N_GROUPS + N_EXPERTS].set(w_re)
    br = jnp.zeros((1, LANES), f32).at[0, :N_GROUPS].set(b_rg).at[0, N_GROUPS:N_GROUPS + N_EXPERTS].set(b_re)
    wr_hi = wr.astype(bf16)
    wr_split = jnp.concatenate([wr_hi, (wr - wr_hi.astype(f32)).astype(bf16)], axis=1)
    const = lambda shape: pl.BlockSpec(shape, lambda b, s: (0,) * len(shape))
    vec = pl.BlockSpec((1, 1, D), lambda b, s: (b, 0, 0))
    row = lambda width: pl.BlockSpec((1, tm, width), lambda b, s: (b, s, 0))
    flat = lambda width, rows_per_token=1: pl.BlockSpec((tm * rows_per_token, width), lambda b, s: (b * nt + s, 0))
    words = D // 2 // LANES
    return pl.pallas_call(
        _post_kernel,
        grid=(B, nt),
        in_specs=[row(aw), row(D), row(D), row(D), vec, vec, vec, const((1, D)),
                  const((aw, D)), const((D, D)), const((D, 2 * LANES)), const((1, LANES))],
        out_specs=(row(D), flat(LANES, words), pl.BlockSpec((SUBLANES, tm), lambda b, s: (0, b * nt + s))),
        out_shape=(jax.ShapeDtypeStruct((B, S, D), f32),
                   jax.ShapeDtypeStruct((B * S * words, LANES), jnp.uint32),
                   jax.ShapeDtypeStruct((SUBLANES, B * S), f32)),
        compiler_params=_params(("parallel", "parallel")),
        name="post",
    )(a, mc, ga, x, g1.reshape(B, 1, D), sc2.reshape(B, 1, D), sh2.reshape(B, 1, D), ng.reshape(1, D),
      w_attn_out.astype(bf16), w_o.astype(bf16), wr_split, br)


def _dispatch_tables(route, n_chunks):
    T = route.shape[1]
    tc = T // n_chunks
    na = tc * TOP_K_EXPERTS
    per_chunk = lambda rows: rows.reshape(TOP_K_EXPERTS, n_chunks, tc).transpose(1, 0, 2).reshape(n_chunks, na)
    e = per_chunk(route[:TOP_K_EXPERTS]).astype(jnp.int32)
    w = per_chunk(route[TOP_K_EXPERTS:2 * TOP_K_EXPERTS])
    tok = jnp.broadcast_to(jnp.arange(na, dtype=jnp.int32) % tc, e.shape)
    _, stok, sw = lax.sort((e, tok, w), dimension=1, num_keys=1, is_stable=True)
    counts = jnp.sum((e[:, None, :] == jnp.arange(N_EXPERTS)[None, :, None]).astype(jnp.int32), axis=2)
    starts = jnp.cumsum(counts, axis=1) - counts
    flat = lambda t: t.reshape(-1).astype(jnp.int32)
    return flat(starts), flat(counts), stok.reshape(n_chunks, 1, na), sw.reshape(n_chunks, 1, na)


def _moe_kernel(start_ref, count_ref, stok_ref, sw_ref, h2p_ref, wg_ref, wu_ref, wd_ref,
                x1_ref, g2_ref, fg_ref, o_ref, yacc, xs, outs, *, normalize):
    c = pl.program_id(0)
    b = pl.program_id(1)
    block_rows = outs.shape[0] // SUBLANES
    words = xs.shape[0] // block_rows
    half = words * LANES
    run = c * N_EXPERTS + jnp.minimum(b, N_EXPERTS - 1)
    count = jnp.where(b < N_EXPERTS, count_ref[run], 0)
    first = start_ref[run]

    @pl.when(b == 0)
    def _():
        yacc[...] = jnp.zeros(yacc.shape, f32)

    @pl.when((b == 0) & (c == 0))
    def _():
        xs[...] = jnp.zeros(xs.shape, jnp.uint32)

    def token_rows(index, rows_per_token):
        return pl.ds(pl.multiple_of(index * rows_per_token, rows_per_token), rows_per_token)

    def block(blk, carry):
        st = first + blk * block_rows
        n = jnp.minimum(count - blk * block_rows, block_rows)

        def for_rows(rows_fn):
            full = n // ROW_UNROLL

            def group(g, carry):
                rows_fn([g * ROW_UNROLL + j for j in range(ROW_UNROLL)], st + g * ROW_UNROLL)
                return carry

            def single(j, carry):
                rows_fn([full * ROW_UNROLL + j], st + full * ROW_UNROLL + j)
                return carry

            lax.fori_loop(0, full, group, 0)
            lax.fori_loop(0, n - full * ROW_UNROLL, single, 0)

        def gather(rs, i0):
            for k, r in enumerate(rs):
                xs[token_rows(r, words), :] = h2p_ref[token_rows(stok_ref[0, 0, i0 + k], words), :]

        for_rows(gather)
        lo, hi = [], []
        for j in range(words):
            packed = xs[pl.ds(j, block_rows, stride=words), :]
            lo.append(pltpu.bitcast(packed << 16, f32).astype(bf16))
            hi.append(pltpu.bitcast(packed & jnp.uint32(0xFFFF0000), f32).astype(bf16))
        x_lo = jnp.concatenate(lo, axis=1)
        x_hi = jnp.concatenate(hi, axis=1)

        def up(w_ref):
            return (jnp.dot(x_lo, w_ref[0, :half, :], preferred_element_type=f32)
                    + jnp.dot(x_hi, w_ref[0, half:, :], preferred_element_type=f32))

        hidden = (jax.nn.silu(up(wg_ref)) * up(wu_ref)).astype(bf16)
        out = jnp.dot(hidden, wd_ref[0], preferred_element_type=f32)
        for j in range(SUBLANES):
            outs[pl.ds(j, block_rows, stride=SUBLANES), :] = out[:, j * LANES:(j + 1) * LANES]

        def scatter(rs, i0):
            toks = [stok_ref[0, 0, i0 + k] for k in range(len(rs))]
            acc = [yacc[token_rows(tok, SUBLANES), :] for tok in toks]
            for k, (r, tok, a) in enumerate(zip(rs, toks, acc)):
                yacc[token_rows(tok, SUBLANES), :] = a + sw_ref[0, 0, i0 + k] * outs[token_rows(r, SUBLANES), :]

        for_rows(scatter)
        return carry

    lax.fori_loop(0, (count + block_rows - 1) // block_rows, block, 0)

    @pl.when(b >= N_EXPERTS)
    def _():
        rows = o_ref.shape[0]
        start = pl.multiple_of((b - N_EXPERTS) * rows * SUBLANES, rows * SUBLANES)
        y = jnp.concatenate([yacc[pl.ds(start + j, rows, stride=SUBLANES), :] for j in range(SUBLANES)], axis=1)
        x2 = x1_ref[...] + g2_ref[0] * y
        if normalize:
            x2 = (x2 * lax.rsqrt(jnp.mean(x2 * x2, axis=-1, keepdims=True) + NORM_EPS)) * fg_ref[...]
        o_ref[...] = x2


def _moe(route, h2p, w_gate, w_up, w_down, x1, g2, fg, normalize):
    B, S, D = x1.shape
    T = B * S
    words = h2p.shape[0] // T
    assert D == words * 2 * LANES == SUBLANES * LANES
    de = w_gate.shape[2]
    n_chunks = MOE_CHUNKS
    tc = T // n_chunks
    fin_rows = ROW_TILE // 2
    n_fin = tc // fin_rows
    assert n_chunks % B == 0 and tc % fin_rows == 0 and w_gate.shape[0] == N_EXPERTS
    starts, counts, stok, sw = _dispatch_tables(route, n_chunks)
    na = stok.shape[2]
    wmap = lambda c, b, *_: (jnp.minimum(b, N_EXPERTS - 1), 0, 0)
    smem = lambda: pl.BlockSpec((1, 1, na), lambda c, b, *_: (c, 0, 0), memory_space=pltpu.SMEM)
    tile = pl.BlockSpec((fin_rows, D), lambda c, b, *_: (c * n_fin + jnp.maximum(b - N_EXPERTS, 0), 0))
    grid_spec = pltpu.PrefetchScalarGridSpec(
        num_scalar_prefetch=2,
        grid=(n_chunks, N_EXPERTS + n_fin),
        in_specs=[smem(), smem(),
                  pl.BlockSpec((tc * words, LANES), lambda c, b, *_: (c, 0)),
                  pl.BlockSpec((1, D, de), wmap),
                  pl.BlockSpec((1, D, de), wmap),
                  pl.BlockSpec((1, de, D), wmap),
                  tile,
                  pl.BlockSpec((1, 1, D), lambda c, b, *_: (c // (n_chunks // B), 0, 0)),
                  pl.BlockSpec((1, D), lambda c, b, *_: (0, 0))],
        out_specs=tile,
        scratch_shapes=[pltpu.VMEM((tc * SUBLANES, LANES), f32),
                        pltpu.VMEM((EXPERT_ROWS * words, LANES), jnp.uint32),
                        pltpu.VMEM((EXPERT_ROWS * SUBLANES, LANES), f32)],
    )
    out = pl.pallas_call(
        functools.partial(_moe_kernel, normalize=normalize),
        grid_spec=grid_spec,
        out_shape=jax.ShapeDtypeStruct((T, D), f32),
        compiler_params=_params(("arbitrary", "arbitrary")),
        name="moe",
    )(starts, counts, stok, sw, h2p, w_gate, w_up, w_down,
      x1.reshape(T, D), g2.reshape(B, 1, D), fg.reshape(1, D))
    return out.reshape(B, S, D)


def kernel(x, c, rel_bias, norm1_g, norm2_g, w_ada, b_ada, w_in, b_gate, conv_w, w_attn_out,
           w_conv_out, w_o, w_router_group, b_router_group, w_router_expert, b_router_expert,
           w_exp_gate, w_exp_up, w_exp_down, final_norm_g):
    B, S, D = x.shape
    depth = w_ada.shape[0]
    assert S % ROW_TILE == 0 and ROW_TILE % MOBA_BLOCK == 0
    own, adj = _bias_tiles(rel_bias)
    for l in range(depth):
        mod = _mod(c, w_ada[l], b_ada[l])
        sh1, sc1, g1, sh2, sc2, g2 = jnp.split(mod, 6, axis=-1)
        qT, k4, vT4, km, mc, ga = _inproj(x, norm1_g[l], sc1, sh1, w_in[l], b_gate[l], conv_w[l],
                                          w_conv_out[l])
        a, expert_w = _attention(qT, k4, vT4, km, own, adj, [w_exp_gate[l], w_exp_up[l], w_exp_down[l]])
        x1, h2p, route = _post(a, mc, ga, x, g1, sc2, sh2, norm2_g[l], w_attn_out[l], w_o[l],
                               w_router_group[l], b_router_group[l], w_router_expert[l],
                               b_router_expert[l])
        x = _moe(route, h2p, *expert_w, x1, g2, final_norm_g, normalize=(l + 1 == depth))
    return x
```

```python
import functools
import math

import numpy as np
import jax
import jax.numpy as jnp
from jax import lax
from jax.experimental import pallas as pl
from jax.experimental.pallas import tpu as pltpu

f32 = jnp.float32
bf16 = jnp.bfloat16

N_HEADS = 8
HEAD_DIM = 64
ATTN_WIDTH = N_HEADS * HEAD_DIM
MOBA_BLOCK = 256
MOBA_TOPK = 3
CONV_K = 3
N_BUCKETS = 32
MAX_EXACT = N_BUCKETS // 2
MAX_DISTANCE = 128
N_GROUPS = 4
EXPERTS_PER_GROUP = 8
N_EXPERTS = N_GROUPS * EXPERTS_PER_GROUP
TOP_K_EXPERTS = 2
NORM_EPS = 1e-6
NEG_INF = -1e30

LANES = 128
SUBLANES = 8
MOD_COLS = 1024
ROW_TILE = 1024
POST_SPLIT = 2
EXPERT_ROWS = 288
ROW_UNROLL = 8
MOE_CHUNKS = 4
FAR_BLOCKS_PER_ITER = 1
FAR_SUBSTEPS = 8
SUM_ROWS = 16
LOG2E = 1.4426950408889634
VMEM_LIMIT = 56 * 1024 * 1024


def _params(sem, vmem=VMEM_LIMIT):
    return pltpu.CompilerParams(dimension_semantics=sem, vmem_limit_bytes=vmem)


def _mod_kernel(c_ref, w_ref, b_ref, o_ref, *, batch):
    act = jax.nn.silu(c_ref[...]).T
    w = w_ref[...]
    rows = [jnp.sum(act[:, r:r + 1] * w, axis=0, keepdims=True) for r in range(batch)]
    rows.append(jnp.zeros((o_ref.shape[0] - batch, w.shape[1]), f32))
    o_ref[...] = jnp.concatenate(rows, axis=0) + b_ref[...]


def _mod(c, w_ada, b_ada):
    B, D = c.shape
    n_out = w_ada.shape[1]
    rows = SUBLANES
    cp = jnp.zeros((rows, D), f32).at[:B].set(c)
    tn = MOD_COLS
    out = pl.pallas_call(
        functools.partial(_mod_kernel, batch=B),
        grid=(n_out // tn,),
        in_specs=[pl.BlockSpec((rows, D), lambda j: (0, 0)),
                  pl.BlockSpec((D, tn), lambda j: (0, j)),
                  pl.BlockSpec((1, tn), lambda j: (0, j))],
        out_specs=pl.BlockSpec((rows, tn), lambda j: (0, j)),
        out_shape=jax.ShapeDtypeStruct((rows, n_out), f32),
        compiler_params=_params(("arbitrary",)),
        name="mod",
    )(cp, w_ada, b_ada.reshape(1, n_out))
    return out[:B]


def _bucket_thresholds():
    n = np.arange(2 * MOBA_BLOCK)

    def buckets(ft):
        nf = np.maximum(n, 1).astype(ft)
        large = MAX_EXACT + (np.log(nf / ft(MAX_EXACT)) / ft(math.log(MAX_DISTANCE / MAX_EXACT))
                             * ft(N_BUCKETS - MAX_EXACT)).astype(np.int32)
        return np.where(n < MAX_EXACT, n, np.minimum(large, N_BUCKETS - 1))

    b = buckets(np.float32)
    assert np.array_equal(b, buckets(np.float64)) and np.all(np.diff(b) >= 0)
    assert b[-1] == N_BUCKETS - 1 and b[MOBA_BLOCK + 1] == N_BUCKETS - 1
    return [int(np.argmax(b >= j)) for j in range(N_BUCKETS)]


def _bias_kernel(rb_ref, own_ref, adj_ref, *, thresholds):
    h = pl.program_id(0)
    kk = lax.broadcasted_iota(jnp.int32, (MOBA_BLOCK, MOBA_BLOCK), 0)
    qq = lax.broadcasted_iota(jnp.int32, (MOBA_BLOCK, MOBA_BLOCK), 1)
    far = rb_ref[N_BUCKETS - 1, h]

    def table(dist):
        acc = jnp.full(dist.shape, rb_ref[0, h], f32)
        for j in range(1, N_BUCKETS):
            acc = jnp.where(dist >= thresholds[j], rb_ref[j, h], acc)
        return (acc - far) * LOG2E

    own_ref[0] = table(qq - kk)
    adj_ref[0] = table(qq - kk + MOBA_BLOCK)


def _bias_tiles(rel_bias):
    shp = jax.ShapeDtypeStruct((N_HEADS, MOBA_BLOCK, MOBA_BLOCK), f32)
    spec = pl.BlockSpec((1, MOBA_BLOCK, MOBA_BLOCK), lambda h: (h, 0, 0))
    return pl.pallas_call(
        functools.partial(_bias_kernel, thresholds=_bucket_thresholds()),
        grid=(N_HEADS,),
        in_specs=[pl.BlockSpec(memory_space=pltpu.SMEM)],
        out_specs=(spec, spec),
        out_shape=(shp, shp),
        compiler_params=_params(("arbitrary",)),
        name="bias",
    )(rel_bias)


def _ada_norm(x, g, scale, shift):
    y = x * lax.rsqrt(jnp.mean(x * x, axis=-1, keepdims=True) + NORM_EPS)
    return (y * g) * (1.0 + scale) + shift


def _inproj_kernel(x_ref, g_ref, sc_ref, sh_ref, wqT_ref, wk_ref, wvT_ref, wc_ref, wgl_ref,
                   wco_ref, cw_ref, bg_ref,
                   qT_ref, k_ref, vT_ref, km_ref, mc_ref, ga_ref, ubuf):
    tm = x_ref.shape[1]
    cwid = wco_ref.shape[0]
    d = x_ref.shape[2]
    s = pl.program_id(1)
    hb = _ada_norm(x_ref[0], g_ref[...], sc_ref[0], sh_ref[0]).astype(bf16)

    nt = (((1,), (1,)), ((), ()))
    qT_ref[0] = lax.dot_general(wqT_ref[...], hb, nt, preferred_element_type=f32).astype(bf16)
    k = jnp.dot(hb, wk_ref[...], preferred_element_type=f32)
    vT = lax.dot_general(wvT_ref[...], hb, nt, preferred_element_type=f32).astype(bf16)
    for j in range(tm // MOBA_BLOCK):
        rows = slice(j * MOBA_BLOCK, (j + 1) * MOBA_BLOCK)
        k_ref[0, j] = k[rows].astype(bf16)
        vT_ref[0, j] = vT[:, rows]
        km_ref[0, 0, j:j + 1, :] = jnp.mean(k[rows], axis=0, keepdims=True)

    cbx = jnp.dot(hb, wc_ref[...], preferred_element_type=f32)
    u = cbx[:, :cwid] * cbx[:, 2 * cwid:]

    halo = SUBLANES

    @pl.when(s == 0)
    def _():
        ubuf[0:halo, :] = jnp.zeros((halo, cwid), f32)

    ubuf[halo:halo + tm, :] = u
    cw = cw_ref[...]
    taps = [ubuf[halo - back:halo - back + tm, :] for back in range(CONV_K - 1, 0, -1)] + [u]
    conv = cw[0:1] * taps[0]
    for j in range(1, CONV_K):
        conv = conv + cw[j:j + 1] * taps[j]
    ubuf[0:halo, :] = ubuf[tm:tm + halo, :]
    cv = (cbx[:, cwid:2 * cwid] * conv).astype(bf16)
    y_conv = jnp.dot(cv, wco_ref[...], preferred_element_type=f32)

    gl = jnp.dot(hb, wgl_ref[...], preferred_element_type=f32) + bg_ref[...]
    ga_ref[0] = jax.nn.sigmoid(gl[:, :d]).astype(bf16)
    mc_ref[0] = (jax.nn.sigmoid(gl[:, d:]) * y_conv).astype(bf16)


def _inproj(x, g, sc, sh, w_in, b_gate, conv_w, w_conv_out):
    B, S, D = x.shape
    tm = ROW_TILE
    nb = S // MOBA_BLOCK
    bpt = tm // MOBA_BLOCK
    aw = ATTN_WIDTH
    cwid = conv_w.shape[1]
    scale = HEAD_DIM ** -0.5 * LOG2E
    wqT = (w_in[:, :aw] * scale).T.astype(bf16)
    wk = w_in[:, aw:2 * aw].astype(bf16)
    wvT = w_in[:, 2 * aw:3 * aw].T.astype(bf16)
    wc = w_in[:, 3 * aw:3 * aw + 3 * cwid].astype(bf16)
    wgl = w_in[:, 3 * aw + 3 * cwid:].astype(bf16)
    wco = w_conv_out.astype(bf16)

    const = lambda shape: pl.BlockSpec(shape, lambda b, s: (0,) * len(shape))
    vec = pl.BlockSpec((1, 1, D), lambda b, s: (b, 0, 0))
    row = lambda width: pl.BlockSpec((1, tm, width), lambda b, s: (b, s, 0))
    outs = pl.pallas_call(
        _inproj_kernel,
        grid=(B, S // tm),
        in_specs=[row(D), const((1, D)), vec, vec,
                  const((aw, D)), const((D, aw)), const((aw, D)), const((D, 3 * cwid)),
                  const((D, 2 * D)), const((cwid, D)), const((CONV_K, cwid)), const((1, 2 * D))],
        out_specs=(pl.BlockSpec((1, aw, tm), lambda b, s: (b, 0, s)),
                   pl.BlockSpec((1, bpt, MOBA_BLOCK, aw), lambda b, s: (b, s, 0, 0)),
                   pl.BlockSpec((1, bpt, aw, MOBA_BLOCK), lambda b, s: (b, s, 0, 0)),
                   pl.BlockSpec((1, 1, bpt, aw), lambda b, s: (b, s, 0, 0)),
                   row(D), row(D)),
        out_shape=(jax.ShapeDtypeStruct((B, aw, S), bf16),
                   jax.ShapeDtypeStruct((B, nb, MOBA_BLOCK, aw), bf16),
                   jax.ShapeDtypeStruct((B, nb, aw, MOBA_BLOCK), bf16),
                   jax.ShapeDtypeStruct((B, S // tm, bpt, aw), f32),
                   jax.ShapeDtypeStruct((B, S, D), bf16),
                   jax.ShapeDtypeStruct((B, S, D), bf16)),
        scratch_shapes=[pltpu.VMEM((tm + SUBLANES, cwid), f32)],
        compiler_params=_params(("parallel", "arbitrary")),
        name="inproj",
    )(x, g.reshape(1, D), sc.reshape(B, 1, D), sh.reshape(B, 1, D),
      wqT, wk, wvT, wc, wgl, wco, conv_w, b_gate.reshape(1, 2 * D))
    qT, k4, vT4, km, mc, ga = outs
    return qT, k4, vT4, km.reshape(B, nb, aw), mc, ga


def _attn_kernel(qT_ref, k_ref, vT_ref, km_ref, own_ref, adj_ref, *side_refs):
    n_side = len(side_refs) // 2
    o_ref = side_refs[n_side]
    for src, dst in zip(side_refs[:n_side], side_refs[n_side + 1:]):
        dst[...] = src[...].astype(bf16)

    i = pl.program_id(1)
    bs = MOBA_BLOCK
    nb = km_ref.shape[1]
    n_heads = qT_ref.shape[1] // HEAD_DIM
    heads = range(n_heads)
    blk = lax.broadcasted_iota(jnp.int32, (nb, bs), 0)

    qcat = []
    for h in heads:
        pair, a = divmod(h, LANES // HEAD_DIM)
        qT = qT_ref[0, pair * LANES:(pair + 1) * LANES, :]
        row = lax.broadcasted_iota(jnp.int32, qT.shape, 0)
        qa = jnp.where((row >= a * HEAD_DIM) & (row < (a + 1) * HEAD_DIM), qT, jnp.zeros_like(qT))
        km = km_ref[0, :, pair * LANES:(pair + 1) * LANES]
        km_hi = km.astype(bf16)
        km_mid = (km - km_hi.astype(f32)).astype(bf16)
        km_lo = (km - km_hi.astype(f32) - km_mid.astype(f32)).astype(bf16)
        g = (jnp.dot(km_hi, qa, preferred_element_type=f32)
             + (jnp.dot(km_mid, qa, preferred_element_type=f32) + jnp.dot(km_lo, qa, preferred_element_type=f32)))
        g = jnp.where(blk < i, g, NEG_INF)
        fq = jnp.full((nb, bs), NEG_INF, f32)
        for _ in range(MOBA_TOPK):
            top = jnp.max(g, axis=0, keepdims=True)
            first = jnp.min(jnp.where(g == top, blk, nb), axis=0, keepdims=True)
            pick = blk == first
            fq = jnp.where(pick, 0.0, fq)
            g = jnp.where(pick, -jnp.inf, g)
        fq = jnp.where(blk < i, fq, 0.0).astype(bf16)
        pad = jnp.zeros((LANES - nb, bs), bf16)
        qcat.append(jnp.concatenate([qa, fq, pad], axis=0))

    lane = lax.broadcasted_iota(jnp.int32, (bs, LANES), 1)
    causal = (lax.broadcasted_iota(jnp.int32, (bs, bs), 0) <= lax.broadcasted_iota(jnp.int32, (bs, bs), 1))
    ones_rows = jnp.ones((SUM_ROWS, bs), bf16)

    def qk(blocks, h):
        pair = h // (LANES // HEAD_DIM)
        cols = slice(pair * LANES, (pair + 1) * LANES)
        return [jnp.dot(jnp.concatenate([k_ref[0, n, :, cols], jnp.where(lane == n, 1.0, 0.0).astype(bf16)], axis=1),
                        qcat[h], preferred_element_type=f32) for n, _ in blocks]

    def step(blocks, state, scores=None, next_blocks=None):
        if scores is None:
            scores = [qk(blocks, h) for h in heads]
        next_scores = []
        out = []
        for h in heads:
            sTs = []
            for (n, kind), sT in zip(blocks, scores[h]):
                if kind == "own":
                    sT = jnp.where(causal, own_ref[h] + sT, NEG_INF)
                elif kind == "adj":
                    sT = adj_ref[h] + sT
                sTs.append(sT)
            m_new = None if state is None else state[h][0]
            for sT in sTs:
                top = jnp.max(sT, axis=0, keepdims=True)
                m_new = top if m_new is None else jnp.maximum(m_new, top)
            acc = None if state is None else jnp.exp2(state[h][0] - m_new) * state[h][1]
            if next_blocks is not None:
                next_scores.append(qk(next_blocks, h))
            for (n, _), sT in zip(blocks, sTs):
                p = jnp.exp2(sT - m_new).astype(bf16)
                v_ext = jnp.concatenate([vT_ref[0, n, h * HEAD_DIM:(h + 1) * HEAD_DIM, :], ones_rows], axis=0)
                pv = jnp.dot(v_ext, p, preferred_element_type=f32)
                acc = pv if acc is None else acc + pv
            out.append((m_new, acc))
        return tuple(out), next_scores

    per = FAR_BLOCKS_PER_ITER

    def chained(first, n_sub, st, scores=None):
        blocks_of = lambda u: [(first + u * per + j, "far") for j in range(per)]
        for u in range(n_sub):
            st, scores = step(blocks_of(u), st, scores, blocks_of(u + 1) if u + 1 < n_sub else None)
        return st

    def opening_with_far():
        st, scores = step([(i, "own"), (i - 1, "adj")], None, None, [(j, "far") for j in range(per)])
        return chained(0, FAR_SUBSTEPS - 1, st, scores)

    span = per * FAR_SUBSTEPS
    joined = i - 1 >= span - per
    state = lax.cond(joined, opening_with_far,
                     lambda: lax.cond(i >= 1,
                                      lambda: step([(i, "own"), (i - 1, "adj")], None)[0],
                                      lambda: step([(i, "own")], None)[0]))
    base = jnp.where(joined, span - per, 0)
    n_far = jnp.maximum(i - 1, 0) - base
    state = lax.fori_loop(0, n_far // span, lambda it, st: chained(base + it * span, FAR_SUBSTEPS, st), state)
    done = base + n_far - n_far % span
    size = span // 2
    while size >= 1:
        take = (n_far % span) & size
        if size >= per:
            region = lambda st, d=done, sz=size: chained(d, sz // per, st)
        else:
            region = lambda st, d=done, sz=size: step([(d + j, "far") for j in range(sz)], st)[0]
        state = lax.cond(take > 0, region, lambda st: st, state)
        done = done + take
        size //= 2
    outT = jnp.concatenate([acc[:HEAD_DIM] / acc[HEAD_DIM:HEAD_DIM + 1] for (_, acc) in state], axis=0)
    o_ref[0] = outT.T.astype(bf16)


def _attention(qT, k4, vT4, km, own, adj, side_weights):
    B, aw, S = qT.shape
    nb = S // MOBA_BLOCK
    assert MOBA_TOPK < nb <= LANES
    n_heads = aw // HEAD_DIM
    steps = B * nb
    whole = lambda shape: pl.BlockSpec(shape, lambda b, i: (b,) + (0,) * (len(shape) - 1))
    table = pl.BlockSpec((n_heads, MOBA_BLOCK, MOBA_BLOCK), lambda b, i: (0, 0, 0))
    slabs = [w.reshape(steps, -1, w.shape[-1]) for w in side_weights]
    slab_specs = [pl.BlockSpec((1,) + w.shape[1:], lambda b, i: (b * nb + i, 0, 0)) for w in slabs]
    outs = pl.pallas_call(
        _attn_kernel,
        grid=(B, nb),
        in_specs=[pl.BlockSpec((1, aw, MOBA_BLOCK), lambda b, i: (b, 0, i)),
                  whole((1, nb, MOBA_BLOCK, aw)), whole((1, nb, aw, MOBA_BLOCK)), whole((1, nb, aw)),
                  table, table] + slab_specs,
        out_specs=[pl.BlockSpec((1, MOBA_BLOCK, aw), lambda b, i: (b, i, 0))] + slab_specs,
        out_shape=[jax.ShapeDtypeStruct((B, S, aw), bf16)]
                  + [jax.ShapeDtypeStruct(w.shape, bf16) for w in slabs],
        compiler_params=_params(("parallel", "arbitrary")),
        name="attn",
    )(qT, k4, vT4, km, own, adj, *slabs)
    return outs[0], [o.reshape(w.shape) for o, w in zip(outs[1:], side_weights)]


def _post_kernel(a_ref, mc_ref, ga_ref, x_ref, g1_ref, sc_ref, sh_ref, ng_ref,
                 wao_ref, wo_ref, wr_ref, br_ref, x1_ref, h2p_ref, route_ref):
    tm = x_ref.shape[1]
    d = x_ref.shape[2]
    sub = tm // POST_SPLIT
    lane = lax.broadcasted_iota(jnp.int32, (sub, LANES), 1)
    ninf = -jnp.inf
    for t in range(POST_SPLIT):
        rows = slice(t * sub, (t + 1) * sub)
        ya = jnp.dot(a_ref[0, rows, :], wao_ref[...], preferred_element_type=f32)
        merged = ga_ref[0, rows, :].astype(f32) * ya + mc_ref[0, rows, :].astype(f32)
        z = jnp.dot(merged.astype(bf16), wo_ref[...], preferred_element_type=f32)
        x1 = x_ref[0, rows, :] + g1_ref[0] * z
        x1_ref[0, rows, :] = x1
        h2 = _ada_norm(x1, ng_ref[...], sc_ref[0], sh_ref[0])
        h_hi = h2.astype(bf16)
        h_hi32 = h_hi.astype(f32)

        bits = pltpu.bitcast(h_hi32, jnp.uint32)
        packed = (bits[:, :d // 2] >> 16) | (bits[:, d // 2:] & jnp.uint32(0xFFFF0000))
        chunks = packed.shape[1] // LANES
        for j in range(chunks):
            h2p_ref[pl.ds(t * sub * chunks + j, sub, stride=chunks), :] = packed[:, j * LANES:(j + 1) * LANES]

        h_lo = (h2 - h_hi32).astype(bf16)
        hi_both = jnp.dot(h_hi, wr_ref[...], preferred_element_type=f32)
        logits = (hi_both[:, :LANES]
                  + (jnp.dot(h_lo, wr_ref[:, :LANES], preferred_element_type=f32) + hi_both[:, LANES:])) + br_ref[...]
        gl = jnp.where(lane < N_GROUPS, logits, ninf)
        gmax = jnp.max(gl, axis=1, keepdims=True)
        g_sel = jnp.min(jnp.where(gl == gmax, lane, LANES), axis=1, keepdims=True)
        g_w = 1.0 / jnp.sum(jnp.exp(gl - gmax), axis=1, keepdims=True)
        off = lane - (N_GROUPS + EXPERTS_PER_GROUP * g_sel)
        in_group = (off >= 0) & (off < EXPERTS_PER_GROUP)
        el = jnp.where(in_group, logits, ninf)
        ex = jnp.exp(el - jnp.max(el, axis=1, keepdims=True))
        ep = jnp.where(in_group, ex / jnp.sum(ex, axis=1, keepdims=True), -1.0)
        p1 = jnp.max(ep, axis=1, keepdims=True)
        i1 = jnp.min(jnp.where(ep == p1, lane, LANES), axis=1, keepdims=True)
        ep2 = jnp.where(lane == i1, -1.0, ep)
        p2 = jnp.max(ep2, axis=1, keepdims=True)
        i2 = jnp.min(jnp.where(ep2 == p2, lane, LANES), axis=1, keepdims=True)
        tot = p1 + p2
        w1 = p1 / tot * g_w
        w2 = p2 / tot * g_w
        e1 = (i1 - N_GROUPS).astype(f32)
        e2 = (i2 - N_GROUPS).astype(f32)
        route = jnp.where(lane == 0, e1, jnp.where(lane == 1, e2,
                          jnp.where(lane == 2, w1, jnp.where(lane == 3, w2, 0.0))))
        route_ref[:, rows] = route.T[:SUBLANES, :]


def _post(a, mc, ga, x, g1, sc2, sh2, ng, w_attn_out, w_o, w_rg, b_rg, w_re, b_re):
    B, S, D = x.shape
    tm = ROW_TILE
    aw = a.shape[2]
    nt = S // tm
    wr = jnp.zeros((D, LANES), f32).at[:, :N_GROUPS].set(w_rg).at[:, N_GROUPS:N_GROUPS + N_EXPERTS].set(w_re)
    br = jnp.zeros((1, LANES), f32).at[0, :N_GROUPS].set(b_rg).at[0, N_GROUPS:N_GROUPS + N_EXPERTS].set(b_re)
    wr_hi = wr.astype(bf16)
    wr_split = jnp.concatenate([wr_hi, (wr - wr_hi.astype(f32)).astype(bf16)], axis=1)
    const = lambda shape: pl.BlockSpec(shape, lambda b, s: (0,) * len(shape))
    vec = pl.BlockSpec((1, 1, D), lambda b, s: (b, 0, 0))
    row = lambda width: pl.BlockSpec((1, tm, width), lambda b, s: (b, s, 0))
    flat = lambda width, rows_per_token=1: pl.BlockSpec((tm * rows_per_token, width), lambda b, s: (b * nt + s, 0))
    words = D // 2 // LANES
    return pl.pallas_call(
        _post_kernel,
        grid=(B, nt),
        in_specs=[row(aw), row(D), row(D), row(D), vec, vec, vec, const((1, D)),
                  const((aw, D)), const((D, D)), const((D, 2 * LANES)), const((1, LANES))],
        out_specs=(row(D), flat(LANES, words), pl.BlockSpec((SUBLANES, tm), lambda b, s: (0, b * nt + s))),
        out_shape=(jax.ShapeDtypeStruct((B, S, D), f32),
                   jax.ShapeDtypeStruct((B * S * words, LANES), jnp.uint32),
                   jax.ShapeDtypeStruct((SUBLANES, B * S), f32)),
        compiler_params=_params(("parallel", "parallel")),
        name="post",
    )(a, mc, ga, x, g1.reshape(B, 1, D), sc2.reshape(B, 1, D), sh2.reshape(B, 1, D), ng.reshape(1, D),
      w_attn_out.astype(bf16), w_o.astype(bf16), wr_split, br)


def _dispatch_tables(route, n_chunks):
    T = route.shape[1]
    tc = T // n_chunks
    na = tc * TOP_K_EXPERTS
    per_chunk = lambda rows: rows.reshape(TOP_K_EXPERTS, n_chunks, tc).transpose(1, 0, 2).reshape(n_chunks, na)
    e = per_chunk(route[:TOP_K_EXPERTS]).astype(jnp.int32)
    w = per_chunk(route[TOP_K_EXPERTS:2 * TOP_K_EXPERTS])
    tok = jnp.broadcast_to(jnp.arange(na, dtype=jnp.int32) % tc, e.shape)
    _, stok, sw = lax.sort((e, tok, w), dimension=1, num_keys=1, is_stable=True)
    counts = jnp.sum((e[:, None, :] == jnp.arange(N_EXPERTS)[None, :, None]).astype(jnp.int32), axis=2)
    starts = jnp.cumsum(counts, axis=1) - counts
    flat = lambda t: t.reshape(-1).astype(jnp.int32)
    return flat(starts), flat(counts), stok.reshape(n_chunks, 1, na), sw.reshape(n_chunks, 1, na)


def _moe_kernel(start_ref, count_ref, stok_ref, sw_ref, h2p_ref, wg_ref, wu_ref, wd_ref,
                x1_ref, g2_ref, fg_ref, o_ref, yacc, xs, outs, *, normalize):
    c = pl.program_id(0)
    b = pl.program_id(1)
    block_rows = outs.shape[0] // SUBLANES
    words = xs.shape[0] // block_rows
    half = words * LANES
    run = c * N_EXPERTS + jnp.minimum(b, N_EXPERTS - 1)
    count = jnp.where(b < N_EXPERTS, count_ref[run], 0)
    first = start_ref[run]

    @pl.when(b == 0)
    def _():
        yacc[...] = jnp.zeros(yacc.shape, f32)

    @pl.when((b == 0) & (c == 0))
    def _():
        xs[...] = jnp.zeros(xs.shape, jnp.uint32)

    def token_rows(index, rows_per_token):
        return pl.ds(pl.multiple_of(index * rows_per_token, rows_per_token), rows_per_token)

    def block(blk, carry):
        st = first + blk * block_rows
        n = jnp.minimum(count - blk * block_rows, block_rows)

        def for_rows(rows_fn):
            full = n // ROW_UNROLL

            def group(g, carry):
                rows_fn([g * ROW_UNROLL + j for j in range(ROW_UNROLL)], st + g * ROW_UNROLL)
                return carry

            def single(j, carry):
                rows_fn([full * ROW_UNROLL + j], st + full * ROW_UNROLL + j)
                return carry

            lax.fori_loop(0, full, group, 0)
            lax.fori_loop(0, n - full * ROW_UNROLL, single, 0)

        def gather(rs, i0):
            for k, r in enumerate(rs):
                xs[token_rows(r, words), :] = h2p_ref[token_rows(stok_ref[0, 0, i0 + k], words), :]

        for_rows(gather)
        lo, hi = [], []
        for j in range(words):
            packed = xs[pl.ds(j, block_rows, stride=words), :]
            lo.append(pltpu.bitcast(packed << 16, f32).astype(bf16))
            hi.append(pltpu.bitcast(packed & jnp.uint32(0xFFFF0000), f32).astype(bf16))
        x_lo = jnp.concatenate(lo, axis=1)
        x_hi = jnp.concatenate(hi, axis=1)

        def up(w_ref):
            return (jnp.dot(x_lo, w_ref[0, :half, :], preferred_element_type=f32)
                    + jnp.dot(x_hi, w_ref[0, half:, :], preferred_element_type=f32))

        hidden = (jax.nn.silu(up(wg_ref)) * up(wu_ref)).astype(bf16)
        out = jnp.dot(hidden, wd_ref[0], preferred_element_type=f32)
        for j in range(SUBLANES):
            outs[pl.ds(j, block_rows, stride=SUBLANES), :] = out[:, j * LANES:(j + 1) * LANES]

        def scatter(rs, i0):
            toks = [stok_ref[0, 0, i0 + k] for k in range(len(rs))]
            acc = [yacc[token_rows(tok, SUBLANES), :] for tok in toks]
            for k, (r, tok, a) in enumerate(zip(rs, toks, acc)):
                yacc[token_rows(tok, SUBLANES), :] = a + sw_ref[0, 0, i0 + k] * outs[token_rows(r, SUBLANES), :]

        for_rows(scatter)
        return carry

    lax.fori_loop(0, (count + block_rows - 1) // block_rows, block, 0)

    @pl.when(b >= N_EXPERTS)
    def _():
        rows = o_ref.shape[0]
        start = pl.multiple_of((b - N_EXPERTS) * rows * SUBLANES, rows * SUBLANES)
        y = jnp.concatenate([yacc[pl.ds(start + j, rows, stride=SUBLANES), :] for j in range(SUBLANES)], axis=1)
        x2 = x1_ref[...] + g2_ref[0] * y
        if normalize:
            x2 = (x2 * lax.rsqrt(jnp.mean(x2 * x2, axis=-1, keepdims=True) + NORM_EPS)) * fg_ref[...]
        o_ref[...] = x2


def _moe(route, h2p, w_gate, w_up, w_down, x1, g2, fg, normalize):
    B, S, D = x1.shape
    T = B * S
    words = h2p.shape[0] // T
    assert D == words * 2 * LANES == SUBLANES * LANES
    de = w_gate.shape[2]
    n_chunks = MOE_CHUNKS
    tc = T // n_chunks
    fin_rows = ROW_TILE // 2
    n_fin = tc // fin_rows
    assert n_chunks % B == 0 and tc % fin_rows == 0 and w_gate.shape[0] == N_EXPERTS
    starts, counts, stok, sw = _dispatch_tables(route, n_chunks)
    na = stok.shape[2]
    wmap = lambda c, b, *_: (jnp.minimum(b, N_EXPERTS - 1), 0, 0)
    smem = lambda: pl.BlockSpec((1, 1, na), lambda c, b, *_: (c, 0, 0), memory_space=pltpu.SMEM)
    tile = pl.BlockSpec((fin_rows, D), lambda c, b, *_: (c * n_fin + jnp.maximum(b - N_EXPERTS, 0), 0))
    grid_spec = pltpu.PrefetchScalarGridSpec(
        num_scalar_prefetch=2,
        grid=(n_chunks, N_EXPERTS + n_fin),
        in_specs=[smem(), smem(),
                  pl.BlockSpec((tc * words, LANES), lambda c, b, *_: (c, 0)),
                  pl.BlockSpec((1, D, de), wmap),
                  pl.BlockSpec((1, D, de), wmap),
                  pl.BlockSpec((1, de, D), wmap),
                  tile,
                  pl.BlockSpec((1, 1, D), lambda c, b, *_: (c // (n_chunks // B), 0, 0)),
                  pl.BlockSpec((1, D), lambda c, b, *_: (0, 0))],
        out_specs=tile,
        scratch_shapes=[pltpu.VMEM((tc * SUBLANES, LANES), f32),
                        pltpu.VMEM((EXPERT_ROWS * words, LANES), jnp.uint32),
                        pltpu.VMEM((EXPERT_ROWS * SUBLANES, LANES), f32)],
    )
    out = pl.pallas_call(
        functools.partial(_moe_kernel, normalize=normalize),
        grid_spec=grid_spec,
        out_shape=jax.ShapeDtypeStruct((T, D), f32),
        compiler_params=_params(("arbitrary", "arbitrary")),
        name="moe",
    )(starts, counts, stok, sw, h2p, w_gate, w_up, w_down,
      x1.reshape(T, D), g2.reshape(B, 1, D), fg.reshape(1, D))
    return out.reshape(B, S, D)


def kernel(x, c, rel_bias, norm1_g, norm2_g, w_ada, b_ada, w_in, b_gate, conv_w, w_attn_out,
           w_conv_out, w_o, w_router_group, b_router_group, w_router_expert, b_router_expert,
           w_exp_gate, w_exp_up, w_exp_down, final_norm_g):
    B, S, D = x.shape
    depth = w_ada.shape[0]
    assert S % ROW_TILE == 0 and ROW_TILE % MOBA_BLOCK == 0
    own, adj = _bias_tiles(rel_bias)
    for l in range(depth):
        mod = _mod(c, w_ada[l], b_ada[l])
        sh1, sc1, g1, sh2, sc2, g2 = jnp.split(mod, 6, axis=-1)
        qT, k4, vT4, km, mc, ga = _inproj(x, norm1_g[l], sc1, sh1, w_in[l], b_gate[l], conv_w[l],
                                          w_conv_out[l])
        a, expert_w = _attention(qT, k4, vT4, km, own, adj, [w_exp_gate[l], w_exp_up[l], w_exp_down[l]])
        x1, h2p, route = _post(a, mc, ga, x, g1, sc2, sh2, norm2_g[l], w_attn_out[l], w_o[l],
                               w_router_group[l], b_router_group[l], w_router_expert[l],
                               b_router_expert[l])
        x = _moe(route, h2p, *expert_w, x1, g2, final_norm_g, normalize=(l + 1 == depth))
    return x
```

```python
import functools
import math

import numpy as np
import jax
import jax.numpy as jnp
from jax import lax
from jax.experimental import pallas as pl
from jax.experimental.pallas import tpu as pltpu

f32 = jnp.float32
bf16 = jnp.bfloat16

N_HEADS = 8
HEAD_DIM = 64
ATTN_WIDTH = N_HEADS * HEAD_DIM
MOBA_BLOCK = 256
MOBA_TOPK = 3
CONV_K = 3
N_BUCKETS = 32
MAX_EXACT = N_BUCKETS // 2
MAX_DISTANCE = 128
N_GROUPS = 4
EXPERTS_PER_GROUP = 8
N_EXPERTS = N_GROUPS * EXPERTS_PER_GROUP
TOP_K_EXPERTS = 2
NORM_EPS = 1e-6
NEG_INF = -1e30

LANES = 128
SUBLANES = 8
MOD_COLS = 1024
ROW_TILE = 1024
POST_SPLIT = 2
EXPERT_ROWS = 288
ROW_UNROLL = 8
MOE_CHUNKS = 4
FAR_BLOCKS_PER_ITER = 1
FAR_SUBSTEPS = 8
SUM_ROWS = 16
LOG2E = 1.4426950408889634
VMEM_LIMIT = 56 * 1024 * 1024


def _params(sem, vmem=VMEM_LIMIT):
    return pltpu.CompilerParams(dimension_semantics=sem, vmem_limit_bytes=vmem)


def _mod_kernel(c_ref, w_ref, b_ref, o_ref, *, batch):
    act = jax.nn.silu(c_ref[...]).T
    w = w_ref[...]
    rows = [jnp.sum(act[:, r:r + 1] * w, axis=0, keepdims=True) for r in range(batch)]
    rows.append(jnp.zeros((o_ref.shape[0] - batch, w.shape[1]), f32))
    o_ref[...] = jnp.concatenate(rows, axis=0) + b_ref[...]


def _mod(c, w_ada, b_ada):
    B, D = c.shape
    n_out = w_ada.shape[1]
    rows = SUBLANES
    cp = jnp.zeros((rows, D), f32).at[:B].set(c)
    tn = MOD_COLS
    out = pl.pallas_call(
        functools.partial(_mod_kernel, batch=B),
        grid=(n_out // tn,),
        in_specs=[pl.BlockSpec((rows, D), lambda j: (0, 0)),
                  pl.BlockSpec((D, tn), lambda j: (0, j)),
                  pl.BlockSpec((1, tn), lambda j: (0, j))],
        out_specs=pl.BlockSpec((rows, tn), lambda j: (0, j)),
        out_shape=jax.ShapeDtypeStruct((rows, n_out), f32),
        compiler_params=_params(("arbitrary",)),
        name="mod",
    )(cp, w_ada, b_ada.reshape(1, n_out))
    return out[:B]


def _bucket_thresholds():
    n = np.arange(2 * MOBA_BLOCK)

    def buckets(ft):
        nf = np.maximum(n, 1).astype(ft)
        large = MAX_EXACT + (np.log(nf / ft(MAX_EXACT)) / ft(math.log(MAX_DISTANCE / MAX_EXACT))
                             * ft(N_BUCKETS - MAX_EXACT)).astype(np.int32)
        return np.where(n < MAX_EXACT, n, np.minimum(large, N_BUCKETS - 1))

    b = buckets(np.float32)
    assert np.array_equal(b, buckets(np.float64)) and np.all(np.diff(b) >= 0)
    assert b[-1] == N_BUCKETS - 1 and b[MOBA_BLOCK + 1] == N_BUCKETS - 1
    return [int(np.argmax(b >= j)) for j in range(N_BUCKETS)]


def _bias_kernel(rb_ref, own_ref, adj_ref, *, thresholds):
    h = pl.program_id(0)
    kk = lax.broadcasted_iota(jnp.int32, (MOBA_BLOCK, MOBA_BLOCK), 0)
    qq = lax.broadcasted_iota(jnp.int32, (MOBA_BLOCK, MOBA_BLOCK), 1)
    far = rb_ref[N_BUCKETS - 1, h]

    def table(dist):
        acc = jnp.full(dist.shape, rb_ref[0, h], f32)
        for j in range(1, N_BUCKETS):
            acc = jnp.where(dist >= thresholds[j], rb_ref[j, h], acc)
        return (acc - far) * LOG2E

    own_ref[0] = table(qq - kk)
    adj_ref[0] = table(qq - kk + MOBA_BLOCK)


def _bias_tiles(rel_bias):
    shp = jax.ShapeDtypeStruct((N_HEADS, MOBA_BLOCK, MOBA_BLOCK), f32)
    spec = pl.BlockSpec((1, MOBA_BLOCK, MOBA_BLOCK), lambda h: (h, 0, 0))
    return pl.pallas_call(
        functools.partial(_bias_kernel, thresholds=_bucket_thresholds()),
        grid=(N_HEADS,),
        in_specs=[pl.BlockSpec(memory_space=pltpu.SMEM)],
        out_specs=(spec, spec),
        out_shape=(shp, shp),
        compiler_params=_params(("arbitrary",)),
        name="bias",
    )(rel_bias)


def _ada_norm(x, g, scale, shift):
    y = x * lax.rsqrt(jnp.mean(x * x, axis=-1, keepdims=True) + NORM_EPS)
    return (y * g) * (1.0 + scale) + shift


def _inproj_kernel(x_ref, g_ref, sc_ref, sh_ref, wqT_ref, wk_ref, wvT_ref, wc_ref, wgl_ref,
                   wco_ref, cw_ref, bg_ref,
                   qT_ref, k_ref, vT_ref, km_ref, mc_ref, ga_ref, ubuf):
    tm = x_ref.shape[1]
    cwid = wco_ref.shape[0]
    d = x_ref.shape[2]
    s = pl.program_id(1)
    hb = _ada_norm(x_ref[0], g_ref[...], sc_ref[0], sh_ref[0]).astype(bf16)

    nt = (((1,), (1,)), ((), ()))
    qT_ref[0] = lax.dot_general(wqT_ref[...], hb, nt, preferred_element_type=f32).astype(bf16)
    k = jnp.dot(hb, wk_ref[...], preferred_element_type=f32)
    vT = lax.dot_general(wvT_ref[...], hb, nt, preferred_element_type=f32).astype(bf16)
    for j in range(tm // MOBA_BLOCK):
        rows = slice(j * MOBA_BLOCK, (j + 1) * MOBA_BLOCK)
        k_ref[0, j] = k[rows].astype(bf16)
        vT_ref[0, j] = vT[:, rows]
        km_ref[0, 0, j:j + 1, :] = jnp.mean(k[rows], axis=0, keepdims=True)

    cbx = jnp.dot(hb, wc_ref[...], preferred_element_type=f32)
    u = cbx[:, :cwid] * cbx[:, 2 * cwid:]

    halo = SUBLANES

    @pl.when(s == 0)
    def _():
        ubuf[0:halo, :] = jnp.zeros((halo, cwid), f32)

    ubuf[halo:halo + tm, :] = u
    cw = cw_ref[...]
    taps = [ubuf[halo - back:halo - back + tm, :] for back in range(CONV_K - 1, 0, -1)] + [u]
    conv = cw[0:1] * taps[0]
    for j in range(1, CONV_K):
        conv = conv + cw[j:j + 1] * taps[j]
    ubuf[0:halo, :] = ubuf[tm:tm + halo, :]
    cv = (cbx[:, cwid:2 * cwid] * conv).astype(bf16)
    y_conv = jnp.dot(cv, wco_ref[...], preferred_element_type=f32)

    gl = jnp.dot(hb, wgl_ref[...], preferred_element_type=f32) + bg_ref[...]
    ga_ref[0] = jax.nn.sigmoid(gl[:, :d]).astype(bf16)
    mc_ref[0] = (jax.nn.sigmoid(gl[:, d:]) * y_conv).astype(bf16)


def _inproj(x, g, sc, sh, w_in, b_gate, conv_w, w_conv_out):
    B, S, D = x.shape
    tm = ROW_TILE
    nb = S // MOBA_BLOCK
    bpt = tm // MOBA_BLOCK
    aw = ATTN_WIDTH
    cwid = conv_w.shape[1]
    scale = HEAD_DIM ** -0.5 * LOG2E
    wqT = (w_in[:, :aw] * scale).T.astype(bf16)
    wk = w_in[:, aw:2 * aw].astype(bf16)
    wvT = w_in[:, 2 * aw:3 * aw].T.astype(bf16)
    wc = w_in[:, 3 * aw:3 * aw + 3 * cwid].astype(bf16)
    wgl = w_in[:, 3 * aw + 3 * cwid:].astype(bf16)
    wco = w_conv_out.astype(bf16)

    const = lambda shape: pl.BlockSpec(shape, lambda b, s: (0,) * len(shape))
    vec = pl.BlockSpec((1, 1, D), lambda b, s: (b, 0, 0))
    row = lambda width: pl.BlockSpec((1, tm, width), lambda b, s: (b, s, 0))
    outs = pl.pallas_call(
        _inproj_kernel,
        grid=(B, S // tm),
        in_specs=[row(D), const((1, D)), vec, vec,
                  const((aw, D)), const((D, aw)), const((aw, D)), const((D, 3 * cwid)),
                  const((D, 2 * D)), const((cwid, D)), const((CONV_K, cwid)), const((1, 2 * D))],
        out_specs=(pl.BlockSpec((1, aw, tm), lambda b, s: (b, 0, s)),
                   pl.BlockSpec((1, bpt, MOBA_BLOCK, aw), lambda b, s: (b, s, 0, 0)),
                   pl.BlockSpec((1, bpt, aw, MOBA_BLOCK), lambda b, s: (b, s, 0, 0)),
                   pl.BlockSpec((1, 1, bpt, aw), lambda b, s: (b, s, 0, 0)),
                   row(D), row(D)),
        out_shape=(jax.ShapeDtypeStruct((B, aw, S), bf16),
                   jax.ShapeDtypeStruct((B, nb, MOBA_BLOCK, aw), bf16),
                   jax.ShapeDtypeStruct((B, nb, aw, MOBA_BLOCK), bf16),
                   jax.ShapeDtypeStruct((B, S // tm, bpt, aw), f32),
                   jax.ShapeDtypeStruct((B, S, D), bf16),
                   jax.ShapeDtypeStruct((B, S, D), bf16)),
        scratch_shapes=[pltpu.VMEM((tm + SUBLANES, cwid), f32)],
        compiler_params=_params(("parallel", "arbitrary")),
        name="inproj",
    )(x, g.reshape(1, D), sc.reshape(B, 1, D), sh.reshape(B, 1, D),
      wqT, wk, wvT, wc, wgl, wco, conv_w, b_gate.reshape(1, 2 * D))
    qT, k4, vT4, km, mc, ga = outs
    return qT, k4, vT4, km.reshape(B, nb, aw), mc, ga


def _attn_kernel(qT_ref, k_ref, vT_ref, km_ref, own_ref, adj_ref, *side_refs):
    n_side = len(side_refs) // 2
    o_ref = side_refs[n_side]
    for src, dst in zip(side_refs[:n_side], side_refs[n_side + 1:]):
        dst[...] = src[...].astype(bf16)

    i = pl.program_id(1)
    bs = MOBA_BLOCK
    nb = km_ref.shape[1]
    n_heads = qT_ref.shape[1] // HEAD_DIM
    heads = range(n_heads)
    blk = lax.broadcasted_iota(jnp.int32, (nb, bs), 0)

    qcat = []
    for h in heads:
        pair, a = divmod(h, LANES // HEAD_DIM)
        qT = qT_ref[0, pair * LANES:(pair + 1) * LANES, :]
        row = lax.broadcasted_iota(jnp.int32, qT.shape, 0)
        qa = jnp.where((row >= a * HEAD_DIM) & (row < (a + 1) * HEAD_DIM), qT, jnp.zeros_like(qT))
        km = km_ref[0, :, pair * LANES:(pair + 1) * LANES]
        km_hi = km.astype(bf16)
        km_mid = (km - km_hi.astype(f32)).astype(bf16)
        km_lo = (km - km_hi.astype(f32) - km_mid.astype(f32)).astype(bf16)
        g = (jnp.dot(km_hi, qa, preferred_element_type=f32)
             + (jnp.dot(km_mid, qa, preferred_element_type=f32) + jnp.dot(km_lo, qa, preferred_element_type=f32)))
        g = jnp.where(blk < i, g, NEG_INF)
        fq = jnp.full((nb, bs), NEG_INF, f32)
        for _ in range(MOBA_TOPK):
            top = jnp.max(g, axis=0, keepdims=True)
            first = jnp.min(jnp.where(g == top, blk, nb), axis=0, keepdims=True)
            pick = blk == first
            fq = jnp.where(pick, 0.0, fq)
            g = jnp.where(pick, -jnp.inf, g)
        fq = jnp.where(blk < i, fq, 0.0).astype(bf16)
        pad = jnp.zeros((LANES - nb, bs), bf16)
        qcat.append(jnp.concatenate([qa, fq, pad], axis=0))

    lane = lax.broadcasted_iota(jnp.int32, (bs, LANES), 1)
    causal = (lax.broadcasted_iota(jnp.int32, (bs, bs), 0) <= lax.broadcasted_iota(jnp.int32, (bs, bs), 1))
    ones_rows = jnp.ones((SUM_ROWS, bs), bf16)

    def qk(blocks, h):
        pair = h // (LANES // HEAD_DIM)
        cols = slice(pair * LANES, (pair + 1) * LANES)
        return [jnp.dot(jnp.concatenate([k_ref[0, n, :, cols], jnp.where(lane == n, 1.0, 0.0).astype(bf16)], axis=1),
                        qcat[h], preferred_element_type=f32) for n, _ in blocks]

    def step(blocks, state, scores=None, next_blocks=None):
        if scores is None:
            scores = [qk(blocks, h) for h in heads]
        next_scores = []
        out = []
        for h in heads:
            sTs = []
            for (n, kind), sT in zip(blocks, scores[h]):
                if kind == "own":
                    sT = jnp.where(causal, own_ref[h] + sT, NEG_INF)
                elif kind == "adj":
                    sT = adj_ref[h] + sT
                sTs.append(sT)
            m_new = None if state is None else state[h][0]
            for sT in sTs:
                top = jnp.max(sT, axis=0, keepdims=True)
                m_new = top if m_new is None else jnp.maximum(m_new, top)
            acc = None if state is None else jnp.exp2(state[h][0] - m_new) * state[h][1]
            if next_blocks is not None:
                next_scores.append(qk(next_blocks, h))
            for (n, _), sT in zip(blocks, sTs):
                p = jnp.exp2(sT - m_new).astype(bf16)
                v_ext = jnp.concatenate([vT_ref[0, n, h * HEAD_DIM:(h + 1) * HEAD_DIM, :], ones_rows], axis=0)
                pv = jnp.dot(v_ext, p, preferred_element_type=f32)
                acc = pv if acc is None else acc + pv
            out.append((m_new, acc))
        return tuple(out), next_scores

    per = FAR_BLOCKS_PER_ITER

    def chained(first, n_sub, st, scores=None):
        blocks_of = lambda u: [(first + u * per + j, "far") for j in range(per)]
        for u in range(n_sub):
            st, scores = step(blocks_of(u), st, scores, blocks_of(u + 1) if u + 1 < n_sub else None)
        return st

    def own_then(next_blocks):
        return step([(i, "own")], None, None, next_blocks)

    def opening():
        return step([(i - 1, "adj")], *own_then([(i - 1, "adj")]))[0]

    def opening_with_far():
        st, scores = step([(i - 1, "adj")], *own_then([(i - 1, "adj")]), [(j, "far") for j in range(per)])
        return chained(0, FAR_SUBSTEPS - 1, st, scores)

    span = per * FAR_SUBSTEPS
    joined = i - 1 >= span - per
    state = lax.cond(joined, opening_with_far,
                     lambda: lax.cond(i >= 1, opening, lambda: own_then(None)[0]))
    base = jnp.where(joined, span - per, 0)
    n_far = jnp.maximum(i - 1, 0) - base
    state = lax.fori_loop(0, n_far // span, lambda it, st: chained(base + it * span, FAR_SUBSTEPS, st), state)
    done = base + n_far - n_far % span
    size = span // 2
    while size >= 1:
        take = (n_far % span) & size
        if size >= per:
            region = lambda st, d=done, sz=size: chained(d, sz // per, st)
        else:
            region = lambda st, d=done, sz=size: step([(d + j, "far") for j in range(sz)], st)[0]
        state = lax.cond(take > 0, region, lambda st: st, state)
        done = done + take
        size //= 2
    outT = jnp.concatenate([acc[:HEAD_DIM] / acc[HEAD_DIM:HEAD_DIM + 1] for (_, acc) in state], axis=0)
    o_ref[0] = outT.T.astype(bf16)


def _attention(qT, k4, vT4, km, own, adj, side_weights):
    B, aw, S = qT.shape
    nb = S // MOBA_BLOCK
    assert MOBA_TOPK < nb <= LANES
    n_heads = aw // HEAD_DIM
    steps = B * nb
    whole = lambda shape: pl.BlockSpec(shape, lambda b, i: (b,) + (0,) * (len(shape) - 1))
    table = pl.BlockSpec((n_heads, MOBA_BLOCK, MOBA_BLOCK), lambda b, i: (0, 0, 0))
    slabs = [w.reshape(steps, -1, w.shape[-1]) for w in side_weights]
    slab_specs = [pl.BlockSpec((1,) + w.shape[1:], lambda b, i: (b * nb + i, 0, 0)) for w in slabs]
    outs = pl.pallas_call(
        _attn_kernel,
        grid=(B, nb),
        in_specs=[pl.BlockSpec((1, aw, MOBA_BLOCK), lambda b, i: (b, 0, i)),
                  whole((1, nb, MOBA_BLOCK, aw)), whole((1, nb, aw, MOBA_BLOCK)), whole((1, nb, aw)),
                  table, table] + slab_specs,
        out_specs=[pl.BlockSpec((1, MOBA_BLOCK, aw), lambda b, i: (b, i, 0))] + slab_specs,
        out_shape=[jax.ShapeDtypeStruct((B, S, aw), bf16)]
                  + [jax.ShapeDtypeStruct(w.shape, bf16) for w in slabs],
        compiler_params=_params(("parallel", "arbitrary")),
        name="attn",
    )(qT, k4, vT4, km, own, adj, *slabs)
    return outs[0], [o.reshape(w.shape) for o, w in zip(outs[1:], side_weights)]


def _post_kernel(a_ref, mc_ref, ga_ref, x_ref, g1_ref, sc_ref, sh_ref, ng_ref,
                 wao_ref, wo_ref, wr_ref, br_ref, x1_ref, h2p_ref, route_ref):
    tm = x_ref.shape[1]
    d = x_ref.shape[2]
    sub = tm // POST_SPLIT
    lane = lax.broadcasted_iota(jnp.int32, (sub, LANES), 1)
    ninf = -jnp.inf
    for t in range(POST_SPLIT):
        rows = slice(t * sub, (t + 1) * sub)
        ya = jnp.dot(a_ref[0, rows, :], wao_ref[...], preferred_element_type=f32)
        merged = ga_ref[0, rows, :].astype(f32) * ya + mc_ref[0, rows, :].astype(f32)
        z = jnp.dot(merged.astype(bf16), wo_ref[...], preferred_element_type=f32)
        x1 = x_ref[0, rows, :] + g1_ref[0] * z
        x1_ref[0, rows, :] = x1
        h2 = _ada_norm(x1, ng_ref[...], sc_ref[0], sh_ref[0])
        h_hi = h2.astype(bf16)
        h_hi32 = h_hi.astype(f32)

        bits = pltpu.bitcast(h_hi32, jnp.uint32)
        packed = (bits[:, :d // 2] >> 16) | (bits[:, d // 2:] & jnp.uint32(0xFFFF0000))
        chunks = packed.shape[1] // LANES
        for j in range(chunks):
            h2p_ref[pl.ds(t * sub * chunks + j, sub, stride=chunks), :] = packed[:, j * LANES:(j + 1) * LANES]

        h_lo = (h2 - h_hi32).astype(bf16)
        hi_both = jnp.dot(h_hi, wr_ref[...], preferred_element_type=f32)
        logits = (hi_both[:, :LANES]
                  + (jnp.dot(h_lo, wr_ref[:, :LANES], preferred_element_type=f32) + hi_both[:, LANES:])) + br_ref[...]
        gl = jnp.where(lane < N_GROUPS, logits, ninf)
        gmax = jnp.max(gl, axis=1, keepdims=True)
        g_sel = jnp.min(jnp.where(gl == gmax, lane, LANES), axis=1, keepdims=True)
        g_w = 1.0 / jnp.sum(jnp.exp(gl - gmax), axis=1, keepdims=True)
        off = lane - (N_GROUPS + EXPERTS_PER_GROUP * g_sel)
        in_group = (off >= 0) & (off < EXPERTS_PER_GROUP)
        el = jnp.where(in_group, logits, ninf)
        ex = jnp.exp(el - jnp.max(el, axis=1, keepdims=True))
        ep = jnp.where(in_group, ex / jnp.sum(ex, axis=1, keepdims=True), -1.0)
        p1 = jnp.max(ep, axis=1, keepdims=True)
        i1 = jnp.min(jnp.where(ep == p1, lane, LANES), axis=1, keepdims=True)
        ep2 = jnp.where(lane == i1, -1.0, ep)
        p2 = jnp.max(ep2, axis=1, keepdims=True)
        i2 = jnp.min(jnp.where(ep2 == p2, lane, LANES), axis=1, keepdims=True)
        tot = p1 + p2
        w1 = p1 / tot * g_w
        w2 = p2 / tot * g_w
        e1 = (i1 - N_GROUPS).astype(f32)
        e2 = (i2 - N_GROUPS).astype(f32)
        route = jnp.where(lane == 0, e1, jnp.where(lane == 1, e2,
                          jnp.where(lane == 2, w1, jnp.where(lane == 3, w2, 0.0))))
        route_ref[:, rows] = route.T[:SUBLANES, :]


def _post(a, mc, ga, x, g1, sc2, sh2, ng, w_attn_out, w_o, w_rg, b_rg, w_re, b_re):
    B, S, D = x.shape
    tm = ROW_TILE
    aw = a.shape[2]
    nt = S // tm
    wr = jnp.zeros((D, LANES), f32).at[:, :N_GROUPS].set(w_rg).at[:, N_GROUPS:N_GROUPS + N_EXPERTS].set(w_re)
    br = jnp.zeros((1, LANES), f32).at[0, :N_GROUPS].set(b_rg).at[0, N_GROUPS:N_GROUPS + N_EXPERTS].set(b_re)
    wr_hi = wr.astype(bf16)
    wr_split = jnp.concatenate([wr_hi, (wr - wr_hi.astype(f32)).astype(bf16)], axis=1)
    const = lambda shape: pl.BlockSpec(shape, lambda b, s: (0,) * len(shape))
    vec = pl.BlockSpec((1, 1, D), lambda b, s: (b, 0, 0))
    row = lambda width: pl.BlockSpec((1, tm, width), lambda b, s: (b, s, 0))
    flat = lambda width, rows_per_token=1: pl.BlockSpec((tm * rows_per_token, width), lambda b, s: (b * nt + s, 0))
    words = D // 2 // LANES
    return pl.pallas_call(
        _post_kernel,
        grid=(B, nt),
        in_specs=[row(aw), row(D), row(D), row(D), vec, vec, vec, const((1, D)),
                  const((aw, D)), const((D, D)), const((D, 2 * LANES)), const((1, LANES))],
        out_specs=(row(D), flat(LANES, words), pl.BlockSpec((SUBLANES, tm), lambda b, s: (0, b * nt + s))),
        out_shape=(jax.ShapeDtypeStruct((B, S, D), f32),
                   jax.ShapeDtypeStruct((B * S * words, LANES), jnp.uint32),
                   jax.ShapeDtypeStruct((SUBLANES, B * S), f32)),
        compiler_params=_params(("parallel", "parallel")),
        name="post",
    )(a, mc, ga, x, g1.reshape(B, 1, D), sc2.reshape(B, 1, D), sh2.reshape(B, 1, D), ng.reshape(1, D),
      w_attn_out.astype(bf16), w_o.astype(bf16), wr_split, br)


def _dispatch_tables(route, n_chunks):
    T = route.shape[1]
    tc = T // n_chunks
    na = tc * TOP_K_EXPERTS
    per_chunk = lambda rows: rows.reshape(TOP_K_EXPERTS, n_chunks, tc).transpose(1, 0, 2).reshape(n_chunks, na)
    e = per_chunk(route[:TOP_K_EXPERTS]).astype(jnp.int32)
    w = per_chunk(route[TOP_K_EXPERTS:2 * TOP_K_EXPERTS])
    tok = jnp.broadcast_to(jnp.arange(na, dtype=jnp.int32) % tc, e.shape)
    _, stok, sw = lax.sort((e, tok, w), dimension=1, num_keys=1, is_stable=True)
    counts = jnp.sum((e[:, None, :] == jnp.arange(N_EXPERTS)[None, :, None]).astype(jnp.int32), axis=2)
    starts = jnp.cumsum(counts, axis=1) - counts
    flat = lambda t: t.reshape(-1).astype(jnp.int32)
    return flat(starts), flat(counts), stok.reshape(n_chunks, 1, na), sw.reshape(n_chunks, 1, na)


def _moe_kernel(start_ref, count_ref, stok_ref, sw_ref, h2p_ref, wg_ref, wu_ref, wd_ref,
                x1_ref, g2_ref, fg_ref, o_ref, yacc, xs, outs, *, normalize):
    c = pl.program_id(0)
    b = pl.program_id(1)
    block_rows = outs.shape[0] // SUBLANES
    words = xs.shape[0] // block_rows
    half = words * LANES
    run = c * N_EXPERTS + jnp.minimum(b, N_EXPERTS - 1)
    count = jnp.where(b < N_EXPERTS, count_ref[run], 0)
    first = start_ref[run]

    @pl.when(b == 0)
    def _():
        yacc[...] = jnp.zeros(yacc.shape, f32)

    @pl.when((b == 0) & (c == 0))
    def _():
        xs[...] = jnp.zeros(xs.shape, jnp.uint32)

    def token_rows(index, rows_per_token):
        return pl.ds(pl.multiple_of(index * rows_per_token, rows_per_token), rows_per_token)

    def block(blk, carry):
        st = first + blk * block_rows
        n = jnp.minimum(count - blk * block_rows, block_rows)

        def for_rows(rows_fn):
            full = n // ROW_UNROLL

            def group(g, carry):
                rows_fn([g * ROW_UNROLL + j for j in range(ROW_UNROLL)], st + g * ROW_UNROLL)
                return carry

            def single(j, carry):
                rows_fn([full * ROW_UNROLL + j], st + full * ROW_UNROLL + j)
                return carry

            lax.fori_loop(0, full, group, 0)
            lax.fori_loop(0, n - full * ROW_UNROLL, single, 0)

        def gather(rs, i0):
            for k, r in enumerate(rs):
                xs[token_rows(r, words), :] = h2p_ref[token_rows(stok_ref[0, 0, i0 + k], words), :]

        for_rows(gather)
        lo, hi = [], []
        for j in range(words):
            packed = xs[pl.ds(j, block_rows, stride=words), :]
            lo.append(pltpu.bitcast(packed << 16, f32).astype(bf16))
            hi.append(pltpu.bitcast(packed & jnp.uint32(0xFFFF0000), f32).astype(bf16))
        x_lo = jnp.concatenate(lo, axis=1)
        x_hi = jnp.concatenate(hi, axis=1)

        def up(w_ref):
            return (jnp.dot(x_lo, w_ref[0, :half, :], preferred_element_type=f32)
                    + jnp.dot(x_hi, w_ref[0, half:, :], preferred_element_type=f32))

        hidden = (jax.nn.silu(up(wg_ref)) * up(wu_ref)).astype(bf16)
        out = jnp.dot(hidden, wd_ref[0], preferred_element_type=f32)
        for j in range(SUBLANES):
            outs[pl.ds(j, block_rows, stride=SUBLANES), :] = out[:, j * LANES:(j + 1) * LANES]

        def scatter(rs, i0):
            toks = [stok_ref[0, 0, i0 + k] for k in range(len(rs))]
            acc = [yacc[token_rows(tok, SUBLANES), :] for tok in toks]
            for k, (r, tok, a) in enumerate(zip(rs, toks, acc)):
                yacc[token_rows(tok, SUBLANES), :] = a + sw_ref[0, 0, i0 + k] * outs[token_rows(r, SUBLANES), :]

        for_rows(scatter)
        return carry

    lax.fori_loop(0, (count + block_rows - 1) // block_rows, block, 0)

    @pl.when(b >= N_EXPERTS)
    def _():
        rows = o_ref.shape[0]
        start = pl.multiple_of((b - N_EXPERTS) * rows * SUBLANES, rows * SUBLANES)
        y = jnp.concatenate([yacc[pl.ds(start + j, rows, stride=SUBLANES), :] for j in range(SUBLANES)], axis=1)
        x2 = x1_ref[...] + g2_ref[0] * y
        if normalize:
            x2 = (x2 * lax.rsqrt(jnp.mean(x2 * x2, axis=-1, keepdims=True) + NORM_EPS)) * fg_ref[...]
        o_ref[...] = x2


def _moe(route, h2p, w_gate, w_up, w_down, x1, g2, fg, normalize):
    B, S, D = x1.shape
    T = B * S
    words = h2p.shape[0] // T
    assert D == words * 2 * LANES == SUBLANES * LANES
    de = w_gate.shape[2]
    n_chunks = MOE_CHUNKS
    tc = T // n_chunks
    fin_rows = ROW_TILE // 2
    n_fin = tc // fin_rows
    assert n_chunks % B == 0 and tc % fin_rows == 0 and w_gate.shape[0] == N_EXPERTS
    starts, counts, stok, sw = _dispatch_tables(route, n_chunks)
    na = stok.shape[2]
    wmap = lambda c, b, *_: (jnp.minimum(b, N_EXPERTS - 1), 0, 0)
    smem = lambda: pl.BlockSpec((1, 1, na), lambda c, b, *_: (c, 0, 0), memory_space=pltpu.SMEM)
    tile = pl.BlockSpec((fin_rows, D), lambda c, b, *_: (c * n_fin + jnp.maximum(b - N_EXPERTS, 0), 0))
    grid_spec = pltpu.PrefetchScalarGridSpec(
        num_scalar_prefetch=2,
        grid=(n_chunks, N_EXPERTS + n_fin),
        in_specs=[smem(), smem(),
                  pl.BlockSpec((tc * words, LANES), lambda c, b, *_: (c, 0)),
                  pl.BlockSpec((1, D, de), wmap),
                  pl.BlockSpec((1, D, de), wmap),
                  pl.BlockSpec((1, de, D), wmap),
                  tile,
                  pl.BlockSpec((1, 1, D), lambda c, b, *_: (c // (n_chunks // B), 0, 0)),
                  pl.BlockSpec((1, D), lambda c, b, *_: (0, 0))],
        out_specs=tile,
        scratch_shapes=[pltpu.VMEM((tc * SUBLANES, LANES), f32),
                        pltpu.VMEM((EXPERT_ROWS * words, LANES), jnp.uint32),
                        pltpu.VMEM((EXPERT_ROWS * SUBLANES, LANES), f32)],
    )
    out = pl.pallas_call(
        functools.partial(_moe_kernel, normalize=normalize),
        grid_spec=grid_spec,
        out_shape=jax.ShapeDtypeStruct((T, D), f32),
        compiler_params=_params(("arbitrary", "arbitrary")),
        name="moe",
    )(starts, counts, stok, sw, h2p, w_gate, w_up, w_down,
      x1.reshape(T, D), g2.reshape(B, 1, D), fg.reshape(1, D))
    return out.reshape(B, S, D)


def kernel(x, c, rel_bias, norm1_g, norm2_g, w_ada, b_ada, w_in, b_gate, conv_w, w_attn_out,
           w_conv_out, w_o, w_router_group, b_router_group, w_router_expert, b_router_expert,
           w_exp_gate, w_exp_up, w_exp_down, final_norm_g):
    B, S, D = x.shape
    depth = w_ada.shape[0]
    assert S % ROW_TILE == 0 and ROW_TILE % MOBA_BLOCK == 0
    own, adj = _bias_tiles(rel_bias)
    for l in range(depth):
        mod = _mod(c, w_ada[l], b_ada[l])
        sh1, sc1, g1, sh2, sc2, g2 = jnp.split(mod, 6, axis=-1)
        qT, k4, vT4, km, mc, ga = _inproj(x, norm1_g[l], sc1, sh1, w_in[l], b_gate[l], conv_w[l],
                                          w_conv_out[l])
        a, expert_w = _attention(qT, k4, vT4, km, own, adj, [w_exp_gate[l], w_exp_up[l], w_exp_down[l]])
        x1, h2p, route = _post(a, mc, ga, x, g1, sc2, sh2, norm2_g[l], w_attn_out[l], w_o[l],
                               w_router_group[l], b_router_group[l], w_router_expert[l],
                               b_router_expert[l])
        x = _moe(route, h2p, *expert_w, x1, g2, final_norm_g, normalize=(l + 1 == depth))
    return x
```
